```python
import math
import jax, jax.numpy as jnp
from jax import lax
import numpy as np

D_MODEL = 1024
BATCH = 8
SEQ = 4096
DEPTH = 1

CHUNK = 64
Q_BLOCK = 128
HEAD_DIM = 64
DIFF_HEADS = 4
DIFF_V_DIM = 2 * HEAD_DIM
DIFF_WIDTH = DIFF_HEADS * DIFF_V_DIM
FOX_HEADS = 8
FOX_WIDTH = FOX_HEADS * HEAD_DIM
MIX_WIDTH = DIFF_WIDTH + FOX_WIDTH
DIFF_QK_WIDTH = DIFF_HEADS * 2 * HEAD_DIM
IN_COLS = 2 * DIFF_QK_WIDTH + DIFF_WIDTH + 3 * FOX_WIDTH + FOX_HEADS
ROPE_THETA = 10000.0
N_GROUPS = 4
EXPERTS_PER_GROUP = 8
N_EXPERTS = N_GROUPS * EXPERTS_PER_GROUP
TOP_K_IN_GROUP = 2
D_EXPERT = D_MODEL // 2
MOE_BLOCK = 128
EPS = 1e-6
NEG_INF = -1e30

kernel_name = "hybrid_diff_fox_hmoe_block"


def rms_norm(x, g):
    xf = x.astype(jnp.float32)
    y = xf * lax.rsqrt(jnp.mean(xf * xf, axis=-1, keepdims=True) + EPS)
    return (y * g.astype(jnp.float32)).astype(x.dtype)


def rope_tables(seq_len, dtype):
    inv = 1.0 / (ROPE_THETA ** (jnp.arange(0, HEAD_DIM, 2, dtype=jnp.float32) / HEAD_DIM))
    ang = jnp.arange(seq_len, dtype=jnp.float32)[:, None] * inv[None, :]
    return jnp.cos(ang).astype(dtype), jnp.sin(ang).astype(dtype)


def apply_rope(x, cos, sin):
    x1, x2 = jnp.split(x, 2, axis=-1)
    c = cos[None, :, None, :]
    s = sin[None, :, None, :]
    return jnp.concatenate([x1 * c - x2 * s, x2 * c + x1 * s], axis=-1)


def diff_attention(q, k, v, lam, subln_g, lambda_init):
    B, S = q.shape[0], q.shape[1]
    scale = HEAD_DIM ** -0.5
    pos = jnp.arange(S)
    outs = []
    for i in range(S // Q_BLOCK):
        q0, q1 = i * Q_BLOCK, (i + 1) * Q_BLOCK
        s = jnp.einsum('bqhmd,bkhmd->bhmqk', q[:, q0:q1], k[:, :q1]).astype(jnp.float32) * scale
        mask = (pos[None, :q1] // CHUNK) <= (pos[q0:q1, None] // CHUNK)
        p = jax.nn.softmax(jnp.where(mask, s, NEG_INF), axis=-1)
        a = p[:, :, 0] - lam * p[:, :, 1]
        outs.append(jnp.einsum('bhqk,bkhe->bqhe', a.astype(v.dtype), v[:, :q1]))
    o = jnp.concatenate(outs, axis=1)
    return rms_norm(o, subln_g) * (1.0 - lambda_init)


def forgetting_attention(q, k, v, log_f):
    S = q.shape[1]
    scale = HEAD_DIM ** -0.5
    F = jnp.cumsum(log_f, axis=1).transpose(0, 2, 1)
    pos = jnp.arange(S)
    outs = []
    for i in range(S // Q_BLOCK):
        q0, q1 = i * Q_BLOCK, (i + 1) * Q_BLOCK
        s = jnp.einsum('bqhd,bkhd->bhqk', q[:, q0:q1], k[:, :q1]).astype(jnp.float32) * scale
        s = s + F[:, :, q0:q1, None] - F[:, :, None, :q1]
        mask = pos[None, :q1] <= pos[q0:q1, None]
        p = jax.nn.softmax(jnp.where(mask, s, NEG_INF), axis=-1)
        outs.append(jnp.einsum('bhqk,bkhd->bqhd', p.astype(v.dtype), v[:, :q1]))
    return jnp.concatenate(outs, axis=1)


def hierarchical_moe(h, w_group, b_group, w_router, b_router, w_gate, w_up, w_down):
    B, S, D = h.shape
    T = B * S
    xt = h.reshape(T, D)
    g_logits = (xt @ w_group).astype(jnp.float32) + b_group.astype(jnp.float32)
    g_prob = jax.nn.softmax(g_logits, axis=-1)
    g_idx = jnp.argmax(g_logits, axis=-1)
    g_w = jnp.take_along_axis(g_prob, g_idx[:, None], axis=1)[:, 0]
    e_logits = ((xt @ w_router).astype(jnp.float32) + b_router.astype(jnp.float32))
    e_logits = e_logits.reshape(T, N_GROUPS, EXPERTS_PER_GROUP)
    e_sel = jnp.take_along_axis(e_logits, g_idx[:, None, None], axis=1)[:, 0]
    top_v, top_i = lax.top_k(e_sel, TOP_K_IN_GROUP)
    weights = g_w[:, None] * jax.nn.softmax(top_v, axis=-1)
    expert_ids = g_idx[:, None] * EXPERTS_PER_GROUP + top_i
    M = T * TOP_K_IN_GROUP
    flat_e = expert_ids.reshape(M)
    flat_w = weights.reshape(M)
    flat_tok = jnp.arange(M) // TOP_K_IN_GROUP
    order = jnp.argsort(flat_e)
    e_sorted, tok_sorted, w_sorted = flat_e[order], flat_tok[order], flat_w[order]
    counts = jnp.bincount(flat_e, length=N_EXPERTS)
    start = jnp.cumsum(counts) - counts
    padded = ((counts + MOE_BLOCK - 1) // MOE_BLOCK) * MOE_BLOCK
    pend = jnp.cumsum(padded)
    pstart = pend - padded
    dest = pstart[e_sorted] + (jnp.arange(M) - start[e_sorted])
    n_blocks = -(-M // MOE_BLOCK) + N_EXPERTS
    P = n_blocks * MOE_BLOCK
    buf = jnp.zeros((P, D), h.dtype).at[dest].set(xt[tok_sorted])
    block_e = jnp.minimum(jnp.searchsorted(pend, jnp.arange(n_blocks) * MOE_BLOCK, side='right'),
                          N_EXPERTS - 1)

    def expert_block(args):
        xb, e = args
        a = xb @ w_gate[e]
        u = xb @ w_up[e]
        return (jax.nn.silu(a) * u) @ w_down[e]

    out = lax.map(expert_block, (buf.reshape(n_blocks, MOE_BLOCK, D), block_e))
    y_sorted = out.reshape(P, D)[dest] * w_sorted[:, None].astype(h.dtype)
    y = jax.ops.segment_sum(y_sorted, tok_sorted, num_segments=T)
    return y.reshape(B, S, D)


def setup_inputs(seed: int = 0) -> dict:
    key = jax.random.key(seed)
    ks = jax.random.split(key, 24)
    f32 = jnp.float32
    L, D, E, Fd = DEPTH, D_MODEL, N_EXPERTS, D_EXPERT
    nrm = lambda k, shape, s: jax.random.normal(k, shape, f32) * s
    gain = lambda k, shape: 1.0 + 0.01 * jax.random.normal(k, shape, f32)
    return {
        "x": jax.random.normal(ks[0], (BATCH, SEQ, D), f32),
        "c": jax.random.normal(ks[1], (BATCH, D), f32),
        "w_ada": nrm(ks[2], (L, D, 6 * D), 0.5 * D ** -0.5),
        "b_ada": nrm(ks[3], (L, 6 * D), 0.01),
        "g_pre_attn": gain(ks[4], (L, D)),
        "g_post_attn": gain(ks[5], (L, D)),
        "w_in": nrm(ks[6], (L, D, IN_COLS), D ** -0.5),
        "b_forget": 3.0 + 2.0 * jax.random.uniform(ks[7], (L, FOX_HEADS), f32),
        "lambda_q1": nrm(ks[8], (L, HEAD_DIM), 0.1),
        "lambda_k1": nrm(ks[9], (L, HEAD_DIM), 0.1),
        "lambda_q2": nrm(ks[10], (L, HEAD_DIM), 0.1),
        "lambda_k2": nrm(ks[11], (L, HEAD_DIM), 0.1),
        "g_diff_subln": gain(ks[12], (L, DIFF_V_DIM)),
        "w_out": nrm(ks[13], (L, MIX_WIDTH, D), MIX_WIDTH ** -0.5),
        "g_pre_ffn": gain(ks[14], (L, D)),
        "g_post_ffn": gain(ks[15], (L, D)),
        "w_group": nrm(ks[16], (L, D, N_GROUPS), D ** -0.5),
        "b_group": nrm(ks[17], (L, N_GROUPS), 0.01),
        "w_router": nrm(ks[18], (L, D, E), D ** -0.5),
        "b_router": nrm(ks[19], (L, E), 0.01),
        "w_gate": nrm(ks[20], (L, E, D, Fd), D ** -0.5),
        "w_up": nrm(ks[21], (L, E, D, Fd), D ** -0.5),
        "w_down": nrm(ks[22], (L, E, Fd, D), Fd ** -0.5),
    }


def reference(x, c, w_ada, b_ada, g_pre_attn, g_post_attn, w_in, b_forget,
              lambda_q1, lambda_k1, lambda_q2, lambda_k2, g_diff_subln, w_out,
              g_pre_ffn, g_post_ffn, w_group, b_group, w_router, b_router,
              w_gate, w_up, w_down):
    B, S, D = x.shape
    cos, sin = rope_tables(S, x.dtype)
    split_at = np.cumsum([DIFF_QK_WIDTH, DIFF_QK_WIDTH, DIFF_WIDTH,
                          FOX_WIDTH, FOX_WIDTH, FOX_WIDTH]).tolist()
    for l in range(DEPTH):
        lambda_init = 0.8 - 0.6 * math.exp(-0.3 * l)
        mod = (jax.nn.silu(c) @ w_ada[l] + b_ada[l])[:, None, :]
        sh_a, sc_a, gt_a, sh_f, sc_f, gt_f = jnp.split(mod, 6, axis=-1)

        h = rms_norm(x, g_pre_attn[l]) * (1.0 + sc_a) + sh_a
        proj = h @ w_in[l]
        dq, dk, dv, fq, fk, fv, f_logit = jnp.split(proj, split_at, axis=-1)

        dq = apply_rope(dq.reshape(B, S, 2 * DIFF_HEADS, HEAD_DIM), cos, sin)
        dk = apply_rope(dk.reshape(B, S, 2 * DIFF_HEADS, HEAD_DIM), cos, sin)
        dq = dq.reshape(B, S, DIFF_HEADS, 2, HEAD_DIM)
        dk = dk.reshape(B, S, DIFF_HEADS, 2, HEAD_DIM)
        dv = dv.reshape(B, S, DIFF_HEADS, DIFF_V_DIM)
        lam = (jnp.exp(jnp.sum(lambda_q1[l].astype(jnp.float32) * lambda_k1[l].astype(jnp.float32)))
               - jnp.exp(jnp.sum(lambda_q2[l].astype(jnp.float32) * lambda_k2[l].astype(jnp.float32)))
               + lambda_init)
        diff_out = diff_attention(dq, dk, dv, lam, g_diff_subln[l], lambda_init)

        log_f = jax.nn.log_sigmoid((f_logit + b_forget[l]).astype(jnp.float32))
        fox_out = forgetting_attention(fq.reshape(B, S, FOX_HEADS, HEAD_DIM),
                                       fk.reshape(B, S, FOX_HEADS, HEAD_DIM),
                                       fv.reshape(B, S, FOX_HEADS, HEAD_DIM), log_f)

        mixed = jnp.concatenate([diff_out.reshape(B, S, DIFF_WIDTH),
                                 fox_out.reshape(B, S, FOX_WIDTH)], axis=-1) @ w_out[l]
        x = x + gt_a * rms_norm(mixed, g_post_attn[l])

        h = rms_norm(x, g_pre_ffn[l]) * (1.0 + sc_f) + sh_f
        y = hierarchical_moe(h, w_group[l], b_group[l], w_router[l], b_router[l],
                             w_gate[l], w_up[l], w_down[l])
        x = x + gt_f * rms_norm(y, g_post_ffn[l])
    return x
```

```python
import functools
import math

import jax
import jax.numpy as jnp
from jax import lax
from jax.experimental import pallas as pl
from jax.experimental.pallas import tpu as pltpu

F32 = jnp.float32
BF16 = jnp.bfloat16

D_MODEL = 1024
HEAD_DIM = 64
CHUNK = 64
DIFF_HEADS = 4
FOX_HEADS = 8
SEC = 512
N_GROUPS = 4
EXPERTS_PER_GROUP = 8
N_EXPERTS = 32
D_EXPERT = 512
EPS = 1e-6
NEG_INF = -1e30
LOG2E = 1.4426950408889634
LAMBDA_INIT = 0.8 - 0.6 * math.exp(-0.3 * 0)

LANES = 128
TM_PROJ = 512
TQ = 512
TK = 512
TM_ROUTE = 512
BM = 256
TM_ROWS = 256
VMEM_LIMIT = 48 * 1024 * 1024


def _cparams(sem):
    return pltpu.CompilerParams(dimension_semantics=sem, vmem_limit_bytes=VMEM_LIMIT)


def _rms(x, g):
    return x * lax.rsqrt(jnp.mean(x * x, axis=-1, keepdims=True) + EPS) * g


def _ada_kernel(c_ref, w_ref, b_ref, o_ref):
    c = c_ref[...]
    cs = c * jax.nn.sigmoid(c)
    o_ref[...] = jnp.dot(cs, w_ref[...], preferred_element_type=F32,
                         precision=lax.Precision.HIGHEST) + b_ref[...]


def _ada(c, w_ada, b_ada):
    B, D = c.shape
    N = w_ada.shape[1]
    tn = 1024
    return pl.pallas_call(
        _ada_kernel,
        grid=(N // tn,),
        in_specs=[pl.BlockSpec((B, D), lambda j: (0, 0)),
                  pl.BlockSpec((D, tn), lambda j: (0, j)),
                  pl.BlockSpec((1, tn), lambda j: (0, j))],
        out_specs=pl.BlockSpec((B, tn), lambda j: (0, j)),
        out_shape=jax.ShapeDtypeStruct((B, N), F32),
        compiler_params=_cparams(("arbitrary",)),
        name="ada_mod",
    )(c, w_ada, b_ada.reshape(1, N))


def _split3(x):
    hi = x.astype(BF16)
    r1 = x - hi.astype(F32)
    mid = r1.astype(BF16)
    lo = (r1 - mid.astype(F32)).astype(BF16)
    return hi, mid, lo


def _inproj_kernel(x_ref, mod_ref, g_ref, w_ref, wf_ref, bf_ref, cos_ref, sin_ref,
                   dq_ref, dk_ref, dv_ref, fq_ref, fk_ref, fv_ref, fcum_ref, carry_ref):
    si = pl.program_id(1)
    x = x_ref[...]
    h = _rms(x, g_ref[...]) * (1.0 + mod_ref[1:2, :]) + mod_ref[0:1, :]
    hb = h.astype(BF16)
    tm = x.shape[0]

    cos = cos_ref[...]
    sin = sin_ref[...]
    lane = lax.broadcasted_iota(jnp.int32, (tm, LANES), 1)
    first_half = (lane % HEAD_DIM) < (HEAD_DIM // 2)
    qscale = HEAD_DIM ** -0.5 * LOG2E

    def proj(sec):
        return jnp.dot(hb, w_ref[:, sec * SEC:(sec + 1) * SEC], preferred_element_type=F32)

    def rope_store(p, out_ref, scale):
        for j in range(SEC // LANES):
            ch = p[:, j * LANES:(j + 1) * LANES]
            partner = jnp.where(first_half,
                                pltpu.roll(ch, LANES - HEAD_DIM // 2, 1),
                                pltpu.roll(ch, HEAD_DIM // 2, 1))
            r = ch * cos + partner * sin
            if scale != 1.0:
                r = r * scale
            out_ref[:, j * LANES:(j + 1) * LANES] = r.astype(out_ref.dtype)

    rope_store(proj(0), dq_ref, qscale)
    rope_store(proj(1), dk_ref, 1.0)
    dv_ref[...] = proj(2).astype(dv_ref.dtype)
    fq_ref[...] = (proj(3) * qscale).astype(fq_ref.dtype)
    fk_ref[...] = proj(4).astype(fk_ref.dtype)
    fv_ref[...] = proj(5).astype(fv_ref.dtype)

    z = lax.dot_general(wf_ref[...], hb, (((1,), (1,)), ((), ())),
                        preferred_element_type=F32) + bf_ref[:, 0:1]
    lf = (jnp.minimum(z, 0.0) - jnp.log1p(jnp.exp(-jnp.abs(z)))) * LOG2E

    @pl.when(si == 0)
    def _():
        carry_ref[...] = jnp.zeros_like(carry_ref)

    r_i = lax.broadcasted_iota(jnp.int32, (LANES, LANES), 0)
    c_i = lax.broadcasted_iota(jnp.int32, (LANES, LANES), 1)
    tri = jnp.where(r_i <= c_i, 1.0, 0.0).astype(BF16)
    run = carry_ref[...]
    for j in range(tm // LANES):
        hi, mid, lo = _split3(lf[:, j * LANES:(j + 1) * LANES])
        cs = (jnp.dot(hi, tri, preferred_element_type=F32)
              + jnp.dot(mid, tri, preferred_element_type=F32)
              + jnp.dot(lo, tri, preferred_element_type=F32)) + run
        fcum_ref[:, j * LANES:(j + 1) * LANES] = cs
        run = jnp.broadcast_to(cs[:, LANES - 1:LANES], run.shape)
    carry_ref[...] = run


def _inproj(x, mod, g_pre, w_main, wf_t, bf, cos, sin):
    B, S, D = x.shape
    tm = TM_PROJ
    sec_out = jax.ShapeDtypeStruct((B, S, SEC), BF16)
    sec_spec = pl.BlockSpec((None, tm, SEC), lambda b, s: (b, s, 0))
    return pl.pallas_call(
        _inproj_kernel,
        grid=(B, S // tm),
        in_specs=[pl.BlockSpec((None, tm, D), lambda b, s: (b, s, 0)),
                  pl.BlockSpec((None, 6, D), lambda b, s: (b, 0, 0)),
                  pl.BlockSpec((1, D), lambda b, s: (0, 0)),
                  pl.BlockSpec((D, 6 * SEC), lambda b, s: (0, 0)),
                  pl.BlockSpec((FOX_HEADS, D), lambda b, s: (0, 0)),
                  pl.BlockSpec((FOX_HEADS, LANES), lambda b, s: (0, 0)),
                  pl.BlockSpec((tm, LANES), lambda b, s: (s, 0)),
                  pl.BlockSpec((tm, LANES), lambda b, s: (s, 0))],
        out_specs=[sec_spec] * 6 + [pl.BlockSpec((None, FOX_HEADS, tm), lambda b, s: (b, 0, s))],
        out_shape=[sec_out] * 6 + [jax.ShapeDtypeStruct((B, FOX_HEADS, S), F32)],
        scratch_shapes=[pltpu.VMEM((FOX_HEADS, LANES), F32)],
        compiler_params=_cparams(("arbitrary", "arbitrary")),
        name="in_proj",
    )(x, mod, g_pre, w_main, wf_t, bf, cos, sin)


def _online_step(carry, s, vb):
    m, l, acc = carry
    m_new = jnp.maximum(m, jnp.max(s, axis=1, keepdims=True))
    alpha = jnp.exp2(m - m_new)
    p = jnp.exp2(s - m_new)
    l = alpha * l + jnp.sum(p, axis=1, keepdims=True)
    acc = alpha * acc + jnp.dot(p.astype(BF16), vb, preferred_element_type=F32)
    return m_new, l, acc


def _attend(qm, k_ref, v_ref, qi, bias_fn, diag_mask):
    tq = qm.shape[0]
    nt = (((1,), (1,)), ((), ()))

    def scores(k0):
        kb = k_ref[pl.ds(k0, TK), :]
        s = lax.dot_general(qm, kb, nt, preferred_element_type=F32)
        return bias_fn(s, k0), v_ref[pl.ds(k0, TK), :]

    def body(ki, carry):
        s, vb = scores(pl.multiple_of(ki * TK, TK))
        return _online_step(carry, s, vb)

    init = (jnp.full((tq, 1), NEG_INF, F32), jnp.zeros((tq, 1), F32),
            jnp.zeros((tq, LANES), F32))
    carry = lax.fori_loop(0, qi, body, init)
    s, vb = scores(pl.multiple_of(qi * TK, TK))
    s = jnp.where(diag_mask, s, NEG_INF)
    _, l, acc = _online_step(carry, s, vb)
    return acc / l


def _fox_kernel(q_ref, k_ref, v_ref, fc_ref, fr_ref, o_ref):
    qi = pl.program_id(2)
    q2 = q_ref[...]
    lane = lax.broadcasted_iota(jnp.int32, q2.shape, 1)
    row = lax.broadcasted_iota(jnp.int32, (TQ, TK), 0)
    col = lax.broadcasted_iota(jnp.int32, (TQ, TK), 1)
    causal = col <= row
    outs = []
    for j in range(2):
        in_head = (lane >= j * HEAD_DIM) & (lane < (j + 1) * HEAD_DIM)
        qm = jnp.where(in_head, q2, jnp.zeros_like(q2))
        fq = fc_ref[:, j:j + 1]

        def bias(s, k0, j=j, fq=fq):
            return s + fq - fr_ref[j:j + 1, pl.ds(k0, TK)]

        outs.append(_attend(qm, k_ref, v_ref, qi, bias, causal))
    o_ref[...] = jnp.where(lane < HEAD_DIM, outs[0], outs[1]).astype(o_ref.dtype)


def _fox_attention(fq, fk, fv, fcol, frow):
    B, S, _ = fq.shape
    npair = FOX_HEADS // 2
    return pl.pallas_call(
        _fox_kernel,
        grid=(B, npair, S // TQ),
        in_specs=[pl.BlockSpec((None, TQ, LANES), lambda b, p, q: (b, q, p)),
                  pl.BlockSpec((None, S, LANES), lambda b, p, q: (b, 0, p)),
                  pl.BlockSpec((None, S, LANES), lambda b, p, q: (b, 0, p)),
                  pl.BlockSpec((None, None, TQ, 2), lambda b, p, q: (b, p, q, 0)),
                  pl.BlockSpec((None, None, 2, S), lambda b, p, q: (b, p, 0, 0))],
        out_specs=pl.BlockSpec((None, TQ, LANES), lambda b, p, q: (b, q, p)),
        out_shape=jax.ShapeDtypeStruct((B, S, SEC), BF16),
        compiler_params=_cparams(("arbitrary", "arbitrary", "arbitrary")),
        name="fox_attention",
    )(fq, fk, fv, fcol, frow)


def _diff_kernel(q_ref, k_ref, v_ref, lam_ref, g_ref, o_ref):
    qi = pl.program_id(2)
    q2 = q_ref[...]
    lane = lax.broadcasted_iota(jnp.int32, q2.shape, 1)
    row = lax.broadcasted_iota(jnp.int32, (TQ, TK), 0)
    col = lax.broadcasted_iota(jnp.int32, (TQ, TK), 1)
    chunk_causal = (col // CHUNK) <= (row // CHUNK)
    outs = []
    for j in range(2):
        in_map = (lane >= j * HEAD_DIM) & (lane < (j + 1) * HEAD_DIM)
        qm = jnp.where(in_map, q2, jnp.zeros_like(q2))
        outs.append(_attend(qm, k_ref, v_ref, qi, lambda s, k0: s, chunk_causal))
    lp = lam_ref[...]
    lam = (jnp.exp(jnp.sum(lp[0:1] * lp[1:2], axis=1, keepdims=True))
           - jnp.exp(jnp.sum(lp[2:3] * lp[3:4], axis=1, keepdims=True)) + LAMBDA_INIT)
    o = outs[0] - lam * outs[1]
    o_ref[...] = (_rms(o, g_ref[...]) * (1.0 - LAMBDA_INIT)).astype(o_ref.dtype)


def _diff_attention(dq, dk, dv, lam_params, g_subln):
    B, S, _ = dq.shape
    return pl.pallas_call(
        _diff_kernel,
        grid=(B, DIFF_HEADS, S // TQ),
        in_specs=[pl.BlockSpec((None, TQ, LANES), lambda b, h, q: (b, q, h)),
                  pl.BlockSpec((None, S, LANES), lambda b, h, q: (b, 0, h)),
                  pl.BlockSpec((None, S, LANES), lambda b, h, q: (b, 0, h)),
                  pl.BlockSpec((4, HEAD_DIM), lambda b, h, q: (0, 0)),
                  pl.BlockSpec((1, LANES), lambda b, h, q: (0, 0))],
        out_specs=pl.BlockSpec((None, TQ, LANES), lambda b, h, q: (b, q, h)),
        out_shape=jax.ShapeDtypeStruct((B, S, SEC), BF16),
        compiler_params=_cparams(("arbitrary", "arbitrary", "arbitrary")),
        name="diff_attention",
    )(dq, dk, dv, lam_params, g_subln)


def _route_kernel(d_ref, f_ref, wd_ref, wf_ref, x_ref, mod_ref, gpa_ref, gpf_ref, wr_ref, br_ref,
                  x1_ref, h2_ref, ri_ref, rw_ref, cnt_ref):
    first = (pl.program_id(0) == 0) & (pl.program_id(1) == 0)

    @pl.when(first)
    def _():
        cnt_ref[...] = jnp.zeros_like(cnt_ref)

    mixed = (jnp.dot(d_ref[...], wd_ref[...], preferred_element_type=F32)
             + jnp.dot(f_ref[...], wf_ref[...], preferred_element_type=F32))
    x1 = x_ref[...] + mod_ref[2:3, :] * _rms(mixed, gpa_ref[...])
    x1_ref[...] = x1
    h2 = _rms(x1, gpf_ref[...]) * (1.0 + mod_ref[4:5, :]) + mod_ref[3:4, :]
    h2_ref[...] = h2
    tm = h2.shape[0]

    nt = (((1,), (1,)), ((), ()))
    hh, hm, hl = _split3(h2)
    wh, wm, wl = _split3(wr_ref[...])
    lt = br_ref[:, 0:1]
    for a, b_ in ((wh, hh), (wh, hm), (wm, hh), (wm, hm), (wh, hl), (wl, hh)):
        lt = lt + lax.dot_general(a, b_, nt, preferred_element_type=F32)

    g = lt[0:N_GROUPS]
    row4 = lax.broadcasted_iota(jnp.int32, g.shape, 0)
    gmax = jnp.max(g, axis=0, keepdims=True)
    gidx = jnp.min(jnp.where(g == gmax, row4, N_GROUPS), axis=0, keepdims=True)
    gw = 1.0 / jnp.sum(jnp.exp(g - gmax), axis=0, keepdims=True)
    esel = jnp.zeros((EXPERTS_PER_GROUP, tm), F32)
    for gg in range(N_GROUPS):
        esel = jnp.where(gidx == gg, lt[8 + 8 * gg:16 + 8 * gg], esel)
    row8 = lax.broadcasted_iota(jnp.int32, esel.shape, 0)
    v1 = jnp.max(esel, axis=0, keepdims=True)
    i1 = jnp.min(jnp.where(esel == v1, row8, EXPERTS_PER_GROUP), axis=0, keepdims=True)
    esel2 = jnp.where(row8 == i1, -jnp.inf, esel)
    v2 = jnp.max(esel2, axis=0, keepdims=True)
    i2 = jnp.min(jnp.where(esel2 == v2, row8, EXPERTS_PER_GROUP), axis=0, keepdims=True)
    e = jnp.exp(v2 - v1)
    w1 = gw / (1.0 + e)
    w2 = gw * e / (1.0 + e)
    e1 = gidx * EXPERTS_PER_GROUP + i1
    e2 = gidx * EXPERTS_PER_GROUP + i2

    row32 = lax.broadcasted_iota(jnp.int32, (N_EXPERTS, tm), 0)
    oh1 = row32 == e1
    oh2 = row32 == e2
    oh = jnp.where(oh1 | oh2, 1.0, 0.0)
    t_r = lax.broadcasted_iota(jnp.int32, (tm, tm), 0)
    t_c = lax.broadcasted_iota(jnp.int32, (tm, tm), 1)
    tri = jnp.where(t_r < t_c, 1.0, 0.0).astype(BF16)
    tot = jnp.dot(oh.astype(BF16), tri, preferred_element_type=F32) + cnt_ref[:, 0:1]
    r1 = jnp.sum(jnp.where(oh1, tot, 0.0), axis=0, keepdims=True)
    r2 = jnp.sum(jnp.where(oh2, tot, 0.0), axis=0, keepdims=True)
    cnt_ref[...] = cnt_ref[...] + jnp.sum(oh, axis=1, keepdims=True)

    zi = jnp.zeros((4, tm), jnp.int32)
    ri_ref[...] = jnp.concatenate([e1, e2, r1.astype(jnp.int32), r2.astype(jnp.int32), zi], axis=0)
    rw_ref[...] = jnp.concatenate([w1, w2, jnp.zeros((6, tm), F32)], axis=0)


def _route(d_out, f_out, w_out_d, w_out_f, x, mod, g_post_attn, g_pre_ffn, w_rt, b_rt):
    B, S, D = x.shape
    tm = TM_ROUTE
    nrt = w_rt.shape[0]
    n_s = S // tm
    T = B * S
    return pl.pallas_call(
        _route_kernel,
        grid=(B, n_s),
        in_specs=[pl.BlockSpec((None, tm, SEC), lambda b, s: (b, s, 0)),
                  pl.BlockSpec((None, tm, SEC), lambda b, s: (b, s, 0)),
                  pl.BlockSpec((SEC, D), lambda b, s: (0, 0)),
                  pl.BlockSpec((SEC, D), lambda b, s: (0, 0)),
                  pl.BlockSpec((None, tm, D), lambda b, s: (b, s, 0)),
                  pl.BlockSpec((None, 6, D), lambda b, s: (b, 0, 0)),
                  pl.BlockSpec((1, D), lambda b, s: (0, 0)),
                  pl.BlockSpec((1, D), lambda b, s: (0, 0)),
                  pl.BlockSpec((nrt, D), lambda b, s: (0, 0)),
                  pl.BlockSpec((nrt, LANES), lambda b, s: (0, 0))],
        out_specs=[pl.BlockSpec((None, tm, D), lambda b, s: (b, s, 0)),
                   pl.BlockSpec((None, tm, D), lambda b, s: (b, s, 0)),
                   pl.BlockSpec((8, tm), lambda b, s: (0, b * n_s + s)),
                   pl.BlockSpec((8, tm), lambda b, s: (0, b * n_s + s)),
                   pl.BlockSpec((N_EXPERTS, LANES), lambda b, s: (0, 0))],
        out_shape=[jax.ShapeDtypeStruct((B, S, D), F32),
                   jax.ShapeDtypeStruct((B, S, D), F32),
                   jax.ShapeDtypeStruct((8, T), jnp.int32),
                   jax.ShapeDtypeStruct((8, T), F32),
                   jax.ShapeDtypeStruct((N_EXPERTS, LANES), F32)],
        compiler_params=_cparams(("arbitrary", "arbitrary")),
        name="out_proj_route",
    )(d_out, f_out, w_out_d, w_out_f, x, mod, g_post_attn, g_pre_ffn, w_rt, b_rt)


def _row_copy(src, s_row, dst, d_row, sem):
    return pltpu.make_async_copy(src.at[pl.ds(s_row, 1)], dst.at[pl.ds(d_row, 1)], sem)


def _dispatch_kernel(dest_ref, h_ref, xs_in_ref, xs_ref, sem):
    del xs_in_ref
    tm = h_ref.shape[0]

    def issue(r, c):
        _row_copy(h_ref, r, xs_ref, dest_ref[0, 0, r], sem).start()
        _row_copy(h_ref, r, xs_ref, dest_ref[0, 0, tm + r], sem).start()
        return c

    lax.fori_loop(0, tm, issue, 0)

    def drain(r, c):
        _row_copy(h_ref, r, xs_ref, 0, sem).wait()
        _row_copy(h_ref, r, xs_ref, 0, sem).wait()
        return c

    lax.fori_loop(0, tm, drain, 0)


def _dispatch(dest_tiles, h2, xs_init):
    T, D = h2.shape
    tm = TM_ROWS
    return pl.pallas_call(
        _dispatch_kernel,
        grid=(T // tm,),
        in_specs=[pl.BlockSpec((1, 1, 2 * tm), lambda i: (i, 0, 0), memory_space=pltpu.SMEM),
                  pl.BlockSpec((tm, D), lambda i: (i, 0)),
                  pl.BlockSpec(memory_space=pl.ANY)],
        out_specs=pl.BlockSpec(memory_space=pl.ANY),
        out_shape=jax.ShapeDtypeStruct(xs_init.shape, xs_init.dtype),
        scratch_shapes=[pltpu.SemaphoreType.DMA],
        input_output_aliases={2: 0},
        compiler_params=_cparams(("arbitrary",)),
        name="moe_dispatch",
    )(dest_tiles, h2, xs_init)


def _expert_kernel(be_ref, nu_ref, xs_ref, wg_ref, wu_ref, wd_ref, ys_ref, wgb, wub, wdb):
    i = pl.program_id(0)

    @pl.when(i < nu_ref[0])
    def _():
        prev = be_ref[jnp.maximum(i - 1, 0)]

        @pl.when((i == 0) | (be_ref[i] != prev))
        def _():
            wgb[...] = wg_ref[...].astype(BF16)
            wub[...] = wu_ref[...].astype(BF16)
            wdb[...] = wd_ref[...].astype(BF16)

        xb = xs_ref[...].astype(BF16)
        a = jnp.dot(xb, wgb[...], preferred_element_type=F32)
        u = jnp.dot(xb, wub[...], preferred_element_type=F32)
        hmid = (a * jax.nn.sigmoid(a) * u).astype(BF16)
        ys_ref[...] = jnp.dot(hmid, wdb[...], preferred_element_type=F32)

    @pl.when(i >= nu_ref[0])
    def _():
        ys_ref[...] = jnp.zeros_like(ys_ref)


def _experts(block_e, n_used, xs, w_gate, w_up, w_down):
    P, D = xs.shape
    nb = P // BM
    Fd = w_gate.shape[2]

    def row_map(i, be, nu):
        return (jnp.minimum(i, nu[0] - 1), 0)

    def w_map(i, be, nu):
        return (be[i], 0, 0)

    return pl.pallas_call(
        _expert_kernel,
        grid_spec=pltpu.PrefetchScalarGridSpec(
            num_scalar_prefetch=2,
            grid=(nb,),
            in_specs=[pl.BlockSpec((BM, D), row_map),
                      pl.BlockSpec((None, D, Fd), w_map),
                      pl.BlockSpec((None, D, Fd), w_map),
                      pl.BlockSpec((None, Fd, D), w_map)],
            out_specs=pl.BlockSpec((BM, D), lambda i, be, nu: (i, 0)),
            scratch_shapes=[pltpu.VMEM((D, Fd), BF16), pltpu.VMEM((D, Fd), BF16),
                            pltpu.VMEM((Fd, D), BF16)]),
        out_shape=jax.ShapeDtypeStruct((P, D), F32),
        compiler_params=_cparams(("arbitrary",)),
        name="moe_experts",
    )(block_e, n_used, xs, w_gate, w_up, w_down)


def _combine_kernel(dest_ref, ys_ref, w_ref, x1_ref, mod_ref, g_ref, o_ref, buf, sem):
    tm = x1_ref.shape[0]

    def issue(r, c):
        _row_copy(ys_ref, dest_ref[0, 0, r], buf.at[0], r, sem).start()
        _row_copy(ys_ref, dest_ref[0, 0, tm + r], buf.at[1], r, sem).start()
        return c

    lax.fori_loop(0, tm, issue, 0)

    def drain(r, c):
        _row_copy(ys_ref, 0, buf.at[0], r, sem).wait()
        _row_copy(ys_ref, 0, buf.at[1], r, sem).wait()
        return c

    lax.fori_loop(0, tm, drain, 0)
    y = w_ref[:, 0:1] * buf[0] + w_ref[:, 1:2] * buf[1]
    o_ref[...] = x1_ref[...] + mod_ref[5:6, :] * _rms(y, g_ref[...])


def _combine(dest_tiles, ys, wts, x1, mod, g_post_ffn, B, S):
    T, D = x1.shape
    tm = TM_ROWS
    per_b = S // tm
    return pl.pallas_call(
        _combine_kernel,
        grid=(T // tm,),
        in_specs=[pl.BlockSpec((1, 1, 2 * tm), lambda i: (i, 0, 0), memory_space=pltpu.SMEM),
                  pl.BlockSpec(memory_space=pl.ANY),
                  pl.BlockSpec((tm, 8), lambda i: (i, 0)),
                  pl.BlockSpec((tm, D), lambda i: (i, 0)),
                  pl.BlockSpec((None, 6, D), lambda i: (i // per_b, 0, 0)),
                  pl.BlockSpec((1, D), lambda i: (0, 0))],
        out_specs=pl.BlockSpec((tm, D), lambda i: (i, 0)),
        out_shape=jax.ShapeDtypeStruct((T, D), F32),
        scratch_shapes=[pltpu.VMEM((2, tm, D), F32), pltpu.SemaphoreType.DMA],
        compiler_params=_cparams(("arbitrary",)),
        name="moe_combine",
    )(dest_tiles, ys, wts, x1, mod, g_post_ffn)


def _rope_tables(S):
    inv = 1.0 / (10000.0 ** (jnp.arange(0, HEAD_DIM, 2, dtype=F32) / HEAD_DIM))
    ang = jnp.arange(S, dtype=F32)[:, None] * inv[None, :]
    c, s = jnp.cos(ang), jnp.sin(ang)
    return jnp.tile(c, (1, 4)), jnp.tile(jnp.concatenate([-s, s], axis=1), (1, 2))


def kernel(x, c, w_ada, b_ada, g_pre_attn, g_post_attn, w_in, b_forget, lambda_q1, lambda_k1,
           lambda_q2, lambda_k2, g_diff_subln, w_out, g_pre_ffn, g_post_ffn, w_group, b_group,
           w_router, b_router, w_gate, w_up, w_down):
    B, S, D = x.shape
    T = B * S
    l = 0
    mod = _ada(c, w_ada[l], b_ada[l]).reshape(B, 6, D)

    w_main = w_in[l][:, :6 * SEC].astype(BF16)
    wf_t = w_in[l][:, 6 * SEC:].T.astype(BF16)
    bf = jnp.broadcast_to(b_forget[l][:, None], (FOX_HEADS, LANES))
    cos, sin = _rope_tables(S)
    dq, dk, dv, fq, fk, fv, fcum = _inproj(x, mod, g_pre_attn[l][None, :], w_main, wf_t, bf,
                                           cos, sin)

    lam_params = jnp.stack([lambda_q1[l], lambda_k1[l], lambda_q2[l], lambda_k2[l]])
    d_out = _diff_attention(dq, dk, dv, lam_params, g_diff_subln[l][None, :])
    frow = fcum.reshape(B, FOX_HEADS // 2, 2, S)
    fcol = jnp.transpose(frow, (0, 1, 3, 2))
    f_out = _fox_attention(fq, fk, fv, fcol, frow)

    w_o = w_out[l].astype(BF16)
    w_rt = jnp.zeros((8 + N_EXPERTS, D), F32)
    w_rt = w_rt.at[0:N_GROUPS].set(w_group[l].T).at[8:].set(w_router[l].T)
    b_rt = jnp.zeros((8 + N_EXPERTS,), F32).at[0:N_GROUPS].set(b_group[l]).at[8:].set(b_router[l])
    b_rt = jnp.broadcast_to(b_rt[:, None], (8 + N_EXPERTS, LANES))
    x1, h2, ri, rw, cnt = _route(d_out, f_out, w_o[:SEC], w_o[SEC:], x, mod,
                                 g_post_attn[l][None, :], g_pre_ffn[l][None, :], w_rt, b_rt)

    counts = cnt[:, 0].astype(jnp.int32)
    padded = ((counts + BM - 1) // BM) * BM
    pend = jnp.cumsum(padded)
    pstart = pend - padded
    dest = jnp.take(pstart, ri[0:2]) + ri[2:4]
    nb = (2 * T) // BM + N_EXPERTS
    n_used = (pend[-1] // BM).astype(jnp.int32).reshape(1)
    block_e = jnp.minimum(jnp.searchsorted(pend, jnp.arange(nb, dtype=jnp.int32) * BM,
                                           side='right'), N_EXPERTS - 1).astype(jnp.int32)
    tm = TM_ROWS
    dest_tiles = dest.reshape(2, T // tm, tm).transpose(1, 0, 2).reshape(T // tm, 1, 2 * tm)

    xs = _dispatch(dest_tiles, h2.reshape(T, D), jnp.zeros((nb * BM, D), F32))
    ys = _experts(block_e, n_used, xs, w_gate[l], w_up[l], w_down[l])
    out = _combine(dest_tiles, ys, rw.T, x1.reshape(T, D), mod, g_post_ffn[l][None, :], B, S)
    return out.reshape(B, S, D)
```

```python
import math

import jax
import jax.numpy as jnp
from jax import lax
from jax.experimental import pallas as pl
from jax.experimental.pallas import tpu as pltpu

F32 = jnp.float32
BF16 = jnp.bfloat16

D_MODEL = 1024
HEAD_DIM = 64
CHUNK = 64
DIFF_HEADS = 4
FOX_HEADS = 8
SEC = 512
N_GROUPS = 4
EXPERTS_PER_GROUP = 8
N_EXPERTS = 32
D_EXPERT = 512
EPS = 1e-6
NEG_INF = -1e30
LOG2E = 1.4426950408889634
LAMBDA_INIT = 0.8 - 0.6 * math.exp(-0.3 * 0)

LANES = 128
TM_PROJ = 512
TQ = 512
TK = 512
TM_ROUTE = 512
BM = 256
TM_ROWS = 256
VMEM_LIMIT = 48 * 1024 * 1024
_NT = (((1,), (1,)), ((), ()))


def _cparams(sem):
    return pltpu.CompilerParams(dimension_semantics=sem, vmem_limit_bytes=VMEM_LIMIT)


def _rms(x, g):
    return x * lax.rsqrt(jnp.mean(x * x, axis=-1, keepdims=True) + EPS) * g


def _ada_kernel(c_ref, w_ref, b_ref, o_ref):
    c = c_ref[...]
    cs = c * jax.nn.sigmoid(c)
    o_ref[...] = jnp.dot(cs, w_ref[...], preferred_element_type=F32,
                         precision=lax.Precision.HIGHEST) + b_ref[...]


def _ada(c, w_ada, b_ada):
    B, D = c.shape
    N = w_ada.shape[1]
    tn = 1024
    return pl.pallas_call(
        _ada_kernel,
        grid=(N // tn,),
        in_specs=[pl.BlockSpec((B, D), lambda j: (0, 0)),
                  pl.BlockSpec((D, tn), lambda j: (0, j)),
                  pl.BlockSpec((1, tn), lambda j: (0, j))],
        out_specs=pl.BlockSpec((B, tn), lambda j: (0, j)),
        out_shape=jax.ShapeDtypeStruct((B, N), F32),
        compiler_params=_cparams(("arbitrary",)),
        name="ada_mod",
    )(c, w_ada, b_ada.reshape(1, N))


def _split3(x):
    hi = x.astype(BF16)
    r1 = x - hi.astype(F32)
    mid = r1.astype(BF16)
    lo = (r1 - mid.astype(F32)).astype(BF16)
    return hi, mid, lo


def _inproj_kernel(x_ref, mod_ref, g_ref, w_ref, wvt_ref, wf_ref, bf_ref, cos_ref, sin_ref,
                   dq_ref, dk_ref, fq_ref, fk_ref, dvt_ref, fvt_ref, fcum_ref, carry_ref):
    si = pl.program_id(1)
    x = x_ref[...]
    h = _rms(x, g_ref[...]) * (1.0 + mod_ref[1:2, :]) + mod_ref[0:1, :]
    hb = h.astype(BF16)
    tm = x.shape[0]

    cos = cos_ref[...]
    sin = sin_ref[...]
    lane = lax.broadcasted_iota(jnp.int32, (tm, LANES), 1)
    first_half = (lane % HEAD_DIM) < (HEAD_DIM // 2)
    qscale = HEAD_DIM ** -0.5 * LOG2E

    def proj(sec):
        return jnp.dot(hb, w_ref[:, sec * SEC:(sec + 1) * SEC], preferred_element_type=F32)

    def rope_store(p, out_ref, scale):
        for j in range(SEC // LANES):
            ch = p[:, j * LANES:(j + 1) * LANES]
            partner = jnp.where(first_half,
                                pltpu.roll(ch, LANES - HEAD_DIM // 2, 1),
                                pltpu.roll(ch, HEAD_DIM // 2, 1))
            r = ch * cos + partner * sin
            if scale != 1.0:
                r = r * scale
            out_ref[:, j * LANES:(j + 1) * LANES] = r.astype(out_ref.dtype)

    rope_store(proj(0), dq_ref, qscale)
    rope_store(proj(1), dk_ref, 1.0)
    fq_ref[...] = (proj(2) * qscale).astype(fq_ref.dtype)
    fk_ref[...] = proj(3).astype(fk_ref.dtype)
    vt = lax.dot_general(wvt_ref[...], hb, _NT, preferred_element_type=F32)
    dvt_ref[...] = vt[0:SEC].astype(dvt_ref.dtype)
    fvt_ref[...] = vt[SEC:2 * SEC].astype(fvt_ref.dtype)

    z = lax.dot_general(wf_ref[...], hb, _NT, preferred_element_type=F32) + bf_ref[:, 0:1]
    lf = (jnp.minimum(z, 0.0) - jnp.log1p(jnp.exp(-jnp.abs(z)))) * LOG2E

    @pl.when(si == 0)
    def _():
        carry_ref[...] = jnp.zeros_like(carry_ref)

    r_i = lax.broadcasted_iota(jnp.int32, (LANES, LANES), 0)
    c_i = lax.broadcasted_iota(jnp.int32, (LANES, LANES), 1)
    tri = jnp.where(r_i <= c_i, 1.0, 0.0).astype(BF16)
    run = carry_ref[...]
    for j in range(tm // LANES):
        hi, mid, lo = _split3(lf[:, j * LANES:(j + 1) * LANES])
        cs = (jnp.dot(hi, tri, preferred_element_type=F32)
              + jnp.dot(mid, tri, preferred_element_type=F32)
              + jnp.dot(lo, tri, preferred_element_type=F32)) + run
        fcum_ref[:, j * LANES:(j + 1) * LANES] = cs
        run = jnp.broadcast_to(cs[:, LANES - 1:LANES], run.shape)
    carry_ref[...] = run


def _inproj(x, mod, g_pre, w_main, w_vt, wf_t, bf, cos, sin):
    B, S, D = x.shape
    tm = TM_PROJ
    sec_out = jax.ShapeDtypeStruct((B, S, SEC), BF16)
    sec_spec = pl.BlockSpec((None, tm, SEC), lambda b, s: (b, s, 0))
    t_out = jax.ShapeDtypeStruct((B, SEC, S), BF16)
    t_spec = pl.BlockSpec((None, SEC, tm), lambda b, s: (b, 0, s))
    return pl.pallas_call(
        _inproj_kernel,
        grid=(B, S // tm),
        in_specs=[pl.BlockSpec((None, tm, D), lambda b, s: (b, s, 0)),
                  pl.BlockSpec((None, 6, D), lambda b, s: (b, 0, 0)),
                  pl.BlockSpec((1, D), lambda b, s: (0, 0)),
                  pl.BlockSpec((D, 4 * SEC), lambda b, s: (0, 0)),
                  pl.BlockSpec((2 * SEC, D), lambda b, s: (0, 0)),
                  pl.BlockSpec((FOX_HEADS, D), lambda b, s: (0, 0)),
                  pl.BlockSpec((FOX_HEADS, LANES), lambda b, s: (0, 0)),
                  pl.BlockSpec((tm, LANES), lambda b, s: (s, 0)),
                  pl.BlockSpec((tm, LANES), lambda b, s: (s, 0))],
        out_specs=[sec_spec] * 4 + [t_spec] * 2
        + [pl.BlockSpec((None, FOX_HEADS, tm), lambda b, s: (b, 0, s))],
        out_shape=[sec_out] * 4 + [t_out] * 2 + [jax.ShapeDtypeStruct((B, FOX_HEADS, S), F32)],
        scratch_shapes=[pltpu.VMEM((FOX_HEADS, LANES), F32)],
        compiler_params=_cparams(("arbitrary", "arbitrary")),
        name="in_proj",
    )(x, mod, g_pre, w_main, w_vt, wf_t, bf, cos, sin)


def _flash_t(chains, k_ref, qi, diag_mask):
    def step(k0, carry, mask):
        kb = k_ref[pl.ds(k0, TK), :]
        out = []
        for ch, (m, l, acc) in zip(chains, carry):
            t = lax.dot_general(kb, ch["qm"], _NT, preferred_element_type=F32)
            if ch["kbias"] is not None:
                t = t - ch["kbias"](k0)
            if mask is not None:
                t = jnp.where(mask, t, NEG_INF)
            mt = jnp.max(t, axis=0, keepdims=True)
            qb = ch["qbias"]
            m_new = jnp.maximum(m, mt if qb is None else mt + qb)
            alpha = jnp.exp2(m - m_new)
            p = jnp.exp2(t - (m_new if qb is None else m_new - qb))
            if ch["sum_rows"]:
                l = alpha * l + jnp.sum(p, axis=0, keepdims=True)
            acc = alpha * acc + jnp.dot(ch["vt"](k0), p.astype(BF16), preferred_element_type=F32)
            out.append((m_new, l, acc))
        return tuple(out)

    init = tuple((jnp.full((1, TQ), NEG_INF, F32), jnp.zeros((1, TQ), F32),
                  jnp.zeros((ch["rows"], TQ), F32)) for ch in chains)
    carry = lax.fori_loop(0, qi, lambda ki, c: step(pl.multiple_of(ki * TK, TK), c, None), init)
    return step(pl.multiple_of(qi * TK, TK), carry, diag_mask)


def _fox_kernel(q_ref, k_ref, vt_ref, fcol_ref, frow_ref, o_ref):
    qi = pl.program_id(2)
    q2 = q_ref[...]
    lane = lax.broadcasted_iota(jnp.int32, q2.shape, 1)
    vrow = lax.broadcasted_iota(jnp.int32, (LANES, TK), 0)
    r = lax.broadcasted_iota(jnp.int32, (TK, TQ), 0)
    c = lax.broadcasted_iota(jnp.int32, (TK, TQ), 1)
    chains = []
    for j in range(2):
        in_head = (lane >= j * HEAD_DIM) & (lane < (j + 1) * HEAD_DIM)
        v_rows = (vrow >= j * HEAD_DIM) & (vrow < (j + 1) * HEAD_DIM)

        def vt(k0, v_rows=v_rows):
            v = vt_ref[:, pl.ds(k0, TK)]
            return jnp.where(v_rows, v, jnp.ones_like(v))

        chains.append(dict(
            qm=jnp.where(in_head, q2, jnp.zeros_like(q2)),
            kbias=lambda k0, j=j: fcol_ref[pl.ds(k0, TK), j:j + 1],
            qbias=frow_ref[j:j + 1, :],
            vt=vt, rows=LANES, sum_rows=False))
    (_, _, a0), (_, _, a1) = _flash_t(chains, k_ref, qi, r <= c)
    h = HEAD_DIM
    o_t = jnp.concatenate([a0[0:h] / a0[h:2 * h], a1[h:2 * h] / a1[0:h]], axis=0)
    o_ref[...] = o_t.T.astype(o_ref.dtype)


def _fox_attention(fq, fk, fvt, fcol, frow):
    B, S, _ = fq.shape
    npair = FOX_HEADS // 2
    return pl.pallas_call(
        _fox_kernel,
        grid=(B, npair, S // TQ),
        in_specs=[pl.BlockSpec((None, TQ, LANES), lambda b, p, q: (b, q, p)),
                  pl.BlockSpec((None, S, LANES), lambda b, p, q: (b, 0, p)),
                  pl.BlockSpec((None, LANES, S), lambda b, p, q: (b, p, 0)),
                  pl.BlockSpec((None, None, S, 2), lambda b, p, q: (b, p, 0, 0)),
                  pl.BlockSpec((None, None, 2, TQ), lambda b, p, q: (b, p, 0, q))],
        out_specs=pl.BlockSpec((None, TQ, LANES), lambda b, p, q: (b, q, p)),
        out_shape=jax.ShapeDtypeStruct((B, S, SEC), BF16),
        compiler_params=_cparams(("arbitrary", "arbitrary", "arbitrary")),
        name="fox_attention",
    )(fq, fk, fvt, fcol, frow)


def _diff_kernel(q_ref, k_ref, vt_ref, lam_ref, g_ref, o_ref):
    qi = pl.program_id(2)
    q2 = q_ref[...]
    lane = lax.broadcasted_iota(jnp.int32, q2.shape, 1)
    r = lax.broadcasted_iota(jnp.int32, (TK, TQ), 0)
    c = lax.broadcasted_iota(jnp.int32, (TK, TQ), 1)
    chains = []
    for j in range(2):
        in_map = (lane >= j * HEAD_DIM) & (lane < (j + 1) * HEAD_DIM)
        chains.append(dict(
            qm=jnp.where(in_map, q2, jnp.zeros_like(q2)),
            kbias=None, qbias=None,
            vt=lambda k0: vt_ref[:, pl.ds(k0, TK)], rows=LANES, sum_rows=True))
    (_, l0, a0), (_, l1, a1) = _flash_t(chains, k_ref, qi, (r // CHUNK) <= (c // CHUNK))
    lp = lam_ref[...]
    lam = (jnp.exp(jnp.sum(lp[0:1] * lp[1:2], axis=1, keepdims=True))
           - jnp.exp(jnp.sum(lp[2:3] * lp[3:4], axis=1, keepdims=True)) + LAMBDA_INIT)
    o = (a0 / l0 - lam * (a1 / l1)).T
    o_ref[...] = (_rms(o, g_ref[...]) * (1.0 - LAMBDA_INIT)).astype(o_ref.dtype)


def _diff_attention(dq, dk, dvt, lam_params, g_subln):
    B, S, _ = dq.shape
    return pl.pallas_call(
        _diff_kernel,
        grid=(B, DIFF_HEADS, S // TQ),
        in_specs=[pl.BlockSpec((None, TQ, LANES), lambda b, h, q: (b, q, h)),
                  pl.BlockSpec((None, S, LANES), lambda b, h, q: (b, 0, h)),
                  pl.BlockSpec((None, LANES, S), lambda b, h, q: (b, h, 0)),
                  pl.BlockSpec((4, HEAD_DIM), lambda b, h, q: (0, 0)),
                  pl.BlockSpec((1, LANES), lambda b, h, q: (0, 0))],
        out_specs=pl.BlockSpec((None, TQ, LANES), lambda b, h, q: (b, q, h)),
        out_shape=jax.ShapeDtypeStruct((B, S, SEC), BF16),
        compiler_params=_cparams(("arbitrary", "arbitrary", "arbitrary")),
        name="diff_attention",
    )(dq, dk, dvt, lam_params, g_subln)


def _route_kernel(d_ref, f_ref, wd_ref, wf_ref, x_ref, mod_ref, gpa_ref, gpf_ref, wr_ref, br_ref,
                  x1_ref, h2_ref, ri_ref, rw_ref, cnt_ref):
    first = (pl.program_id(0) == 0) & (pl.program_id(1) == 0)

    @pl.when(first)
    def _():
        cnt_ref[...] = jnp.zeros_like(cnt_ref)

    mixed = (jnp.dot(d_ref[...], wd_ref[...], preferred_element_type=F32)
             + jnp.dot(f_ref[...], wf_ref[...], preferred_element_type=F32))
    x1 = x_ref[...] + mod_ref[2:3, :] * _rms(mixed, gpa_ref[...])
    x1_ref[...] = x1
    h2 = _rms(x1, gpf_ref[...]) * (1.0 + mod_ref[4:5, :]) + mod_ref[3:4, :]
    h2_ref[...] = h2
    tm = h2.shape[0]

    hh, hm, hl = _split3(h2)
    wh, wm, wl = _split3(wr_ref[...])
    lt = br_ref[:, 0:1]
    for a, b_ in ((wh, hh), (wh, hm), (wm, hh), (wm, hm), (wh, hl), (wl, hh)):
        lt = lt + lax.dot_general(a, b_, _NT, preferred_element_type=F32)

    g = lt[0:N_GROUPS]
    row4 = lax.broadcasted_iota(jnp.int32, g.shape, 0)
    gmax = jnp.max(g, axis=0, keepdims=True)
    gidx = jnp.min(jnp.where(g == gmax, row4, N_GROUPS), axis=0, keepdims=True)
    gw = 1.0 / jnp.sum(jnp.exp(g - gmax), axis=0, keepdims=True)
    esel = jnp.zeros((EXPERTS_PER_GROUP, tm), F32)
    for gg in range(N_GROUPS):
        esel = jnp.where(gidx == gg, lt[8 + 8 * gg:16 + 8 * gg], esel)
    row8 = lax.broadcasted_iota(jnp.int32, esel.shape, 0)
    v1 = jnp.max(esel, axis=0, keepdims=True)
    i1 = jnp.min(jnp.where(esel == v1, row8, EXPERTS_PER_GROUP), axis=0, keepdims=True)
    esel2 = jnp.where(row8 == i1, -jnp.inf, esel)
    v2 = jnp.max(esel2, axis=0, keepdims=True)
    i2 = jnp.min(jnp.where(esel2 == v2, row8, EXPERTS_PER_GROUP), axis=0, keepdims=True)
    e = jnp.exp(v2 - v1)
    w1 = gw / (1.0 + e)
    w2 = gw * e / (1.0 + e)
    e1 = gidx * EXPERTS_PER_GROUP + i1
    e2 = gidx * EXPERTS_PER_GROUP + i2

    row32 = lax.broadcasted_iota(jnp.int32, (N_EXPERTS, tm), 0)
    oh1 = row32 == e1
    oh2 = row32 == e2
    oh = jnp.where(oh1 | oh2, 1.0, 0.0)
    t_r = lax.broadcasted_iota(jnp.int32, (tm, tm), 0)
    t_c = lax.broadcasted_iota(jnp.int32, (tm, tm), 1)
    tri = jnp.where(t_r < t_c, 1.0, 0.0).astype(BF16)
    tot = jnp.dot(oh.astype(BF16), tri, preferred_element_type=F32) + cnt_ref[:, 0:1]
    r1 = jnp.sum(jnp.where(oh1, tot, 0.0), axis=0, keepdims=True)
    r2 = jnp.sum(jnp.where(oh2, tot, 0.0), axis=0, keepdims=True)
    cnt_ref[...] = cnt_ref[...] + jnp.sum(oh, axis=1, keepdims=True)

    zi = jnp.zeros((4, tm), jnp.int32)
    ri_ref[...] = jnp.concatenate([e1, e2, r1.astype(jnp.int32), r2.astype(jnp.int32), zi], axis=0)
    rw_ref[...] = jnp.concatenate([w1, w2, jnp.zeros((6, tm), F32)], axis=0)


def _route(d_out, f_out, w_out_d, w_out_f, x, mod, g_post_attn, g_pre_ffn, w_rt, b_rt):
    B, S, D = x.shape
    tm = TM_ROUTE
    nrt = w_rt.shape[0]
    n_s = S // tm
    T = B * S
    return pl.pallas_call(
        _route_kernel,
        grid=(B, n_s),
        in_specs=[pl.BlockSpec((None, tm, SEC), lambda b, s: (b, s, 0)),
                  pl.BlockSpec((None, tm, SEC), lambda b, s: (b, s, 0)),
                  pl.BlockSpec((SEC, D), lambda b, s: (0, 0)),
                  pl.BlockSpec((SEC, D), lambda b, s: (0, 0)),
                  pl.BlockSpec((None, tm, D), lambda b, s: (b, s, 0)),
                  pl.BlockSpec((None, 6, D), lambda b, s: (b, 0, 0)),
                  pl.BlockSpec((1, D), lambda b, s: (0, 0)),
                  pl.BlockSpec((1, D), lambda b, s: (0, 0)),
                  pl.BlockSpec((nrt, D), lambda b, s: (0, 0)),
                  pl.BlockSpec((nrt, LANES), lambda b, s: (0, 0))],
        out_specs=[pl.BlockSpec((None, tm, D), lambda b, s: (b, s, 0)),
                   pl.BlockSpec((None, tm, D), lambda b, s: (b, s, 0)),
                   pl.BlockSpec((8, tm), lambda b, s: (0, b * n_s + s)),
                   pl.BlockSpec((8, tm), lambda b, s: (0, b * n_s + s)),
                   pl.BlockSpec((N_EXPERTS, LANES), lambda b, s: (0, 0))],
        out_shape=[jax.ShapeDtypeStruct((B, S, D), F32),
                   jax.ShapeDtypeStruct((B, S, D), F32),
                   jax.ShapeDtypeStruct((8, T), jnp.int32),
                   jax.ShapeDtypeStruct((8, T), F32),
                   jax.ShapeDtypeStruct((N_EXPERTS, LANES), F32)],
        compiler_params=_cparams(("arbitrary", "arbitrary")),
        name="out_proj_route",
    )(d_out, f_out, w_out_d, w_out_f, x, mod, g_post_attn, g_pre_ffn, w_rt, b_rt)


def _row_copy(src, s_row, dst, d_row, sem):
    return pltpu.make_async_copy(src.at[pl.ds(s_row, 1)], dst.at[pl.ds(d_row, 1)], sem)


def _dispatch_kernel(dest_ref, h_ref, xs_in_ref, xs_ref, sem):
    del xs_in_ref
    tm = h_ref.shape[0]

    def issue(r, c):
        _row_copy(h_ref, r, xs_ref, dest_ref[0, 0, r], sem).start()
        _row_copy(h_ref, r, xs_ref, dest_ref[0, 0, tm + r], sem).start()
        return c

    lax.fori_loop(0, tm, issue, 0)

    def drain(r, c):
        _row_copy(h_ref, r, xs_ref, 0, sem).wait()
        _row_copy(h_ref, r, xs_ref, 0, sem).wait()
        return c

    lax.fori_loop(0, tm, drain, 0)


def _dispatch(dest_tiles, h2, xs_init):
    T, D = h2.shape
    tm = TM_ROWS
    return pl.pallas_call(
        _dispatch_kernel,
        grid=(T // tm,),
        in_specs=[pl.BlockSpec((1, 1, 2 * tm), lambda i: (i, 0, 0), memory_space=pltpu.SMEM),
                  pl.BlockSpec((tm, D), lambda i: (i, 0)),
                  pl.BlockSpec(memory_space=pl.ANY)],
        out_specs=pl.BlockSpec(memory_space=pl.ANY),
        out_shape=jax.ShapeDtypeStruct(xs_init.shape, xs_init.dtype),
        scratch_shapes=[pltpu.SemaphoreType.DMA],
        input_output_aliases={2: 0},
        compiler_params=_cparams(("arbitrary",)),
        name="moe_dispatch",
    )(dest_tiles, h2, xs_init)


def _expert_kernel(be_ref, nu_ref, xs_ref, wg_ref, wu_ref, wd_ref, ys_ref, wgb, wub, wdb):
    i = pl.program_id(0)

    @pl.when(i < nu_ref[0])
    def _():
        prev = be_ref[jnp.maximum(i - 1, 0)]

        @pl.when((i == 0) | (be_ref[i] != prev))
        def _():
            wgb[...] = wg_ref[...].astype(BF16)
            wub[...] = wu_ref[...].astype(BF16)
            wdb[...] = wd_ref[...].astype(BF16)

        xb = xs_ref[...].astype(BF16)
        a = jnp.dot(xb, wgb[...], preferred_element_type=F32)
        u = jnp.dot(xb, wub[...], preferred_element_type=F32)
        hmid = (a * jax.nn.sigmoid(a) * u).astype(BF16)
        ys_ref[...] = jnp.dot(hmid, wdb[...], preferred_element_type=F32)

    @pl.when(i >= nu_ref[0])
    def _():
        ys_ref[...] = jnp.zeros_like(ys_ref)


def _experts(block_e, n_used, xs, w_gate, w_up, w_down):
    P, D = xs.shape
    nb = P // BM
    Fd = w_gate.shape[2]

    def row_map(i, be, nu):
        return (jnp.minimum(i, nu[0] - 1), 0)

    def w_map(i, be, nu):
        return (be[i], 0, 0)

    return pl.pallas_call(
        _expert_kernel,
        grid_spec=pltpu.PrefetchScalarGridSpec(
            num_scalar_prefetch=2,
            grid=(nb,),
            in_specs=[pl.BlockSpec((BM, D), row_map),
                      pl.BlockSpec((None, D, Fd), w_map),
                      pl.BlockSpec((None, D, Fd), w_map),
                      pl.BlockSpec((None, Fd, D), w_map)],
            out_specs=pl.BlockSpec((BM, D), lambda i, be, nu: (i, 0)),
            scratch_shapes=[pltpu.VMEM((D, Fd), BF16), pltpu.VMEM((D, Fd), BF16),
                            pltpu.VMEM((Fd, D), BF16)]),
        out_shape=jax.ShapeDtypeStruct((P, D), F32),
        compiler_params=_cparams(("arbitrary",)),
        name="moe_experts",
    )(block_e, n_used, xs, w_gate, w_up, w_down)


def _combine_kernel(dest_ref, ys_ref, w_ref, x1_ref, mod_ref, g_ref, o_ref, buf, sem):
    tm = x1_ref.shape[0]

    def issue(r, c):
        _row_copy(ys_ref, dest_ref[0, 0, r], buf.at[0], r, sem).start()
        _row_copy(ys_ref, dest_ref[0, 0, tm + r], buf.at[1], r, sem).start()
        return c

    lax.fori_loop(0, tm, issue, 0)

    def drain(r, c):
        _row_copy(ys_ref, 0, buf.at[0], r, sem).wait()
        _row_copy(ys_ref, 0, buf.at[1], r, sem).wait()
        return c

    lax.fori_loop(0, tm, drain, 0)
    y = w_ref[:, 0:1] * buf[0] + w_ref[:, 1:2] * buf[1]
    o_ref[...] = x1_ref[...] + mod_ref[5:6, :] * _rms(y, g_ref[...])


def _combine(dest_tiles, ys, wts, x1, mod, g_post_ffn, B, S):
    T, D = x1.shape
    tm = TM_ROWS
    per_b = S // tm
    return pl.pallas_call(
        _combine_kernel,
        grid=(T // tm,),
        in_specs=[pl.BlockSpec((1, 1, 2 * tm), lambda i: (i, 0, 0), memory_space=pltpu.SMEM),
                  pl.BlockSpec(memory_space=pl.ANY),
                  pl.BlockSpec((tm, 8), lambda i: (i, 0)),
                  pl.BlockSpec((tm, D), lambda i: (i, 0)),
                  pl.BlockSpec((None, 6, D), lambda i: (i // per_b, 0, 0)),
                  pl.BlockSpec((1, D), lambda i: (0, 0))],
        out_specs=pl.BlockSpec((tm, D), lambda i: (i, 0)),
        out_shape=jax.ShapeDtypeStruct((T, D), F32),
        scratch_shapes=[pltpu.VMEM((2, tm, D), F32), pltpu.SemaphoreType.DMA],
        compiler_params=_cparams(("arbitrary",)),
        name="moe_combine",
    )(dest_tiles, ys, wts, x1, mod, g_post_ffn)


def _rope_tables(S):
    inv = 1.0 / (10000.0 ** (jnp.arange(0, HEAD_DIM, 2, dtype=F32) / HEAD_DIM))
    ang = jnp.arange(S, dtype=F32)[:, None] * inv[None, :]
    c, s = jnp.cos(ang), jnp.sin(ang)
    return jnp.tile(c, (1, 4)), jnp.tile(jnp.concatenate([-s, s], axis=1), (1, 2))


def kernel(x, c, w_ada, b_ada, g_pre_attn, g_post_attn, w_in, b_forget, lambda_q1, lambda_k1,
           lambda_q2, lambda_k2, g_diff_subln, w_out, g_pre_ffn, g_post_ffn, w_group, b_group,
           w_router, b_router, w_gate, w_up, w_down):
    B, S, D = x.shape
    T = B * S
    l = 0
    mod = _ada(c, w_ada[l], b_ada[l]).reshape(B, 6, D)

    wi = w_in[l]
    w_main = jnp.concatenate([wi[:, 0:2 * SEC], wi[:, 3 * SEC:5 * SEC]], axis=1).astype(BF16)
    w_vt = jnp.concatenate([wi[:, 2 * SEC:3 * SEC], wi[:, 5 * SEC:6 * SEC]], axis=1).T.astype(BF16)
    wf_t = wi[:, 6 * SEC:].T.astype(BF16)
    bf = jnp.broadcast_to(b_forget[l][:, None], (FOX_HEADS, LANES))
    cos, sin = _rope_tables(S)
    dq, dk, fq, fk, dvt, fvt, fcum = _inproj(x, mod, g_pre_attn[l][None, :], w_main, w_vt, wf_t,
                                             bf, cos, sin)

    lam_params = jnp.stack([lambda_q1[l], lambda_k1[l], lambda_q2[l], lambda_k2[l]])
    d_out = _diff_attention(dq, dk, dvt, lam_params, g_diff_subln[l][None, :])
    frow = fcum.reshape(B, FOX_HEADS // 2, 2, S)
    fcol = jnp.transpose(frow, (0, 1, 3, 2))
    f_out = _fox_attention(fq, fk, fvt, fcol, frow)

    w_o = w_out[l].astype(BF16)
    zpad = jnp.zeros((8 - N_GROUPS, D), F32)
    w_rt = jnp.concatenate([w_group[l].T, zpad, w_router[l].T], axis=0)
    b_rt = jnp.concatenate([b_group[l], jnp.zeros((8 - N_GROUPS,), F32), b_router[l]])
    b_rt = jnp.broadcast_to(b_rt[:, None], (8 + N_EXPERTS, LANES))
    x1, h2, ri, rw, cnt = _route(d_out, f_out, w_o[:SEC], w_o[SEC:], x, mod,
                                 g_post_attn[l][None, :], g_pre_ffn[l][None, :], w_rt, b_rt)

    counts = cnt[:, 0].astype(jnp.int32)
    padded = ((counts + BM - 1) // BM) * BM
    pend = jnp.cumsum(padded)
    pstart = pend - padded
    eids = jnp.arange(N_EXPERTS, dtype=jnp.int32)
    dest = ri[2:4] + jnp.sum(jnp.where(ri[0:2, :, None] == eids, pstart, 0), axis=-1)
    nb = (2 * T) // BM + N_EXPERTS
    n_used = (pend[-1] // BM).astype(jnp.int32).reshape(1)
    blk0 = jnp.arange(nb, dtype=jnp.int32) * BM
    block_e = jnp.minimum(jnp.sum((blk0[:, None] >= pend[None, :]).astype(jnp.int32), axis=1),
                          N_EXPERTS - 1)
    tm = TM_ROWS
    dest_tiles = dest.reshape(2, T // tm, tm).transpose(1, 0, 2).reshape(T // tm, 1, 2 * tm)

    xs = _dispatch(dest_tiles, h2.reshape(T, D), jnp.zeros((nb * BM, D), F32))
    ys = _experts(block_e, n_used, xs, w_gate[l], w_up[l], w_down[l])
    out = _combine(dest_tiles, ys, rw.T, x1.reshape(T, D), mod, g_post_ffn[l][None, :], B, S)
    return out.reshape(B, S, D)
```

```python
import math

import jax
import jax.numpy as jnp
from jax import lax
from jax.experimental import pallas as pl
from jax.experimental.pallas import tpu as pltpu

F32 = jnp.float32
BF16 = jnp.bfloat16

D_MODEL = 1024
HEAD_DIM = 64
CHUNK = 64
DIFF_HEADS = 4
FOX_HEADS = 8
SEC = 512
N_GROUPS = 4
EXPERTS_PER_GROUP = 8
N_EXPERTS = 32
D_EXPERT = 512
EPS = 1e-6
NEG_INF = -1e30
LOG2E = 1.4426950408889634
LAMBDA_INIT = 0.8 - 0.6 * math.exp(-0.3 * 0)

LANES = 128
TM_PROJ = 512
TQ = 512
TK = 512
HK = TK // 2
TM_ROUTE = 512
BM = 256
TM_ROWS = 256
VMEM_LIMIT = 48 * 1024 * 1024
_NT = (((1,), (1,)), ((), ()))


def _cparams(sem, flags=None):
    return pltpu.CompilerParams(dimension_semantics=sem, vmem_limit_bytes=VMEM_LIMIT, flags=flags)


def _rms(x, g):
    return x * lax.rsqrt(jnp.mean(x * x, axis=-1, keepdims=True) + EPS) * g


def _ada_kernel(c_ref, w_ref, b_ref, o_ref):
    c = c_ref[...]
    cs = c * jax.nn.sigmoid(c)
    o_ref[...] = jnp.dot(cs, w_ref[...], preferred_element_type=F32,
                         precision=lax.Precision.HIGHEST) + b_ref[...]


def _ada(c, w_ada, b_ada):
    B, D = c.shape
    N = w_ada.shape[1]
    tn = 1024
    return pl.pallas_call(
        _ada_kernel,
        grid=(N // tn,),
        in_specs=[pl.BlockSpec((B, D), lambda j: (0, 0)),
                  pl.BlockSpec((D, tn), lambda j: (0, j)),
                  pl.BlockSpec((1, tn), lambda j: (0, j))],
        out_specs=pl.BlockSpec((B, tn), lambda j: (0, j)),
        out_shape=jax.ShapeDtypeStruct((B, N), F32),
        compiler_params=_cparams(("arbitrary",)),
        name="ada_mod",
    )(c, w_ada, b_ada.reshape(1, N))


def _split3(x):
    hi = x.astype(BF16)
    r1 = x - hi.astype(F32)
    mid = r1.astype(BF16)
    lo = (r1 - mid.astype(F32)).astype(BF16)
    return hi, mid, lo


def _inproj_kernel(x_ref, mod_ref, g_ref, w_ref, wvt_ref, wf_ref, bf_ref, cos_ref, sin_ref,
                   dq_ref, dk_ref, fq_ref, fk_ref, dvt_ref, fvt_ref, fcum_ref, carry_ref):
    si = pl.program_id(1)
    x = x_ref[...]
    h = _rms(x, g_ref[...]) * (1.0 + mod_ref[1:2, :]) + mod_ref[0:1, :]
    hb = h.astype(BF16)
    tm = x.shape[0]

    cos = cos_ref[...]
    sin = sin_ref[...]
    lane = lax.broadcasted_iota(jnp.int32, (tm, LANES), 1)
    first_half = (lane % HEAD_DIM) < (HEAD_DIM // 2)
    qscale = HEAD_DIM ** -0.5 * LOG2E

    def proj(sec):
        return jnp.dot(hb, w_ref[:, sec * SEC:(sec + 1) * SEC], preferred_element_type=F32)

    def rope_store(p, out_ref, scale):
        for j in range(SEC // LANES):
            ch = p[:, j * LANES:(j + 1) * LANES]
            partner = jnp.where(first_half,
                                pltpu.roll(ch, LANES - HEAD_DIM // 2, 1),
                                pltpu.roll(ch, HEAD_DIM // 2, 1))
            r = ch * cos + partner * sin
            if scale != 1.0:
                r = r * scale
            out_ref[:, j * LANES:(j + 1) * LANES] = r.astype(out_ref.dtype)

    rope_store(proj(0), dq_ref, qscale)
    rope_store(proj(1), dk_ref, 1.0)
    fq_ref[...] = (proj(2) * qscale).astype(fq_ref.dtype)
    fk_ref[...] = proj(3).astype(fk_ref.dtype)
    vt = lax.dot_general(wvt_ref[...], hb, _NT, preferred_element_type=F32)
    dvt_ref[...] = vt[0:SEC].astype(dvt_ref.dtype)
    fvt_ref[...] = vt[SEC:2 * SEC].astype(fvt_ref.dtype)

    z = lax.dot_general(wf_ref[...], hb, _NT, preferred_element_type=F32) + bf_ref[:, 0:1]
    lf = (jnp.minimum(z, 0.0) - jnp.log1p(jnp.exp(-jnp.abs(z)))) * LOG2E

    @pl.when(si == 0)
    def _():
        carry_ref[...] = jnp.zeros_like(carry_ref)

    r_i = lax.broadcasted_iota(jnp.int32, (LANES, LANES), 0)
    c_i = lax.broadcasted_iota(jnp.int32, (LANES, LANES), 1)
    tri = jnp.where(r_i <= c_i, 1.0, 0.0).astype(BF16)
    run = carry_ref[...]
    for j in range(tm // LANES):
        hi, mid, lo = _split3(lf[:, j * LANES:(j + 1) * LANES])
        cs = (jnp.dot(hi, tri, preferred_element_type=F32)
              + jnp.dot(mid, tri, preferred_element_type=F32)
              + jnp.dot(lo, tri, preferred_element_type=F32)) + run
        fcum_ref[:, j * LANES:(j + 1) * LANES] = cs
        run = jnp.broadcast_to(cs[:, LANES - 1:LANES], run.shape)
    carry_ref[...] = run


def _inproj(x, mod, g_pre, w_main, w_vt, wf_t, bf, cos, sin):
    B, S, D = x.shape
    tm = TM_PROJ
    sec_out = jax.ShapeDtypeStruct((B, S, SEC), BF16)
    sec_spec = pl.BlockSpec((None, tm, SEC), lambda b, s: (b, s, 0))
    t_out = jax.ShapeDtypeStruct((B, SEC, S), BF16)
    t_spec = pl.BlockSpec((None, SEC, tm), lambda b, s: (b, 0, s))
    return pl.pallas_call(
        _inproj_kernel,
        grid=(B, S // tm),
        in_specs=[pl.BlockSpec((None, tm, D), lambda b, s: (b, s, 0)),
                  pl.BlockSpec((None, 6, D), lambda b, s: (b, 0, 0)),
                  pl.BlockSpec((1, D), lambda b, s: (0, 0)),
                  pl.BlockSpec((D, 4 * SEC), lambda b, s: (0, 0)),
                  pl.BlockSpec((2 * SEC, D), lambda b, s: (0, 0)),
                  pl.BlockSpec((FOX_HEADS, D), lambda b, s: (0, 0)),
                  pl.BlockSpec((FOX_HEADS, LANES), lambda b, s: (0, 0)),
                  pl.BlockSpec((tm, LANES), lambda b, s: (s, 0)),
                  pl.BlockSpec((tm, LANES), lambda b, s: (s, 0))],
        out_specs=[sec_spec] * 4 + [t_spec] * 2
        + [pl.BlockSpec((None, FOX_HEADS, tm), lambda b, s: (b, 0, s))],
        out_shape=[sec_out] * 4 + [t_out] * 2 + [jax.ShapeDtypeStruct((B, FOX_HEADS, S), F32)],
        scratch_shapes=[pltpu.VMEM((FOX_HEADS, LANES), F32)],
        compiler_params=_cparams(("arbitrary", "arbitrary")),
        name="in_proj",
    )(x, mod, g_pre, w_main, w_vt, wf_t, bf, cos, sin)


def _flash_t(chains, k_ref, sa_ref, qi, diag_mask):
    def scores(ch, r0):
        r0 = pl.multiple_of(r0, HK)
        t = lax.dot_general(k_ref[pl.ds(r0, HK), :], ch["qm"], _NT, preferred_element_type=F32)
        if ch["kbias"] is not None:
            t = t - ch["kbias"](r0)
        return t

    def update(ch, state, t, r0):
        m, l, acc = state
        mt = jnp.max(t, axis=0, keepdims=True)
        qb = ch["qbias"]
        m_new = jnp.maximum(m, mt if qb is None else mt + qb)
        alpha = jnp.exp2(m - m_new)
        p = jnp.exp2(t - (m_new if qb is None else m_new - qb))
        if ch["sum_rows"]:
            l = alpha * l + jnp.sum(p, axis=0, keepdims=True)
        vt = ch["vt"](pl.multiple_of(r0, HK))
        acc = alpha * acc + jnp.dot(vt, p.astype(BF16), preferred_element_type=F32)
        return m_new, l, acc

    for ci, ch in enumerate(chains):
        sa_ref[ci] = scores(ch, 0)

    def body(ki, carry):
        k0 = ki * TK
        tb = [scores(ch, k0 + HK) for ch in chains]
        st = [update(ch, s, sa_ref[ci], k0) for ci, (ch, s) in enumerate(zip(chains, carry))]
        ta = [scores(ch, k0 + TK) for ch in chains]
        st = [update(ch, s, tb[ci], k0 + HK) for ci, (ch, s) in enumerate(zip(chains, st))]
        for ci in range(len(chains)):
            sa_ref[ci] = ta[ci]
        return tuple(st)

    init = tuple((jnp.full((1, TQ), NEG_INF, F32), jnp.zeros((1, TQ), F32),
                  jnp.zeros((ch["rows"], TQ), F32)) for ch in chains)
    carry = lax.fori_loop(0, qi, body, init)
    k0 = qi * TK
    tb = [scores(ch, k0 + HK) for ch in chains]
    st = [update(ch, s, jnp.where(diag_mask[0:HK], sa_ref[ci], NEG_INF), k0)
          for ci, (ch, s) in enumerate(zip(chains, carry))]
    st = [update(ch, s, jnp.where(diag_mask[HK:TK], tb[ci], NEG_INF), k0 + HK)
          for ci, (ch, s) in enumerate(zip(chains, st))]
    return tuple(st)


def _fox_kernel(q_ref, k_ref, vt_ref, fcol_ref, frow_ref, o_ref, sa_ref):
    qi = pl.program_id(2)
    q2 = q_ref[...]
    lane = lax.broadcasted_iota(jnp.int32, q2.shape, 1)
    vrow = lax.broadcasted_iota(jnp.int32, (LANES, HK), 0)
    r = lax.broadcasted_iota(jnp.int32, (TK, TQ), 0)
    c = lax.broadcasted_iota(jnp.int32, (TK, TQ), 1)
    chains = []
    for j in range(2):
        in_head = (lane >= j * HEAD_DIM) & (lane < (j + 1) * HEAD_DIM)
        v_rows = (vrow >= j * HEAD_DIM) & (vrow < (j + 1) * HEAD_DIM)

        def vt(k0, v_rows=v_rows):
            v = vt_ref[:, pl.ds(k0, HK)]
            return jnp.where(v_rows, v, jnp.ones_like(v))

        chains.append(dict(
            qm=jnp.where(in_head, q2, jnp.zeros_like(q2)),
            kbias=lambda k0, j=j: fcol_ref[pl.ds(k0, HK), j:j + 1],
            qbias=frow_ref[j:j + 1, :],
            vt=vt, rows=LANES, sum_rows=False))
    (_, _, a0), (_, _, a1) = _flash_t(chains, k_ref, sa_ref, qi, r <= c)
    h = HEAD_DIM
    o_t = jnp.concatenate([a0[0:h] / a0[h:2 * h], a1[h:2 * h] / a1[0:h]], axis=0)
    o_ref[...] = o_t.T.astype(o_ref.dtype)


def _fox_attention(fq, fk, fvt, fcol, frow):
    B, S, _ = fq.shape
    npair = FOX_HEADS // 2
    return pl.pallas_call(
        _fox_kernel,
        grid=(B, npair, S // TQ),
        in_specs=[pl.BlockSpec((None, TQ, LANES), lambda b, p, q: (b, q, p)),
                  pl.BlockSpec((None, S, LANES), lambda b, p, q: (b, 0, p)),
                  pl.BlockSpec((None, LANES, S), lambda b, p, q: (b, p, 0)),
                  pl.BlockSpec((None, None, S, 2), lambda b, p, q: (b, p, 0, 0)),
                  pl.BlockSpec((None, None, 2, TQ), lambda b, p, q: (b, p, 0, q))],
        out_specs=pl.BlockSpec((None, TQ, LANES), lambda b, p, q: (b, q, p)),
        out_shape=jax.ShapeDtypeStruct((B, S, SEC), BF16),
        scratch_shapes=[pltpu.VMEM((2, HK, TQ), F32)],
        compiler_params=_cparams(("arbitrary", "arbitrary", "arbitrary")),
        name="fox_attention",
    )(fq, fk, fvt, fcol, frow)


def _diff_kernel(q_ref, k_ref, vt_ref, lam_ref, g_ref, o_ref, sa_ref):
    qi = pl.program_id(2)
    q2 = q_ref[...]
    lane = lax.broadcasted_iota(jnp.int32, q2.shape, 1)
    r = lax.broadcasted_iota(jnp.int32, (TK, TQ), 0)
    c = lax.broadcasted_iota(jnp.int32, (TK, TQ), 1)
    chains = []
    for j in range(2):
        in_map = (lane >= j * HEAD_DIM) & (lane < (j + 1) * HEAD_DIM)
        chains.append(dict(
            qm=jnp.where(in_map, q2, jnp.zeros_like(q2)),
            kbias=None, qbias=None,
            vt=lambda k0: vt_ref[:, pl.ds(k0, HK)], rows=LANES, sum_rows=True))
    (_, l0, a0), (_, l1, a1) = _flash_t(chains, k_ref, sa_ref, qi,
                                        (r // CHUNK) <= (c // CHUNK))
    lp = lam_ref[...]
    lam = (jnp.exp(jnp.sum(lp[0:1] * lp[1:2], axis=1, keepdims=True))
           - jnp.exp(jnp.sum(lp[2:3] * lp[3:4], axis=1, keepdims=True)) + LAMBDA_INIT)
    o = (a0 / l0 - lam * (a1 / l1)).T
    o_ref[...] = (_rms(o, g_ref[...]) * (1.0 - LAMBDA_INIT)).astype(o_ref.dtype)


def _diff_attention(dq, dk, dvt, lam_params, g_subln):
    B, S, _ = dq.shape
    return pl.pallas_call(
        _diff_kernel,
        grid=(B, DIFF_HEADS, S // TQ),
        in_specs=[pl.BlockSpec((None, TQ, LANES), lambda b, h, q: (b, q, h)),
                  pl.BlockSpec((None, S, LANES), lambda b, h, q: (b, 0, h)),
                  pl.BlockSpec((None, LANES, S), lambda b, h, q: (b, h, 0)),
                  pl.BlockSpec((4, HEAD_DIM), lambda b, h, q: (0, 0)),
                  pl.BlockSpec((1, LANES), lambda b, h, q: (0, 0))],
        out_specs=pl.BlockSpec((None, TQ, LANES), lambda b, h, q: (b, q, h)),
        out_shape=jax.ShapeDtypeStruct((B, S, SEC), BF16),
        scratch_shapes=[pltpu.VMEM((2, HK, TQ), F32)],
        compiler_params=_cparams(("arbitrary", "arbitrary", "arbitrary")),
        name="diff_attention",
    )(dq, dk, dvt, lam_params, g_subln)


def _route_kernel(d_ref, f_ref, wd_ref, wf_ref, x_ref, mod_ref, gpa_ref, gpf_ref, wr_ref, br_ref,
                  x1_ref, h2_ref, ri_ref, rw_ref, cnt_ref):
    first = (pl.program_id(0) == 0) & (pl.program_id(1) == 0)

    @pl.when(first)
    def _():
        cnt_ref[...] = jnp.zeros_like(cnt_ref)

    mixed = (jnp.dot(d_ref[...], wd_ref[...], preferred_element_type=F32)
             + jnp.dot(f_ref[...], wf_ref[...], preferred_element_type=F32))
    x1 = x_ref[...] + mod_ref[2:3, :] * _rms(mixed, gpa_ref[...])
    x1_ref[...] = x1
    h2 = _rms(x1, gpf_ref[...]) * (1.0 + mod_ref[4:5, :]) + mod_ref[3:4, :]
    h2_ref[...] = h2
    tm = h2.shape[0]

    hh, hm, hl = _split3(h2)
    wh, wm, wl = _split3(wr_ref[...])
    lt = br_ref[:, 0:1]
    for a, b_ in ((wh, hh), (wh, hm), (wm, hh), (wm, hm), (wh, hl), (wl, hh)):
        lt = lt + lax.dot_general(a, b_, _NT, preferred_element_type=F32)

    g = lt[0:N_GROUPS]
    row4 = lax.broadcasted_iota(jnp.int32, g.shape, 0)
    gmax = jnp.max(g, axis=0, keepdims=True)
    gidx = jnp.min(jnp.where(g == gmax, row4, N_GROUPS), axis=0, keepdims=True)
    gw = 1.0 / jnp.sum(jnp.exp(g - gmax), axis=0, keepdims=True)
    esel = jnp.zeros((EXPERTS_PER_GROUP, tm), F32)
    for gg in range(N_GROUPS):
        esel = jnp.where(gidx == gg, lt[8 + 8 * gg:16 + 8 * gg], esel)
    row8 = lax.broadcasted_iota(jnp.int32, esel.shape, 0)
    v1 = jnp.max(esel, axis=0, keepdims=True)
    i1 = jnp.min(jnp.where(esel == v1, row8, EXPERTS_PER_GROUP), axis=0, keepdims=True)
    esel2 = jnp.where(row8 == i1, -jnp.inf, esel)
    v2 = jnp.max(esel2, axis=0, keepdims=True)
    i2 = jnp.min(jnp.where(esel2 == v2, row8, EXPERTS_PER_GROUP), axis=0, keepdims=True)
    e = jnp.exp(v2 - v1)
    w1 = gw / (1.0 + e)
    w2 = gw * e / (1.0 + e)
    e1 = gidx * EXPERTS_PER_GROUP + i1
    e2 = gidx * EXPERTS_PER_GROUP + i2

    row32 = lax.broadcasted_iota(jnp.int32, (N_EXPERTS, tm), 0)
    oh1 = row32 == e1
    oh2 = row32 == e2
    oh = jnp.where(oh1 | oh2, 1.0, 0.0)
    t_r = lax.broadcasted_iota(jnp.int32, (tm, tm), 0)
    t_c = lax.broadcasted_iota(jnp.int32, (tm, tm), 1)
    tri = jnp.where(t_r < t_c, 1.0, 0.0).astype(BF16)
    tot = jnp.dot(oh.astype(BF16), tri, preferred_element_type=F32) + cnt_ref[:, 0:1]
    r1 = jnp.sum(jnp.where(oh1, tot, 0.0), axis=0, keepdims=True)
    r2 = jnp.sum(jnp.where(oh2, tot, 0.0), axis=0, keepdims=True)
    cnt_ref[...] = cnt_ref[...] + jnp.sum(oh, axis=1, keepdims=True)

    zi = jnp.zeros((4, tm), jnp.int32)
    ri_ref[...] = jnp.concatenate([e1, e2, r1.astype(jnp.int32), r2.astype(jnp.int32), zi], axis=0)
    rw_ref[...] = jnp.concatenate([w1, w2, jnp.zeros((6, tm), F32)], axis=0)


def _route(d_out, f_out, w_out_d, w_out_f, x, mod, g_post_attn, g_pre_ffn, w_rt, b_rt):
    B, S, D = x.shape
    tm = TM_ROUTE
    nrt = w_rt.shape[0]
    n_s = S // tm
    T = B * S
    return pl.pallas_call(
        _route_kernel,
        grid=(B, n_s),
        in_specs=[pl.BlockSpec((None, tm, SEC), lambda b, s: (b, s, 0)),
                  pl.BlockSpec((None, tm, SEC), lambda b, s: (b, s, 0)),
                  pl.BlockSpec((SEC, D), lambda b, s: (0, 0)),
                  pl.BlockSpec((SEC, D), lambda b, s: (0, 0)),
                  pl.BlockSpec((None, tm, D), lambda b, s: (b, s, 0)),
                  pl.BlockSpec((None, 6, D), lambda b, s: (b, 0, 0)),
                  pl.BlockSpec((1, D), lambda b, s: (0, 0)),
                  pl.BlockSpec((1, D), lambda b, s: (0, 0)),
                  pl.BlockSpec((nrt, D), lambda b, s: (0, 0)),
                  pl.BlockSpec((nrt, LANES), lambda b, s: (0, 0))],
        out_specs=[pl.BlockSpec((None, tm, D), lambda b, s: (b, s, 0)),
                   pl.BlockSpec((None, tm, D), lambda b, s: (b, s, 0)),
                   pl.BlockSpec((8, tm), lambda b, s: (0, b * n_s + s)),
                   pl.BlockSpec((8, tm), lambda b, s: (0, b * n_s + s)),
                   pl.BlockSpec((N_EXPERTS, LANES), lambda b, s: (0, 0))],
        out_shape=[jax.ShapeDtypeStruct((B, S, D), F32),
                   jax.ShapeDtypeStruct((B, S, D), F32),
                   jax.ShapeDtypeStruct((8, T), jnp.int32),
                   jax.ShapeDtypeStruct((8, T), F32),
                   jax.ShapeDtypeStruct((N_EXPERTS, LANES), F32)],
        compiler_params=_cparams(("arbitrary", "arbitrary")),
        name="out_proj_route",
    )(d_out, f_out, w_out_d, w_out_f, x, mod, g_post_attn, g_pre_ffn, w_rt, b_rt)


def _row_copy(src, s_row, dst, d_row, sem):
    return pltpu.make_async_copy(src.at[pl.ds(s_row, 1)], dst.at[pl.ds(d_row, 1)], sem)


def _dispatch_kernel(dest_ref, h_ref, xs_in_ref, xs_ref, sem):
    del xs_in_ref
    tm = h_ref.shape[0]

    def issue(r, c):
        _row_copy(h_ref, r, xs_ref, dest_ref[0, 0, r], sem).start()
        _row_copy(h_ref, r, xs_ref, dest_ref[0, 0, tm + r], sem).start()
        return c

    lax.fori_loop(0, tm, issue, 0)

    def drain(r, c):
        _row_copy(h_ref, r, xs_ref, 0, sem).wait()
        _row_copy(h_ref, r, xs_ref, 0, sem).wait()
        return c

    lax.fori_loop(0, tm, drain, 0)


def _dispatch(dest_tiles, h2, xs_init):
    T, D = h2.shape
    tm = TM_ROWS
    return pl.pallas_call(
        _dispatch_kernel,
        grid=(T // tm,),
        in_specs=[pl.BlockSpec((1, 1, 2 * tm), lambda i: (i, 0, 0), memory_space=pltpu.SMEM),
                  pl.BlockSpec((tm, D), lambda i: (i, 0)),
                  pl.BlockSpec(memory_space=pl.ANY)],
        out_specs=pl.BlockSpec(memory_space=pl.ANY),
        out_shape=jax.ShapeDtypeStruct(xs_init.shape, xs_init.dtype),
        scratch_shapes=[pltpu.SemaphoreType.DMA],
        input_output_aliases={2: 0},
        compiler_params=_cparams(("arbitrary",)),
        name="moe_dispatch",
    )(dest_tiles, h2, xs_init)


def _expert_kernel(be_ref, nu_ref, xs_ref, wg_ref, wu_ref, wd_ref, ys_ref, wgb, wub, wdb):
    i = pl.program_id(0)

    @pl.when(i < nu_ref[0])
    def _():
        prev = be_ref[jnp.maximum(i - 1, 0)]

        @pl.when((i == 0) | (be_ref[i] != prev))
        def _():
            wgb[...] = wg_ref[...].astype(BF16)
            wub[...] = wu_ref[...].astype(BF16)
            wdb[...] = wd_ref[...].astype(BF16)

        xb = xs_ref[...].astype(BF16)
        a = jnp.dot(xb, wgb[...], preferred_element_type=F32)
        u = jnp.dot(xb, wub[...], preferred_element_type=F32)
        hmid = (a * jax.nn.sigmoid(a) * u).astype(BF16)
        ys_ref[...] = jnp.dot(hmid, wdb[...], preferred_element_type=F32)

    @pl.when(i >= nu_ref[0])
    def _():
        ys_ref[...] = jnp.zeros_like(ys_ref)


def _experts(block_e, n_used, xs, w_gate, w_up, w_down):
    P, D = xs.shape
    nb = P // BM
    Fd = w_gate.shape[2]

    def row_map(i, be, nu):
        return (jnp.minimum(i, nu[0] - 1), 0)

    def w_map(i, be, nu):
        return (be[i], 0, 0)

    return pl.pallas_call(
        _expert_kernel,
        grid_spec=pltpu.PrefetchScalarGridSpec(
            num_scalar_prefetch=2,
            grid=(nb,),
            in_specs=[pl.BlockSpec((BM, D), row_map),
                      pl.BlockSpec((None, D, Fd), w_map),
                      pl.BlockSpec((None, D, Fd), w_map),
                      pl.BlockSpec((None, Fd, D), w_map)],
            out_specs=pl.BlockSpec((BM, D), lambda i, be, nu: (i, 0)),
            scratch_shapes=[pltpu.VMEM((D, Fd), BF16), pltpu.VMEM((D, Fd), BF16),
                            pltpu.VMEM((Fd, D), BF16)]),
        out_shape=jax.ShapeDtypeStruct((P, D), F32),
        compiler_params=_cparams(("arbitrary",)),
        name="moe_experts",
    )(block_e, n_used, xs, w_gate, w_up, w_down)


def _combine_kernel(dest_ref, ys_ref, w_ref, x1_ref, mod_ref, g_ref, o_ref, buf, sem):
    tm = x1_ref.shape[0]

    def issue(r, c):
        _row_copy(ys_ref, dest_ref[0, 0, r], buf.at[0], r, sem).start()
        _row_copy(ys_ref, dest_ref[0, 0, tm + r], buf.at[1], r, sem).start()
        return c

    lax.fori_loop(0, tm, issue, 0)

    def drain(r, c):
        _row_copy(ys_ref, 0, buf.at[0], r, sem).wait()
        _row_copy(ys_ref, 0, buf.at[1], r, sem).wait()
        return c

    lax.fori_loop(0, tm, drain, 0)
    y = w_ref[:, 0:1] * buf[0] + w_ref[:, 1:2] * buf[1]
    o_ref[...] = x1_ref[...] + mod_ref[5:6, :] * _rms(y, g_ref[...])


def _combine(dest_tiles, ys, wts, x1, mod, g_post_ffn, B, S):
    T, D = x1.shape
    tm = TM_ROWS
    per_b = S // tm
    return pl.pallas_call(
        _combine_kernel,
        grid=(T // tm,),
        in_specs=[pl.BlockSpec((1, 1, 2 * tm), lambda i: (i, 0, 0), memory_space=pltpu.SMEM),
                  pl.BlockSpec(memory_space=pl.ANY),
                  pl.BlockSpec((tm, 8), lambda i: (i, 0)),
                  pl.BlockSpec((tm, D), lambda i: (i, 0)),
                  pl.BlockSpec((None, 6, D), lambda i: (i // per_b, 0, 0)),
                  pl.BlockSpec((1, D), lambda i: (0, 0))],
        out_specs=pl.BlockSpec((tm, D), lambda i: (i, 0)),
        out_shape=jax.ShapeDtypeStruct((T, D), F32),
        scratch_shapes=[pltpu.VMEM((2, tm, D), F32), pltpu.SemaphoreType.DMA],
        compiler_params=_cparams(("arbitrary",)),
        name="moe_combine",
    )(dest_tiles, ys, wts, x1, mod, g_post_ffn)


def _rope_tables(S):
    inv = 1.0 / (10000.0 ** (jnp.arange(0, HEAD_DIM, 2, dtype=F32) / HEAD_DIM))
    ang = jnp.arange(S, dtype=F32)[:, None] * inv[None, :]
    c, s = jnp.cos(ang), jnp.sin(ang)
    return jnp.tile(c, (1, 4)), jnp.tile(jnp.concatenate([-s, s], axis=1), (1, 2))


def kernel(x, c, w_ada, b_ada, g_pre_attn, g_post_attn, w_in, b_forget, lambda_q1, lambda_k1,
           lambda_q2, lambda_k2, g_diff_subln, w_out, g_pre_ffn, g_post_ffn, w_group, b_group,
           w_router, b_router, w_gate, w_up, w_down):
    B, S, D = x.shape
    T = B * S
    l = 0
    mod = _ada(c, w_ada[l], b_ada[l]).reshape(B, 6, D)

    wi = w_in[l]
    w_main = jnp.concatenate([wi[:, 0:2 * SEC], wi[:, 3 * SEC:5 * SEC]], axis=1).astype(BF16)
    w_vt = jnp.concatenate([wi[:, 2 * SEC:3 * SEC], wi[:, 5 * SEC:6 * SEC]], axis=1).T.astype(BF16)
    wf_t = wi[:, 6 * SEC:].T.astype(BF16)
    bf = jnp.broadcast_to(b_forget[l][:, None], (FOX_HEADS, LANES))
    cos, sin = _rope_tables(S)
    dq, dk, fq, fk, dvt, fvt, fcum = _inproj(x, mod, g_pre_attn[l][None, :], w_main, w_vt, wf_t,
                                             bf, cos, sin)

    lam_params = jnp.stack([lambda_q1[l], lambda_k1[l], lambda_q2[l], lambda_k2[l]])
    d_out = _diff_attention(dq, dk, dvt, lam_params, g_diff_subln[l][None, :])
    frow = fcum.reshape(B, FOX_HEADS // 2, 2, S)
    fcol = jnp.transpose(frow, (0, 1, 3, 2))
    f_out = _fox_attention(fq, fk, fvt, fcol, frow)

    w_o = w_out[l].astype(BF16)
    zpad = jnp.zeros((8 - N_GROUPS, D), F32)
    w_rt = jnp.concatenate([w_group[l].T, zpad, w_router[l].T], axis=0)
    b_rt = jnp.concatenate([b_group[l], jnp.zeros((8 - N_GROUPS,), F32), b_router[l]])
    b_rt = jnp.broadcast_to(b_rt[:, None], (8 + N_EXPERTS, LANES))
    x1, h2, ri, rw, cnt = _route(d_out, f_out, w_o[:SEC], w_o[SEC:], x, mod,
                                 g_post_attn[l][None, :], g_pre_ffn[l][None, :], w_rt, b_rt)

    counts = cnt[:, 0].astype(jnp.int32)
    padded = ((counts + BM - 1) // BM) * BM
    pend = jnp.cumsum(padded)
    pstart = pend - padded
    eids = jnp.arange(N_EXPERTS, dtype=jnp.int32)
    dest = ri[2:4] + jnp.sum(jnp.where(ri[0:2, :, None] == eids, pstart, 0), axis=-1)
    nb = (2 * T) // BM + N_EXPERTS
    n_used = (pend[-1] // BM).astype(jnp.int32).reshape(1)
    blk0 = jnp.arange(nb, dtype=jnp.int32) * BM
    block_e = jnp.minimum(jnp.sum((blk0[:, None] >= pend[None, :]).astype(jnp.int32), axis=1),
                          N_EXPERTS - 1)
    tm = TM_ROWS
    dest_tiles = dest.reshape(2, T // tm, tm).transpose(1, 0, 2).reshape(T // tm, 1, 2 * tm)

    xs = _dispatch(dest_tiles, h2.reshape(T, D), jnp.zeros((nb * BM, D), F32))
    ys = _experts(block_e, n_used, xs, w_gate[l], w_up[l], w_down[l])
    out = _combine(dest_tiles, ys, rw.T, x1.reshape(T, D), mod, g_post_ffn[l][None, :], B, S)
    return out.reshape(B, S, D)
```

```python
import functools
import math

import jax
import jax.numpy as jnp
from jax import lax
from jax.experimental import pallas as pl
from jax.experimental.pallas import tpu as pltpu
from jax.experimental.pallas import tpu_sc as plsc

F32 = jnp.float32
BF16 = jnp.bfloat16

D_MODEL = 1024
HEAD_DIM = 64
CHUNK = 64
DIFF_HEADS = 4
FOX_HEADS = 8
SEC = 512
N_GROUPS = 4
EXPERTS_PER_GROUP = 8
N_EXPERTS = 32
D_EXPERT = 512
EPS = 1e-6
NEG_INF = -1e30
LOG2E = 1.4426950408889634
LAMBDA_INIT = 0.8 - 0.6 * math.exp(-0.3 * 0)

LANES = 128
TM_PROJ = 512
TQ = 512
TK = 512
HK = TK // 2
DV_ROWS = 2 * HEAD_DIM + 16
TM_ROUTE = 512
BM = 256
TM_ROWS = 256
SC_CORES = 2
SC_SUBCORES = 16
SC_WORKERS = SC_CORES * SC_SUBCORES
SC_WIN = 64
VMEM_LIMIT = 48 * 1024 * 1024
_NT = (((1,), (1,)), ((), ()))


def _cparams(sem, flags=None):
    return pltpu.CompilerParams(dimension_semantics=sem, vmem_limit_bytes=VMEM_LIMIT, flags=flags)


def _rms(x, g):
    return x * lax.rsqrt(jnp.mean(x * x, axis=-1, keepdims=True) + EPS) * g


def _ada_kernel(c_ref, w_ref, b_ref, o_ref):
    c = c_ref[...]
    cs = c * jax.nn.sigmoid(c)
    o_ref[...] = jnp.dot(cs, w_ref[...], preferred_element_type=F32,
                         precision=lax.Precision.HIGHEST) + b_ref[...]


def _ada(c, w_ada, b_ada):
    B, D = c.shape
    N = w_ada.shape[1]
    tn = 1024
    return pl.pallas_call(
        _ada_kernel,
        grid=(N // tn,),
        in_specs=[pl.BlockSpec((B, D), lambda j: (0, 0)),
                  pl.BlockSpec((D, tn), lambda j: (0, j)),
                  pl.BlockSpec((1, tn), lambda j: (0, j))],
        out_specs=pl.BlockSpec((B, tn), lambda j: (0, j)),
        out_shape=jax.ShapeDtypeStruct((B, N), F32),
        compiler_params=_cparams(("arbitrary",)),
        name="ada_mod",
    )(c, w_ada, b_ada.reshape(1, N))


def _split3(x):
    hi = x.astype(BF16)
    r1 = x - hi.astype(F32)
    mid = r1.astype(BF16)
    lo = (r1 - mid.astype(F32)).astype(BF16)
    return hi, mid, lo


def _inproj_kernel(x_ref, mod_ref, g_ref, w_ref, wvt_ref, wf_ref, bf_ref, cos_ref, sin_ref,
                   dq_ref, dk_ref, fq_ref, fk_ref, dvt_ref, fvt_ref, fcum_ref, carry_ref):
    si = pl.program_id(1)
    x = x_ref[...]
    h = _rms(x, g_ref[...]) * (1.0 + mod_ref[1:2, :]) + mod_ref[0:1, :]
    hb = h.astype(BF16)
    tm = x.shape[0]

    cos = cos_ref[...]
    sin = sin_ref[...]
    lane = lax.broadcasted_iota(jnp.int32, (tm, LANES), 1)
    first_half = (lane % HEAD_DIM) < (HEAD_DIM // 2)
    qscale = HEAD_DIM ** -0.5 * LOG2E

    def proj(sec):
        return jnp.dot(hb, w_ref[:, sec * SEC:(sec + 1) * SEC], preferred_element_type=F32)

    def rope_store(p, out_ref, scale):
        for j in range(SEC // LANES):
            ch = p[:, j * LANES:(j + 1) * LANES]
            partner = jnp.where(first_half,
                                pltpu.roll(ch, LANES - HEAD_DIM // 2, 1),
                                pltpu.roll(ch, HEAD_DIM // 2, 1))
            r = ch * cos + partner * sin
            if scale != 1.0:
                r = r * scale
            out_ref[:, j * LANES:(j + 1) * LANES] = r.astype(out_ref.dtype)

    rope_store(proj(0), dq_ref, qscale)
    rope_store(proj(1), dk_ref, 1.0)
    fq_ref[...] = (proj(2) * qscale).astype(fq_ref.dtype)
    fk_ref[...] = proj(3).astype(fk_ref.dtype)
    vt = lax.dot_general(wvt_ref[...], hb, _NT, preferred_element_type=F32)
    for hd in range(DIFF_HEADS):
        dvt_ref[hd, 0:LANES, :] = vt[hd * LANES:(hd + 1) * LANES].astype(dvt_ref.dtype)
        dvt_ref[hd, LANES:DV_ROWS, :] = jnp.ones((DV_ROWS - LANES, tm), dvt_ref.dtype)
    fvt_ref[...] = vt[SEC:2 * SEC].astype(fvt_ref.dtype)

    z = lax.dot_general(wf_ref[...], hb, _NT, preferred_element_type=F32) + bf_ref[:, 0:1]
    lf = (jnp.minimum(z, 0.0) - jnp.log1p(jnp.exp(-jnp.abs(z)))) * LOG2E

    @pl.when(si == 0)
    def _():
        carry_ref[...] = jnp.zeros_like(carry_ref)

    r_i = lax.broadcasted_iota(jnp.int32, (LANES, LANES), 0)
    c_i = lax.broadcasted_iota(jnp.int32, (LANES, LANES), 1)
    tri = jnp.where(r_i <= c_i, 1.0, 0.0).astype(BF16)
    run = carry_ref[...]
    for j in range(tm // LANES):
        hi, mid, lo = _split3(lf[:, j * LANES:(j + 1) * LANES])
        cs = (jnp.dot(hi, tri, preferred_element_type=F32)
              + jnp.dot(mid, tri, preferred_element_type=F32)
              + jnp.dot(lo, tri, preferred_element_type=F32)) + run
        fcum_ref[:, j * LANES:(j + 1) * LANES] = cs
        run = jnp.broadcast_to(cs[:, LANES - 1:LANES], run.shape)
    carry_ref[...] = run


def _inproj(x, mod, g_pre, w_main, w_vt, wf_t, bf, cos, sin):
    B, S, D = x.shape
    tm = TM_PROJ
    sec_out = jax.ShapeDtypeStruct((B, S, SEC), BF16)
    sec_spec = pl.BlockSpec((None, tm, SEC), lambda b, s: (b, s, 0))
    t_out = jax.ShapeDtypeStruct((B, SEC, S), BF16)
    t_spec = pl.BlockSpec((None, SEC, tm), lambda b, s: (b, 0, s))
    return pl.pallas_call(
        _inproj_kernel,
        grid=(B, S // tm),
        in_specs=[pl.BlockSpec((None, tm, D), lambda b, s: (b, s, 0)),
                  pl.BlockSpec((None, 6, D), lambda b, s: (b, 0, 0)),
                  pl.BlockSpec((1, D), lambda b, s: (0, 0)),
                  pl.BlockSpec((D, 4 * SEC), lambda b, s: (0, 0)),
                  pl.BlockSpec((2 * SEC, D), lambda b, s: (0, 0)),
                  pl.BlockSpec((FOX_HEADS, D), lambda b, s: (0, 0)),
                  pl.BlockSpec((FOX_HEADS, LANES), lambda b, s: (0, 0)),
                  pl.BlockSpec((tm, LANES), lambda b, s: (s, 0)),
                  pl.BlockSpec((tm, LANES), lambda b, s: (s, 0))],
        out_specs=[sec_spec] * 4
        + [pl.BlockSpec((None, DIFF_HEADS, DV_ROWS, tm), lambda b, s: (b, 0, 0, s)), t_spec,
           pl.BlockSpec((None, FOX_HEADS, tm), lambda b, s: (b, 0, s))],
        out_shape=[sec_out] * 4
        + [jax.ShapeDtypeStruct((B, DIFF_HEADS, DV_ROWS, S), BF16), t_out,
           jax.ShapeDtypeStruct((B, FOX_HEADS, S), F32)],
        scratch_shapes=[pltpu.VMEM((FOX_HEADS, LANES), F32)],
        compiler_params=_cparams(("arbitrary", "arbitrary")),
        name="in_proj",
    )(x, mod, g_pre, w_main, w_vt, wf_t, bf, cos, sin)


def _flash_t(chains, k_ref, sa_ref, qi, diag_mask):
    def scores(ch, r0):
        r0 = pl.multiple_of(r0, HK)
        t = lax.dot_general(k_ref[pl.ds(r0, HK), :], ch["qm"], _NT, preferred_element_type=F32)
        if ch["kbias"] is not None:
            t = t - ch["kbias"](r0)
        return t

    def update(ch, state, t, r0):
        m, acc = state
        mt = jnp.max(t, axis=0, keepdims=True)
        qb = ch["qbias"]
        m_new = jnp.maximum(m, mt if qb is None else mt + qb)
        alpha = jnp.exp2(m - m_new)
        p = jnp.exp2((t - (m_new if qb is None else m_new - qb)).astype(BF16))
        vt = ch["vt"](pl.multiple_of(r0, HK))
        acc = alpha * acc + jnp.dot(vt, p, preferred_element_type=F32)
        return m_new, acc

    for ci, ch in enumerate(chains):
        sa_ref[ci] = scores(ch, 0)

    def body(ki, carry):
        k0 = ki * TK
        tb = [scores(ch, k0 + HK) for ch in chains]
        st = [update(ch, s, sa_ref[ci], k0) for ci, (ch, s) in enumerate(zip(chains, carry))]
        ta = [scores(ch, k0 + TK) for ch in chains]
        st = [update(ch, s, tb[ci], k0 + HK) for ci, (ch, s) in enumerate(zip(chains, st))]
        for ci in range(len(chains)):
            sa_ref[ci] = ta[ci]
        return tuple(st)

    init = tuple((jnp.full((1, TQ), NEG_INF, F32), jnp.zeros((ch["rows"], TQ), F32))
                 for ch in chains)
    carry = lax.fori_loop(0, qi, body, init)
    k0 = qi * TK
    tb = [scores(ch, k0 + HK) for ch in chains]
    st = [update(ch, s, jnp.where(diag_mask[0:HK], sa_ref[ci], NEG_INF), k0)
          for ci, (ch, s) in enumerate(zip(chains, carry))]
    st = [update(ch, s, jnp.where(diag_mask[HK:TK], tb[ci], NEG_INF), k0 + HK)
          for ci, (ch, s) in enumerate(zip(chains, st))]
    return tuple(st)


def _fox_kernel(q_ref, k_ref, vt_ref, fcol_ref, frow_ref, o_ref, sa_ref):
    qi = pl.program_id(2)
    q2 = q_ref[...]
    lane = lax.broadcasted_iota(jnp.int32, q2.shape, 1)
    vrow = lax.broadcasted_iota(jnp.int32, (LANES, HK), 0)
    r = lax.broadcasted_iota(jnp.int32, (TK, TQ), 0)
    c = lax.broadcasted_iota(jnp.int32, (TK, TQ), 1)
    chains = []
    for j in range(2):
        in_head = (lane >= j * HEAD_DIM) & (lane < (j + 1) * HEAD_DIM)
        v_rows = (vrow >= j * HEAD_DIM) & (vrow < (j + 1) * HEAD_DIM)

        def vt(k0, v_rows=v_rows):
            v = vt_ref[:, pl.ds(k0, HK)]
            return jnp.where(v_rows, v, jnp.ones_like(v))

        chains.append(dict(
            qm=jnp.where(in_head, q2, jnp.zeros_like(q2)),
            kbias=lambda k0, j=j: fcol_ref[pl.ds(k0, HK), j:j + 1],
            qbias=frow_ref[j:j + 1, :],
            vt=vt, rows=LANES))
    (_, a0), (_, a1) = _flash_t(chains, k_ref, sa_ref, qi, r <= c)
    h = HEAD_DIM
    o_t = jnp.concatenate([a0[0:h] / a0[h:2 * h], a1[h:2 * h] / a1[0:h]], axis=0)
    o_ref[...] = o_t.T.astype(o_ref.dtype)


def _fox_attention(fq, fk, fvt, fcol, frow):
    B, S, _ = fq.shape
    npair = FOX_HEADS // 2
    return pl.pallas_call(
        _fox_kernel,
        grid=(B, npair, S // TQ),
        in_specs=[pl.BlockSpec((None, TQ, LANES), lambda b, p, q: (b, q, p)),
                  pl.BlockSpec((None, S, LANES), lambda b, p, q: (b, 0, p)),
                  pl.BlockSpec((None, LANES, S), lambda b, p, q: (b, p, 0)),
                  pl.BlockSpec((None, None, S, 2), lambda b, p, q: (b, p, 0, 0)),
                  pl.BlockSpec((None, None, 2, TQ), lambda b, p, q: (b, p, 0, q))],
        out_specs=pl.BlockSpec((None, TQ, LANES), lambda b, p, q: (b, q, p)),
        out_shape=jax.ShapeDtypeStruct((B, S, SEC), BF16),
        scratch_shapes=[pltpu.VMEM((2, HK, TQ), F32)],
        compiler_params=_cparams(("arbitrary", "arbitrary", "arbitrary")),
        name="fox_attention",
    )(fq, fk, fvt, fcol, frow)


def _diff_kernel(q_ref, k_ref, vt_ref, lam_ref, g_ref, o_ref, sa_ref):
    qi = pl.program_id(2)
    q2 = q_ref[...]
    lane = lax.broadcasted_iota(jnp.int32, q2.shape, 1)
    r = lax.broadcasted_iota(jnp.int32, (TK, TQ), 0)
    c = lax.broadcasted_iota(jnp.int32, (TK, TQ), 1)
    chains = []
    for j in range(2):
        in_map = (lane >= j * HEAD_DIM) & (lane < (j + 1) * HEAD_DIM)
        chains.append(dict(
            qm=jnp.where(in_map, q2, jnp.zeros_like(q2)),
            kbias=None, qbias=None,
            vt=lambda k0: vt_ref[:, pl.ds(k0, HK)], rows=DV_ROWS))
    (_, a0), (_, a1) = _flash_t(chains, k_ref, sa_ref, qi, (r // CHUNK) <= (c // CHUNK))
    lp = lam_ref[...]
    lam = (jnp.exp(jnp.sum(lp[0:1] * lp[1:2], axis=1, keepdims=True))
           - jnp.exp(jnp.sum(lp[2:3] * lp[3:4], axis=1, keepdims=True)) + LAMBDA_INIT)
    v = LANES
    o = (a0[0:v] / a0[v:v + 1] - lam * (a1[0:v] / a1[v:v + 1])).T
    o_ref[...] = (_rms(o, g_ref[...]) * (1.0 - LAMBDA_INIT)).astype(o_ref.dtype)


def _diff_attention(dq, dk, dvt, lam_params, g_subln):
    B, S, _ = dq.shape
    return pl.pallas_call(
        _diff_kernel,
        grid=(B, DIFF_HEADS, S // TQ),
        in_specs=[pl.BlockSpec((None, TQ, LANES), lambda b, h, q: (b, q, h)),
                  pl.BlockSpec((None, S, LANES), lambda b, h, q: (b, 0, h)),
                  pl.BlockSpec((None, None, DV_ROWS, S), lambda b, h, q: (b, h, 0, 0)),
                  pl.BlockSpec((4, HEAD_DIM), lambda b, h, q: (0, 0)),
                  pl.BlockSpec((1, LANES), lambda b, h, q: (0, 0))],
        out_specs=pl.BlockSpec((None, TQ, LANES), lambda b, h, q: (b, q, h)),
        out_shape=jax.ShapeDtypeStruct((B, S, SEC), BF16),
        scratch_shapes=[pltpu.VMEM((2, HK, TQ), F32)],
        compiler_params=_cparams(("arbitrary", "arbitrary", "arbitrary")),
        name="diff_attention",
    )(dq, dk, dvt, lam_params, g_subln)


def _route_kernel(d_ref, f_ref, wd_ref, wf_ref, x_ref, mod_ref, gpa_ref, gpf_ref, wr_ref, br_ref,
                  x1_ref, h2_ref, ri_ref, rw_ref, cnt_ref):
    first = (pl.program_id(0) == 0) & (pl.program_id(1) == 0)

    @pl.when(first)
    def _():
        cnt_ref[...] = jnp.zeros_like(cnt_ref)

    mixed = (jnp.dot(d_ref[...], wd_ref[...], preferred_element_type=F32)
             + jnp.dot(f_ref[...], wf_ref[...], preferred_element_type=F32))
    x1 = x_ref[...] + mod_ref[2:3, :] * _rms(mixed, gpa_ref[...])
    x1_ref[...] = x1
    h2 = _rms(x1, gpf_ref[...]) * (1.0 + mod_ref[4:5, :]) + mod_ref[3:4, :]
    h2_ref[...] = h2
    tm = h2.shape[0]

    hh, hm, hl = _split3(h2)
    wh, wm, wl = _split3(wr_ref[...])
    lt = br_ref[:, 0:1]
    for a, b_ in ((wh, hh), (wh, hm), (wm, hh), (wm, hm), (wh, hl), (wl, hh)):
        lt = lt + lax.dot_general(a, b_, _NT, preferred_element_type=F32)

    g = lt[0:N_GROUPS]
    row4 = lax.broadcasted_iota(jnp.int32, g.shape, 0)
    gmax = jnp.max(g, axis=0, keepdims=True)
    gidx = jnp.min(jnp.where(g == gmax, row4, N_GROUPS), axis=0, keepdims=True)
    gw = 1.0 / jnp.sum(jnp.exp(g - gmax), axis=0, keepdims=True)
    esel = jnp.zeros((EXPERTS_PER_GROUP, tm), F32)
    for gg in range(N_GROUPS):
        esel = jnp.where(gidx == gg, lt[8 + 8 * gg:16 + 8 * gg], esel)
    row8 = lax.broadcasted_iota(jnp.int32, esel.shape, 0)
    v1 = jnp.max(esel, axis=0, keepdims=True)
    i1 = jnp.min(jnp.where(esel == v1, row8, EXPERTS_PER_GROUP), axis=0, keepdims=True)
    esel2 = jnp.where(row8 == i1, -jnp.inf, esel)
    v2 = jnp.max(esel2, axis=0, keepdims=True)
    i2 = jnp.min(jnp.where(esel2 == v2, row8, EXPERTS_PER_GROUP), axis=0, keepdims=True)
    e = jnp.exp(v2 - v1)
    w1 = gw / (1.0 + e)
    w2 = gw * e / (1.0 + e)
    e1 = gidx * EXPERTS_PER_GROUP + i1
    e2 = gidx * EXPERTS_PER_GROUP + i2

    row32 = lax.broadcasted_iota(jnp.int32, (N_EXPERTS, tm), 0)
    oh1 = row32 == e1
    oh2 = row32 == e2
    oh = jnp.where(oh1 | oh2, 1.0, 0.0)
    t_r = lax.broadcasted_iota(jnp.int32, (tm, tm), 0)
    t_c = lax.broadcasted_iota(jnp.int32, (tm, tm), 1)
    tri = jnp.where(t_r < t_c, 1.0, 0.0).astype(BF16)
    tot = jnp.dot(oh.astype(BF16), tri, preferred_element_type=F32) + cnt_ref[:, 0:1]
    r1 = jnp.sum(jnp.where(oh1, tot, 0.0), axis=0, keepdims=True)
    r2 = jnp.sum(jnp.where(oh2, tot, 0.0), axis=0, keepdims=True)
    cnt_ref[...] = cnt_ref[...] + jnp.sum(oh, axis=1, keepdims=True)

    zi = jnp.zeros((4, tm), jnp.int32)
    ri_ref[...] = jnp.concatenate([e1, e2, r1.astype(jnp.int32), r2.astype(jnp.int32), zi], axis=0)
    rw_ref[...] = jnp.concatenate([w1, w2, jnp.zeros((6, tm), F32)], axis=0)


def _route(d_out, f_out, w_out_d, w_out_f, x, mod, g_post_attn, g_pre_ffn, w_rt, b_rt):
    B, S, D = x.shape
    tm = TM_ROUTE
    nrt = w_rt.shape[0]
    n_s = S // tm
    T = B * S
    return pl.pallas_call(
        _route_kernel,
        grid=(B, n_s),
        in_specs=[pl.BlockSpec((None, tm, SEC), lambda b, s: (b, s, 0)),
                  pl.BlockSpec((None, tm, SEC), lambda b, s: (b, s, 0)),
                  pl.BlockSpec((SEC, D), lambda b, s: (0, 0)),
                  pl.BlockSpec((SEC, D), lambda b, s: (0, 0)),
                  pl.BlockSpec((None, tm, D), lambda b, s: (b, s, 0)),
                  pl.BlockSpec((None, 6, D), lambda b, s: (b, 0, 0)),
                  pl.BlockSpec((1, D), lambda b, s: (0, 0)),
                  pl.BlockSpec((1, D), lambda b, s: (0, 0)),
                  pl.BlockSpec((nrt, D), lambda b, s: (0, 0)),
                  pl.BlockSpec((nrt, LANES), lambda b, s: (0, 0))],
        out_specs=[pl.BlockSpec((None, tm, D), lambda b, s: (b, s, 0)),
                   pl.BlockSpec((None, tm, D), lambda b, s: (b, s, 0)),
                   pl.BlockSpec((8, tm), lambda b, s: (0, b * n_s + s)),
                   pl.BlockSpec((8, tm), lambda b, s: (0, b * n_s + s)),
                   pl.BlockSpec((N_EXPERTS, LANES), lambda b, s: (0, 0))],
        out_shape=[jax.ShapeDtypeStruct((B, S, D), F32),
                   jax.ShapeDtypeStruct((B, S, D), F32),
                   jax.ShapeDtypeStruct((8, T), jnp.int32),
                   jax.ShapeDtypeStruct((8, T), F32),
                   jax.ShapeDtypeStruct((N_EXPERTS, LANES), F32)],
        compiler_params=_cparams(("arbitrary", "arbitrary")),
        name="out_proj_route",
    )(d_out, f_out, w_out_d, w_out_f, x, mod, g_post_attn, g_pre_ffn, w_rt, b_rt)


def _sc_mesh():
    return plsc.VectorSubcoreMesh(core_axis_name="c", subcore_axis_name="s",
                                  num_cores=SC_CORES, num_subcores=SC_SUBCORES)


def _sc_worker_base(per_worker):
    return (lax.axis_index("s") * SC_CORES + lax.axis_index("c")) * per_worker


def _dispatch(dest, h2, n_rows):
    T, D = h2.shape
    per_worker = T // SC_WORKERS

    @functools.partial(
        pl.kernel, mesh=_sc_mesh(),
        out_type=jax.ShapeDtypeStruct((n_rows, D), h2.dtype),
        scratch_types=[pltpu.VMEM((SC_WIN,), jnp.int32), pltpu.VMEM((SC_WIN, D), h2.dtype)],
        name="moe_dispatch")
    def run(h_hbm, dest_hbm, xs_hbm, idx_v, rows_v):
        base = _sc_worker_base(per_worker)

        @pl.loop(0, per_worker // SC_WIN)
        def _(ci):
            off = pl.multiple_of(base + ci * SC_WIN, SC_WIN)
            pltpu.sync_copy(h_hbm.at[pl.ds(off, SC_WIN)], rows_v)
            for k in range(2):
                pltpu.sync_copy(dest_hbm.at[pl.ds(k * T + off, SC_WIN)], idx_v)
                pltpu.sync_copy(rows_v, xs_hbm.at[idx_v])

    return run(h2, dest.reshape(2 * T))


def _gather_rows(ys, dest):
    _, D = ys.shape
    K, T = dest.shape
    n = K * T
    per_worker = n // SC_WORKERS

    @functools.partial(
        pl.kernel, mesh=_sc_mesh(),
        out_type=jax.ShapeDtypeStruct((n, D), ys.dtype),
        scratch_types=[pltpu.VMEM((SC_WIN,), jnp.int32), pltpu.VMEM((SC_WIN, D), ys.dtype)],
        name="moe_gather")
    def run(ys_hbm, dest_hbm, g_hbm, idx_v, rows_v):
        base = _sc_worker_base(per_worker)

        @pl.loop(0, per_worker // SC_WIN)
        def _(ci):
            off = pl.multiple_of(base + ci * SC_WIN, SC_WIN)
            pltpu.sync_copy(dest_hbm.at[pl.ds(off, SC_WIN)], idx_v)
            pltpu.sync_copy(ys_hbm.at[idx_v], rows_v)
            pltpu.sync_copy(rows_v, g_hbm.at[pl.ds(off, SC_WIN)])

    return run(ys, dest.reshape(n)).reshape(K, T, D)


def _expert_kernel(be_ref, nu_ref, nv_ref, xs_ref, wg_ref, wu_ref, wd_ref, ys_ref,
                   wgb, wub, wdb):
    i = pl.program_id(0)

    @pl.when(i < nu_ref[0])
    def _():
        prev = be_ref[jnp.maximum(i - 1, 0)]

        @pl.when((i == 0) | (be_ref[i] != prev))
        def _():
            wgb[...] = wg_ref[...].astype(BF16)
            wub[...] = wu_ref[...].astype(BF16)
            wdb[...] = wd_ref[...].astype(BF16)

        row = lax.broadcasted_iota(jnp.int32, xs_ref.shape, 0)
        xb = jnp.where(row < nv_ref[i], xs_ref[...], 0.0).astype(BF16)
        a = jnp.dot(xb, wgb[...], preferred_element_type=F32)
        u = jnp.dot(xb, wub[...], preferred_element_type=F32)
        hmid = (a * jax.nn.sigmoid(a) * u).astype(BF16)
        ys_ref[...] = jnp.dot(hmid, wdb[...], preferred_element_type=F32)

    @pl.when(i >= nu_ref[0])
    def _():
        ys_ref[...] = jnp.zeros_like(ys_ref)


def _experts(block_e, n_used, n_valid, xs, w_gate, w_up, w_down):
    P, D = xs.shape
    nb = P // BM
    Fd = w_gate.shape[2]

    def row_map(i, be, nu, nv):
        return (jnp.minimum(i, nu[0] - 1), 0)

    def w_map(i, be, nu, nv):
        return (be[i], 0, 0)

    return pl.pallas_call(
        _expert_kernel,
        grid_spec=pltpu.PrefetchScalarGridSpec(
            num_scalar_prefetch=3,
            grid=(nb,),
            in_specs=[pl.BlockSpec((BM, D), row_map),
                      pl.BlockSpec((None, D, Fd), w_map),
                      pl.BlockSpec((None, D, Fd), w_map),
                      pl.BlockSpec((None, Fd, D), w_map)],
            out_specs=pl.BlockSpec((BM, D), lambda i, be, nu, nv: (i, 0)),
            scratch_shapes=[pltpu.VMEM((D, Fd), BF16), pltpu.VMEM((D, Fd), BF16),
                            pltpu.VMEM((Fd, D), BF16)]),
        out_shape=jax.ShapeDtypeStruct((P, D), F32),
        compiler_params=_cparams(("arbitrary",)),
        name="moe_experts",
    )(block_e, n_used, n_valid, xs, w_gate, w_up, w_down)


def _combine_kernel(g_ref, w_ref, x1_ref, mod_ref, g_post_ref, o_ref):
    y = w_ref[:, 0:1] * g_ref[0] + w_ref[:, 1:2] * g_ref[1]
    o_ref[...] = x1_ref[...] + mod_ref[5:6, :] * _rms(y, g_post_ref[...])


def _combine(g, wts, x1, mod, g_post_ffn, S):
    T, D = x1.shape
    tm = TM_ROWS
    per_b = S // tm
    return pl.pallas_call(
        _combine_kernel,
        grid=(T // tm,),
        in_specs=[pl.BlockSpec((2, tm, D), lambda i: (0, i, 0)),
                  pl.BlockSpec((tm, 8), lambda i: (i, 0)),
                  pl.BlockSpec((tm, D), lambda i: (i, 0)),
                  pl.BlockSpec((None, 6, D), lambda i: (i // per_b, 0, 0)),
                  pl.BlockSpec((1, D), lambda i: (0, 0))],
        out_specs=pl.BlockSpec((tm, D), lambda i: (i, 0)),
        out_shape=jax.ShapeDtypeStruct((T, D), F32),
        compiler_params=_cparams(("arbitrary",)),
        name="moe_combine",
    )(g, wts, x1, mod, g_post_ffn)


def _rope_tables(S):
    inv = 1.0 / (10000.0 ** (jnp.arange(0, HEAD_DIM, 2, dtype=F32) / HEAD_DIM))
    ang = jnp.arange(S, dtype=F32)[:, None] * inv[None, :]
    c, s = jnp.cos(ang), jnp.sin(ang)
    return jnp.tile(c, (1, 4)), jnp.tile(jnp.concatenate([-s, s], axis=1), (1, 2))


def kernel(x, c, w_ada, b_ada, g_pre_attn, g_post_attn, w_in, b_forget, lambda_q1, lambda_k1,
           lambda_q2, lambda_k2, g_diff_subln, w_out, g_pre_ffn, g_post_ffn, w_group, b_group,
           w_router, b_router, w_gate, w_up, w_down):
    B, S, D = x.shape
    T = B * S
    l = 0
    mod = _ada(c, w_ada[l], b_ada[l]).reshape(B, 6, D)

    wi = w_in[l]
    w_main = jnp.concatenate([wi[:, 0:2 * SEC], wi[:, 3 * SEC:5 * SEC]], axis=1).astype(BF16)
    w_vt = jnp.concatenate([wi[:, 2 * SEC:3 * SEC], wi[:, 5 * SEC:6 * SEC]], axis=1).T.astype(BF16)
    wf_t = wi[:, 6 * SEC:].T.astype(BF16)
    bf = jnp.broadcast_to(b_forget[l][:, None], (FOX_HEADS, LANES))
    cos, sin = _rope_tables(S)
    dq, dk, fq, fk, dvt, fvt, fcum = _inproj(x, mod, g_pre_attn[l][None, :], w_main, w_vt, wf_t,
                                             bf, cos, sin)

    lam_params = jnp.stack([lambda_q1[l], lambda_k1[l], lambda_q2[l], lambda_k2[l]])
    d_out = _diff_attention(dq, dk, dvt, lam_params, g_diff_subln[l][None, :])
    frow = fcum.reshape(B, FOX_HEADS // 2, 2, S)
    fcol = jnp.transpose(frow, (0, 1, 3, 2))
    f_out = _fox_attention(fq, fk, fvt, fcol, frow)

    w_o = w_out[l].astype(BF16)
    zpad = jnp.zeros((8 - N_GROUPS, D), F32)
    w_rt = jnp.concatenate([w_group[l].T, zpad, w_router[l].T], axis=0)
    b_rt = jnp.concatenate([b_group[l], jnp.zeros((8 - N_GROUPS,), F32), b_router[l]])
    b_rt = jnp.broadcast_to(b_rt[:, None], (8 + N_EXPERTS, LANES))
    x1, h2, ri, rw, cnt = _route(d_out, f_out, w_o[:SEC], w_o[SEC:], x, mod,
                                 g_post_attn[l][None, :], g_pre_ffn[l][None, :], w_rt, b_rt)

    counts = cnt[:, 0].astype(jnp.int32)
    padded = ((counts + BM - 1) // BM) * BM
    pend = jnp.cumsum(padded)
    pstart = pend - padded
    eids = jnp.arange(N_EXPERTS, dtype=jnp.int32)
    dest = ri[2:4] + jnp.sum(jnp.where(ri[0:2, :, None] == eids, pstart, 0), axis=-1)
    nb = (2 * T) // BM + N_EXPERTS
    n_used = (pend[-1] // BM).astype(jnp.int32).reshape(1)
    blk0 = jnp.arange(nb, dtype=jnp.int32) * BM
    block_e = jnp.minimum(jnp.sum((blk0[:, None] >= pend[None, :]).astype(jnp.int32), axis=1),
                          N_EXPERTS - 1)
    cend = pstart + counts
    blk_end = jnp.sum(jnp.where(block_e[:, None] == eids, cend, 0), axis=-1)
    n_valid = jnp.clip(blk_end - blk0, 0, BM).astype(jnp.int32)

    xs = _dispatch(dest, h2.reshape(T, D), nb * BM)
    ys = _experts(block_e, n_used, n_valid, xs, w_gate[l], w_up[l], w_down[l])
    g = _gather_rows(ys, dest)
    out = _combine(g, rw.T, x1.reshape(T, D), mod, g_post_ffn[l][None, :], S)
    return out.reshape(B, S, D)
```

```python
import functools
import math

import jax
import jax.numpy as jnp
from jax import lax
from jax.experimental import pallas as pl
from jax.experimental.pallas import tpu as pltpu
from jax.experimental.pallas import tpu_sc as plsc

F32 = jnp.float32
BF16 = jnp.bfloat16

D_MODEL = 1024
HEAD_DIM = 64
CHUNK = 64
DIFF_HEADS = 4
FOX_HEADS = 8
SEC = 512
N_GROUPS = 4
EXPERTS_PER_GROUP = 8
N_EXPERTS = 32
D_EXPERT = 512
EPS = 1e-6
NEG_INF = -1e30
LOG2E = 1.4426950408889634
LAMBDA_INIT = 0.8 - 0.6 * math.exp(-0.3 * 0)

LANES = 128
TM_PROJ = 512
TQ = 512
TK = 512
HK = TK // 2
DV_ROWS = 2 * HEAD_DIM + 16
TM_ROUTE = 512
BM = 256
TM_ROWS = 256
SC_CORES = 2
SC_SUBCORES = 16
SC_WORKERS = SC_CORES * SC_SUBCORES
SC_WIN = 128
VMEM_LIMIT = 48 * 1024 * 1024
_NT = (((1,), (1,)), ((), ()))


def _cparams(sem, flags=None):
    return pltpu.CompilerParams(dimension_semantics=sem, vmem_limit_bytes=VMEM_LIMIT, flags=flags)


def _rms(x, g):
    return x * lax.rsqrt(jnp.mean(x * x, axis=-1, keepdims=True) + EPS) * g


def _pack_halves(x):
    m = x.shape[1] // 2
    bits = lax.bitcast_convert_type(x.astype(BF16).astype(F32), jnp.uint32)
    return (bits[:, :m] >> 16) | (bits[:, m:] & jnp.uint32(0xFFFF0000))


def _unpack_halves(w):
    lo = lax.bitcast_convert_type(w << 16, F32)
    hi = lax.bitcast_convert_type(w & jnp.uint32(0xFFFF0000), F32)
    return lo, hi


def _ada_kernel(c_ref, w_ref, b_ref, o_ref):
    c = c_ref[...]
    cs = c * jax.nn.sigmoid(c)
    o_ref[...] = jnp.dot(cs, w_ref[...], preferred_element_type=F32,
                         precision=lax.Precision.HIGHEST) + b_ref[...]


def _ada(c, w_ada, b_ada):
    B, D = c.shape
    N = w_ada.shape[1]
    tn = 1024
    return pl.pallas_call(
        _ada_kernel,
        grid=(N // tn,),
        in_specs=[pl.BlockSpec((B, D), lambda j: (0, 0)),
                  pl.BlockSpec((D, tn), lambda j: (0, j)),
                  pl.BlockSpec((1, tn), lambda j: (0, j))],
        out_specs=pl.BlockSpec((B, tn), lambda j: (0, j)),
        out_shape=jax.ShapeDtypeStruct((B, N), F32),
        compiler_params=_cparams(("arbitrary",)),
        name="ada_mod",
    )(c, w_ada, b_ada.reshape(1, N))


def _split3(x):
    hi = x.astype(BF16)
    r1 = x - hi.astype(F32)
    mid = r1.astype(BF16)
    lo = (r1 - mid.astype(F32)).astype(BF16)
    return hi, mid, lo


def _inproj_kernel(x_ref, mod_ref, g_ref, w_ref, wvt_ref, wf_ref, bf_ref, cos_ref, sin_ref,
                   dq_ref, dk_ref, fq_ref, fk_ref, dvt_ref, fvt_ref, fcum_ref, carry_ref):
    si = pl.program_id(1)
    x = x_ref[...]
    h = _rms(x, g_ref[...]) * (1.0 + mod_ref[1:2, :]) + mod_ref[0:1, :]
    hb = h.astype(BF16)
    tm = x.shape[0]

    cos = cos_ref[...]
    sin = sin_ref[...]
    lane = lax.broadcasted_iota(jnp.int32, (tm, LANES), 1)
    first_half = (lane % HEAD_DIM) < (HEAD_DIM // 2)
    qscale = HEAD_DIM ** -0.5 * LOG2E

    def proj(sec):
        return jnp.dot(hb, w_ref[:, sec * SEC:(sec + 1) * SEC], preferred_element_type=F32)

    def rope_store(p, out_ref, scale):
        for j in range(SEC // LANES):
            ch = p[:, j * LANES:(j + 1) * LANES]
            partner = jnp.where(first_half,
                                pltpu.roll(ch, LANES - HEAD_DIM // 2, 1),
                                pltpu.roll(ch, HEAD_DIM // 2, 1))
            r = ch * cos + partner * sin
            if scale != 1.0:
                r = r * scale
            out_ref[:, j * LANES:(j + 1) * LANES] = r.astype(out_ref.dtype)

    rope_store(proj(0), dq_ref, qscale)
    rope_store(proj(1), dk_ref, 1.0)
    fq_ref[...] = (proj(2) * qscale).astype(fq_ref.dtype)
    fk_ref[...] = proj(3).astype(fk_ref.dtype)
    vt = lax.dot_general(wvt_ref[...], hb, _NT, preferred_element_type=F32)
    for hd in range(DIFF_HEADS):
        dvt_ref[hd, 0:LANES, :] = vt[hd * LANES:(hd + 1) * LANES].astype(dvt_ref.dtype)
        dvt_ref[hd, LANES:DV_ROWS, :] = jnp.ones((DV_ROWS - LANES, tm), dvt_ref.dtype)
    fvt_ref[...] = vt[SEC:2 * SEC].astype(fvt_ref.dtype)

    z = lax.dot_general(wf_ref[...], hb, _NT, preferred_element_type=F32) + bf_ref[:, 0:1]
    lf = (jnp.minimum(z, 0.0) - jnp.log1p(jnp.exp(-jnp.abs(z)))) * LOG2E

    @pl.when(si == 0)
    def _():
        carry_ref[...] = jnp.zeros_like(carry_ref)

    r_i = lax.broadcasted_iota(jnp.int32, (LANES, LANES), 0)
    c_i = lax.broadcasted_iota(jnp.int32, (LANES, LANES), 1)
    tri = jnp.where(r_i <= c_i, 1.0, 0.0).astype(BF16)
    run = carry_ref[...]
    for j in range(tm // LANES):
        hi, mid, lo = _split3(lf[:, j * LANES:(j + 1) * LANES])
        cs = (jnp.dot(hi, tri, preferred_element_type=F32)
              + jnp.dot(mid, tri, preferred_element_type=F32)
              + jnp.dot(lo, tri, preferred_element_type=F32)) + run
        fcum_ref[:, j * LANES:(j + 1) * LANES] = cs
        run = jnp.broadcast_to(cs[:, LANES - 1:LANES], run.shape)
    carry_ref[...] = run


def _inproj(x, mod, g_pre, w_main, w_vt, wf_t, bf, cos, sin):
    B, S, D = x.shape
    tm = TM_PROJ
    sec_out = jax.ShapeDtypeStruct((B, S, SEC), BF16)
    sec_spec = pl.BlockSpec((None, tm, SEC), lambda b, s: (b, s, 0))
    t_out = jax.ShapeDtypeStruct((B, SEC, S), BF16)
    t_spec = pl.BlockSpec((None, SEC, tm), lambda b, s: (b, 0, s))
    return pl.pallas_call(
        _inproj_kernel,
        grid=(B, S // tm),
        in_specs=[pl.BlockSpec((None, tm, D), lambda b, s: (b, s, 0)),
                  pl.BlockSpec((None, 6, D), lambda b, s: (b, 0, 0)),
                  pl.BlockSpec((1, D), lambda b, s: (0, 0)),
                  pl.BlockSpec((D, 4 * SEC), lambda b, s: (0, 0)),
                  pl.BlockSpec((2 * SEC, D), lambda b, s: (0, 0)),
                  pl.BlockSpec((FOX_HEADS, D), lambda b, s: (0, 0)),
                  pl.BlockSpec((FOX_HEADS, LANES), lambda b, s: (0, 0)),
                  pl.BlockSpec((tm, LANES), lambda b, s: (s, 0)),
                  pl.BlockSpec((tm, LANES), lambda b, s: (s, 0))],
        out_specs=[sec_spec] * 4
        + [pl.BlockSpec((None, DIFF_HEADS, DV_ROWS, tm), lambda b, s: (b, 0, 0, s)), t_spec,
           pl.BlockSpec((None, FOX_HEADS, tm), lambda b, s: (b, 0, s))],
        out_shape=[sec_out] * 4
        + [jax.ShapeDtypeStruct((B, DIFF_HEADS, DV_ROWS, S), BF16), t_out,
           jax.ShapeDtypeStruct((B, FOX_HEADS, S), F32)],
        scratch_shapes=[pltpu.VMEM((FOX_HEADS, LANES), F32)],
        compiler_params=_cparams(("arbitrary", "arbitrary")),
        name="in_proj",
    )(x, mod, g_pre, w_main, w_vt, wf_t, bf, cos, sin)


def _flash_t(chains, k_ref, sa_ref, qi, diag_mask):
    def scores(ch, r0):
        r0 = pl.multiple_of(r0, HK)
        t = lax.dot_general(k_ref[pl.ds(r0, HK), :], ch["qm"], _NT, preferred_element_type=F32)
        if ch["kbias"] is not None:
            t = t - ch["kbias"](r0)
        return t

    def update(ch, state, t, r0):
        m, acc = state
        mt = jnp.max(t, axis=0, keepdims=True)
        qb = ch["qbias"]
        m_new = jnp.maximum(m, mt if qb is None else mt + qb)
        alpha = jnp.exp2(m - m_new)
        p = jnp.exp2((t - (m_new if qb is None else m_new - qb)).astype(BF16))
        vt = ch["vt"](pl.multiple_of(r0, HK))
        acc = alpha * acc + jnp.dot(vt, p, preferred_element_type=F32)
        return m_new, acc

    for ci, ch in enumerate(chains):
        sa_ref[ci] = scores(ch, 0)

    def body(ki, carry):
        k0 = ki * TK
        tb = [scores(ch, k0 + HK) for ch in chains]
        st = [update(ch, s, sa_ref[ci], k0) for ci, (ch, s) in enumerate(zip(chains, carry))]
        ta = [scores(ch, k0 + TK) for ch in chains]
        st = [update(ch, s, tb[ci], k0 + HK) for ci, (ch, s) in enumerate(zip(chains, st))]
        for ci in range(len(chains)):
            sa_ref[ci] = ta[ci]
        return tuple(st)

    init = tuple((jnp.full((1, TQ), NEG_INF, F32), jnp.zeros((ch["rows"], TQ), F32))
                 for ch in chains)
    carry = lax.fori_loop(0, qi, body, init)
    k0 = qi * TK
    tb = [scores(ch, k0 + HK) for ch in chains]
    st = [update(ch, s, jnp.where(diag_mask[0:HK], sa_ref[ci], NEG_INF), k0)
          for ci, (ch, s) in enumerate(zip(chains, carry))]
    st = [update(ch, s, jnp.where(diag_mask[HK:TK], tb[ci], NEG_INF), k0 + HK)
          for ci, (ch, s) in enumerate(zip(chains, st))]
    return tuple(st)


def _rows_to_columns(x):
    pad = jnp.zeros((LANES - x.shape[0], LANES), x.dtype)
    return jnp.concatenate([x, pad], axis=0).T


def _fox_kernel(q_ref, k_ref, vt_ref, frow_ref, o_ref, sa_ref, fcol_ref):
    qi = pl.program_id(2)

    @pl.when(qi == 0)
    def _():
        for j in range(frow_ref.shape[1] // LANES):
            fcol_ref[j * LANES:(j + 1) * LANES, :] = _rows_to_columns(
                frow_ref[:, j * LANES:(j + 1) * LANES])

    q2 = q_ref[...]
    lane = lax.broadcasted_iota(jnp.int32, q2.shape, 1)
    vrow = lax.broadcasted_iota(jnp.int32, (LANES, HK), 0)
    r = lax.broadcasted_iota(jnp.int32, (TK, TQ), 0)
    c = lax.broadcasted_iota(jnp.int32, (TK, TQ), 1)
    chains = []
    for j in range(2):
        in_head = (lane >= j * HEAD_DIM) & (lane < (j + 1) * HEAD_DIM)
        v_rows = (vrow >= j * HEAD_DIM) & (vrow < (j + 1) * HEAD_DIM)

        def vt(k0, v_rows=v_rows):
            v = vt_ref[:, pl.ds(k0, HK)]
            return jnp.where(v_rows, v, jnp.ones_like(v))

        chains.append(dict(
            qm=jnp.where(in_head, q2, jnp.zeros_like(q2)),
            kbias=lambda k0, j=j: fcol_ref[pl.ds(k0, HK), j:j + 1],
            qbias=frow_ref[j:j + 1, pl.ds(pl.multiple_of(qi * TQ, TQ), TQ)],
            vt=vt, rows=LANES))
    (_, a0), (_, a1) = _flash_t(chains, k_ref, sa_ref, qi, r <= c)
    h = HEAD_DIM
    o_t = jnp.concatenate([a0[0:h] / a0[h:2 * h], a1[h:2 * h] / a1[0:h]], axis=0)
    o_ref[...] = o_t.T.astype(o_ref.dtype)


def _fox_attention(fq, fk, fvt, frow):
    B, S, _ = fq.shape
    npair = FOX_HEADS // 2
    return pl.pallas_call(
        _fox_kernel,
        grid=(B, npair, S // TQ),
        in_specs=[pl.BlockSpec((None, TQ, LANES), lambda b, p, q: (b, q, p)),
                  pl.BlockSpec((None, S, LANES), lambda b, p, q: (b, 0, p)),
                  pl.BlockSpec((None, LANES, S), lambda b, p, q: (b, p, 0)),
                  pl.BlockSpec((None, None, 2, S), lambda b, p, q: (b, p, 0, 0))],
        out_specs=pl.BlockSpec((None, TQ, LANES), lambda b, p, q: (b, q, p)),
        out_shape=jax.ShapeDtypeStruct((B, S, SEC), BF16),
        scratch_shapes=[pltpu.VMEM((2, HK, TQ), F32), pltpu.VMEM((S, LANES), F32)],
        compiler_params=_cparams(("arbitrary", "arbitrary", "arbitrary")),
        name="fox_attention",
    )(fq, fk, fvt, frow)


def _diff_kernel(q_ref, k_ref, vt_ref, lam_ref, g_ref, o_ref, sa_ref):
    qi = pl.program_id(2)
    q2 = q_ref[...]
    lane = lax.broadcasted_iota(jnp.int32, q2.shape, 1)
    r = lax.broadcasted_iota(jnp.int32, (TK, TQ), 0)
    c = lax.broadcasted_iota(jnp.int32, (TK, TQ), 1)
    chains = []
    for j in range(2):
        in_map = (lane >= j * HEAD_DIM) & (lane < (j + 1) * HEAD_DIM)
        chains.append(dict(
            qm=jnp.where(in_map, q2, jnp.zeros_like(q2)),
            kbias=None, qbias=None,
            vt=lambda k0: vt_ref[:, pl.ds(k0, HK)], rows=DV_ROWS))
    (_, a0), (_, a1) = _flash_t(chains, k_ref, sa_ref, qi, (r // CHUNK) <= (c // CHUNK))
    lp = lam_ref[...]
    lam = (jnp.exp(jnp.sum(lp[0:1] * lp[1:2], axis=1, keepdims=True))
           - jnp.exp(jnp.sum(lp[2:3] * lp[3:4], axis=1, keepdims=True)) + LAMBDA_INIT)
    v = LANES
    o = (a0[0:v] / a0[v:v + 1] - lam * (a1[0:v] / a1[v:v + 1])).T
    o_ref[...] = (_rms(o, g_ref[...]) * (1.0 - LAMBDA_INIT)).astype(o_ref.dtype)


def _diff_attention(dq, dk, dvt, lam_params, g_subln):
    B, S, _ = dq.shape
    return pl.pallas_call(
        _diff_kernel,
        grid=(B, DIFF_HEADS, S // TQ),
        in_specs=[pl.BlockSpec((None, TQ, LANES), lambda b, h, q: (b, q, h)),
                  pl.BlockSpec((None, S, LANES), lambda b, h, q: (b, 0, h)),
                  pl.BlockSpec((None, None, DV_ROWS, S), lambda b, h, q: (b, h, 0, 0)),
                  pl.BlockSpec((4, HEAD_DIM), lambda b, h, q: (0, 0)),
                  pl.BlockSpec((1, LANES), lambda b, h, q: (0, 0))],
        out_specs=pl.BlockSpec((None, TQ, LANES), lambda b, h, q: (b, q, h)),
        out_shape=jax.ShapeDtypeStruct((B, S, SEC), BF16),
        scratch_shapes=[pltpu.VMEM((2, HK, TQ), F32)],
        compiler_params=_cparams(("arbitrary", "arbitrary", "arbitrary")),
        name="diff_attention",
    )(dq, dk, dvt, lam_params, g_subln)


def _route_kernel(d_ref, f_ref, wd_ref, wf_ref, x_ref, mod_ref, gpa_ref, gpf_ref, wr_ref, br_ref,
                  x1_ref, h2_ref, ri_ref, rw_ref, cnt_ref):
    first = (pl.program_id(0) == 0) & (pl.program_id(1) == 0)

    @pl.when(first)
    def _():
        cnt_ref[...] = jnp.zeros_like(cnt_ref)

    mixed = (jnp.dot(d_ref[...], wd_ref[...], preferred_element_type=F32)
             + jnp.dot(f_ref[...], wf_ref[...], preferred_element_type=F32))
    x1 = x_ref[...] + mod_ref[2:3, :] * _rms(mixed, gpa_ref[...])
    x1_ref[...] = x1
    h2 = _rms(x1, gpf_ref[...]) * (1.0 + mod_ref[4:5, :]) + mod_ref[3:4, :]
    h2_ref[...] = _pack_halves(h2)
    tm = h2.shape[0]

    hh, hm, _ = _split3(h2)
    wh, wm, _ = _split3(wr_ref[...])
    lt = br_ref[:, 0:1]
    for a, b_ in ((wh, hh), (wh, hm), (wm, hh)):
        lt = lt + lax.dot_general(a, b_, _NT, preferred_element_type=F32)

    g = lt[0:N_GROUPS]
    row4 = lax.broadcasted_iota(jnp.int32, g.shape, 0)
    gmax = jnp.max(g, axis=0, keepdims=True)
    gidx = jnp.min(jnp.where(g == gmax, row4, N_GROUPS), axis=0, keepdims=True)
    gw = 1.0 / jnp.sum(jnp.exp(g - gmax), axis=0, keepdims=True)
    esel = jnp.zeros((EXPERTS_PER_GROUP, tm), F32)
    for gg in range(N_GROUPS):
        esel = jnp.where(gidx == gg, lt[8 + 8 * gg:16 + 8 * gg], esel)
    row8 = lax.broadcasted_iota(jnp.int32, esel.shape, 0)
    v1 = jnp.max(esel, axis=0, keepdims=True)
    i1 = jnp.min(jnp.where(esel == v1, row8, EXPERTS_PER_GROUP), axis=0, keepdims=True)
    esel2 = jnp.where(row8 == i1, -jnp.inf, esel)
    v2 = jnp.max(esel2, axis=0, keepdims=True)
    i2 = jnp.min(jnp.where(esel2 == v2, row8, EXPERTS_PER_GROUP), axis=0, keepdims=True)
    e = jnp.exp(v2 - v1)
    w1 = gw / (1.0 + e)
    w2 = gw * e / (1.0 + e)
    e1 = gidx * EXPERTS_PER_GROUP + i1
    e2 = gidx * EXPERTS_PER_GROUP + i2

    row32 = lax.broadcasted_iota(jnp.int32, (N_EXPERTS, tm), 0)
    oh1 = row32 == e1
    oh2 = row32 == e2
    oh = jnp.where(oh1 | oh2, 1.0, 0.0)
    t_r = lax.broadcasted_iota(jnp.int32, (tm, tm), 0)
    t_c = lax.broadcasted_iota(jnp.int32, (tm, tm), 1)
    tri = jnp.where(t_r < t_c, 1.0, 0.0).astype(BF16)
    tot = jnp.dot(oh.astype(BF16), tri, preferred_element_type=F32) + cnt_ref[:, 0:1]
    r1 = jnp.sum(jnp.where(oh1, tot, 0.0), axis=0, keepdims=True)
    r2 = jnp.sum(jnp.where(oh2, tot, 0.0), axis=0, keepdims=True)
    cnt_ref[...] = cnt_ref[...] + jnp.sum(oh, axis=1, keepdims=True)

    zi = jnp.zeros((4, tm), jnp.int32)
    ri_ref[...] = jnp.concatenate([e1, e2, r1.astype(jnp.int32), r2.astype(jnp.int32), zi], axis=0)
    rw_ref[...] = jnp.concatenate([w1, w2, jnp.zeros((6, tm), F32)], axis=0)


def _route(d_out, f_out, w_out_d, w_out_f, x, mod, g_post_attn, g_pre_ffn, w_rt, b_rt):
    B, S, D = x.shape
    tm = TM_ROUTE
    nrt = w_rt.shape[0]
    n_s = S // tm
    T = B * S
    return pl.pallas_call(
        _route_kernel,
        grid=(B, n_s),
        in_specs=[pl.BlockSpec((None, tm, SEC), lambda b, s: (b, s, 0)),
                  pl.BlockSpec((None, tm, SEC), lambda b, s: (b, s, 0)),
                  pl.BlockSpec((SEC, D), lambda b, s: (0, 0)),
                  pl.BlockSpec((SEC, D), lambda b, s: (0, 0)),
                  pl.BlockSpec((None, tm, D), lambda b, s: (b, s, 0)),
                  pl.BlockSpec((None, 6, D), lambda b, s: (b, 0, 0)),
                  pl.BlockSpec((1, D), lambda b, s: (0, 0)),
                  pl.BlockSpec((1, D), lambda b, s: (0, 0)),
                  pl.BlockSpec((nrt, D), lambda b, s: (0, 0)),
                  pl.BlockSpec((nrt, LANES), lambda b, s: (0, 0))],
        out_specs=[pl.BlockSpec((None, tm, D), lambda b, s: (b, s, 0)),
                   pl.BlockSpec((None, tm, D // 2), lambda b, s: (b, s, 0)),
                   pl.BlockSpec((8, tm), lambda b, s: (0, b * n_s + s)),
                   pl.BlockSpec((8, tm), lambda b, s: (0, b * n_s + s)),
                   pl.BlockSpec((N_EXPERTS, LANES), lambda b, s: (0, 0))],
        out_shape=[jax.ShapeDtypeStruct((B, S, D), F32),
                   jax.ShapeDtypeStruct((B, S, D // 2), jnp.uint32),
                   jax.ShapeDtypeStruct((8, T), jnp.int32),
                   jax.ShapeDtypeStruct((8, T), F32),
                   jax.ShapeDtypeStruct((N_EXPERTS, LANES), F32)],
        compiler_params=_cparams(("arbitrary", "arbitrary")),
        name="out_proj_route",
    )(d_out, f_out, w_out_d, w_out_f, x, mod, g_post_attn, g_pre_ffn, w_rt, b_rt)


def _sc_mesh():
    return plsc.VectorSubcoreMesh(core_axis_name="c", subcore_axis_name="s",
                                  num_cores=SC_CORES, num_subcores=SC_SUBCORES)


def _sc_worker_base(per_worker):
    return (lax.axis_index("s") * SC_CORES + lax.axis_index("c")) * per_worker


def _dispatch(dest, h2, n_rows):
    T, D = h2.shape
    per_worker = T // SC_WORKERS

    @functools.partial(
        pl.kernel, mesh=_sc_mesh(),
        out_type=jax.ShapeDtypeStruct((n_rows, D), h2.dtype),
        scratch_types=[pltpu.VMEM((SC_WIN,), jnp.int32), pltpu.VMEM((SC_WIN, D), h2.dtype)],
        name="moe_dispatch")
    def run(h_hbm, dest_hbm, xs_hbm, idx_v, rows_v):
        base = _sc_worker_base(per_worker)

        @pl.loop(0, per_worker // SC_WIN)
        def _(ci):
            off = pl.multiple_of(base + ci * SC_WIN, SC_WIN)
            pltpu.sync_copy(h_hbm.at[pl.ds(off, SC_WIN)], rows_v)
            for k in range(2):
                pltpu.sync_copy(dest_hbm.at[pl.ds(k * T + off, SC_WIN)], idx_v)
                pltpu.sync_copy(rows_v, xs_hbm.at[idx_v])

    return run(h2, dest.reshape(2 * T))


def _gather_rows(ys, dest):
    _, D = ys.shape
    K, T = dest.shape
    n = K * T
    per_worker = n // SC_WORKERS

    @functools.partial(
        pl.kernel, mesh=_sc_mesh(),
        out_type=jax.ShapeDtypeStruct((n, D), ys.dtype),
        scratch_types=[pltpu.VMEM((SC_WIN,), jnp.int32), pltpu.VMEM((SC_WIN, D), ys.dtype)],
        name="moe_gather")
    def run(ys_hbm, dest_hbm, g_hbm, idx_v, rows_v):
        base = _sc_worker_base(per_worker)

        @pl.loop(0, per_worker // SC_WIN)
        def _(ci):
            off = pl.multiple_of(base + ci * SC_WIN, SC_WIN)
            pltpu.sync_copy(dest_hbm.at[pl.ds(off, SC_WIN)], idx_v)
            pltpu.sync_copy(ys_hbm.at[idx_v], rows_v)
            pltpu.sync_copy(rows_v, g_hbm.at[pl.ds(off, SC_WIN)])

    return run(ys, dest.reshape(n)).reshape(K, T, D)


def _expert_kernel(be_ref, nu_ref, nv_ref, xs_ref, wg_ref, wu_ref, wd_ref, ys_ref,
                   wgb, wub, wdb):
    i = pl.program_id(0)

    @pl.when(i < nu_ref[0])
    def _():
        prev = be_ref[jnp.maximum(i - 1, 0)]

        @pl.when((i == 0) | (be_ref[i] != prev))
        def _():
            wgb[...] = wg_ref[...].astype(BF16)
            wub[...] = wu_ref[...].astype(BF16)
            wdb[...] = wd_ref[...].astype(BF16)

        row = lax.broadcasted_iota(jnp.int32, xs_ref.shape, 0)
        lo, hi = _unpack_halves(jnp.where(row < nv_ref[i], xs_ref[...], jnp.uint32(0)))
        lo, hi = lo.astype(BF16), hi.astype(BF16)
        half = lo.shape[1]

        def proj(w):
            return (jnp.dot(lo, w[0:half], preferred_element_type=F32)
                    + jnp.dot(hi, w[half:2 * half], preferred_element_type=F32))

        a = proj(wgb)
        u = proj(wub)
        hmid = (a * jax.nn.sigmoid(a) * u).astype(BF16)
        ys_ref[...] = _pack_halves(jnp.dot(hmid, wdb[...], preferred_element_type=F32))

    @pl.when(i >= nu_ref[0])
    def _():
        ys_ref[...] = jnp.zeros_like(ys_ref)


def _experts(block_e, n_used, n_valid, xs, w_gate, w_up, w_down):
    P, Dh = xs.shape
    nb = P // BM
    _, D, Fd = w_gate.shape

    def row_map(i, be, nu, nv):
        return (jnp.minimum(i, nu[0] - 1), 0)

    def w_map(i, be, nu, nv):
        return (be[i], 0, 0)

    return pl.pallas_call(
        _expert_kernel,
        grid_spec=pltpu.PrefetchScalarGridSpec(
            num_scalar_prefetch=3,
            grid=(nb,),
            in_specs=[pl.BlockSpec((BM, Dh), row_map),
                      pl.BlockSpec((None, D, Fd), w_map),
                      pl.BlockSpec((None, D, Fd), w_map),
                      pl.BlockSpec((None, Fd, D), w_map)],
            out_specs=pl.BlockSpec((BM, Dh), lambda i, be, nu, nv: (i, 0)),
            scratch_shapes=[pltpu.VMEM((D, Fd), BF16), pltpu.VMEM((D, Fd), BF16),
                            pltpu.VMEM((Fd, D), BF16)]),
        out_shape=jax.ShapeDtypeStruct((P, Dh), jnp.uint32),
        compiler_params=_cparams(("arbitrary",)),
        name="moe_experts",
    )(block_e, n_used, n_valid, xs, w_gate, w_up, w_down)


def _combine_kernel(g_ref, w_ref, x1_ref, mod_ref, g_post_ref, o_ref):
    lo0, hi0 = _unpack_halves(g_ref[0])
    lo1, hi1 = _unpack_halves(g_ref[1])
    tm = lo0.shape[0]
    wt = jnp.concatenate([_rows_to_columns(w_ref[:, j * LANES:(j + 1) * LANES])
                          for j in range(tm // LANES)], axis=0)
    w0, w1 = wt[:, 0:1], wt[:, 1:2]
    y = jnp.concatenate([w0 * lo0 + w1 * lo1, w0 * hi0 + w1 * hi1], axis=1)
    o_ref[...] = x1_ref[...] + mod_ref[5:6, :] * _rms(y, g_post_ref[...])


def _combine(g, wts, x1, mod, g_post_ffn, S):
    T, D = x1.shape
    tm = TM_ROWS
    per_b = S // tm
    return pl.pallas_call(
        _combine_kernel,
        grid=(T // tm,),
        in_specs=[pl.BlockSpec((2, tm, D // 2), lambda i: (0, i, 0)),
                  pl.BlockSpec((8, tm), lambda i: (0, i)),
                  pl.BlockSpec((tm, D), lambda i: (i, 0)),
                  pl.BlockSpec((None, 6, D), lambda i: (i // per_b, 0, 0)),
                  pl.BlockSpec((1, D), lambda i: (0, 0))],
        out_specs=pl.BlockSpec((tm, D), lambda i: (i, 0)),
        out_shape=jax.ShapeDtypeStruct((T, D), F32),
        compiler_params=_cparams(("arbitrary",)),
        name="moe_combine",
    )(g, wts, x1, mod, g_post_ffn)


def _rope_tables(S):
    inv = 1.0 / (10000.0 ** (jnp.arange(0, HEAD_DIM, 2, dtype=F32) / HEAD_DIM))
    ang = jnp.arange(S, dtype=F32)[:, None] * inv[None, :]
    c, s = jnp.cos(ang), jnp.sin(ang)
    return jnp.tile(c, (1, 4)), jnp.tile(jnp.concatenate([-s, s], axis=1), (1, 2))


def kernel(x, c, w_ada, b_ada, g_pre_attn, g_post_attn, w_in, b_forget, lambda_q1, lambda_k1,
           lambda_q2, lambda_k2, g_diff_subln, w_out, g_pre_ffn, g_post_ffn, w_group, b_group,
           w_router, b_router, w_gate, w_up, w_down):
    B, S, D = x.shape
    T = B * S
    l = 0
    mod = _ada(c, w_ada[l], b_ada[l]).reshape(B, 6, D)

    wi = w_in[l]
    w_main = jnp.concatenate([wi[:, 0:2 * SEC], wi[:, 3 * SEC:5 * SEC]], axis=1).astype(BF16)
    w_vt = jnp.concatenate([wi[:, 2 * SEC:3 * SEC], wi[:, 5 * SEC:6 * SEC]], axis=1).T.astype(BF16)
    wf_t = wi[:, 6 * SEC:].T.astype(BF16)
    bf = jnp.broadcast_to(b_forget[l][:, None], (FOX_HEADS, LANES))
    cos, sin = _rope_tables(S)
    dq, dk, fq, fk, dvt, fvt, fcum = _inproj(x, mod, g_pre_attn[l][None, :], w_main, w_vt, wf_t,
                                             bf, cos, sin)

    lam_params = jnp.stack([lambda_q1[l], lambda_k1[l], lambda_q2[l], lambda_k2[l]])
    d_out = _diff_attention(dq, dk, dvt, lam_params, g_diff_subln[l][None, :])
    frow = fcum.reshape(B, FOX_HEADS // 2, 2, S)
    f_out = _fox_attention(fq, fk, fvt, frow)

    w_o = w_out[l].astype(BF16)
    zpad = jnp.zeros((8 - N_GROUPS, D), F32)
    w_rt = jnp.concatenate([w_group[l].T, zpad, w_router[l].T], axis=0)
    b_rt = jnp.concatenate([b_group[l], jnp.zeros((8 - N_GROUPS,), F32), b_router[l]])
    b_rt = jnp.broadcast_to(b_rt[:, None], (8 + N_EXPERTS, LANES))
    x1, h2, ri, rw, cnt = _route(d_out, f_out, w_o[:SEC], w_o[SEC:], x, mod,
                                 g_post_attn[l][None, :], g_pre_ffn[l][None, :], w_rt, b_rt)

    counts = cnt[:, 0].astype(jnp.int32)
    padded = ((counts + BM - 1) // BM) * BM
    pend = jnp.cumsum(padded)
    pstart = pend - padded
    eids = jnp.arange(N_EXPERTS, dtype=jnp.int32)
    dest = ri[2:4] + jnp.sum(jnp.where(ri[0:2, :, None] == eids, pstart, 0), axis=-1)
    nb = (2 * T) // BM + N_EXPERTS
    n_used = (pend[-1] // BM).astype(jnp.int32).reshape(1)
    blk0 = jnp.arange(nb, dtype=jnp.int32) * BM
    block_e = jnp.minimum(jnp.sum((blk0[:, None] >= pend[None, :]).astype(jnp.int32), axis=1),
                          N_EXPERTS - 1)
    cend = pstart + counts
    blk_end = jnp.sum(jnp.where(block_e[:, None] == eids, cend, 0), axis=-1)
    n_valid = jnp.clip(blk_end - blk0, 0, BM).astype(jnp.int32)

    xs = _dispatch(dest, h2.reshape(T, D // 2), nb * BM)
    ys = _experts(block_e, n_used, n_valid, xs, w_gate[l], w_up[l], w_down[l])
    g = _gather_rows(ys, dest)
    out = _combine(g, rw, x1.reshape(T, D), mod, g_post_ffn[l][None, :], S)
    return out.reshape(B, S, D)
```

```python
import functools
import math

import jax
import jax.numpy as jnp
from jax import lax
from jax.experimental import pallas as pl
from jax.experimental.pallas import tpu as pltpu
from jax.experimental.pallas import tpu_sc as plsc

F32 = jnp.float32
BF16 = jnp.bfloat16

D_MODEL = 1024
HEAD_DIM = 64
CHUNK = 64
DIFF_HEADS = 4
FOX_HEADS = 8
SEC = 512
N_GROUPS = 4
EXPERTS_PER_GROUP = 8
N_EXPERTS = 32
D_EXPERT = 512
EPS = 1e-6
NEG_INF = -1e30
LOG2E = 1.4426950408889634
LAMBDA_INIT = 0.8 - 0.6 * math.exp(-0.3 * 0)

LANES = 128
TM_PROJ = 512
TQ = 512
TK = 512
HK = TK // 2
DV_ROWS = 2 * HEAD_DIM + 16
TM_ROUTE = 512
BM = 256
EXPERT_COL_GROUPS = 2
TM_ROWS = 256
SC_CORES = 2
SC_SUBCORES = 16
SC_WORKERS = SC_CORES * SC_SUBCORES
SC_WIN = 128
VMEM_LIMIT = 48 * 1024 * 1024
_NT = (((1,), (1,)), ((), ()))


def _cparams(sem, flags=None):
    return pltpu.CompilerParams(dimension_semantics=sem, vmem_limit_bytes=VMEM_LIMIT, flags=flags)


def _rms(x, g):
    return x * lax.rsqrt(jnp.mean(x * x, axis=-1, keepdims=True) + EPS) * g


def _pack_halves(x):
    m = x.shape[1] // 2
    bits = lax.bitcast_convert_type(x.astype(BF16).astype(F32), jnp.uint32)
    return (bits[:, :m] >> 16) | (bits[:, m:] & jnp.uint32(0xFFFF0000))


def _unpack_halves(w):
    lo = lax.bitcast_convert_type(w << 16, F32)
    hi = lax.bitcast_convert_type(w & jnp.uint32(0xFFFF0000), F32)
    return lo, hi


def _ada_kernel(c_ref, w_ref, b_ref, o_ref):
    c = c_ref[...]
    cs = c * jax.nn.sigmoid(c)
    o_ref[...] = jnp.dot(cs, w_ref[...], preferred_element_type=F32,
                         precision=lax.Precision.HIGHEST) + b_ref[...]


def _ada(c, w_ada, b_ada):
    B, D = c.shape
    N = w_ada.shape[1]
    tn = 1024
    return pl.pallas_call(
        _ada_kernel,
        grid=(N // tn,),
        in_specs=[pl.BlockSpec((B, D), lambda j: (0, 0)),
                  pl.BlockSpec((D, tn), lambda j: (0, j)),
                  pl.BlockSpec((1, tn), lambda j: (0, j))],
        out_specs=pl.BlockSpec((B, tn), lambda j: (0, j)),
        out_shape=jax.ShapeDtypeStruct((B, N), F32),
        compiler_params=_cparams(("arbitrary",)),
        name="ada_mod",
    )(c, w_ada, b_ada.reshape(1, N))


def _split3(x):
    hi = x.astype(BF16)
    r1 = x - hi.astype(F32)
    mid = r1.astype(BF16)
    lo = (r1 - mid.astype(F32)).astype(BF16)
    return hi, mid, lo


def _inproj_kernel(x_ref, mod_ref, g_ref, w_ref, wvt_ref, wf_ref, bf_ref, cos_ref, sin_ref,
                   dq_ref, dk_ref, fq_ref, fk_ref, dvt_ref, fvt_ref, fcum_ref, carry_ref):
    si = pl.program_id(1)
    x = x_ref[...]
    h = _rms(x, g_ref[...]) * (1.0 + mod_ref[1:2, :]) + mod_ref[0:1, :]
    hb = h.astype(BF16)
    tm = x.shape[0]

    cos = cos_ref[...]
    sin = sin_ref[...]
    lane = lax.broadcasted_iota(jnp.int32, (tm, LANES), 1)
    first_half = (lane % HEAD_DIM) < (HEAD_DIM // 2)
    qscale = HEAD_DIM ** -0.5 * LOG2E

    def proj(sec):
        return jnp.dot(hb, w_ref[:, sec * SEC:(sec + 1) * SEC], preferred_element_type=F32)

    def rope_store(p, out_ref, scale):
        for j in range(SEC // LANES):
            ch = p[:, j * LANES:(j + 1) * LANES]
            partner = jnp.where(first_half,
                                pltpu.roll(ch, LANES - HEAD_DIM // 2, 1),
                                pltpu.roll(ch, HEAD_DIM // 2, 1))
            r = ch * cos + partner * sin
            if scale != 1.0:
                r = r * scale
            out_ref[:, j * LANES:(j + 1) * LANES] = r.astype(out_ref.dtype)

    rope_store(proj(0), dq_ref, qscale)
    rope_store(proj(1), dk_ref, 1.0)
    fq_ref[...] = (proj(2) * qscale).astype(fq_ref.dtype)
    fk_ref[...] = proj(3).astype(fk_ref.dtype)
    vt = lax.dot_general(wvt_ref[...], hb, _NT, preferred_element_type=F32)
    for hd in range(DIFF_HEADS):
        dvt_ref[hd, 0:LANES, :] = vt[hd * LANES:(hd + 1) * LANES].astype(dvt_ref.dtype)
        dvt_ref[hd, LANES:DV_ROWS, :] = jnp.ones((DV_ROWS - LANES, tm), dvt_ref.dtype)
    fvt_ref[...] = vt[SEC:2 * SEC].astype(fvt_ref.dtype)

    z = lax.dot_general(wf_ref[...], hb, _NT, preferred_element_type=F32) + bf_ref[:, 0:1]
    lf = (jnp.minimum(z, 0.0) - jnp.log1p(jnp.exp(-jnp.abs(z)))) * LOG2E

    @pl.when(si == 0)
    def _():
        carry_ref[...] = jnp.zeros_like(carry_ref)

    r_i = lax.broadcasted_iota(jnp.int32, (LANES, LANES), 0)
    c_i = lax.broadcasted_iota(jnp.int32, (LANES, LANES), 1)
    tri = jnp.where(r_i <= c_i, 1.0, 0.0).astype(BF16)
    run = carry_ref[...]
    for j in range(tm // LANES):
        hi, mid, lo = _split3(lf[:, j * LANES:(j + 1) * LANES])
        cs = (jnp.dot(hi, tri, preferred_element_type=F32)
              + jnp.dot(mid, tri, preferred_element_type=F32)
              + jnp.dot(lo, tri, preferred_element_type=F32)) + run
        fcum_ref[:, j * LANES:(j + 1) * LANES] = cs
        run = jnp.broadcast_to(cs[:, LANES - 1:LANES], run.shape)
    carry_ref[...] = run


def _inproj(x, mod, g_pre, w_main, w_vt, wf_t, bf, cos, sin):
    B, S, D = x.shape
    tm = TM_PROJ
    sec_out = jax.ShapeDtypeStruct((B, S, SEC), BF16)
    sec_spec = pl.BlockSpec((None, tm, SEC), lambda b, s: (b, s, 0))
    t_out = jax.ShapeDtypeStruct((B, SEC, S), BF16)
    t_spec = pl.BlockSpec((None, SEC, tm), lambda b, s: (b, 0, s))
    return pl.pallas_call(
        _inproj_kernel,
        grid=(B, S // tm),
        in_specs=[pl.BlockSpec((None, tm, D), lambda b, s: (b, s, 0)),
                  pl.BlockSpec((None, 6, D), lambda b, s: (b, 0, 0)),
                  pl.BlockSpec((1, D), lambda b, s: (0, 0)),
                  pl.BlockSpec((D, 4 * SEC), lambda b, s: (0, 0)),
                  pl.BlockSpec((2 * SEC, D), lambda b, s: (0, 0)),
                  pl.BlockSpec((FOX_HEADS, D), lambda b, s: (0, 0)),
                  pl.BlockSpec((FOX_HEADS, LANES), lambda b, s: (0, 0)),
                  pl.BlockSpec((tm, LANES), lambda b, s: (s, 0)),
                  pl.BlockSpec((tm, LANES), lambda b, s: (s, 0))],
        out_specs=[sec_spec] * 4
        + [pl.BlockSpec((None, DIFF_HEADS, DV_ROWS, tm), lambda b, s: (b, 0, 0, s)), t_spec,
           pl.BlockSpec((None, FOX_HEADS, tm), lambda b, s: (b, 0, s))],
        out_shape=[sec_out] * 4
        + [jax.ShapeDtypeStruct((B, DIFF_HEADS, DV_ROWS, S), BF16), t_out,
           jax.ShapeDtypeStruct((B, FOX_HEADS, S), F32)],
        scratch_shapes=[pltpu.VMEM((FOX_HEADS, LANES), F32)],
        compiler_params=_cparams(("arbitrary", "arbitrary")),
        name="in_proj",
    )(x, mod, g_pre, w_main, w_vt, wf_t, bf, cos, sin)


def _scores(k_ref, ch, r0, qm=None):
    r0 = pl.multiple_of(r0, HK)
    t = lax.dot_general(k_ref[pl.ds(r0, HK), :], ch["qm"] if qm is None else qm, _NT,
                        preferred_element_type=F32)
    if ch["kbias"] is not None:
        t = t - ch["kbias"](r0)
    return t


def _flash_t(chains, k_ref, sa_ref, qi, diag_mask, next_qm):
    def scores(ch, r0):
        return _scores(k_ref, ch, r0)

    def update(ch, state, t, r0):
        m, acc = state
        mt = jnp.max(t, axis=0, keepdims=True)
        qb = ch["qbias"]
        m_new = jnp.maximum(m, mt if qb is None else mt + qb)
        alpha = jnp.exp2(m - m_new)
        p = jnp.exp2((t - (m_new if qb is None else m_new - qb)).astype(BF16))
        vt = ch["vt"](pl.multiple_of(r0, HK))
        acc = alpha * acc + jnp.dot(vt, p, preferred_element_type=F32)
        return m_new, acc

    def body(ki, carry):
        k0 = ki * TK
        tb = [scores(ch, k0 + HK) for ch in chains]
        st = [update(ch, s, sa_ref[ci], k0) for ci, (ch, s) in enumerate(zip(chains, carry))]
        ta = [scores(ch, k0 + TK) for ch in chains]
        st = [update(ch, s, tb[ci], k0 + HK) for ci, (ch, s) in enumerate(zip(chains, st))]
        for ci in range(len(chains)):
            sa_ref[ci] = ta[ci]
        return tuple(st)

    init = tuple((jnp.full((1, TQ), NEG_INF, F32), jnp.zeros((ch["rows"], TQ), F32))
                 for ch in chains)
    carry = lax.fori_loop(0, qi, body, init)
    k0 = qi * TK
    tb = [scores(ch, k0 + HK) for ch in chains]
    nxt = [_scores(k_ref, ch, 0, qm) for ch, qm in zip(chains, next_qm)]
    st = [update(ch, s, jnp.where(diag_mask[0:HK], sa_ref[ci], NEG_INF), k0)
          for ci, (ch, s) in enumerate(zip(chains, carry))]
    st = [update(ch, s, jnp.where(diag_mask[HK:TK], tb[ci], NEG_INF), k0 + HK)
          for ci, (ch, s) in enumerate(zip(chains, st))]
    for ci in range(len(chains)):
        sa_ref[ci] = nxt[ci]
    return tuple(st)


def _rows_to_columns(x):
    pad = jnp.zeros((LANES - x.shape[0], LANES), x.dtype)
    return jnp.concatenate([x, pad], axis=0).T


def _next_tile_start(q0, seq_len):
    return pl.multiple_of(jnp.minimum(q0 + TQ, seq_len - TQ), TQ)


def _for_each_query_tile(n_tiles, tile_fn):
    def body(qi, carry):
        tile_fn(qi, pl.multiple_of(qi * TQ, TQ))
        return carry

    lax.fori_loop(0, n_tiles, body, 0)


def _fox_kernel(q_ref, k_ref, vt_ref, frow_ref, o_ref, sa_ref, fcol_ref):
    for j in range(frow_ref.shape[1] // LANES):
        fcol_ref[j * LANES:(j + 1) * LANES, :] = _rows_to_columns(
            frow_ref[:, j * LANES:(j + 1) * LANES])

    lane = lax.broadcasted_iota(jnp.int32, (TQ, LANES), 1)
    vrow = lax.broadcasted_iota(jnp.int32, (LANES, HK), 0)
    r = lax.broadcasted_iota(jnp.int32, (TK, TQ), 0)
    c = lax.broadcasted_iota(jnp.int32, (TK, TQ), 1)

    def masked_q(q0):
        q2 = q_ref[pl.ds(q0, TQ), :]
        return [jnp.where((lane >= j * HEAD_DIM) & (lane < (j + 1) * HEAD_DIM), q2,
                          jnp.zeros_like(q2)) for j in range(2)]

    def make_chains(q0):
        chains = []
        for j, qm in enumerate(masked_q(q0)):
            v_rows = (vrow >= j * HEAD_DIM) & (vrow < (j + 1) * HEAD_DIM)

            def vt(k0, v_rows=v_rows):
                v = vt_ref[:, pl.ds(k0, HK)]
                return jnp.where(v_rows, v, jnp.ones_like(v))

            chains.append(dict(
                qm=qm, kbias=lambda k0, j=j: fcol_ref[pl.ds(k0, HK), j:j + 1],
                qbias=frow_ref[j:j + 1, pl.ds(q0, TQ)], vt=vt, rows=LANES))
        return chains

    def tile(qi, q0):
        (_, a0), (_, a1) = _flash_t(make_chains(q0), k_ref, sa_ref, qi, r <= c,
                                    masked_q(_next_tile_start(q0, q_ref.shape[0])))
        h = HEAD_DIM
        o_t = jnp.concatenate([a0[0:h] / a0[h:2 * h], a1[h:2 * h] / a1[0:h]], axis=0)
        o_ref[pl.ds(q0, TQ), :] = o_t.T.astype(o_ref.dtype)

    for ci, ch in enumerate(make_chains(0)):
        sa_ref[ci] = _scores(k_ref, ch, 0)
    _for_each_query_tile(q_ref.shape[0] // TQ, tile)


def _fox_attention(fq, fk, fvt, frow):
    B, S, _ = fq.shape
    npair = FOX_HEADS // 2
    seq_spec = pl.BlockSpec((None, S, LANES), lambda b, p: (b, 0, p))
    return pl.pallas_call(
        _fox_kernel,
        grid=(B, npair),
        in_specs=[seq_spec, seq_spec,
                  pl.BlockSpec((None, LANES, S), lambda b, p: (b, p, 0)),
                  pl.BlockSpec((None, None, 2, S), lambda b, p: (b, p, 0, 0))],
        out_specs=seq_spec,
        out_shape=jax.ShapeDtypeStruct((B, S, SEC), BF16),
        scratch_shapes=[pltpu.VMEM((2, HK, TQ), F32), pltpu.VMEM((S, LANES), F32)],
        compiler_params=_cparams(("arbitrary", "arbitrary")),
        name="fox_attention",
    )(fq, fk, fvt, frow)


def _diff_kernel(q_ref, k_ref, vt_ref, lam_ref, g_ref, o_ref, sa_ref):
    lane = lax.broadcasted_iota(jnp.int32, (TQ, LANES), 1)
    r = lax.broadcasted_iota(jnp.int32, (TK, TQ), 0)
    c = lax.broadcasted_iota(jnp.int32, (TK, TQ), 1)
    lp = lam_ref[...]
    lam = (jnp.exp(jnp.sum(lp[0:1] * lp[1:2], axis=1, keepdims=True))
           - jnp.exp(jnp.sum(lp[2:3] * lp[3:4], axis=1, keepdims=True)) + LAMBDA_INIT)

    def masked_q(q0):
        q2 = q_ref[pl.ds(q0, TQ), :]
        return [jnp.where((lane >= j * HEAD_DIM) & (lane < (j + 1) * HEAD_DIM), q2,
                          jnp.zeros_like(q2)) for j in range(2)]

    def make_chains(q0):
        return [dict(qm=qm, kbias=None, qbias=None,
                     vt=lambda k0: vt_ref[:, pl.ds(k0, HK)], rows=DV_ROWS)
                for qm in masked_q(q0)]

    def tile(qi, q0):
        (_, a0), (_, a1) = _flash_t(make_chains(q0), k_ref, sa_ref, qi,
                                    (r // CHUNK) <= (c // CHUNK),
                                    masked_q(_next_tile_start(q0, q_ref.shape[0])))
        v = LANES
        o = (a0[0:v] / a0[v:v + 1] - lam * (a1[0:v] / a1[v:v + 1])).T
        o_ref[pl.ds(q0, TQ), :] = (_rms(o, g_ref[...]) * (1.0 - LAMBDA_INIT)).astype(o_ref.dtype)

    for ci, ch in enumerate(make_chains(0)):
        sa_ref[ci] = _scores(k_ref, ch, 0)
    _for_each_query_tile(q_ref.shape[0] // TQ, tile)


def _diff_attention(dq, dk, dvt, lam_params, g_subln):
    B, S, _ = dq.shape
    seq_spec = pl.BlockSpec((None, S, LANES), lambda b, h: (b, 0, h))
    return pl.pallas_call(
        _diff_kernel,
        grid=(B, DIFF_HEADS),
        in_specs=[seq_spec, seq_spec,
                  pl.BlockSpec((None, None, DV_ROWS, S), lambda b, h: (b, h, 0, 0)),
                  pl.BlockSpec((4, HEAD_DIM), lambda b, h: (0, 0)),
                  pl.BlockSpec((1, LANES), lambda b, h: (0, 0))],
        out_specs=seq_spec,
        out_shape=jax.ShapeDtypeStruct((B, S, SEC), BF16),
        scratch_shapes=[pltpu.VMEM((2, HK, TQ), F32)],
        compiler_params=_cparams(("arbitrary", "arbitrary")),
        name="diff_attention",
    )(dq, dk, dvt, lam_params, g_subln)


def _route_kernel(d_ref, f_ref, wd_ref, wf_ref, x_ref, mod_ref, gpa_ref, gpf_ref, wr_ref, br_ref,
                  x1_ref, h2_ref, ri_ref, rw_ref, cnt_ref):
    first = (pl.program_id(0) == 0) & (pl.program_id(1) == 0)

    @pl.when(first)
    def _():
        cnt_ref[...] = jnp.zeros_like(cnt_ref)

    mixed = (jnp.dot(d_ref[...], wd_ref[...], preferred_element_type=F32)
             + jnp.dot(f_ref[...], wf_ref[...], preferred_element_type=F32))
    x1 = x_ref[...] + mod_ref[2:3, :] * _rms(mixed, gpa_ref[...])
    x1_ref[...] = x1
    h2 = _rms(x1, gpf_ref[...]) * (1.0 + mod_ref[4:5, :]) + mod_ref[3:4, :]
    h2_ref[...] = _pack_halves(h2)
    tm = h2.shape[0]

    hh, hm, _ = _split3(h2)
    wh, wm, _ = _split3(wr_ref[...])
    lt = br_ref[:, 0:1]
    for a, b_ in ((wh, hh), (wh, hm), (wm, hh)):
        lt = lt + lax.dot_general(a, b_, _NT, preferred_element_type=F32)

    g = lt[0:N_GROUPS]
    row4 = lax.broadcasted_iota(jnp.int32, g.shape, 0)
    gmax = jnp.max(g, axis=0, keepdims=True)
    gidx = jnp.min(jnp.where(g == gmax, row4, N_GROUPS), axis=0, keepdims=True)
    gw = 1.0 / jnp.sum(jnp.exp(g - gmax), axis=0, keepdims=True)
    esel = jnp.zeros((EXPERTS_PER_GROUP, tm), F32)
    for gg in range(N_GROUPS):
        esel = jnp.where(gidx == gg, lt[8 + 8 * gg:16 + 8 * gg], esel)
    row8 = lax.broadcasted_iota(jnp.int32, esel.shape, 0)
    v1 = jnp.max(esel, axis=0, keepdims=True)
    i1 = jnp.min(jnp.where(esel == v1, row8, EXPERTS_PER_GROUP), axis=0, keepdims=True)
    esel2 = jnp.where(row8 == i1, -jnp.inf, esel)
    v2 = jnp.max(esel2, axis=0, keepdims=True)
    i2 = jnp.min(jnp.where(esel2 == v2, row8, EXPERTS_PER_GROUP), axis=0, keepdims=True)
    e = jnp.exp(v2 - v1)
    w1 = gw / (1.0 + e)
    w2 = gw * e / (1.0 + e)
    e1 = gidx * EXPERTS_PER_GROUP + i1
    e2 = gidx * EXPERTS_PER_GROUP + i2

    row32 = lax.broadcasted_iota(jnp.int32, (N_EXPERTS, tm), 0)
    oh1 = row32 == e1
    oh2 = row32 == e2
    oh = jnp.where(oh1 | oh2, 1.0, 0.0)
    t_r = lax.broadcasted_iota(jnp.int32, (tm, tm), 0)
    t_c = lax.broadcasted_iota(jnp.int32, (tm, tm), 1)
    tri = jnp.where(t_r < t_c, 1.0, 0.0).astype(BF16)
    tot = jnp.dot(oh.astype(BF16), tri, preferred_element_type=F32) + cnt_ref[:, 0:1]
    r1 = jnp.sum(jnp.where(oh1, tot, 0.0), axis=0, keepdims=True)
    r2 = jnp.sum(jnp.where(oh2, tot, 0.0), axis=0, keepdims=True)
    cnt_ref[...] = cnt_ref[...] + jnp.sum(oh, axis=1, keepdims=True)

    zi = jnp.zeros((4, tm), jnp.int32)
    ri_ref[...] = jnp.concatenate([e1, e2, r1.astype(jnp.int32), r2.astype(jnp.int32), zi], axis=0)
    rw_ref[...] = jnp.concatenate([w1, w2, jnp.zeros((6, tm), F32)], axis=0)


def _route(d_out, f_out, w_out_d, w_out_f, x, mod, g_post_attn, g_pre_ffn, w_rt, b_rt):
    B, S, D = x.shape
    tm = TM_ROUTE
    nrt = w_rt.shape[0]
    n_s = S // tm
    T = B * S
    return pl.pallas_call(
        _route_kernel,
        grid=(B, n_s),
        in_specs=[pl.BlockSpec((None, tm, SEC), lambda b, s: (b, s, 0)),
                  pl.BlockSpec((None, tm, SEC), lambda b, s: (b, s, 0)),
                  pl.BlockSpec((SEC, D), lambda b, s: (0, 0)),
                  pl.BlockSpec((SEC, D), lambda b, s: (0, 0)),
                  pl.BlockSpec((None, tm, D), lambda b, s: (b, s, 0)),
                  pl.BlockSpec((None, 6, D), lambda b, s: (b, 0, 0)),
                  pl.BlockSpec((1, D), lambda b, s: (0, 0)),
                  pl.BlockSpec((1, D), lambda b, s: (0, 0)),
                  pl.BlockSpec((nrt, D), lambda b, s: (0, 0)),
                  pl.BlockSpec((nrt, LANES), lambda b, s: (0, 0))],
        out_specs=[pl.BlockSpec((None, tm, D), lambda b, s: (b, s, 0)),
                   pl.BlockSpec((None, tm, D // 2), lambda b, s: (b, s, 0)),
                   pl.BlockSpec((8, tm), lambda b, s: (0, b * n_s + s)),
                   pl.BlockSpec((8, tm), lambda b, s: (0, b * n_s + s)),
                   pl.BlockSpec((N_EXPERTS, LANES), lambda b, s: (0, 0))],
        out_shape=[jax.ShapeDtypeStruct((B, S, D), F32),
                   jax.ShapeDtypeStruct((B, S, D // 2), jnp.uint32),
                   jax.ShapeDtypeStruct((8, T), jnp.int32),
                   jax.ShapeDtypeStruct((8, T), F32),
                   jax.ShapeDtypeStruct((N_EXPERTS, LANES), F32)],
        compiler_params=_cparams(("arbitrary", "arbitrary")),
        name="out_proj_route",
    )(d_out, f_out, w_out_d, w_out_f, x, mod, g_post_attn, g_pre_ffn, w_rt, b_rt)


def _sc_mesh():
    return plsc.VectorSubcoreMesh(core_axis_name="c", subcore_axis_name="s",
                                  num_cores=SC_CORES, num_subcores=SC_SUBCORES)


def _sc_worker_base(per_worker):
    return (lax.axis_index("s") * SC_CORES + lax.axis_index("c")) * per_worker


def _dispatch(dest, h2, n_rows):
    T, D = h2.shape
    per_worker = T // SC_WORKERS

    @functools.partial(
        pl.kernel, mesh=_sc_mesh(),
        out_type=jax.ShapeDtypeStruct((n_rows, D), h2.dtype),
        scratch_types=[pltpu.VMEM((SC_WIN,), jnp.int32), pltpu.VMEM((SC_WIN, D), h2.dtype)],
        name="moe_dispatch")
    def run(h_hbm, dest_hbm, xs_hbm, idx_v, rows_v):
        base = _sc_worker_base(per_worker)

        @pl.loop(0, per_worker // SC_WIN)
        def _(ci):
            off = pl.multiple_of(base + ci * SC_WIN, SC_WIN)
            pltpu.sync_copy(h_hbm.at[pl.ds(off, SC_WIN)], rows_v)
            for k in range(2):
                pltpu.sync_copy(dest_hbm.at[pl.ds(k * T + off, SC_WIN)], idx_v)
                pltpu.sync_copy(rows_v, xs_hbm.at[idx_v])

    return run(h2, dest.reshape(2 * T))


def _gather_rows(ys, dest):
    _, D = ys.shape
    K, T = dest.shape
    n = K * T
    per_worker = n // SC_WORKERS

    @functools.partial(
        pl.kernel, mesh=_sc_mesh(),
        out_type=jax.ShapeDtypeStruct((n, D), ys.dtype),
        scratch_types=[pltpu.VMEM((SC_WIN,), jnp.int32), pltpu.VMEM((SC_WIN, D), ys.dtype)],
        name="moe_gather")
    def run(ys_hbm, dest_hbm, g_hbm, idx_v, rows_v):
        base = _sc_worker_base(per_worker)

        @pl.loop(0, per_worker // SC_WIN)
        def _(ci):
            off = pl.multiple_of(base + ci * SC_WIN, SC_WIN)
            pltpu.sync_copy(dest_hbm.at[pl.ds(off, SC_WIN)], idx_v)
            pltpu.sync_copy(ys_hbm.at[idx_v], rows_v)
            pltpu.sync_copy(rows_v, g_hbm.at[pl.ds(off, SC_WIN)])

    return run(ys, dest.reshape(n)).reshape(K, T, D)


def _expert_kernel(be_ref, nu_ref, nv_ref, xs_ref, wg_ref, wu_ref, wd_ref, ys_ref,
                   wgb, wub, wdb):
    i = pl.program_id(0)

    @pl.when(i < nu_ref[0])
    def _():
        prev = be_ref[jnp.maximum(i - 1, 0)]

        @pl.when((i == 0) | (be_ref[i] != prev))
        def _():
            wgb[...] = wg_ref[...].astype(BF16)
            wub[...] = wu_ref[...].astype(BF16)
            wdb[...] = wd_ref[...].astype(BF16)

        row = lax.broadcasted_iota(jnp.int32, xs_ref.shape, 0)
        lo, hi = _unpack_halves(jnp.where(row < nv_ref[i], xs_ref[...], jnp.uint32(0)))
        lo, hi = lo.astype(BF16), hi.astype(BF16)
        half = lo.shape[1]

        def proj(w, c0, c1):
            return (jnp.dot(lo, w[0:half, c0:c1], preferred_element_type=F32)
                    + jnp.dot(hi, w[half:2 * half, c0:c1], preferred_element_type=F32))

        fd = wgb.shape[1]
        step = fd // EXPERT_COL_GROUPS
        au = [(proj(wgb, c, c + step), proj(wub, c, c + step)) for c in range(0, fd, step)]
        y = None
        for gi, (a, u) in enumerate(au):
            hmid = (a * jax.nn.sigmoid(a) * u).astype(BF16)
            part = jnp.dot(hmid, wdb[gi * step:(gi + 1) * step, :], preferred_element_type=F32)
            y = part if y is None else y + part
        ys_ref[...] = _pack_halves(y)

    @pl.when(i >= nu_ref[0])
    def _():
        ys_ref[...] = jnp.zeros_like(ys_ref)


def _experts(block_e, n_used, n_valid, xs, w_gate, w_up, w_down):
    P, Dh = xs.shape
    nb = P // BM
    _, D, Fd = w_gate.shape

    def row_map(i, be, nu, nv):
        return (jnp.minimum(i, nu[0] - 1), 0)

    def w_map(i, be, nu, nv):
        return (be[i], 0, 0)

    return pl.pallas_call(
        _expert_kernel,
        grid_spec=pltpu.PrefetchScalarGridSpec(
            num_scalar_prefetch=3,
            grid=(nb,),
            in_specs=[pl.BlockSpec((BM, Dh), row_map),
                      pl.BlockSpec((None, D, Fd), w_map),
                      pl.BlockSpec((None, D, Fd), w_map),
                      pl.BlockSpec((None, Fd, D), w_map)],
            out_specs=pl.BlockSpec((BM, Dh), lambda i, be, nu, nv: (i, 0)),
            scratch_shapes=[pltpu.VMEM((D, Fd), BF16), pltpu.VMEM((D, Fd), BF16),
                            pltpu.VMEM((Fd, D), BF16)]),
        out_shape=jax.ShapeDtypeStruct((P, Dh), jnp.uint32),
        compiler_params=_cparams(("arbitrary",)),
        name="moe_experts",
    )(block_e, n_used, n_valid, xs, w_gate, w_up, w_down)


def _combine_kernel(g_ref, w_ref, x1_ref, mod_ref, g_post_ref, o_ref):
    lo0, hi0 = _unpack_halves(g_ref[0])
    lo1, hi1 = _unpack_halves(g_ref[1])
    tm = lo0.shape[0]
    wt = jnp.concatenate([_rows_to_columns(w_ref[:, j * LANES:(j + 1) * LANES])
                          for j in range(tm // LANES)], axis=0)
    w0, w1 = wt[:, 0:1], wt[:, 1:2]
    y = jnp.concatenate([w0 * lo0 + w1 * lo1, w0 * hi0 + w1 * hi1], axis=1)
    o_ref[...] = x1_ref[...] + mod_ref[5:6, :] * _rms(y, g_post_ref[...])


def _combine(g, wts, x1, mod, g_post_ffn, S):
    T, D = x1.shape
    tm = TM_ROWS
    per_b = S // tm
    return pl.pallas_call(
        _combine_kernel,
        grid=(T // tm,),
        in_specs=[pl.BlockSpec((2, tm, D // 2), lambda i: (0, i, 0)),
                  pl.BlockSpec((8, tm), lambda i: (0, i)),
                  pl.BlockSpec((tm, D), lambda i: (i, 0)),
                  pl.BlockSpec((None, 6, D), lambda i: (i // per_b, 0, 0)),
                  pl.BlockSpec((1, D), lambda i: (0, 0))],
        out_specs=pl.BlockSpec((tm, D), lambda i: (i, 0)),
        out_shape=jax.ShapeDtypeStruct((T, D), F32),
        compiler_params=_cparams(("arbitrary",)),
        name="moe_combine",
    )(g, wts, x1, mod, g_post_ffn)


def _rope_tables(S):
    inv = 1.0 / (10000.0 ** (jnp.arange(0, HEAD_DIM, 2, dtype=F32) / HEAD_DIM))
    ang = jnp.arange(S, dtype=F32)[:, None] * inv[None, :]
    c, s = jnp.cos(ang), jnp.sin(ang)
    return jnp.tile(c, (1, 4)), jnp.tile(jnp.concatenate([-s, s], axis=1), (1, 2))


def kernel(x, c, w_ada, b_ada, g_pre_attn, g_post_attn, w_in, b_forget, lambda_q1, lambda_k1,
           lambda_q2, lambda_k2, g_diff_subln, w_out, g_pre_ffn, g_post_ffn, w_group, b_group,
           w_router, b_router, w_gate, w_up, w_down):
    B, S, D = x.shape
    T = B * S
    l = 0
    mod = _ada(c, w_ada[l], b_ada[l]).reshape(B, 6, D)

    wi = w_in[l]
    w_main = jnp.concatenate([wi[:, 0:2 * SEC], wi[:, 3 * SEC:5 * SEC]], axis=1).astype(BF16)
    w_vt = jnp.concatenate([wi[:, 2 * SEC:3 * SEC], wi[:, 5 * SEC:6 * SEC]], axis=1).T.astype(BF16)
    wf_t = wi[:, 6 * SEC:].T.astype(BF16)
    bf = jnp.broadcast_to(b_forget[l][:, None], (FOX_HEADS, LANES))
    cos, sin = _rope_tables(S)
    dq, dk, fq, fk, dvt, fvt, fcum = _inproj(x, mod, g_pre_attn[l][None, :], w_main, w_vt, wf_t,
                                             bf, cos, sin)

    lam_params = jnp.stack([lambda_q1[l], lambda_k1[l], lambda_q2[l], lambda_k2[l]])
    d_out = _diff_attention(dq, dk, dvt, lam_params, g_diff_subln[l][None, :])
    frow = fcum.reshape(B, FOX_HEADS // 2, 2, S)
    f_out = _fox_attention(fq, fk, fvt, frow)

    w_o = w_out[l].astype(BF16)
    zpad = jnp.zeros((8 - N_GROUPS, D), F32)
    w_rt = jnp.concatenate([w_group[l].T, zpad, w_router[l].T], axis=0)
    b_rt = jnp.concatenate([b_group[l], jnp.zeros((8 - N_GROUPS,), F32), b_router[l]])
    b_rt = jnp.broadcast_to(b_rt[:, None], (8 + N_EXPERTS, LANES))
    x1, h2, ri, rw, cnt = _route(d_out, f_out, w_o[:SEC], w_o[SEC:], x, mod,
                                 g_post_attn[l][None, :], g_pre_ffn[l][None, :], w_rt, b_rt)

    counts = cnt[:, 0].astype(jnp.int32)
    padded = ((counts + BM - 1) // BM) * BM
    pend = jnp.cumsum(padded)
    pstart = pend - padded
    eids = jnp.arange(N_EXPERTS, dtype=jnp.int32)
    dest = ri[2:4] + jnp.sum(jnp.where(ri[0:2, :, None] == eids, pstart, 0), axis=-1)
    nb = (2 * T) // BM + N_EXPERTS
    n_used = (pend[-1] // BM).astype(jnp.int32).reshape(1)
    blk0 = jnp.arange(nb, dtype=jnp.int32) * BM
    block_e = jnp.minimum(jnp.sum((blk0[:, None] >= pend[None, :]).astype(jnp.int32), axis=1),
                          N_EXPERTS - 1)
    cend = pstart + counts
    blk_end = jnp.sum(jnp.where(block_e[:, None] == eids, cend, 0), axis=-1)
    n_valid = jnp.clip(blk_end - blk0, 0, BM).astype(jnp.int32)

    xs = _dispatch(dest, h2.reshape(T, D // 2), nb * BM)
    ys = _experts(block_e, n_used, n_valid, xs, w_gate[l], w_up[l], w_down[l])
    g = _gather_rows(ys, dest)
    out = _combine(g, rw, x1.reshape(T, D), mod, g_post_ffn[l][None, :], S)
    return out.reshape(B, S, D)
```

```python
import functools
import math

import jax
import jax.numpy as jnp
from jax import lax
from jax.experimental import pallas as pl
from jax.experimental.pallas import tpu as pltpu
from jax.experimental.pallas import tpu_sc as plsc

F32 = jnp.float32
BF16 = jnp.bfloat16

D_MODEL = 1024
HEAD_DIM = 64
CHUNK = 64
DIFF_HEADS = 4
FOX_HEADS = 8
SEC = 512
N_GROUPS = 4
EXPERTS_PER_GROUP = 8
N_EXPERTS = 32
D_EXPERT = 512
EPS = 1e-6
NEG_INF = -1e30
LOG2E = 1.4426950408889634
LAMBDA_INIT = 0.8 - 0.6 * math.exp(-0.3 * 0)

LANES = 128
TM_PROJ = 512
TQ = 512
TK = 512
HK = TK // 2
DV_ROWS = 2 * HEAD_DIM + 16
TM_ROUTE = 512
ROUTE_ROW_GROUPS = 2
BM = 512
EXPERT_COL_GROUPS = 2
TM_ROWS = 256
SC_CORES = 2
SC_SUBCORES = 16
SC_WORKERS = SC_CORES * SC_SUBCORES
SC_WIN = 128
VMEM_LIMIT = 48 * 1024 * 1024
_NT = (((1,), (1,)), ((), ()))


def _cparams(sem, flags=None):
    return pltpu.CompilerParams(dimension_semantics=sem, vmem_limit_bytes=VMEM_LIMIT, flags=flags)


def _rms(x, g):
    return x * lax.rsqrt(jnp.mean(x * x, axis=-1, keepdims=True) + EPS) * g


def _pack_halves(x):
    m = x.shape[1] // 2
    bits = lax.bitcast_convert_type(x.astype(BF16).astype(F32), jnp.uint32)
    return (bits[:, :m] >> 16) | (bits[:, m:] & jnp.uint32(0xFFFF0000))


def _unpack_halves(w):
    lo = lax.bitcast_convert_type(w << 16, F32)
    hi = lax.bitcast_convert_type(w & jnp.uint32(0xFFFF0000), F32)
    return lo, hi


def _ada_kernel(c_ref, w_ref, b_ref, o_ref):
    c = c_ref[...]
    cs = c * jax.nn.sigmoid(c)
    o_ref[...] = jnp.dot(cs, w_ref[...], preferred_element_type=F32,
                         precision=lax.Precision.HIGHEST) + b_ref[...]


def _ada(c, w_ada, b_ada):
    B, D = c.shape
    N = w_ada.shape[1]
    tn = 1024
    return pl.pallas_call(
        _ada_kernel,
        grid=(N // tn,),
        in_specs=[pl.BlockSpec((B, D), lambda j: (0, 0)),
                  pl.BlockSpec((D, tn), lambda j: (0, j)),
                  pl.BlockSpec((1, tn), lambda j: (0, j))],
        out_specs=pl.BlockSpec((B, tn), lambda j: (0, j)),
        out_shape=jax.ShapeDtypeStruct((B, N), F32),
        compiler_params=_cparams(("arbitrary",)),
        name="ada_mod",
    )(c, w_ada, b_ada.reshape(1, N))


def _split3(x):
    hi = x.astype(BF16)
    r1 = x - hi.astype(F32)
    mid = r1.astype(BF16)
    lo = (r1 - mid.astype(F32)).astype(BF16)
    return hi, mid, lo


def _inproj_kernel(x_ref, mod_ref, g_ref, w_ref, wvt_ref, wf_ref, bf_ref, cos_ref, sin_ref,
                   dq_ref, dk_ref, fq_ref, fk_ref, dvt_ref, fvt_ref, fcum_ref, carry_ref):
    si = pl.program_id(1)
    x = x_ref[...]
    h = _rms(x, g_ref[...]) * (1.0 + mod_ref[1:2, :]) + mod_ref[0:1, :]
    hb = h.astype(BF16)
    tm = x.shape[0]

    cos = cos_ref[...]
    sin = sin_ref[...]
    lane = lax.broadcasted_iota(jnp.int32, (tm, LANES), 1)
    first_half = (lane % HEAD_DIM) < (HEAD_DIM // 2)
    qscale = HEAD_DIM ** -0.5 * LOG2E

    def proj(sec):
        return jnp.dot(hb, w_ref[:, sec * SEC:(sec + 1) * SEC], preferred_element_type=F32)

    def rope_store(p, out_ref, scale):
        for j in range(SEC // LANES):
            ch = p[:, j * LANES:(j + 1) * LANES]
            partner = jnp.where(first_half,
                                pltpu.roll(ch, LANES - HEAD_DIM // 2, 1),
                                pltpu.roll(ch, HEAD_DIM // 2, 1))
            r = ch * cos + partner * sin
            if scale != 1.0:
                r = r * scale
            out_ref[:, j * LANES:(j + 1) * LANES] = r.astype(out_ref.dtype)

    rope_store(proj(0), dq_ref, qscale)
    rope_store(proj(1), dk_ref, 1.0)
    fq_ref[...] = (proj(2) * qscale).astype(fq_ref.dtype)
    fk_ref[...] = proj(3).astype(fk_ref.dtype)
    vt = lax.dot_general(wvt_ref[...], hb, _NT, preferred_element_type=F32)
    for hd in range(DIFF_HEADS):
        dvt_ref[hd, 0:LANES, :] = vt[hd * LANES:(hd + 1) * LANES].astype(dvt_ref.dtype)
        dvt_ref[hd, LANES:DV_ROWS, :] = jnp.ones((DV_ROWS - LANES, tm), dvt_ref.dtype)
    fvt_ref[...] = vt[SEC:2 * SEC].astype(fvt_ref.dtype)

    z = lax.dot_general(wf_ref[...], hb, _NT, preferred_element_type=F32) + bf_ref[:, 0:1]
    lf = (jnp.minimum(z, 0.0) - jnp.log1p(jnp.exp(-jnp.abs(z)))) * LOG2E

    @pl.when(si == 0)
    def _():
        carry_ref[...] = jnp.zeros_like(carry_ref)

    r_i = lax.broadcasted_iota(jnp.int32, (LANES, LANES), 0)
    c_i = lax.broadcasted_iota(jnp.int32, (LANES, LANES), 1)
    tri = jnp.where(r_i <= c_i, 1.0, 0.0).astype(BF16)
    run = carry_ref[...]
    for j in range(tm // LANES):
        hi, mid, lo = _split3(lf[:, j * LANES:(j + 1) * LANES])
        cs = (jnp.dot(hi, tri, preferred_element_type=F32)
              + jnp.dot(mid, tri, preferred_element_type=F32)
              + jnp.dot(lo, tri, preferred_element_type=F32)) + run
        fcum_ref[:, j * LANES:(j + 1) * LANES] = cs
        run = jnp.broadcast_to(cs[:, LANES - 1:LANES], run.shape)
    carry_ref[...] = run


def _inproj(x, mod, g_pre, w_main, w_vt, wf_t, bf, cos, sin):
    B, S, D = x.shape
    tm = TM_PROJ
    sec_out = jax.ShapeDtypeStruct((B, S, SEC), BF16)
    sec_spec = pl.BlockSpec((None, tm, SEC), lambda b, s: (b, s, 0))
    t_out = jax.ShapeDtypeStruct((B, SEC, S), BF16)
    t_spec = pl.BlockSpec((None, SEC, tm), lambda b, s: (b, 0, s))
    return pl.pallas_call(
        _inproj_kernel,
        grid=(B, S // tm),
        in_specs=[pl.BlockSpec((None, tm, D), lambda b, s: (b, s, 0)),
                  pl.BlockSpec((None, 6, D), lambda b, s: (b, 0, 0)),
                  pl.BlockSpec((1, D), lambda b, s: (0, 0)),
                  pl.BlockSpec((D, 4 * SEC), lambda b, s: (0, 0)),
                  pl.BlockSpec((2 * SEC, D), lambda b, s: (0, 0)),
                  pl.BlockSpec((FOX_HEADS, D), lambda b, s: (0, 0)),
                  pl.BlockSpec((FOX_HEADS, LANES), lambda b, s: (0, 0)),
                  pl.BlockSpec((tm, LANES), lambda b, s: (s, 0)),
                  pl.BlockSpec((tm, LANES), lambda b, s: (s, 0))],
        out_specs=[sec_spec] * 4
        + [pl.BlockSpec((None, DIFF_HEADS, DV_ROWS, tm), lambda b, s: (b, 0, 0, s)), t_spec,
           pl.BlockSpec((None, FOX_HEADS, tm), lambda b, s: (b, 0, s))],
        out_shape=[sec_out] * 4
        + [jax.ShapeDtypeStruct((B, DIFF_HEADS, DV_ROWS, S), BF16), t_out,
           jax.ShapeDtypeStruct((B, FOX_HEADS, S), F32)],
        scratch_shapes=[pltpu.VMEM((FOX_HEADS, LANES), F32)],
        compiler_params=_cparams(("arbitrary", "arbitrary")),
        name="in_proj",
    )(x, mod, g_pre, w_main, w_vt, wf_t, bf, cos, sin)


def _scores(k_ref, ch, r0, qm=None):
    r0 = pl.multiple_of(r0, HK)
    t = lax.dot_general(k_ref[pl.ds(r0, HK), :], ch["qm"] if qm is None else qm, _NT,
                        preferred_element_type=F32)
    if ch["kbias"] is not None:
        t = t - ch["kbias"](r0)
    return t


def _flash_t(chains, k_ref, sa_ref, qi, diag_mask, next_qm):
    def scores(ch, r0):
        return _scores(k_ref, ch, r0)

    def update(ch, state, t, r0):
        m, acc = state
        mt = jnp.max(t, axis=0, keepdims=True)
        qb = ch["qbias"]
        m_new = jnp.maximum(m, mt if qb is None else mt + qb)
        alpha = jnp.exp2(m - m_new)
        p = jnp.exp2((t - (m_new if qb is None else m_new - qb)).astype(BF16))
        vt = ch["vt"](pl.multiple_of(r0, HK))
        acc = alpha * acc + jnp.dot(vt, p, preferred_element_type=F32)
        return m_new, acc

    def body(ki, carry):
        k0 = ki * TK
        tb = [scores(ch, k0 + HK) for ch in chains]
        st = [update(ch, s, sa_ref[ci], k0) for ci, (ch, s) in enumerate(zip(chains, carry))]
        ta = [scores(ch, k0 + TK) for ch in chains]
        st = [update(ch, s, tb[ci], k0 + HK) for ci, (ch, s) in enumerate(zip(chains, st))]
        for ci in range(len(chains)):
            sa_ref[ci] = ta[ci]
        return tuple(st)

    init = tuple((jnp.full((1, TQ), NEG_INF, F32), jnp.zeros((ch["rows"], TQ), F32))
                 for ch in chains)
    carry = lax.fori_loop(0, qi, body, init)
    k0 = qi * TK
    hq = TQ - HK
    late = [dict(ch, qbias=None if ch["qbias"] is None else ch["qbias"][:, hq:]) for ch in chains]
    tb = [_scores(k_ref, ch, k0 + HK, ch["qm"][hq:]) for ch in chains]
    nxt = [_scores(k_ref, ch, 0, qm) for ch, qm in zip(chains, next_qm)]
    st = [update(ch, s, jnp.where(diag_mask[0:HK], sa_ref[ci], NEG_INF), k0)
          for ci, (ch, s) in enumerate(zip(chains, carry))]
    out = []
    for ci, (ch, (m, acc)) in enumerate(zip(late, st)):
        m_l, acc_l = update(ch, (m[:, hq:], acc[:, hq:]),
                            jnp.where(diag_mask[HK:TK, hq:], tb[ci], NEG_INF), k0 + HK)
        out.append((jnp.concatenate([m[:, :hq], m_l], axis=1),
                    jnp.concatenate([acc[:, :hq], acc_l], axis=1)))
    for ci in range(len(chains)):
        sa_ref[ci] = nxt[ci]
    return tuple(out)


def _rows_to_columns(x):
    pad = jnp.zeros((LANES - x.shape[0], LANES), x.dtype)
    return jnp.concatenate([x, pad], axis=0).T


def _next_tile_start(q0, seq_len):
    return pl.multiple_of(jnp.minimum(q0 + TQ, seq_len - TQ), TQ)


def _for_each_query_tile(n_tiles, tile_fn):
    def body(qi, carry):
        tile_fn(qi, pl.multiple_of(qi * TQ, TQ))
        return carry

    lax.fori_loop(0, n_tiles, body, 0)


def _fox_kernel(q_ref, k_ref, vt_ref, frow_ref, o_ref, sa_ref, fcol_ref):
    for j in range(frow_ref.shape[1] // LANES):
        fcol_ref[j * LANES:(j + 1) * LANES, :] = _rows_to_columns(
            frow_ref[:, j * LANES:(j + 1) * LANES])

    lane = lax.broadcasted_iota(jnp.int32, (TQ, LANES), 1)
    vrow = lax.broadcasted_iota(jnp.int32, (LANES, HK), 0)
    r = lax.broadcasted_iota(jnp.int32, (TK, TQ), 0)
    c = lax.broadcasted_iota(jnp.int32, (TK, TQ), 1)

    def masked_q(q0):
        q2 = q_ref[pl.ds(q0, TQ), :]
        return [jnp.where((lane >= j * HEAD_DIM) & (lane < (j + 1) * HEAD_DIM), q2,
                          jnp.zeros_like(q2)) for j in range(2)]

    def make_chains(q0):
        chains = []
        for j, qm in enumerate(masked_q(q0)):
            v_rows = (vrow >= j * HEAD_DIM) & (vrow < (j + 1) * HEAD_DIM)

            def vt(k0, v_rows=v_rows):
                v = vt_ref[:, pl.ds(k0, HK)]
                return jnp.where(v_rows, v, jnp.ones_like(v))

            chains.append(dict(
                qm=qm, kbias=lambda k0, j=j: fcol_ref[pl.ds(k0, HK), j:j + 1],
                qbias=frow_ref[j:j + 1, pl.ds(q0, TQ)], vt=vt, rows=LANES))
        return chains

    def tile(qi, q0):
        (_, a0), (_, a1) = _flash_t(make_chains(q0), k_ref, sa_ref, qi, r <= c,
                                    masked_q(_next_tile_start(q0, q_ref.shape[0])))
        h = HEAD_DIM
        o_t = jnp.concatenate([a0[0:h] / a0[h:2 * h], a1[h:2 * h] / a1[0:h]], axis=0)
        o_ref[pl.ds(q0, TQ), :] = o_t.T.astype(o_ref.dtype)

    for ci, ch in enumerate(make_chains(0)):
        sa_ref[ci] = _scores(k_ref, ch, 0)
    _for_each_query_tile(q_ref.shape[0] // TQ, tile)


def _fox_attention(fq, fk, fvt, frow):
    B, S, _ = fq.shape
    npair = FOX_HEADS // 2
    seq_spec = pl.BlockSpec((None, S, LANES), lambda b, p: (b, 0, p))
    return pl.pallas_call(
        _fox_kernel,
        grid=(B, npair),
        in_specs=[seq_spec, seq_spec,
                  pl.BlockSpec((None, LANES, S), lambda b, p: (b, p, 0)),
                  pl.BlockSpec((None, None, 2, S), lambda b, p: (b, p, 0, 0))],
        out_specs=seq_spec,
        out_shape=jax.ShapeDtypeStruct((B, S, SEC), BF16),
        scratch_shapes=[pltpu.VMEM((2, HK, TQ), F32), pltpu.VMEM((S, LANES), F32)],
        compiler_params=_cparams(("arbitrary", "arbitrary")),
        name="fox_attention",
    )(fq, fk, fvt, frow)


def _diff_kernel(q_ref, k_ref, vt_ref, lam_ref, g_ref, o_ref, sa_ref):
    lane = lax.broadcasted_iota(jnp.int32, (TQ, LANES), 1)
    r = lax.broadcasted_iota(jnp.int32, (TK, TQ), 0)
    c = lax.broadcasted_iota(jnp.int32, (TK, TQ), 1)
    lp = lam_ref[...]
    lam = (jnp.exp(jnp.sum(lp[0:1] * lp[1:2], axis=1, keepdims=True))
           - jnp.exp(jnp.sum(lp[2:3] * lp[3:4], axis=1, keepdims=True)) + LAMBDA_INIT)

    def masked_q(q0):
        q2 = q_ref[pl.ds(q0, TQ), :]
        return [jnp.where((lane >= j * HEAD_DIM) & (lane < (j + 1) * HEAD_DIM), q2,
                          jnp.zeros_like(q2)) for j in range(2)]

    def make_chains(q0):
        return [dict(qm=qm, kbias=None, qbias=None,
                     vt=lambda k0: vt_ref[:, pl.ds(k0, HK)], rows=DV_ROWS)
                for qm in masked_q(q0)]

    def tile(qi, q0):
        (_, a0), (_, a1) = _flash_t(make_chains(q0), k_ref, sa_ref, qi,
                                    (r // CHUNK) <= (c // CHUNK),
                                    masked_q(_next_tile_start(q0, q_ref.shape[0])))
        v = LANES
        o = (a0[0:v] / a0[v:v + 1] - lam * (a1[0:v] / a1[v:v + 1])).T
        o_ref[pl.ds(q0, TQ), :] = (_rms(o, g_ref[...]) * (1.0 - LAMBDA_INIT)).astype(o_ref.dtype)

    for ci, ch in enumerate(make_chains(0)):
        sa_ref[ci] = _scores(k_ref, ch, 0)
    _for_each_query_tile(q_ref.shape[0] // TQ, tile)


def _diff_attention(dq, dk, dvt, lam_params, g_subln):
    B, S, _ = dq.shape
    seq_spec = pl.BlockSpec((None, S, LANES), lambda b, h: (b, 0, h))
    return pl.pallas_call(
        _diff_kernel,
        grid=(B, DIFF_HEADS),
        in_specs=[seq_spec, seq_spec,
                  pl.BlockSpec((None, None, DV_ROWS, S), lambda b, h: (b, h, 0, 0)),
                  pl.BlockSpec((4, HEAD_DIM), lambda b, h: (0, 0)),
                  pl.BlockSpec((1, LANES), lambda b, h: (0, 0))],
        out_specs=seq_spec,
        out_shape=jax.ShapeDtypeStruct((B, S, SEC), BF16),
        scratch_shapes=[pltpu.VMEM((2, HK, TQ), F32)],
        compiler_params=_cparams(("arbitrary", "arbitrary")),
        name="diff_attention",
    )(dq, dk, dvt, lam_params, g_subln)


def _route_kernel(d_ref, f_ref, wd_ref, wf_ref, x_ref, mod_ref, gpa_ref, gpf_ref, wr_ref, br_ref,
                  x1_ref, h2_ref, ri_ref, rw_ref, cnt_ref):
    first = (pl.program_id(0) == 0) & (pl.program_id(1) == 0)

    @pl.when(first)
    def _():
        cnt_ref[...] = jnp.zeros_like(cnt_ref)

    tm = x_ref.shape[0]
    rows = tm // ROUTE_ROW_GROUPS
    mixed = [jnp.dot(d_ref[r0:r0 + rows, :], wd_ref[...], preferred_element_type=F32)
             + jnp.dot(f_ref[r0:r0 + rows, :], wf_ref[...], preferred_element_type=F32)
             for r0 in range(0, tm, rows)]
    wh, wm, _ = _split3(wr_ref[...])
    lts = []
    for gi, r0 in enumerate(range(0, tm, rows)):
        x1 = x_ref[r0:r0 + rows, :] + mod_ref[2:3, :] * _rms(mixed[gi], gpa_ref[...])
        x1_ref[r0:r0 + rows, :] = x1
        h2 = _rms(x1, gpf_ref[...]) * (1.0 + mod_ref[4:5, :]) + mod_ref[3:4, :]
        h2_ref[r0:r0 + rows, :] = _pack_halves(h2)
        hh, hm, _ = _split3(h2)
        lt_g = br_ref[:, 0:1]
        for a, b_ in ((wh, hh), (wh, hm), (wm, hh)):
            lt_g = lt_g + lax.dot_general(a, b_, _NT, preferred_element_type=F32)
        lts.append(lt_g)
    lt = jnp.concatenate(lts, axis=1)

    g = lt[0:N_GROUPS]
    row4 = lax.broadcasted_iota(jnp.int32, g.shape, 0)
    gmax = jnp.max(g, axis=0, keepdims=True)
    gidx = jnp.min(jnp.where(g == gmax, row4, N_GROUPS), axis=0, keepdims=True)
    gw = 1.0 / jnp.sum(jnp.exp(g - gmax), axis=0, keepdims=True)
    esel = jnp.zeros((EXPERTS_PER_GROUP, tm), F32)
    for gg in range(N_GROUPS):
        esel = jnp.where(gidx == gg, lt[8 + 8 * gg:16 + 8 * gg], esel)
    row8 = lax.broadcasted_iota(jnp.int32, esel.shape, 0)
    v1 = jnp.max(esel, axis=0, keepdims=True)
    i1 = jnp.min(jnp.where(esel == v1, row8, EXPERTS_PER_GROUP), axis=0, keepdims=True)
    esel2 = jnp.where(row8 == i1, -jnp.inf, esel)
    v2 = jnp.max(esel2, axis=0, keepdims=True)
    i2 = jnp.min(jnp.where(esel2 == v2, row8, EXPERTS_PER_GROUP), axis=0, keepdims=True)
    e = jnp.exp(v2 - v1)
    w1 = gw / (1.0 + e)
    w2 = gw * e / (1.0 + e)
    e1 = gidx * EXPERTS_PER_GROUP + i1
    e2 = gidx * EXPERTS_PER_GROUP + i2

    row32 = lax.broadcasted_iota(jnp.int32, (N_EXPERTS, tm), 0)
    oh1 = row32 == e1
    oh2 = row32 == e2
    oh = jnp.where(oh1 | oh2, 1.0, 0.0)
    t_r = lax.broadcasted_iota(jnp.int32, (tm, tm), 0)
    t_c = lax.broadcasted_iota(jnp.int32, (tm, tm), 1)
    tri = jnp.where(t_r < t_c, 1.0, 0.0).astype(BF16)
    tot = jnp.dot(oh.astype(BF16), tri, preferred_element_type=F32) + cnt_ref[:, 0:1]
    r1 = jnp.sum(jnp.where(oh1, tot, 0.0), axis=0, keepdims=True)
    r2 = jnp.sum(jnp.where(oh2, tot, 0.0), axis=0, keepdims=True)
    cnt_ref[...] = cnt_ref[...] + jnp.sum(oh, axis=1, keepdims=True)

    zi = jnp.zeros((4, tm), jnp.int32)
    ri_ref[...] = jnp.concatenate([e1, e2, r1.astype(jnp.int32), r2.astype(jnp.int32), zi], axis=0)
    rw_ref[...] = jnp.concatenate([w1, w2, jnp.zeros((6, tm), F32)], axis=0)


def _route(d_out, f_out, w_out_d, w_out_f, x, mod, g_post_attn, g_pre_ffn, w_rt, b_rt):
    B, S, D = x.shape
    tm = TM_ROUTE
    nrt = w_rt.shape[0]
    n_s = S // tm
    T = B * S
    return pl.pallas_call(
        _route_kernel,
        grid=(B, n_s),
        in_specs=[pl.BlockSpec((None, tm, SEC), lambda b, s: (b, s, 0)),
                  pl.BlockSpec((None, tm, SEC), lambda b, s: (b, s, 0)),
                  pl.BlockSpec((SEC, D), lambda b, s: (0, 0)),
                  pl.BlockSpec((SEC, D), lambda b, s: (0, 0)),
                  pl.BlockSpec((None, tm, D), lambda b, s: (b, s, 0)),
                  pl.BlockSpec((None, 6, D), lambda b, s: (b, 0, 0)),
                  pl.BlockSpec((1, D), lambda b, s: (0, 0)),
                  pl.BlockSpec((1, D), lambda b, s: (0, 0)),
                  pl.BlockSpec((nrt, D), lambda b, s: (0, 0)),
                  pl.BlockSpec((nrt, LANES), lambda b, s: (0, 0))],
        out_specs=[pl.BlockSpec((None, tm, D), lambda b, s: (b, s, 0)),
                   pl.BlockSpec((None, tm, D // 2), lambda b, s: (b, s, 0)),
                   pl.BlockSpec((8, tm), lambda b, s: (0, b * n_s + s)),
                   pl.BlockSpec((8, tm), lambda b, s: (0, b * n_s + s)),
                   pl.BlockSpec((N_EXPERTS, LANES), lambda b, s: (0, 0))],
        out_shape=[jax.ShapeDtypeStruct((B, S, D), F32),
                   jax.ShapeDtypeStruct((B, S, D // 2), jnp.uint32),
                   jax.ShapeDtypeStruct((8, T), jnp.int32),
                   jax.ShapeDtypeStruct((8, T), F32),
                   jax.ShapeDtypeStruct((N_EXPERTS, LANES), F32)],
        compiler_params=_cparams(("arbitrary", "arbitrary")),
        name="out_proj_route",
    )(d_out, f_out, w_out_d, w_out_f, x, mod, g_post_attn, g_pre_ffn, w_rt, b_rt)


def _sc_mesh():
    return plsc.VectorSubcoreMesh(core_axis_name="c", subcore_axis_name="s",
                                  num_cores=SC_CORES, num_subcores=SC_SUBCORES)


def _sc_worker_base(per_worker):
    return (lax.axis_index("s") * SC_CORES + lax.axis_index("c")) * per_worker


def _dispatch(dest, h2, n_rows):
    T, D = h2.shape
    per_worker = T // SC_WORKERS

    @functools.partial(
        pl.kernel, mesh=_sc_mesh(),
        out_type=jax.ShapeDtypeStruct((n_rows, D), h2.dtype),
        scratch_types=[pltpu.VMEM((SC_WIN,), jnp.int32), pltpu.VMEM((SC_WIN, D), h2.dtype)],
        name="moe_dispatch")
    def run(h_hbm, dest_hbm, xs_hbm, idx_v, rows_v):
        base = _sc_worker_base(per_worker)

        @pl.loop(0, per_worker // SC_WIN)
        def _(ci):
            off = pl.multiple_of(base + ci * SC_WIN, SC_WIN)
            pltpu.sync_copy(h_hbm.at[pl.ds(off, SC_WIN)], rows_v)
            for k in range(2):
                pltpu.sync_copy(dest_hbm.at[pl.ds(k * T + off, SC_WIN)], idx_v)
                pltpu.sync_copy(rows_v, xs_hbm.at[idx_v])

    return run(h2, dest.reshape(2 * T))


def _gather_rows(ys, dest):
    _, D = ys.shape
    K, T = dest.shape
    n = K * T
    per_worker = n // SC_WORKERS

    @functools.partial(
        pl.kernel, mesh=_sc_mesh(),
        out_type=jax.ShapeDtypeStruct((n, D), ys.dtype),
        scratch_types=[pltpu.VMEM((SC_WIN,), jnp.int32), pltpu.VMEM((SC_WIN, D), ys.dtype)],
        name="moe_gather")
    def run(ys_hbm, dest_hbm, g_hbm, idx_v, rows_v):
        base = _sc_worker_base(per_worker)

        @pl.loop(0, per_worker // SC_WIN)
        def _(ci):
            off = pl.multiple_of(base + ci * SC_WIN, SC_WIN)
            pltpu.sync_copy(dest_hbm.at[pl.ds(off, SC_WIN)], idx_v)
            pltpu.sync_copy(ys_hbm.at[idx_v], rows_v)
            pltpu.sync_copy(rows_v, g_hbm.at[pl.ds(off, SC_WIN)])

    return run(ys, dest.reshape(n)).reshape(K, T, D)


def _expert_kernel(be_ref, nu_ref, nv_ref, xs_ref, wg_ref, wu_ref, wd_ref, ys_ref,
                   wgb, wub, wdb):
    i = pl.program_id(0)

    @pl.when(i < nu_ref[0])
    def _():
        prev = be_ref[jnp.maximum(i - 1, 0)]

        @pl.when((i == 0) | (be_ref[i] != prev))
        def _():
            wgb[...] = wg_ref[...].astype(BF16)
            wub[...] = wu_ref[...].astype(BF16)
            wdb[...] = wd_ref[...].astype(BF16)

        row = lax.broadcasted_iota(jnp.int32, xs_ref.shape, 0)
        lo, hi = _unpack_halves(jnp.where(row < nv_ref[i], xs_ref[...], jnp.uint32(0)))
        lo, hi = lo.astype(BF16), hi.astype(BF16)
        half = lo.shape[1]

        def proj(w, c0, c1):
            return (jnp.dot(lo, w[0:half, c0:c1], preferred_element_type=F32)
                    + jnp.dot(hi, w[half:2 * half, c0:c1], preferred_element_type=F32))

        fd = wgb.shape[1]
        step = fd // EXPERT_COL_GROUPS
        au = [(proj(wgb, c, c + step), proj(wub, c, c + step)) for c in range(0, fd, step)]
        y = None
        for gi, (a, u) in enumerate(au):
            hmid = (a * jax.nn.sigmoid(a) * u).astype(BF16)
            part = jnp.dot(hmid, wdb[gi * step:(gi + 1) * step, :], preferred_element_type=F32)
            y = part if y is None else y + part
        ys_ref[...] = _pack_halves(y)

    @pl.when(i >= nu_ref[0])
    def _():
        ys_ref[...] = jnp.zeros_like(ys_ref)


def _experts(block_e, n_used, n_valid, xs, w_gate, w_up, w_down):
    P, Dh = xs.shape
    nb = P // BM
    _, D, Fd = w_gate.shape

    def row_map(i, be, nu, nv):
        return (jnp.minimum(i, nu[0] - 1), 0)

    def w_map(i, be, nu, nv):
        return (be[i], 0, 0)

    return pl.pallas_call(
        _expert_kernel,
        grid_spec=pltpu.PrefetchScalarGridSpec(
            num_scalar_prefetch=3,
            grid=(nb,),
            in_specs=[pl.BlockSpec((BM, Dh), row_map),
                      pl.BlockSpec((None, D, Fd), w_map),
                      pl.BlockSpec((None, D, Fd), w_map),
                      pl.BlockSpec((None, Fd, D), w_map)],
            out_specs=pl.BlockSpec((BM, Dh), lambda i, be, nu, nv: (i, 0)),
            scratch_shapes=[pltpu.VMEM((D, Fd), BF16), pltpu.VMEM((D, Fd), BF16),
                            pltpu.VMEM((Fd, D), BF16)]),
        out_shape=jax.ShapeDtypeStruct((P, Dh), jnp.uint32),
        compiler_params=_cparams(("arbitrary",)),
        name="moe_experts",
    )(block_e, n_used, n_valid, xs, w_gate, w_up, w_down)


def _combine_kernel(g_ref, w_ref, x1_ref, mod_ref, g_post_ref, o_ref):
    lo0, hi0 = _unpack_halves(g_ref[0])
    lo1, hi1 = _unpack_halves(g_ref[1])
    tm = lo0.shape[0]
    wt = jnp.concatenate([_rows_to_columns(w_ref[:, j * LANES:(j + 1) * LANES])
                          for j in range(tm // LANES)], axis=0)
    w0, w1 = wt[:, 0:1], wt[:, 1:2]
    y = jnp.concatenate([w0 * lo0 + w1 * lo1, w0 * hi0 + w1 * hi1], axis=1)
    o_ref[...] = x1_ref[...] + mod_ref[5:6, :] * _rms(y, g_post_ref[...])


def _combine(g, wts, x1, mod, g_post_ffn, S):
    T, D = x1.shape
    tm = TM_ROWS
    per_b = S // tm
    return pl.pallas_call(
        _combine_kernel,
        grid=(T // tm,),
        in_specs=[pl.BlockSpec((2, tm, D // 2), lambda i: (0, i, 0)),
                  pl.BlockSpec((8, tm), lambda i: (0, i)),
                  pl.BlockSpec((tm, D), lambda i: (i, 0)),
                  pl.BlockSpec((None, 6, D), lambda i: (i // per_b, 0, 0)),
                  pl.BlockSpec((1, D), lambda i: (0, 0))],
        out_specs=pl.BlockSpec((tm, D), lambda i: (i, 0)),
        out_shape=jax.ShapeDtypeStruct((T, D), F32),
        compiler_params=_cparams(("arbitrary",)),
        name="moe_combine",
    )(g, wts, x1, mod, g_post_ffn)


def _rope_tables(S):
    inv = 1.0 / (10000.0 ** (jnp.arange(0, HEAD_DIM, 2, dtype=F32) / HEAD_DIM))
    ang = jnp.arange(S, dtype=F32)[:, None] * inv[None, :]
    c, s = jnp.cos(ang), jnp.sin(ang)
    return jnp.tile(c, (1, 4)), jnp.tile(jnp.concatenate([-s, s], axis=1), (1, 2))


def kernel(x, c, w_ada, b_ada, g_pre_attn, g_post_attn, w_in, b_forget, lambda_q1, lambda_k1,
           lambda_q2, lambda_k2, g_diff_subln, w_out, g_pre_ffn, g_post_ffn, w_group, b_group,
           w_router, b_router, w_gate, w_up, w_down):
    B, S, D = x.shape
    T = B * S
    l = 0
    mod = _ada(c, w_ada[l], b_ada[l]).reshape(B, 6, D)

    wi = w_in[l]
    w_main = jnp.concatenate([wi[:, 0:2 * SEC], wi[:, 3 * SEC:5 * SEC]], axis=1).astype(BF16)
    w_vt = jnp.concatenate([wi[:, 2 * SEC:3 * SEC], wi[:, 5 * SEC:6 * SEC]], axis=1).T.astype(BF16)
    wf_t = wi[:, 6 * SEC:].T.astype(BF16)
    bf = jnp.broadcast_to(b_forget[l][:, None], (FOX_HEADS, LANES))
    cos, sin = _rope_tables(S)
    dq, dk, fq, fk, dvt, fvt, fcum = _inproj(x, mod, g_pre_attn[l][None, :], w_main, w_vt, wf_t,
                                             bf, cos, sin)

    lam_params = jnp.stack([lambda_q1[l], lambda_k1[l], lambda_q2[l], lambda_k2[l]])
    d_out = _diff_attention(dq, dk, dvt, lam_params, g_diff_subln[l][None, :])
    frow = fcum.reshape(B, FOX_HEADS // 2, 2, S)
    f_out = _fox_attention(fq, fk, fvt, frow)

    w_o = w_out[l].astype(BF16)
    zpad = jnp.zeros((8 - N_GROUPS, D), F32)
    w_rt = jnp.concatenate([w_group[l].T, zpad, w_router[l].T], axis=0)
    b_rt = jnp.concatenate([b_group[l], jnp.zeros((8 - N_GROUPS,), F32), b_router[l]])
    b_rt = jnp.broadcast_to(b_rt[:, None], (8 + N_EXPERTS, LANES))
    x1, h2, ri, rw, cnt = _route(d_out, f_out, w_o[:SEC], w_o[SEC:], x, mod,
                                 g_post_attn[l][None, :], g_pre_ffn[l][None, :], w_rt, b_rt)

    counts = cnt[:, 0].astype(jnp.int32)
    padded = ((counts + BM - 1) // BM) * BM
    pend = jnp.cumsum(padded)
    pstart = pend - padded
    eids = jnp.arange(N_EXPERTS, dtype=jnp.int32)
    dest = ri[2:4] + jnp.sum(jnp.where(ri[0:2, :, None] == eids, pstart, 0), axis=-1)
    nb = (2 * T) // BM + N_EXPERTS
    n_used = (pend[-1] // BM).astype(jnp.int32).reshape(1)
    blk0 = jnp.arange(nb, dtype=jnp.int32) * BM
    block_e = jnp.minimum(jnp.sum((blk0[:, None] >= pend[None, :]).astype(jnp.int32), axis=1),
                          N_EXPERTS - 1)
    cend = pstart + counts
    blk_end = jnp.sum(jnp.where(block_e[:, None] == eids, cend, 0), axis=-1)
    n_valid = jnp.clip(blk_end - blk0, 0, BM).astype(jnp.int32)

    xs = _dispatch(dest, h2.reshape(T, D // 2), nb * BM)
    ys = _experts(block_e, n_used, n_valid, xs, w_gate[l], w_up[l], w_down[l])
    g = _gather_rows(ys, dest)
    out = _combine(g, rw, x1.reshape(T, D), mod, g_post_ffn[l][None, :], S)
    return out.reshape(B, S, D)
```

```python
import functools
import math

import jax
import jax.numpy as jnp
from jax import lax
from jax.experimental import pallas as pl
from jax.experimental.pallas import tpu as pltpu
from jax.experimental.pallas import tpu_sc as plsc

F32 = jnp.float32
BF16 = jnp.bfloat16

D_MODEL = 1024
HEAD_DIM = 64
CHUNK = 64
DIFF_HEADS = 4
FOX_HEADS = 8
SEC = 512
N_GROUPS = 4
EXPERTS_PER_GROUP = 8
N_EXPERTS = 32
D_EXPERT = 512
EPS = 1e-6
NEG_INF = -1e30
LOG2E = 1.4426950408889634
LAMBDA_INIT = 0.8 - 0.6 * math.exp(-0.3 * 0)

LANES = 128
TM_PROJ = 1024
TQ = 512
TK = 512
HK = TK // 2
ATT_GROUPS = 1
DV_ROWS = 2 * HEAD_DIM + 16
TM_ROUTE = 512
ROUTE_ROW_GROUPS = 2
BM = 512
EXPERT_COL_GROUPS = 2
TM_ROWS = 512
COMBINE_CHUNKS = 4
SC_CORES = 2
SC_SUBCORES = 16
SC_WORKERS = SC_CORES * SC_SUBCORES
SC_WIN = 128
VMEM_LIMIT = 48 * 1024 * 1024
_NT = (((1,), (1,)), ((), ()))


def _cparams(sem, flags=None):
    return pltpu.CompilerParams(dimension_semantics=sem, vmem_limit_bytes=VMEM_LIMIT, flags=flags)


def _rms(x, g):
    return x * lax.rsqrt(jnp.mean(x * x, axis=-1, keepdims=True) + EPS) * g


def _pack_halves(x):
    m = x.shape[1] // 2
    bits = lax.bitcast_convert_type(x.astype(BF16).astype(F32), jnp.uint32)
    return (bits[:, :m] >> 16) | (bits[:, m:] & jnp.uint32(0xFFFF0000))


def _unpack_halves(w):
    lo = lax.bitcast_convert_type(w << 16, F32)
    hi = lax.bitcast_convert_type(w & jnp.uint32(0xFFFF0000), F32)
    return lo, hi


def _rows_to_columns(x):
    pad = jnp.zeros((LANES - x.shape[0], LANES), x.dtype)
    return jnp.concatenate([x, pad], axis=0).T


def _ada_kernel(c_ref, w_ref, b_ref, o_ref):
    c = c_ref[...]
    cs = c * jax.nn.sigmoid(c)
    o_ref[...] = jnp.dot(cs, w_ref[...], preferred_element_type=F32,
                         precision=lax.Precision.HIGHEST) + b_ref[...]


def _ada(c, w_ada, b_ada):
    B, D = c.shape
    N = w_ada.shape[1]
    tn = 1024
    return pl.pallas_call(
        _ada_kernel,
        grid=(N // tn,),
        in_specs=[pl.BlockSpec((B, D), lambda j: (0, 0)),
                  pl.BlockSpec((D, tn), lambda j: (0, j)),
                  pl.BlockSpec((1, tn), lambda j: (0, j))],
        out_specs=pl.BlockSpec((B, tn), lambda j: (0, j)),
        out_shape=jax.ShapeDtypeStruct((B, N), F32),
        compiler_params=_cparams(("arbitrary",)),
        name="ada_mod",
    )(c, w_ada, b_ada.reshape(1, N))


def _split3(x):
    hi = x.astype(BF16)
    r1 = x - hi.astype(F32)
    mid = r1.astype(BF16)
    lo = (r1 - mid.astype(F32)).astype(BF16)
    return hi, mid, lo


def _inproj_kernel(x_ref, mod_ref, g_ref, w_ref, wvt_ref, bf_ref, cos_ref, sin_ref,
                   dq_ref, dk_ref, fq_ref, fk_ref, dvt_ref, fvt_ref, fcum_ref, carry_ref):
    si = pl.program_id(1)
    x = x_ref[...]
    h = _rms(x, g_ref[...]) * (1.0 + mod_ref[1:2, :]) + mod_ref[0:1, :]
    hb = h.astype(BF16)
    tm = x.shape[0]

    cos = cos_ref[...]
    sin = sin_ref[...]
    lane = lax.broadcasted_iota(jnp.int32, (tm, LANES), 1)
    first_half = (lane % HEAD_DIM) < (HEAD_DIM // 2)
    qscale = HEAD_DIM ** -0.5 * LOG2E

    def proj(sec):
        return jnp.dot(hb, w_ref[:, sec * SEC:(sec + 1) * SEC], preferred_element_type=F32)

    def rope_store(p, out_ref, scale):
        for j in range(SEC // LANES):
            ch = p[:, j * LANES:(j + 1) * LANES]
            partner = jnp.where(first_half,
                                pltpu.roll(ch, LANES - HEAD_DIM // 2, 1),
                                pltpu.roll(ch, HEAD_DIM // 2, 1))
            r = ch * cos + partner * sin
            if scale != 1.0:
                r = r * scale
            out_ref[:, j * LANES:(j + 1) * LANES] = r.astype(out_ref.dtype)

    rope_store(proj(0), dq_ref, qscale)
    rope_store(proj(1), dk_ref, 1.0)
    fq_ref[...] = (proj(2) * qscale).astype(fq_ref.dtype)
    fk_ref[...] = proj(3).astype(fk_ref.dtype)
    vt = lax.dot_general(wvt_ref[...], hb, _NT, preferred_element_type=F32)
    for hd in range(DIFF_HEADS):
        dvt_ref[hd, 0:LANES, :] = vt[hd * LANES:(hd + 1) * LANES].astype(dvt_ref.dtype)
        dvt_ref[hd, LANES:DV_ROWS, :] = jnp.ones((DV_ROWS - LANES, tm), dvt_ref.dtype)
    fvt_ref[...] = vt[SEC:2 * SEC].astype(fvt_ref.dtype)

    z = vt[2 * SEC:2 * SEC + FOX_HEADS] + bf_ref[:, 0:1]
    lf = (jnp.minimum(z, 0.0) - jnp.log1p(jnp.exp(-jnp.abs(z)))) * LOG2E

    @pl.when(si == 0)
    def _():
        carry_ref[...] = jnp.zeros_like(carry_ref)

    r_i = lax.broadcasted_iota(jnp.int32, (LANES, LANES), 0)
    c_i = lax.broadcasted_iota(jnp.int32, (LANES, LANES), 1)
    tri = jnp.where(r_i <= c_i, 1.0, 0.0).astype(BF16)
    run = carry_ref[...]
    for j in range(tm // LANES):
        hi, mid, lo = _split3(lf[:, j * LANES:(j + 1) * LANES])
        cs = (jnp.dot(hi, tri, preferred_element_type=F32)
              + jnp.dot(mid, tri, preferred_element_type=F32)
              + jnp.dot(lo, tri, preferred_element_type=F32)) + run
        fcum_ref[:, j * LANES:(j + 1) * LANES] = cs
        run = jnp.broadcast_to(cs[:, LANES - 1:LANES], run.shape)
    carry_ref[...] = run


def _inproj(x, mod, g_pre, w_main, w_vt, bf, cos, sin):
    B, S, D = x.shape
    tm = TM_PROJ
    sec_out = jax.ShapeDtypeStruct((B, S, SEC), BF16)
    sec_spec = pl.BlockSpec((None, tm, SEC), lambda b, s: (b, s, 0))
    t_out = jax.ShapeDtypeStruct((B, SEC, S), BF16)
    t_spec = pl.BlockSpec((None, SEC, tm), lambda b, s: (b, 0, s))
    return pl.pallas_call(
        _inproj_kernel,
        grid=(B, S // tm),
        in_specs=[pl.BlockSpec((None, tm, D), lambda b, s: (b, s, 0)),
                  pl.BlockSpec((None, 6, D), lambda b, s: (b, 0, 0)),
                  pl.BlockSpec((1, D), lambda b, s: (0, 0)),
                  pl.BlockSpec((D, 4 * SEC), lambda b, s: (0, 0)),
                  pl.BlockSpec(w_vt.shape, lambda b, s: (0, 0)),
                  pl.BlockSpec((FOX_HEADS, LANES), lambda b, s: (0, 0)),
                  pl.BlockSpec((tm, LANES), lambda b, s: (s, 0)),
                  pl.BlockSpec((tm, LANES), lambda b, s: (s, 0))],
        out_specs=[sec_spec] * 4
        + [pl.BlockSpec((None, DIFF_HEADS, DV_ROWS, tm), lambda b, s: (b, 0, 0, s)), t_spec,
           pl.BlockSpec((None, FOX_HEADS, tm), lambda b, s: (b, 0, s))],
        out_shape=[sec_out] * 4
        + [jax.ShapeDtypeStruct((B, DIFF_HEADS, DV_ROWS, S), BF16), t_out,
           jax.ShapeDtypeStruct((B, FOX_HEADS, S), F32)],
        scratch_shapes=[pltpu.VMEM((FOX_HEADS, LANES), F32)],
        compiler_params=_cparams(("arbitrary", "arbitrary")),
        name="in_proj",
    )(x, mod, g_pre, w_main, w_vt, bf, cos, sin)


def _scores(ch, r0, qm=None):
    r0 = pl.multiple_of(r0, HK)
    t = lax.dot_general(ch["keys"](r0), ch["qm"] if qm is None else qm, _NT,
                        preferred_element_type=F32)
    if ch["kbias"] is not None:
        t = t - ch["kbias"](r0)
    return t


def _flash_t(chains, sa_ref, qi, diag_mask, next_qm):
    def update(ch, state, t, r0):
        m, acc = state
        mt = jnp.max(t, axis=0, keepdims=True)
        qb = ch["qbias"]
        m_new = jnp.maximum(m, mt if qb is None else mt + qb)
        alpha = jnp.exp2(m - m_new)
        p = jnp.exp2((t - (m_new if qb is None else m_new - qb)).astype(BF16))
        vt = ch["vt"](pl.multiple_of(r0, HK))
        acc = alpha * acc + jnp.dot(vt, p, preferred_element_type=F32)
        return m_new, acc

    def body(ki, carry):
        k0 = ki * TK
        tb = [_scores(ch, k0 + HK) for ch in chains]
        st = [update(ch, s, sa_ref[ci], k0) for ci, (ch, s) in enumerate(zip(chains, carry))]
        ta = [_scores(ch, k0 + TK) for ch in chains]
        st = [update(ch, s, tb[ci], k0 + HK) for ci, (ch, s) in enumerate(zip(chains, st))]
        for ci in range(len(chains)):
            sa_ref[ci] = ta[ci]
        return tuple(st)

    init = tuple((jnp.full((1, TQ), NEG_INF, F32), jnp.zeros((ch["rows"], TQ), F32))
                 for ch in chains)
    carry = lax.fori_loop(0, qi, body, init)
    k0 = qi * TK
    hq = TQ - HK
    late = [dict(ch, qbias=None if ch["qbias"] is None else ch["qbias"][:, hq:]) for ch in chains]
    tb = [_scores(ch, k0 + HK, ch["qm"][hq:]) for ch in chains]
    nxt = [_scores(ch, 0, qm) for ch, qm in zip(chains, next_qm)]
    st = [update(ch, s, jnp.where(diag_mask[0:HK], sa_ref[ci], NEG_INF), k0)
          for ci, (ch, s) in enumerate(zip(chains, carry))]
    out = []
    for ci, (ch, (m, acc)) in enumerate(zip(late, st)):
        m_l, acc_l = update(ch, (m[:, hq:], acc[:, hq:]),
                            jnp.where(diag_mask[HK:TK, hq:], tb[ci], NEG_INF), k0 + HK)
        out.append((jnp.concatenate([m[:, :hq], m_l], axis=1),
                    jnp.concatenate([acc[:, :hq], acc_l], axis=1)))
    for ci in range(len(chains)):
        sa_ref[ci] = nxt[ci]
    return tuple(out)


def _next_tile_start(q0, seq_len):
    return pl.multiple_of(jnp.minimum(q0 + TQ, seq_len - TQ), TQ)


def _for_each_query_tile(n_tiles, tile_fn):
    def body(qi, carry):
        tile_fn(qi, pl.multiple_of(qi * TQ, TQ))
        return carry

    lax.fori_loop(0, n_tiles, body, 0)


def _lane_group(ref, rows, g):
    return ref[rows, g * LANES:(g + 1) * LANES]


def _masked_queries(q_ref, q0):
    lane = lax.broadcasted_iota(jnp.int32, (TQ, LANES), 1)
    out = []
    for g in range(ATT_GROUPS):
        q2 = _lane_group(q_ref, pl.ds(q0, TQ), g)
        out += [jnp.where((lane >= j * HEAD_DIM) & (lane < (j + 1) * HEAD_DIM), q2,
                          jnp.zeros_like(q2)) for j in range(2)]
    return out


def _fox_kernel(q_ref, k_ref, vt_ref, frow_ref, o_ref, sa_ref, fcol_ref):
    for g in range(ATT_GROUPS):
        for j in range(frow_ref.shape[2] // LANES):
            fcol_ref[g, j * LANES:(j + 1) * LANES, :] = _rows_to_columns(
                frow_ref[g, :, j * LANES:(j + 1) * LANES])

    vrow = lax.broadcasted_iota(jnp.int32, (LANES, HK), 0)
    r = lax.broadcasted_iota(jnp.int32, (TK, TQ), 0)
    c = lax.broadcasted_iota(jnp.int32, (TK, TQ), 1)

    def make_chains(q0):
        chains = []
        for ci, qm in enumerate(_masked_queries(q_ref, q0)):
            g, j = divmod(ci, 2)
            v_rows = (vrow >= j * HEAD_DIM) & (vrow < (j + 1) * HEAD_DIM)

            def vt(k0, g=g, v_rows=v_rows):
                v = vt_ref[g * LANES:(g + 1) * LANES, pl.ds(k0, HK)]
                return jnp.where(v_rows, v, jnp.ones_like(v))

            chains.append(dict(
                qm=qm, keys=lambda r0, g=g: _lane_group(k_ref, pl.ds(r0, HK), g),
                kbias=lambda r0, g=g, j=j: fcol_ref[g, pl.ds(r0, HK), j:j + 1],
                qbias=frow_ref[g, j:j + 1, pl.ds(q0, TQ)], vt=vt, rows=LANES))
        return chains

    def tile(qi, q0):
        res = _flash_t(make_chains(q0), sa_ref, qi, r <= c,
                       _masked_queries(q_ref, _next_tile_start(q0, q_ref.shape[0])))
        h = HEAD_DIM
        for g in range(ATT_GROUPS):
            a0, a1 = res[2 * g][1], res[2 * g + 1][1]
            o_t = jnp.concatenate([a0[0:h] / a0[h:2 * h], a1[h:2 * h] / a1[0:h]], axis=0)
            o_ref[pl.ds(q0, TQ), g * LANES:(g + 1) * LANES] = o_t.T.astype(o_ref.dtype)

    for ci, ch in enumerate(make_chains(0)):
        sa_ref[ci] = _scores(ch, 0)
    _for_each_query_tile(q_ref.shape[0] // TQ, tile)


def _fox_attention(fq, fk, fvt, frow):
    B, S, _ = fq.shape
    gw = ATT_GROUPS * LANES
    seq_spec = pl.BlockSpec((None, S, gw), lambda b, p: (b, 0, p))
    return pl.pallas_call(
        _fox_kernel,
        grid=(B, FOX_HEADS // 2 // ATT_GROUPS),
        in_specs=[seq_spec, seq_spec,
                  pl.BlockSpec((None, gw, S), lambda b, p: (b, p, 0)),
                  pl.BlockSpec((None, ATT_GROUPS, 2, S), lambda b, p: (b, p, 0, 0))],
        out_specs=seq_spec,
        out_shape=jax.ShapeDtypeStruct((B, S, SEC), BF16),
        scratch_shapes=[pltpu.VMEM((2 * ATT_GROUPS, HK, TQ), F32),
                        pltpu.VMEM((ATT_GROUPS, S, LANES), F32)],
        compiler_params=_cparams(("arbitrary", "arbitrary")),
        name="fox_attention",
    )(fq, fk, fvt, frow)


def _diff_kernel(q_ref, k_ref, vt_ref, lam_ref, g_ref, o_ref, sa_ref):
    r = lax.broadcasted_iota(jnp.int32, (TK, TQ), 0)
    c = lax.broadcasted_iota(jnp.int32, (TK, TQ), 1)
    lp = lam_ref[...]
    lam = (jnp.exp(jnp.sum(lp[0:1] * lp[1:2], axis=1, keepdims=True))
           - jnp.exp(jnp.sum(lp[2:3] * lp[3:4], axis=1, keepdims=True)) + LAMBDA_INIT)

    def make_chains(q0):
        return [dict(qm=qm, keys=lambda r0, g=ci // 2: _lane_group(k_ref, pl.ds(r0, HK), g),
                     kbias=None, qbias=None,
                     vt=lambda k0, g=ci // 2: vt_ref[g, :, pl.ds(k0, HK)], rows=DV_ROWS)
                for ci, qm in enumerate(_masked_queries(q_ref, q0))]

    def tile(qi, q0):
        res = _flash_t(make_chains(q0), sa_ref, qi, (r // CHUNK) <= (c // CHUNK),
                       _masked_queries(q_ref, _next_tile_start(q0, q_ref.shape[0])))
        v = LANES
        for g in range(ATT_GROUPS):
            a0, a1 = res[2 * g][1], res[2 * g + 1][1]
            o = (a0[0:v] / a0[v:v + 1] - lam * (a1[0:v] / a1[v:v + 1])).T
            o_ref[pl.ds(q0, TQ), g * LANES:(g + 1) * LANES] = (
                _rms(o, g_ref[...]) * (1.0 - LAMBDA_INIT)).astype(o_ref.dtype)

    for ci, ch in enumerate(make_chains(0)):
        sa_ref[ci] = _scores(ch, 0)
    _for_each_query_tile(q_ref.shape[0] // TQ, tile)


def _diff_attention(dq, dk, dvt, lam_params, g_subln):
    B, S, _ = dq.shape
    seq_spec = pl.BlockSpec((None, S, ATT_GROUPS * LANES), lambda b, h: (b, 0, h))
    return pl.pallas_call(
        _diff_kernel,
        grid=(B, DIFF_HEADS // ATT_GROUPS),
        in_specs=[seq_spec, seq_spec,
                  pl.BlockSpec((None, ATT_GROUPS, DV_ROWS, S), lambda b, h: (b, h, 0, 0)),
                  pl.BlockSpec((4, HEAD_DIM), lambda b, h: (0, 0)),
                  pl.BlockSpec((1, LANES), lambda b, h: (0, 0))],
        out_specs=seq_spec,
        out_shape=jax.ShapeDtypeStruct((B, S, SEC), BF16),
        scratch_shapes=[pltpu.VMEM((2 * ATT_GROUPS, HK, TQ), F32)],
        compiler_params=_cparams(("arbitrary", "arbitrary")),
        name="diff_attention",
    )(dq, dk, dvt, lam_params, g_subln)


def _route_kernel(d_ref, f_ref, wd_ref, wf_ref, x_ref, mod_ref, gpa_ref, gpf_ref, wr_ref, br_ref,
                  x1_ref, h2_ref, ri_ref, rw_ref, cnt_ref):
    first = (pl.program_id(0) == 0) & (pl.program_id(1) == 0)

    @pl.when(first)
    def _():
        cnt_ref[...] = jnp.zeros_like(cnt_ref)

    tm = x_ref.shape[0]
    rows = tm // ROUTE_ROW_GROUPS
    mixed = [jnp.dot(d_ref[r0:r0 + rows, :], wd_ref[...], preferred_element_type=F32)
             + jnp.dot(f_ref[r0:r0 + rows, :], wf_ref[...], preferred_element_type=F32)
             for r0 in range(0, tm, rows)]
    wh, wm, _ = _split3(wr_ref[...])
    lts = []
    for gi, r0 in enumerate(range(0, tm, rows)):
        x1 = x_ref[r0:r0 + rows, :] + mod_ref[2:3, :] * _rms(mixed[gi], gpa_ref[...])
        x1_ref[r0:r0 + rows, :] = x1
        h2 = _rms(x1, gpf_ref[...]) * (1.0 + mod_ref[4:5, :]) + mod_ref[3:4, :]
        h2_ref[r0:r0 + rows, :] = _pack_halves(h2)
        hh, hm, _ = _split3(h2)
        lt_g = br_ref[:, 0:1]
        for a, b_ in ((wh, hh), (wh, hm), (wm, hh)):
            lt_g = lt_g + lax.dot_general(a, b_, _NT, preferred_element_type=F32)
        lts.append(lt_g)
    lt = jnp.concatenate(lts, axis=1)

    g = lt[0:N_GROUPS]
    row4 = lax.broadcasted_iota(jnp.int32, g.shape, 0)
    gmax = jnp.max(g, axis=0, keepdims=True)
    gidx = jnp.min(jnp.where(g == gmax, row4, N_GROUPS), axis=0, keepdims=True)
    gw = 1.0 / jnp.sum(jnp.exp(g - gmax), axis=0, keepdims=True)
    esel = jnp.zeros((EXPERTS_PER_GROUP, tm), F32)
    for gg in range(N_GROUPS):
        esel = jnp.where(gidx == gg, lt[8 + 8 * gg:16 + 8 * gg], esel)
    row8 = lax.broadcasted_iota(jnp.int32, esel.shape, 0)
    v1 = jnp.max(esel, axis=0, keepdims=True)
    i1 = jnp.min(jnp.where(esel == v1, row8, EXPERTS_PER_GROUP), axis=0, keepdims=True)
    esel2 = jnp.where(row8 == i1, -jnp.inf, esel)
    v2 = jnp.max(esel2, axis=0, keepdims=True)
    i2 = jnp.min(jnp.where(esel2 == v2, row8, EXPERTS_PER_GROUP), axis=0, keepdims=True)
    e = jnp.exp(v2 - v1)
    w1 = gw / (1.0 + e)
    w2 = gw * e / (1.0 + e)
    e1 = gidx * EXPERTS_PER_GROUP + i1
    e2 = gidx * EXPERTS_PER_GROUP + i2

    row32 = lax.broadcasted_iota(jnp.int32, (N_EXPERTS, tm), 0)
    oh1 = row32 == e1
    oh2 = row32 == e2
    oh = jnp.where(oh1 | oh2, 1.0, 0.0)
    t_r = lax.broadcasted_iota(jnp.int32, (tm, tm), 0)
    t_c = lax.broadcasted_iota(jnp.int32, (tm, tm), 1)
    tri = jnp.where(t_r < t_c, 1.0, 0.0).astype(BF16)
    tot = jnp.dot(oh.astype(BF16), tri, preferred_element_type=F32) + cnt_ref[:, 0:1]
    r1 = jnp.sum(jnp.where(oh1, tot, 0.0), axis=0, keepdims=True)
    r2 = jnp.sum(jnp.where(oh2, tot, 0.0), axis=0, keepdims=True)
    cnt_ref[...] = cnt_ref[...] + jnp.sum(oh, axis=1, keepdims=True)

    zi = jnp.zeros((4, tm), jnp.int32)
    ri_ref[...] = jnp.concatenate([e1, e2, r1.astype(jnp.int32), r2.astype(jnp.int32), zi], axis=0)
    rw_ref[...] = jnp.concatenate([w1, w2, jnp.zeros((6, tm), F32)], axis=0)


def _route(d_out, f_out, w_out_d, w_out_f, x, mod, g_post_attn, g_pre_ffn, w_rt, b_rt):
    B, S, D = x.shape
    tm = TM_ROUTE
    nrt = w_rt.shape[0]
    n_s = S // tm
    T = B * S
    return pl.pallas_call(
        _route_kernel,
        grid=(B, n_s),
        in_specs=[pl.BlockSpec((None, tm, SEC), lambda b, s: (b, s, 0)),
                  pl.BlockSpec((None, tm, SEC), lambda b, s: (b, s, 0)),
                  pl.BlockSpec((SEC, D), lambda b, s: (0, 0)),
                  pl.BlockSpec((SEC, D), lambda b, s: (0, 0)),
                  pl.BlockSpec((None, tm, D), lambda b, s: (b, s, 0)),
                  pl.BlockSpec((None, 6, D), lambda b, s: (b, 0, 0)),
                  pl.BlockSpec((1, D), lambda b, s: (0, 0)),
                  pl.BlockSpec((1, D), lambda b, s: (0, 0)),
                  pl.BlockSpec((nrt, D), lambda b, s: (0, 0)),
                  pl.BlockSpec((nrt, LANES), lambda b, s: (0, 0))],
        out_specs=[pl.BlockSpec((None, tm, D), lambda b, s: (b, s, 0)),
                   pl.BlockSpec((None, tm, D // 2), lambda b, s: (b, s, 0)),
                   pl.BlockSpec((8, tm), lambda b, s: (0, b * n_s + s)),
                   pl.BlockSpec((8, tm), lambda b, s: (0, b * n_s + s)),
                   pl.BlockSpec((N_EXPERTS, LANES), lambda b, s: (0, 0))],
        out_shape=[jax.ShapeDtypeStruct((B, S, D), F32),
                   jax.ShapeDtypeStruct((B, S, D // 2), jnp.uint32),
                   jax.ShapeDtypeStruct((8, T), jnp.int32),
                   jax.ShapeDtypeStruct((8, T), F32),
                   jax.ShapeDtypeStruct((N_EXPERTS, LANES), F32)],
        compiler_params=_cparams(("arbitrary", "arbitrary")),
        name="out_proj_route",
    )(d_out, f_out, w_out_d, w_out_f, x, mod, g_post_attn, g_pre_ffn, w_rt, b_rt)


def _sc_mesh():
    return plsc.VectorSubcoreMesh(core_axis_name="c", subcore_axis_name="s",
                                  num_cores=SC_CORES, num_subcores=SC_SUBCORES)


def _sc_worker_base(per_worker):
    return (lax.axis_index("s") * SC_CORES + lax.axis_index("c")) * per_worker


def _dispatch(dest, h2, n_rows):
    T, D = h2.shape
    per_worker = T // SC_WORKERS

    @functools.partial(
        pl.kernel, mesh=_sc_mesh(),
        out_type=jax.ShapeDtypeStruct((n_rows, D), h2.dtype),
        scratch_types=[pltpu.VMEM((SC_WIN,), jnp.int32), pltpu.VMEM((SC_WIN, D), h2.dtype)],
        name="moe_dispatch")
    def run(h_hbm, dest_hbm, xs_hbm, idx_v, rows_v):
        base = _sc_worker_base(per_worker)

        @pl.loop(0, per_worker // SC_WIN)
        def _(ci):
            off = pl.multiple_of(base + ci * SC_WIN, SC_WIN)
            pltpu.sync_copy(h_hbm.at[pl.ds(off, SC_WIN)], rows_v)
            for k in range(2):
                pltpu.sync_copy(dest_hbm.at[pl.ds(k * T + off, SC_WIN)], idx_v)
                pltpu.sync_copy(rows_v, xs_hbm.at[idx_v])

    return run(h2, dest.reshape(2 * T))


def _gather_rows(ys, dest):
    _, D = ys.shape
    K, T = dest.shape
    n = K * T
    per_worker = n // SC_WORKERS

    @functools.partial(
        pl.kernel, mesh=_sc_mesh(),
        out_type=jax.ShapeDtypeStruct((n, D), ys.dtype),
        scratch_types=[pltpu.VMEM((SC_WIN,), jnp.int32), pltpu.VMEM((SC_WIN, D), ys.dtype)],
        name="moe_gather")
    def run(ys_hbm, dest_hbm, g_hbm, idx_v, rows_v):
        base = _sc_worker_base(per_worker)

        @pl.loop(0, per_worker // SC_WIN)
        def _(ci):
            off = pl.multiple_of(base + ci * SC_WIN, SC_WIN)
            pltpu.sync_copy(dest_hbm.at[pl.ds(off, SC_WIN)], idx_v)
            pltpu.sync_copy(ys_hbm.at[idx_v], rows_v)
            pltpu.sync_copy(rows_v, g_hbm.at[pl.ds(off, SC_WIN)])

    return run(ys, dest.reshape(n)).reshape(K, T, D)


def _expert_kernel(be_ref, nu_ref, nv_ref, xs_ref, wg_ref, wu_ref, wd_ref, ys_ref,
                   wgb, wub, wdb):
    i = pl.program_id(0)

    @pl.when(i < nu_ref[0])
    def _():
        prev = be_ref[jnp.maximum(i - 1, 0)]

        @pl.when((i == 0) | (be_ref[i] != prev))
        def _():
            wgb[...] = wg_ref[...].astype(BF16)
            wub[...] = wu_ref[...].astype(BF16)
            wdb[...] = wd_ref[...].astype(BF16)

        row = lax.broadcasted_iota(jnp.int32, xs_ref.shape, 0)
        lo, hi = _unpack_halves(jnp.where(row < nv_ref[i], xs_ref[...], jnp.uint32(0)))
        lo, hi = lo.astype(BF16), hi.astype(BF16)
        half = lo.shape[1]

        def proj(w, c0, c1):
            return (jnp.dot(lo, w[0:half, c0:c1], preferred_element_type=F32)
                    + jnp.dot(hi, w[half:2 * half, c0:c1], preferred_element_type=F32))

        fd = wgb.shape[1]
        step = fd // EXPERT_COL_GROUPS
        au = [(proj(wgb, c, c + step), proj(wub, c, c + step)) for c in range(0, fd, step)]
        y = None
        for gi, (a, u) in enumerate(au):
            hmid = (a * jax.nn.sigmoid(a) * u).astype(BF16)
            part = jnp.dot(hmid, wdb[gi * step:(gi + 1) * step, :], preferred_element_type=F32)
            y = part if y is None else y + part
        ys_ref[...] = _pack_halves(y)

    @pl.when(i >= nu_ref[0])
    def _():
        ys_ref[...] = jnp.zeros_like(ys_ref)


def _experts(block_e, n_used, n_valid, xs, w_gate, w_up, w_down):
    P, Dh = xs.shape
    nb = P // BM
    _, D, Fd = w_gate.shape

    def row_map(i, be, nu, nv):
        return (jnp.minimum(i, nu[0] - 1), 0)

    def w_map(i, be, nu, nv):
        return (be[i], 0, 0)

    return pl.pallas_call(
        _expert_kernel,
        grid_spec=pltpu.PrefetchScalarGridSpec(
            num_scalar_prefetch=3,
            grid=(nb,),
            in_specs=[pl.BlockSpec((BM, Dh), row_map),
                      pl.BlockSpec((None, D, Fd), w_map),
                      pl.BlockSpec((None, D, Fd), w_map),
                      pl.BlockSpec((None, Fd, D), w_map)],
            out_specs=pl.BlockSpec((BM, Dh), lambda i, be, nu, nv: (i, 0)),
            scratch_shapes=[pltpu.VMEM((D, Fd), BF16), pltpu.VMEM((D, Fd), BF16),
                            pltpu.VMEM((Fd, D), BF16)]),
        out_shape=jax.ShapeDtypeStruct((P, Dh), jnp.uint32),
        compiler_params=_cparams(("arbitrary",)),
        name="moe_experts",
    )(block_e, n_used, n_valid, xs, w_gate, w_up, w_down)


def _combine_kernel(g_ref, w_ref, x1_ref, mod_ref, g_post_ref, *rest):
    o_ref = rest[-1]
    lo0, hi0 = _unpack_halves(g_ref[0])
    lo1, hi1 = _unpack_halves(g_ref[1])
    tm = lo0.shape[0]
    wt = jnp.concatenate([_rows_to_columns(w_ref[:, j * LANES:(j + 1) * LANES])
                          for j in range(tm // LANES)], axis=0)
    w0, w1 = wt[:, 0:1], wt[:, 1:2]
    y = jnp.concatenate([w0 * lo0 + w1 * lo1, w0 * hi0 + w1 * hi1], axis=1)
    o_ref[...] = x1_ref[...] + mod_ref[5:6, :] * _rms(y, g_post_ref[...])


def _combine(g, wts, x1, mod, g_post_ffn, S, chunk, prev_out):
    T, D = x1.shape
    tm = TM_ROWS
    per_b = S // tm
    steps = g.shape[1] // tm
    off = chunk * steps
    in_specs = [pl.BlockSpec((2, tm, D // 2), lambda i: (0, i, 0)),
                pl.BlockSpec((8, tm), lambda i: (0, i + off)),
                pl.BlockSpec((tm, D), lambda i: (i + off, 0)),
                pl.BlockSpec((None, 6, D), lambda i: ((i + off) // per_b, 0, 0)),
                pl.BlockSpec((1, D), lambda i: (0, 0))]
    args = [g, wts, x1, mod, g_post_ffn]
    aliases = {}
    if prev_out is not None:
        in_specs.append(pl.BlockSpec(memory_space=pl.ANY))
        args.append(prev_out)
        aliases = {len(args) - 1: 0}
    return pl.pallas_call(
        _combine_kernel,
        grid=(steps,),
        in_specs=in_specs,
        out_specs=pl.BlockSpec((tm, D), lambda i: (i + off, 0)),
        out_shape=jax.ShapeDtypeStruct((T, D), F32),
        input_output_aliases=aliases,
        compiler_params=_cparams(("arbitrary",)),
        name="moe_combine",
    )(*args)


def _rope_tables(S):
    inv = 1.0 / (10000.0 ** (jnp.arange(0, HEAD_DIM, 2, dtype=F32) / HEAD_DIM))
    ang = jnp.arange(S, dtype=F32)[:, None] * inv[None, :]
    c, s = jnp.cos(ang), jnp.sin(ang)
    return jnp.tile(c, (1, 4)), jnp.tile(jnp.concatenate([-s, s], axis=1), (1, 2))


def kernel(x, c, w_ada, b_ada, g_pre_attn, g_post_attn, w_in, b_forget, lambda_q1, lambda_k1,
           lambda_q2, lambda_k2, g_diff_subln, w_out, g_pre_ffn, g_post_ffn, w_group, b_group,
           w_router, b_router, w_gate, w_up, w_down):
    B, S, D = x.shape
    T = B * S
    l = 0
    mod = _ada(c, w_ada[l], b_ada[l]).reshape(B, 6, D)

    wi = w_in[l]
    w_main = jnp.concatenate([wi[:, 0:2 * SEC], wi[:, 3 * SEC:5 * SEC]], axis=1).astype(BF16)
    w_vt = jnp.concatenate([wi[:, 2 * SEC:3 * SEC], wi[:, 5 * SEC:6 * SEC], wi[:, 6 * SEC:],
                            jnp.zeros((D, 16 - FOX_HEADS), F32)], axis=1).T.astype(BF16)
    bf = jnp.broadcast_to(b_forget[l][:, None], (FOX_HEADS, LANES))
    cos, sin = _rope_tables(S)
    dq, dk, fq, fk, dvt, fvt, fcum = _inproj(x, mod, g_pre_attn[l][None, :], w_main, w_vt,
                                             bf, cos, sin)

    lam_params = jnp.stack([lambda_q1[l], lambda_k1[l], lambda_q2[l], lambda_k2[l]])
    d_out = _diff_attention(dq, dk, dvt, lam_params, g_diff_subln[l][None, :])
    frow = fcum.reshape(B, FOX_HEADS // 2, 2, S)
    f_out = _fox_attention(fq, fk, fvt, frow)

    w_o = w_out[l].astype(BF16)
    zpad = jnp.zeros((8 - N_GROUPS, D), F32)
    w_rt = jnp.concatenate([w_group[l].T, zpad, w_router[l].T], axis=0)
    b_rt = jnp.concatenate([b_group[l], jnp.zeros((8 - N_GROUPS,), F32), b_router[l]])
    b_rt = jnp.broadcast_to(b_rt[:, None], (8 + N_EXPERTS, LANES))
    x1, h2, ri, rw, cnt = _route(d_out, f_out, w_o[:SEC], w_o[SEC:], x, mod,
                                 g_post_attn[l][None, :], g_pre_ffn[l][None, :], w_rt, b_rt)

    counts = cnt[:, 0].astype(jnp.int32)
    padded = ((counts + BM - 1) // BM) * BM
    pend = jnp.cumsum(padded)
    pstart = pend - padded
    eids = jnp.arange(N_EXPERTS, dtype=jnp.int32)
    dest = ri[2:4] + jnp.sum(jnp.where(ri[0:2, :, None] == eids, pstart, 0), axis=-1)
    nb = (2 * T) // BM + N_EXPERTS
    n_used = (pend[-1] // BM).astype(jnp.int32).reshape(1)
    blk0 = jnp.arange(nb, dtype=jnp.int32) * BM
    block_e = jnp.minimum(jnp.sum((blk0[:, None] >= pend[None, :]).astype(jnp.int32), axis=1),
                          N_EXPERTS - 1)
    cend = pstart + counts
    blk_end = jnp.sum(jnp.where(block_e[:, None] == eids, cend, 0), axis=-1)
    n_valid = jnp.clip(blk_end - blk0, 0, BM).astype(jnp.int32)

    xs = _dispatch(dest, h2.reshape(T, D // 2), nb * BM)
    ys = _experts(block_e, n_used, n_valid, xs, w_gate[l], w_up[l], w_down[l])
    out = None
    tc = T // COMBINE_CHUNKS
    for ck in range(COMBINE_CHUNKS):
        g = _gather_rows(ys, dest[:, ck * tc:(ck + 1) * tc])
        out = _combine(g, rw, x1.reshape(T, D), mod, g_post_ffn[l][None, :], S, ck, out)
    return out.reshape(B, S, D)
```

```python
import functools
import math

import jax
import jax.numpy as jnp
from jax import lax
from jax.experimental import pallas as pl
from jax.experimental.pallas import tpu as pltpu
from jax.experimental.pallas import tpu_sc as plsc

F32 = jnp.float32
BF16 = jnp.bfloat16

D_MODEL = 1024
HEAD_DIM = 64
CHUNK = 64
DIFF_HEADS = 4
FOX_HEADS = 8
SEC = 512
N_GROUPS = 4
EXPERTS_PER_GROUP = 8
N_EXPERTS = 32
D_EXPERT = 512
EPS = 1e-6
NEG_INF = -1e30
LOG2E = 1.4426950408889634
LAMBDA_INIT = 0.8 - 0.6 * math.exp(-0.3 * 0)

LANES = 128
SUBLANES = 8
BF16_SUBLANES = 16
TN_ADA = 1024
TM_PROJ = 1024
TQ = 512
TK = 512
HK = TK // 2
ATT_GROUPS = 1
DV_ROWS = 2 * HEAD_DIM + BF16_SUBLANES
TM_ROUTE = 512
ROUTE_ROW_GROUPS = 2
BM = 512
EXPERT_ROW_GROUPS = 2
TM_ROWS = 512
COMBINE_CHUNKS = 4
SC_CORES = 2
SC_SUBCORES = 16
SC_WORKERS = SC_CORES * SC_SUBCORES
SC_WIN = 128
VMEM_LIMIT = 48 * 1024 * 1024
_NT = (((1,), (1,)), ((), ()))


def _cparams(sem, flags=None):
    return pltpu.CompilerParams(dimension_semantics=sem, vmem_limit_bytes=VMEM_LIMIT, flags=flags)


def _rms(x, g):
    return x * lax.rsqrt(jnp.mean(x * x, axis=-1, keepdims=True) + EPS) * g


def _pack_halves(x):
    m = x.shape[1] // 2
    bits = lax.bitcast_convert_type(x.astype(BF16).astype(F32), jnp.uint32)
    return (bits[:, :m] >> 16) | (bits[:, m:] & jnp.uint32(0xFFFF0000))


def _unpack_halves(w):
    lo = lax.bitcast_convert_type(w << 16, F32)
    hi = lax.bitcast_convert_type(w & jnp.uint32(0xFFFF0000), F32)
    return lo, hi


def _rows_to_columns(x):
    pad = jnp.zeros((LANES - x.shape[0], LANES), x.dtype)
    return jnp.concatenate([x, pad], axis=0).T


def _ada_kernel(c_ref, w_ref, b_ref, o_ref):
    c = c_ref[...]
    cs = c * jax.nn.sigmoid(c)
    o_ref[...] = jnp.dot(cs, w_ref[...], preferred_element_type=F32,
                         precision=lax.Precision.HIGHEST) + b_ref[...]


def _ada(c, w_ada, b_ada):
    B, D = c.shape
    N = w_ada.shape[1]
    tn = TN_ADA
    return pl.pallas_call(
        _ada_kernel,
        grid=(N // tn,),
        in_specs=[pl.BlockSpec((B, D), lambda j: (0, 0)),
                  pl.BlockSpec((D, tn), lambda j: (0, j)),
                  pl.BlockSpec((1, tn), lambda j: (0, j))],
        out_specs=pl.BlockSpec((B, tn), lambda j: (0, j)),
        out_shape=jax.ShapeDtypeStruct((B, N), F32),
        compiler_params=_cparams(("arbitrary",)),
        name="ada_mod",
    )(c, w_ada, b_ada.reshape(1, N))


def _split3(x):
    hi = x.astype(BF16)
    r1 = x - hi.astype(F32)
    mid = r1.astype(BF16)
    lo = (r1 - mid.astype(F32)).astype(BF16)
    return hi, mid, lo


def _inproj_kernel(x_ref, mod_ref, g_ref, w_ref, wvt_ref, bf_ref, cos_ref, sin_ref,
                   dq_ref, dk_ref, fq_ref, fk_ref, dvt_ref, fvt_ref, fcum_ref, carry_ref):
    si = pl.program_id(1)
    x = x_ref[...]
    h = _rms(x, g_ref[...] * (1.0 + mod_ref[1:2, :])) + mod_ref[0:1, :]
    hb = h.astype(BF16)
    tm = x.shape[0]

    cos = cos_ref[...]
    sin = sin_ref[...]
    lane = lax.broadcasted_iota(jnp.int32, (tm, LANES), 1)
    first_half = (lane % HEAD_DIM) < (HEAD_DIM // 2)
    qscale = HEAD_DIM ** -0.5 * LOG2E

    def proj(sec):
        return jnp.dot(hb, w_ref[:, sec * SEC:(sec + 1) * SEC], preferred_element_type=F32)

    def rope_store(p, out_ref, scale):
        for j in range(SEC // LANES):
            ch = p[:, j * LANES:(j + 1) * LANES]
            partner = jnp.where(first_half,
                                pltpu.roll(ch, LANES - HEAD_DIM // 2, 1),
                                pltpu.roll(ch, HEAD_DIM // 2, 1))
            r = ch * cos + partner * sin
            if scale != 1.0:
                r = r * scale
            out_ref[:, j * LANES:(j + 1) * LANES] = r.astype(out_ref.dtype)

    rope_store(proj(0), dq_ref, qscale)
    rope_store(proj(1), dk_ref, 1.0)
    fq_ref[...] = (proj(2) * qscale).astype(fq_ref.dtype)
    fk_ref[...] = proj(3).astype(fk_ref.dtype)
    vt = lax.dot_general(wvt_ref[...], hb, _NT, preferred_element_type=F32)
    for hd in range(DIFF_HEADS):
        dvt_ref[hd, 0:LANES, :] = vt[hd * LANES:(hd + 1) * LANES].astype(dvt_ref.dtype)
        dvt_ref[hd, LANES:DV_ROWS, :] = jnp.ones((DV_ROWS - LANES, tm), dvt_ref.dtype)
    fvt_ref[...] = vt[SEC:2 * SEC].astype(fvt_ref.dtype)

    z = vt[2 * SEC:2 * SEC + FOX_HEADS] + bf_ref[:, 0:1]
    lf = (jnp.minimum(z, 0.0) - jnp.log1p(jnp.exp(-jnp.abs(z)))) * LOG2E

    @pl.when(si == 0)
    def _():
        carry_ref[...] = jnp.zeros_like(carry_ref)

    r_i = lax.broadcasted_iota(jnp.int32, (LANES, LANES), 0)
    c_i = lax.broadcasted_iota(jnp.int32, (LANES, LANES), 1)
    tri = jnp.where(r_i <= c_i, 1.0, 0.0).astype(BF16)
    run = carry_ref[...]
    for j in range(tm // LANES):
        hi, mid, lo = _split3(lf[:, j * LANES:(j + 1) * LANES])
        cs = (jnp.dot(hi, tri, preferred_element_type=F32)
              + jnp.dot(mid, tri, preferred_element_type=F32)
              + jnp.dot(lo, tri, preferred_element_type=F32)) + run
        fcum_ref[:, j * LANES:(j + 1) * LANES] = cs
        run = jnp.broadcast_to(cs[:, LANES - 1:LANES], run.shape)
    carry_ref[...] = run


def _inproj(x, mod, g_pre, w_main, w_vt, bf, cos, sin):
    B, S, D = x.shape
    tm = TM_PROJ
    sec_out = jax.ShapeDtypeStruct((B, S, SEC), BF16)
    sec_spec = pl.BlockSpec((None, tm, SEC), lambda b, s: (b, s, 0))
    t_out = jax.ShapeDtypeStruct((B, SEC, S), BF16)
    t_spec = pl.BlockSpec((None, SEC, tm), lambda b, s: (b, 0, s))
    return pl.pallas_call(
        _inproj_kernel,
        grid=(B, S // tm),
        in_specs=[pl.BlockSpec((None, tm, D), lambda b, s: (b, s, 0)),
                  pl.BlockSpec((None, 6, D), lambda b, s: (b, 0, 0)),
                  pl.BlockSpec((1, D), lambda b, s: (0, 0)),
                  pl.BlockSpec((D, 4 * SEC), lambda b, s: (0, 0)),
                  pl.BlockSpec(w_vt.shape, lambda b, s: (0, 0)),
                  pl.BlockSpec((FOX_HEADS, LANES), lambda b, s: (0, 0)),
                  pl.BlockSpec((tm, LANES), lambda b, s: (s, 0)),
                  pl.BlockSpec((tm, LANES), lambda b, s: (s, 0))],
        out_specs=[sec_spec] * 4
        + [pl.BlockSpec((None, DIFF_HEADS, DV_ROWS, tm), lambda b, s: (b, 0, 0, s)), t_spec,
           pl.BlockSpec((None, FOX_HEADS, tm), lambda b, s: (b, 0, s))],
        out_shape=[sec_out] * 4
        + [jax.ShapeDtypeStruct((B, DIFF_HEADS, DV_ROWS, S), BF16), t_out,
           jax.ShapeDtypeStruct((B, FOX_HEADS, S), F32)],
        scratch_shapes=[pltpu.VMEM((FOX_HEADS, LANES), F32)],
        compiler_params=_cparams(("arbitrary", "arbitrary")),
        name="in_proj",
    )(x, mod, g_pre, w_main, w_vt, bf, cos, sin)


def _scores(ch, r0, qm=None):
    r0 = pl.multiple_of(r0, HK)
    t = lax.dot_general(ch["keys"](r0), ch["qm"] if qm is None else qm, _NT,
                        preferred_element_type=F32)
    if ch["kbias"] is not None:
        t = t - ch["kbias"](r0)
    return t


def _flash_t(chains, sa_ref, qi, diag_mask, next_qm):
    def update(ch, state, t, r0):
        m, acc = state
        mt = jnp.max(t, axis=0, keepdims=True)
        qb = ch["qbias"]
        m_new = jnp.maximum(m, mt if qb is None else mt + qb)
        alpha = jnp.exp2(m - m_new)
        p = jnp.exp2((t - (m_new if qb is None else m_new - qb)).astype(BF16))
        vt = ch["vt"](pl.multiple_of(r0, HK))
        acc = alpha * acc + jnp.dot(vt, p, preferred_element_type=F32)
        return m_new, acc

    def body(ki, carry):
        k0 = ki * TK
        tb = [_scores(ch, k0 + HK) for ch in chains]
        st = [update(ch, s, sa_ref[ci], k0) for ci, (ch, s) in enumerate(zip(chains, carry))]
        for ci, ch in enumerate(chains):
            sa_ref[ci] = _scores(ch, k0 + TK)
        st = [update(ch, s, tb[ci], k0 + HK) for ci, (ch, s) in enumerate(zip(chains, st))]
        return tuple(st)

    init = tuple((jnp.full((1, TQ), NEG_INF, F32), jnp.zeros((ch["rows"], TQ), F32))
                 for ch in chains)
    carry = lax.fori_loop(0, qi, body, init)
    k0 = qi * TK
    hq = TQ - HK
    late = [dict(ch, qbias=None if ch["qbias"] is None else ch["qbias"][:, hq:]) for ch in chains]
    tb = [_scores(ch, k0 + HK, ch["qm"][hq:]) for ch in chains]
    nxt = [_scores(ch, 0, qm) for ch, qm in zip(chains, next_qm)]
    st = [update(ch, s, jnp.where(diag_mask[0:HK], sa_ref[ci], NEG_INF), k0)
          for ci, (ch, s) in enumerate(zip(chains, carry))]
    out = []
    for ci, (ch, (m, acc)) in enumerate(zip(late, st)):
        m_l, acc_l = update(ch, (m[:, hq:], acc[:, hq:]),
                            jnp.where(diag_mask[HK:TK, hq:], tb[ci], NEG_INF), k0 + HK)
        out.append((jnp.concatenate([m[:, :hq], m_l], axis=1),
                    jnp.concatenate([acc[:, :hq], acc_l], axis=1)))
    for ci in range(len(chains)):
        sa_ref[ci] = nxt[ci]
    return tuple(out)


def _next_tile_start(q0, seq_len):
    return pl.multiple_of(jnp.minimum(q0 + TQ, seq_len - TQ), TQ)


def _for_each_query_tile(n_tiles, tile_fn):
    def body(qi, carry):
        tile_fn(qi, pl.multiple_of(qi * TQ, TQ))
        return carry

    lax.fori_loop(0, n_tiles, body, 0)


def _lane_group(ref, rows, g):
    return ref[rows, g * LANES:(g + 1) * LANES]


def _masked_queries(q_ref, q0):
    lane = lax.broadcasted_iota(jnp.int32, (TQ, LANES), 1)
    out = []
    for g in range(ATT_GROUPS):
        q2 = _lane_group(q_ref, pl.ds(q0, TQ), g)
        out += [jnp.where((lane >= j * HEAD_DIM) & (lane < (j + 1) * HEAD_DIM), q2,
                          jnp.zeros_like(q2)) for j in range(2)]
    return out


def _fox_kernel(q_ref, k_ref, vt_ref, frow_ref, o_ref, sa_ref, fcol_ref):
    for g in range(ATT_GROUPS):
        for j in range(frow_ref.shape[2] // LANES):
            fcol_ref[g, j * LANES:(j + 1) * LANES, :] = _rows_to_columns(
                frow_ref[g, :, j * LANES:(j + 1) * LANES])

    vrow = lax.broadcasted_iota(jnp.int32, (LANES, HK), 0)
    r = lax.broadcasted_iota(jnp.int32, (TK, TQ), 0)
    c = lax.broadcasted_iota(jnp.int32, (TK, TQ), 1)

    def make_chains(q0):
        chains = []
        for ci, qm in enumerate(_masked_queries(q_ref, q0)):
            g, j = divmod(ci, 2)
            v_rows = (vrow >= j * HEAD_DIM) & (vrow < (j + 1) * HEAD_DIM)

            def vt(k0, g=g, v_rows=v_rows):
                v = vt_ref[g * LANES:(g + 1) * LANES, pl.ds(k0, HK)]
                return jnp.where(v_rows, v, jnp.ones_like(v))

            chains.append(dict(
                qm=qm, keys=lambda r0, g=g: _lane_group(k_ref, pl.ds(r0, HK), g),
                kbias=lambda r0, g=g, j=j: fcol_ref[g, pl.ds(r0, HK), j:j + 1],
                qbias=frow_ref[g, j:j + 1, pl.ds(q0, TQ)], vt=vt, rows=LANES))
        return chains

    def tile(qi, q0):
        res = _flash_t(make_chains(q0), sa_ref, qi, r <= c,
                       _masked_queries(q_ref, _next_tile_start(q0, q_ref.shape[0])))
        h = HEAD_DIM
        for g in range(ATT_GROUPS):
            a0, a1 = res[2 * g][1], res[2 * g + 1][1]
            o_t = jnp.concatenate([a0[0:h] / a0[h:2 * h], a1[h:2 * h] / a1[0:h]], axis=0)
            o_ref[pl.ds(q0, TQ), g * LANES:(g + 1) * LANES] = o_t.T.astype(o_ref.dtype)

    for ci, ch in enumerate(make_chains(0)):
        sa_ref[ci] = _scores(ch, 0)
    _for_each_query_tile(q_ref.shape[0] // TQ, tile)


def _fox_attention(fq, fk, fvt, frow):
    B, S, _ = fq.shape
    gw = ATT_GROUPS * LANES
    seq_spec = pl.BlockSpec((None, S, gw), lambda b, p: (b, 0, p))
    return pl.pallas_call(
        _fox_kernel,
        grid=(B, FOX_HEADS // 2 // ATT_GROUPS),
        in_specs=[seq_spec, seq_spec,
                  pl.BlockSpec((None, gw, S), lambda b, p: (b, p, 0)),
                  pl.BlockSpec((None, ATT_GROUPS, 2, S), lambda b, p: (b, p, 0, 0))],
        out_specs=seq_spec,
        out_shape=jax.ShapeDtypeStruct((B, S, SEC), BF16),
        scratch_shapes=[pltpu.VMEM((2 * ATT_GROUPS, HK, TQ), F32),
                        pltpu.VMEM((ATT_GROUPS, S, LANES), F32)],
        compiler_params=_cparams(("arbitrary", "arbitrary")),
        name="fox_attention",
    )(fq, fk, fvt, frow)


def _diff_kernel(q_ref, k_ref, vt_ref, lam_ref, g_ref, o_ref, sa_ref):
    r = lax.broadcasted_iota(jnp.int32, (TK, TQ), 0)
    c = lax.broadcasted_iota(jnp.int32, (TK, TQ), 1)
    lp = lam_ref[...]
    lam = (jnp.exp(jnp.sum(lp[0:1] * lp[1:2], axis=1, keepdims=True))
           - jnp.exp(jnp.sum(lp[2:3] * lp[3:4], axis=1, keepdims=True)) + LAMBDA_INIT)

    def make_chains(q0):
        return [dict(qm=qm, keys=lambda r0, g=ci // 2: _lane_group(k_ref, pl.ds(r0, HK), g),
                     kbias=None, qbias=None,
                     vt=lambda k0, g=ci // 2: vt_ref[g, :, pl.ds(k0, HK)], rows=DV_ROWS)
                for ci, qm in enumerate(_masked_queries(q_ref, q0))]

    def tile(qi, q0):
        res = _flash_t(make_chains(q0), sa_ref, qi, (r // CHUNK) <= (c // CHUNK),
                       _masked_queries(q_ref, _next_tile_start(q0, q_ref.shape[0])))
        v = LANES
        for g in range(ATT_GROUPS):
            a0, a1 = res[2 * g][1], res[2 * g + 1][1]
            o = (a0[0:v] / a0[v:v + 1] - lam * (a1[0:v] / a1[v:v + 1])).T
            o_ref[pl.ds(q0, TQ), g * LANES:(g + 1) * LANES] = (
                _rms(o, g_ref[...]) * (1.0 - LAMBDA_INIT)).astype(o_ref.dtype)

    for ci, ch in enumerate(make_chains(0)):
        sa_ref[ci] = _scores(ch, 0)
    _for_each_query_tile(q_ref.shape[0] // TQ, tile)


def _diff_attention(dq, dk, dvt, lam_params, g_subln):
    B, S, _ = dq.shape
    seq_spec = pl.BlockSpec((None, S, ATT_GROUPS * LANES), lambda b, h: (b, 0, h))
    return pl.pallas_call(
        _diff_kernel,
        grid=(B, DIFF_HEADS // ATT_GROUPS),
        in_specs=[seq_spec, seq_spec,
                  pl.BlockSpec((None, ATT_GROUPS, DV_ROWS, S), lambda b, h: (b, h, 0, 0)),
                  pl.BlockSpec((4, HEAD_DIM), lambda b, h: (0, 0)),
                  pl.BlockSpec((1, LANES), lambda b, h: (0, 0))],
        out_specs=seq_spec,
        out_shape=jax.ShapeDtypeStruct((B, S, SEC), BF16),
        scratch_shapes=[pltpu.VMEM((2 * ATT_GROUPS, HK, TQ), F32)],
        compiler_params=_cparams(("arbitrary", "arbitrary")),
        name="diff_attention",
    )(dq, dk, dvt, lam_params, g_subln)


def _route_kernel(d_ref, f_ref, wd_ref, wf_ref, x_ref, mod_ref, gpa_ref, gpf_ref, wr_ref, br_ref,
                  x1_ref, h2_ref, ri_ref, rw_ref, cnt_ref):
    first = (pl.program_id(0) == 0) & (pl.program_id(1) == 0)

    @pl.when(first)
    def _():
        cnt_ref[...] = jnp.zeros_like(cnt_ref)

    tm = x_ref.shape[0]
    rows = tm // ROUTE_ROW_GROUPS
    mixed = [jnp.dot(d_ref[r0:r0 + rows, :], wd_ref[...], preferred_element_type=F32)
             + jnp.dot(f_ref[r0:r0 + rows, :], wf_ref[...], preferred_element_type=F32)
             for r0 in range(0, tm, rows)]
    wh, wm, _ = _split3(wr_ref[...])
    gate_gain = mod_ref[2:3, :] * gpa_ref[...]
    ffn_gain = gpf_ref[...] * (1.0 + mod_ref[4:5, :])
    lts = []
    for gi, r0 in enumerate(range(0, tm, rows)):
        x1 = x_ref[r0:r0 + rows, :] + _rms(mixed[gi], gate_gain)
        x1_ref[r0:r0 + rows, :] = x1
        h2 = _rms(x1, ffn_gain) + mod_ref[3:4, :]
        h2_ref[r0:r0 + rows, :] = _pack_halves(h2)
        hh, hm, _ = _split3(h2)
        lt_g = br_ref[:, 0:1]
        for a, b_ in ((wh, hh), (wh, hm), (wm, hh)):
            lt_g = lt_g + lax.dot_general(a, b_, _NT, preferred_element_type=F32)
        lts.append(lt_g)
    lt = jnp.concatenate(lts, axis=1)

    g = lt[0:N_GROUPS]
    row4 = lax.broadcasted_iota(jnp.int32, g.shape, 0)
    gmax = jnp.max(g, axis=0, keepdims=True)
    gidx = jnp.min(jnp.where(g == gmax, row4, N_GROUPS), axis=0, keepdims=True)
    gw = 1.0 / jnp.sum(jnp.exp(g - gmax), axis=0, keepdims=True)
    esel = jnp.zeros((EXPERTS_PER_GROUP, tm), F32)
    for gg in range(N_GROUPS):
        e0 = SUBLANES + EXPERTS_PER_GROUP * gg
        esel = jnp.where(gidx == gg, lt[e0:e0 + EXPERTS_PER_GROUP], esel)
    row8 = lax.broadcasted_iota(jnp.int32, esel.shape, 0)
    v1 = jnp.max(esel, axis=0, keepdims=True)
    i1 = jnp.min(jnp.where(esel == v1, row8, EXPERTS_PER_GROUP), axis=0, keepdims=True)
    esel2 = jnp.where(row8 == i1, -jnp.inf, esel)
    v2 = jnp.max(esel2, axis=0, keepdims=True)
    i2 = jnp.min(jnp.where(esel2 == v2, row8, EXPERTS_PER_GROUP), axis=0, keepdims=True)
    e = jnp.exp(v2 - v1)
    w1 = gw / (1.0 + e)
    w2 = gw * e / (1.0 + e)
    e1 = gidx * EXPERTS_PER_GROUP + i1
    e2 = gidx * EXPERTS_PER_GROUP + i2

    row32 = lax.broadcasted_iota(jnp.int32, (N_EXPERTS, tm), 0)
    oh1 = row32 == e1
    oh2 = row32 == e2
    oh = jnp.where(oh1 | oh2, 1.0, 0.0)
    t_r = lax.broadcasted_iota(jnp.int32, (tm, tm), 0)
    t_c = lax.broadcasted_iota(jnp.int32, (tm, tm), 1)
    tri = jnp.where(t_r < t_c, 1.0, 0.0).astype(BF16)
    tot = jnp.dot(oh.astype(BF16), tri, preferred_element_type=F32) + cnt_ref[:, 0:1]
    r1 = jnp.sum(jnp.where(oh1, tot, 0.0), axis=0, keepdims=True)
    r2 = jnp.sum(jnp.where(oh2, tot, 0.0), axis=0, keepdims=True)
    cnt_ref[...] = cnt_ref[...] + jnp.sum(oh, axis=1, keepdims=True)

    zi = jnp.zeros((SUBLANES - 4, tm), jnp.int32)
    ri_ref[...] = jnp.concatenate([e1, e2, r1.astype(jnp.int32), r2.astype(jnp.int32), zi], axis=0)
    rw_ref[...] = jnp.concatenate([w1, w2, jnp.zeros((SUBLANES - 2, tm), F32)], axis=0)


def _route(d_out, f_out, w_out_d, w_out_f, x, mod, g_post_attn, g_pre_ffn, w_rt, b_rt):
    B, S, D = x.shape
    tm = TM_ROUTE
    nrt = w_rt.shape[0]
    n_s = S // tm
    T = B * S
    return pl.pallas_call(
        _route_kernel,
        grid=(B, n_s),
        in_specs=[pl.BlockSpec((None, tm, SEC), lambda b, s: (b, s, 0)),
                  pl.BlockSpec((None, tm, SEC), lambda b, s: (b, s, 0)),
                  pl.BlockSpec((SEC, D), lambda b, s: (0, 0)),
                  pl.BlockSpec((SEC, D), lambda b, s: (0, 0)),
                  pl.BlockSpec((None, tm, D), lambda b, s: (b, s, 0)),
                  pl.BlockSpec((None, 6, D), lambda b, s: (b, 0, 0)),
                  pl.BlockSpec((1, D), lambda b, s: (0, 0)),
                  pl.BlockSpec((1, D), lambda b, s: (0, 0)),
                  pl.BlockSpec((nrt, D), lambda b, s: (0, 0)),
                  pl.BlockSpec((nrt, LANES), lambda b, s: (0, 0))],
        out_specs=[pl.BlockSpec((None, tm, D), lambda b, s: (b, s, 0)),
                   pl.BlockSpec((None, tm, D // 2), lambda b, s: (b, s, 0)),
                   pl.BlockSpec((SUBLANES, tm), lambda b, s: (0, b * n_s + s)),
                   pl.BlockSpec((SUBLANES, tm), lambda b, s: (0, b * n_s + s)),
                   pl.BlockSpec((N_EXPERTS, LANES), lambda b, s: (0, 0))],
        out_shape=[jax.ShapeDtypeStruct((B, S, D), F32),
                   jax.ShapeDtypeStruct((B, S, D // 2), jnp.uint32),
                   jax.ShapeDtypeStruct((SUBLANES, T), jnp.int32),
                   jax.ShapeDtypeStruct((SUBLANES, T), F32),
                   jax.ShapeDtypeStruct((N_EXPERTS, LANES), F32)],
        compiler_params=_cparams(("arbitrary", "arbitrary")),
        name="out_proj_route",
    )(d_out, f_out, w_out_d, w_out_f, x, mod, g_post_attn, g_pre_ffn, w_rt, b_rt)


def _sc_mesh():
    return plsc.VectorSubcoreMesh(core_axis_name="c", subcore_axis_name="s",
                                  num_cores=SC_CORES, num_subcores=SC_SUBCORES)


def _sc_worker_base(per_worker):
    return (lax.axis_index("s") * SC_CORES + lax.axis_index("c")) * per_worker


def _dispatch(dest, h2, n_rows):
    T, D = h2.shape
    per_worker = T // SC_WORKERS

    @functools.partial(
        pl.kernel, mesh=_sc_mesh(),
        out_type=jax.ShapeDtypeStruct((n_rows, D), h2.dtype),
        scratch_types=[pltpu.VMEM((SC_WIN,), jnp.int32), pltpu.VMEM((SC_WIN, D), h2.dtype)],
        name="moe_dispatch")
    def run(h_hbm, dest_hbm, xs_hbm, idx_v, rows_v):
        base = _sc_worker_base(per_worker)

        @pl.loop(0, per_worker // SC_WIN)
        def _(ci):
            off = pl.multiple_of(base + ci * SC_WIN, SC_WIN)
            pltpu.sync_copy(h_hbm.at[pl.ds(off, SC_WIN)], rows_v)
            for k in range(2):
                pltpu.sync_copy(dest_hbm.at[pl.ds(k * T + off, SC_WIN)], idx_v)
                pltpu.sync_copy(rows_v, xs_hbm.at[idx_v])

    return run(h2, dest.reshape(2 * T))


def _gather_rows(ys, dest):
    _, D = ys.shape
    K, T = dest.shape
    n = K * T
    per_worker = n // SC_WORKERS

    @functools.partial(
        pl.kernel, mesh=_sc_mesh(),
        out_type=jax.ShapeDtypeStruct((n, D), ys.dtype),
        scratch_types=[pltpu.VMEM((SC_WIN,), jnp.int32), pltpu.VMEM((SC_WIN, D), ys.dtype)],
        name="moe_gather")
    def run(ys_hbm, dest_hbm, g_hbm, idx_v, rows_v):
        base = _sc_worker_base(per_worker)

        @pl.loop(0, per_worker // SC_WIN)
        def _(ci):
            off = pl.multiple_of(base + ci * SC_WIN, SC_WIN)
            pltpu.sync_copy(dest_hbm.at[pl.ds(off, SC_WIN)], idx_v)
            pltpu.sync_copy(ys_hbm.at[idx_v], rows_v)
            pltpu.sync_copy(rows_v, g_hbm.at[pl.ds(off, SC_WIN)])

    return run(ys, dest.reshape(n)).reshape(K, T, D)


def _expert_kernel(be_ref, nu_ref, nv_ref, xs_ref, wg_ref, wu_ref, wd_ref, ys_ref,
                   wgb, wub, wdb):
    i = pl.program_id(0)

    @pl.when(i < nu_ref[0])
    def _():
        prev = be_ref[jnp.maximum(i - 1, 0)]

        @pl.when((i == 0) | (be_ref[i] != prev))
        def _():
            wgb[...] = wg_ref[...].astype(BF16)
            wub[...] = wu_ref[...].astype(BF16)
            wdb[...] = wd_ref[...].astype(BF16)

        half = xs_ref.shape[1]
        rows = xs_ref.shape[0] // EXPERT_ROW_GROUPS
        row = lax.broadcasted_iota(jnp.int32, (rows, half), 0)
        au = []
        for r0 in range(0, xs_ref.shape[0], rows):
            words = jnp.where(row + r0 < nv_ref[i], xs_ref[r0:r0 + rows, :], jnp.uint32(0))
            lo, hi = (v.astype(BF16) for v in _unpack_halves(words))

            def proj(w, lo=lo, hi=hi):
                return (jnp.dot(lo, w[0:half, :], preferred_element_type=F32)
                        + jnp.dot(hi, w[half:2 * half, :], preferred_element_type=F32))

            au.append((proj(wgb), proj(wub)))
        for gi, (a, u) in enumerate(au):
            hmid = (a * jax.nn.sigmoid(a) * u).astype(BF16)
            y = jnp.dot(hmid, wdb[...], preferred_element_type=F32)
            ys_ref[gi * rows:(gi + 1) * rows, :] = _pack_halves(y)

    @pl.when(i >= nu_ref[0])
    def _():
        ys_ref[...] = jnp.zeros_like(ys_ref)


def _experts(block_e, n_used, n_valid, xs, w_gate, w_up, w_down):
    P, Dh = xs.shape
    nb = P // BM
    _, D, Fd = w_gate.shape

    def row_map(i, be, nu, nv):
        return (jnp.minimum(i, nu[0] - 1), 0)

    def w_map(i, be, nu, nv):
        return (be[i], 0, 0)

    return pl.pallas_call(
        _expert_kernel,
        grid_spec=pltpu.PrefetchScalarGridSpec(
            num_scalar_prefetch=3,
            grid=(nb,),
            in_specs=[pl.BlockSpec((BM, Dh), row_map),
                      pl.BlockSpec((None, D, Fd), w_map),
                      pl.BlockSpec((None, D, Fd), w_map),
                      pl.BlockSpec((None, Fd, D), w_map)],
            out_specs=pl.BlockSpec((BM, Dh), lambda i, be, nu, nv: (i, 0)),
            scratch_shapes=[pltpu.VMEM((D, Fd), BF16), pltpu.VMEM((D, Fd), BF16),
                            pltpu.VMEM((Fd, D), BF16)]),
        out_shape=jax.ShapeDtypeStruct((P, Dh), jnp.uint32),
        compiler_params=_cparams(("arbitrary",)),
        name="moe_experts",
    )(block_e, n_used, n_valid, xs, w_gate, w_up, w_down)


def _combine_kernel(g_ref, w_ref, x1_ref, mod_ref, g_post_ref, *rest):
    o_ref = rest[-1]
    lo0, hi0 = _unpack_halves(g_ref[0])
    lo1, hi1 = _unpack_halves(g_ref[1])
    tm = lo0.shape[0]
    wt = jnp.concatenate([_rows_to_columns(w_ref[:, j * LANES:(j + 1) * LANES])
                          for j in range(tm // LANES)], axis=0)
    w0, w1 = wt[:, 0:1], wt[:, 1:2]
    y = jnp.concatenate([w0 * lo0 + w1 * lo1, w0 * hi0 + w1 * hi1], axis=1)
    o_ref[...] = x1_ref[...] + _rms(y, mod_ref[5:6, :] * g_post_ref[...])


def _combine(g, wts, x1, mod, g_post_ffn, S, chunk, prev_out):
    T, D = x1.shape
    tm = TM_ROWS
    per_b = S // tm
    steps = g.shape[1] // tm
    off = chunk * steps
    in_specs = [pl.BlockSpec((2, tm, D // 2), lambda i: (0, i, 0)),
                pl.BlockSpec((SUBLANES, tm), lambda i: (0, i + off)),
                pl.BlockSpec((tm, D), lambda i: (i + off, 0)),
                pl.BlockSpec((None, 6, D), lambda i: ((i + off) // per_b, 0, 0)),
                pl.BlockSpec((1, D), lambda i: (0, 0))]
    args = [g, wts, x1, mod, g_post_ffn]
    aliases = {}
    if prev_out is not None:
        in_specs.append(pl.BlockSpec(memory_space=pl.ANY))
        args.append(prev_out)
        aliases = {len(args) - 1: 0}
    return pl.pallas_call(
        _combine_kernel,
        grid=(steps,),
        in_specs=in_specs,
        out_specs=pl.BlockSpec((tm, D), lambda i: (i + off, 0)),
        out_shape=jax.ShapeDtypeStruct((T, D), F32),
        input_output_aliases=aliases,
        compiler_params=_cparams(("arbitrary",)),
        name="moe_combine",
    )(*args)


def _rope_tables(S):
    inv = 1.0 / (10000.0 ** (jnp.arange(0, HEAD_DIM, 2, dtype=F32) / HEAD_DIM))
    ang = jnp.arange(S, dtype=F32)[:, None] * inv[None, :]
    c, s = jnp.cos(ang), jnp.sin(ang)
    return jnp.tile(c, (1, 4)), jnp.tile(jnp.concatenate([-s, s], axis=1), (1, 2))


def kernel(x, c, w_ada, b_ada, g_pre_attn, g_post_attn, w_in, b_forget, lambda_q1, lambda_k1,
           lambda_q2, lambda_k2, g_diff_subln, w_out, g_pre_ffn, g_post_ffn, w_group, b_group,
           w_router, b_router, w_gate, w_up, w_down):
    B, S, D = x.shape
    T = B * S
    l = 0
    mod = _ada(c, w_ada[l], b_ada[l]).reshape(B, 6, D)

    wi = w_in[l]
    w_main = jnp.concatenate([wi[:, 0:2 * SEC], wi[:, 3 * SEC:5 * SEC]], axis=1).astype(BF16)
    w_vt = jnp.concatenate([wi[:, 2 * SEC:3 * SEC], wi[:, 5 * SEC:6 * SEC], wi[:, 6 * SEC:],
                            jnp.zeros((D, BF16_SUBLANES - FOX_HEADS), F32)], axis=1).T.astype(BF16)
    bf = jnp.broadcast_to(b_forget[l][:, None], (FOX_HEADS, LANES))
    cos, sin = _rope_tables(S)
    dq, dk, fq, fk, dvt, fvt, fcum = _inproj(x, mod, g_pre_attn[l][None, :], w_main, w_vt,
                                             bf, cos, sin)

    lam_params = jnp.stack([lambda_q1[l], lambda_k1[l], lambda_q2[l], lambda_k2[l]])
    d_out = _diff_attention(dq, dk, dvt, lam_params, g_diff_subln[l][None, :])
    frow = fcum.reshape(B, FOX_HEADS // 2, 2, S)
    f_out = _fox_attention(fq, fk, fvt, frow)

    w_o = w_out[l].astype(BF16)
    zpad = jnp.zeros((SUBLANES - N_GROUPS, D), F32)
    w_rt = jnp.concatenate([w_group[l].T, zpad, w_router[l].T], axis=0)
    b_rt = jnp.concatenate([b_group[l], jnp.zeros((SUBLANES - N_GROUPS,), F32), b_router[l]])
    b_rt = jnp.broadcast_to(b_rt[:, None], (SUBLANES + N_EXPERTS, LANES))
    x1, h2, ri, rw, cnt = _route(d_out, f_out, w_o[:SEC], w_o[SEC:], x, mod,
                                 g_post_attn[l][None, :], g_pre_ffn[l][None, :], w_rt, b_rt)

    counts = cnt[:, 0].astype(jnp.int32)
    padded = ((counts + BM - 1) // BM) * BM
    pend = jnp.cumsum(padded)
    pstart = pend - padded
    eids = jnp.arange(N_EXPERTS, dtype=jnp.int32)
    dest = ri[2:4] + jnp.sum(jnp.where(ri[0:2, :, None] == eids, pstart, 0), axis=-1)
    nb = (2 * T) // BM + N_EXPERTS
    n_used = (pend[-1] // BM).astype(jnp.int32).reshape(1)
    blk0 = jnp.arange(nb, dtype=jnp.int32) * BM
    block_e = jnp.minimum(jnp.sum((blk0[:, None] >= pend[None, :]).astype(jnp.int32), axis=1),
                          N_EXPERTS - 1)
    cend = pstart + counts
    blk_end = jnp.sum(jnp.where(block_e[:, None] == eids, cend, 0), axis=-1)
    n_valid = jnp.clip(blk_end - blk0, 0, BM).astype(jnp.int32)

    xs = _dispatch(dest, h2.reshape(T, D // 2), nb * BM)
    ys = _experts(block_e, n_used, n_valid, xs, w_gate[l], w_up[l], w_down[l])
    out = None
    tc = T // COMBINE_CHUNKS
    for ck in range(COMBINE_CHUNKS):
        g = _gather_rows(ys, dest[:, ck * tc:(ck + 1) * tc])
        out = _combine(g, rw, x1.reshape(T, D), mod, g_post_ffn[l][None, :], S, ck, out)
    return out.reshape(B, S, D)
```

```python
import functools
import math

import jax
import jax.numpy as jnp
from jax import lax
from jax.experimental import pallas as pl
from jax.experimental.pallas import tpu as pltpu
from jax.experimental.pallas import tpu_sc as plsc

F32 = jnp.float32
BF16 = jnp.bfloat16

D_MODEL = 1024
HEAD_DIM = 64
CHUNK = 64
DIFF_HEADS = 4
FOX_HEADS = 8
SEC = 512
N_GROUPS = 4
EXPERTS_PER_GROUP = 8
N_EXPERTS = 32
D_EXPERT = 512
EPS = 1e-6
NEG_INF = -1e30
LOG2E = 1.4426950408889634
LAMBDA_INIT = 0.8 - 0.6 * math.exp(-0.3 * 0)

LANES = 128
SUBLANES = 8
BF16_SUBLANES = 16
TN_ADA = 1024
TM_PROJ = 1024
TQ = 512
TK = 512
HK = TK // 2
ATT_GROUPS = 1
DV_ROWS = 2 * HEAD_DIM + BF16_SUBLANES
TM_ROUTE = 512
ROUTE_ROW_GROUPS = 2
BM = 512
EXPERT_ROW_GROUPS = 2
TM_ROWS = 512
COMBINE_CHUNKS = 4
SC_CORES = 2
SC_SUBCORES = 16
SC_WORKERS = SC_CORES * SC_SUBCORES
SC_WIN = 128
VMEM_LIMIT = 48 * 1024 * 1024
_NT = (((1,), (1,)), ((), ()))


def _cparams(sem, flags=None):
    return pltpu.CompilerParams(dimension_semantics=sem, vmem_limit_bytes=VMEM_LIMIT, flags=flags)


def _rms(x, g):
    return x * lax.rsqrt(jnp.mean(x * x, axis=-1, keepdims=True) + EPS) * g


def _pack_halves(x):
    m = x.shape[1] // 2
    bits = lax.bitcast_convert_type(x.astype(BF16).astype(F32), jnp.uint32)
    return (bits[:, :m] >> 16) | (bits[:, m:] & jnp.uint32(0xFFFF0000))


def _unpack_halves(w):
    lo = lax.bitcast_convert_type(w << 16, F32)
    hi = lax.bitcast_convert_type(w & jnp.uint32(0xFFFF0000), F32)
    return lo, hi


def _rows_to_columns(x):
    pad = jnp.zeros((LANES - x.shape[0], LANES), x.dtype)
    return jnp.concatenate([x, pad], axis=0).T


def _ada_kernel(c_ref, w_ref, b_ref, o_ref):
    c = c_ref[...]
    cs = c * jax.nn.sigmoid(c)
    o_ref[...] = jnp.dot(cs, w_ref[...], preferred_element_type=F32,
                         precision=lax.Precision.HIGHEST) + b_ref[...]


def _ada(c, w_ada, b_ada):
    B, D = c.shape
    N = w_ada.shape[1]
    tn = TN_ADA
    return pl.pallas_call(
        _ada_kernel,
        grid=(N // tn,),
        in_specs=[pl.BlockSpec((B, D), lambda j: (0, 0)),
                  pl.BlockSpec((D, tn), lambda j: (0, j)),
                  pl.BlockSpec((1, tn), lambda j: (0, j))],
        out_specs=pl.BlockSpec((B, tn), lambda j: (0, j)),
        out_shape=jax.ShapeDtypeStruct((B, N), F32),
        compiler_params=_cparams(("arbitrary",)),
        name="ada_mod",
    )(c, w_ada, b_ada.reshape(1, N))


def _split3(x):
    hi = x.astype(BF16)
    r1 = x - hi.astype(F32)
    mid = r1.astype(BF16)
    lo = (r1 - mid.astype(F32)).astype(BF16)
    return hi, mid, lo


def _inproj_kernel(x_ref, mod_ref, g_ref, w_ref, wvt_ref, bf_ref, cos_ref, sin_ref,
                   dq_ref, dk_ref, fq_ref, fk_ref, dvt_ref, fvt_ref, fcum_ref, carry_ref):
    si = pl.program_id(1)
    x = x_ref[...]
    h = _rms(x, g_ref[...] * (1.0 + mod_ref[1:2, :])) + mod_ref[0:1, :]
    hb = h.astype(BF16)
    tm = x.shape[0]

    cos = cos_ref[...]
    sin = sin_ref[...]
    lane = lax.broadcasted_iota(jnp.int32, (tm, LANES), 1)
    first_half = (lane % HEAD_DIM) < (HEAD_DIM // 2)
    qscale = HEAD_DIM ** -0.5 * LOG2E

    def proj(sec):
        return jnp.dot(hb, w_ref[:, sec * SEC:(sec + 1) * SEC], preferred_element_type=F32)

    def rope_store(p, out_ref, scale):
        for j in range(SEC // LANES):
            ch = p[:, j * LANES:(j + 1) * LANES]
            partner = jnp.where(first_half,
                                pltpu.roll(ch, LANES - HEAD_DIM // 2, 1),
                                pltpu.roll(ch, HEAD_DIM // 2, 1))
            r = ch * cos + partner * sin
            if scale != 1.0:
                r = r * scale
            out_ref[:, j * LANES:(j + 1) * LANES] = r.astype(out_ref.dtype)

    rope_store(proj(0), dq_ref, qscale)
    rope_store(proj(1), dk_ref, 1.0)
    fq_ref[...] = (proj(2) * qscale).astype(fq_ref.dtype)
    fk_ref[...] = proj(3).astype(fk_ref.dtype)
    vt = lax.dot_general(wvt_ref[...], hb, _NT, preferred_element_type=F32)
    for hd in range(DIFF_HEADS):
        dvt_ref[hd, 0:LANES, :] = vt[hd * LANES:(hd + 1) * LANES].astype(dvt_ref.dtype)
        dvt_ref[hd, LANES:DV_ROWS, :] = jnp.ones((DV_ROWS - LANES, tm), dvt_ref.dtype)
    fvt_ref[...] = vt[SEC:2 * SEC].astype(fvt_ref.dtype)

    z = vt[2 * SEC:2 * SEC + FOX_HEADS] + bf_ref[:, 0:1]
    lf = (jnp.minimum(z, 0.0) - jnp.log1p(jnp.exp(-jnp.abs(z)))) * LOG2E

    @pl.when(si == 0)
    def _():
        carry_ref[...] = jnp.zeros_like(carry_ref)

    r_i = lax.broadcasted_iota(jnp.int32, (LANES, LANES), 0)
    c_i = lax.broadcasted_iota(jnp.int32, (LANES, LANES), 1)
    tri = jnp.where(r_i <= c_i, 1.0, 0.0).astype(BF16)
    run = carry_ref[...]
    for j in range(tm // LANES):
        hi, mid, lo = _split3(lf[:, j * LANES:(j + 1) * LANES])
        cs = (jnp.dot(hi, tri, preferred_element_type=F32)
              + jnp.dot(mid, tri, preferred_element_type=F32)
              + jnp.dot(lo, tri, preferred_element_type=F32)) + run
        fcum_ref[:, j * LANES:(j + 1) * LANES] = cs
        run = jnp.broadcast_to(cs[:, LANES - 1:LANES], run.shape)
    carry_ref[...] = run


def _inproj(x, mod, g_pre, w_main, w_vt, bf, cos, sin):
    B, S, D = x.shape
    tm = TM_PROJ
    sec_out = jax.ShapeDtypeStruct((B, S, SEC), BF16)
    sec_spec = pl.BlockSpec((None, tm, SEC), lambda b, s: (b, s, 0))
    t_out = jax.ShapeDtypeStruct((B, SEC, S), BF16)
    t_spec = pl.BlockSpec((None, SEC, tm), lambda b, s: (b, 0, s))
    return pl.pallas_call(
        _inproj_kernel,
        grid=(B, S // tm),
        in_specs=[pl.BlockSpec((None, tm, D), lambda b, s: (b, s, 0)),
                  pl.BlockSpec((None, 6, D), lambda b, s: (b, 0, 0)),
                  pl.BlockSpec((1, D), lambda b, s: (0, 0)),
                  pl.BlockSpec((D, 4 * SEC), lambda b, s: (0, 0)),
                  pl.BlockSpec(w_vt.shape, lambda b, s: (0, 0)),
                  pl.BlockSpec((FOX_HEADS, LANES), lambda b, s: (0, 0)),
                  pl.BlockSpec((tm, LANES), lambda b, s: (s, 0)),
                  pl.BlockSpec((tm, LANES), lambda b, s: (s, 0))],
        out_specs=[sec_spec] * 4
        + [pl.BlockSpec((None, DIFF_HEADS, DV_ROWS, tm), lambda b, s: (b, 0, 0, s)), t_spec,
           pl.BlockSpec((None, FOX_HEADS, tm), lambda b, s: (b, 0, s))],
        out_shape=[sec_out] * 4
        + [jax.ShapeDtypeStruct((B, DIFF_HEADS, DV_ROWS, S), BF16), t_out,
           jax.ShapeDtypeStruct((B, FOX_HEADS, S), F32)],
        scratch_shapes=[pltpu.VMEM((FOX_HEADS, LANES), F32)],
        compiler_params=_cparams(("arbitrary", "arbitrary")),
        name="in_proj",
    )(x, mod, g_pre, w_main, w_vt, bf, cos, sin)


def _scores(ch, r0, qm=None):
    r0 = pl.multiple_of(r0, HK)
    t = lax.dot_general(ch["keys"](r0), ch["qm"] if qm is None else qm, _NT,
                        preferred_element_type=F32)
    if ch["kbias"] is not None:
        t = t - ch["kbias"](r0)
    return t


def _flash_t(chains, sa_ref, qi, diag_mask, next_qm):
    def update(ch, state, t, r0):
        m, acc = state
        mt = jnp.max(t, axis=0, keepdims=True)
        qb = ch["qbias"]
        m_new = jnp.maximum(m, mt if qb is None else mt + qb)
        alpha = jnp.exp2(m - m_new)
        p = jnp.exp2((t - (m_new if qb is None else m_new - qb)).astype(BF16))
        vt = ch["vt"](pl.multiple_of(r0, HK))
        acc = alpha * acc + jnp.dot(vt, p, preferred_element_type=F32)
        return m_new, acc

    def body(ki, carry):
        k0 = ki * TK
        tb = [_scores(ch, k0 + HK) for ch in chains]
        st = [update(ch, s, sa_ref[ci], k0) for ci, (ch, s) in enumerate(zip(chains, carry))]
        for ci, ch in enumerate(chains):
            sa_ref[ci] = _scores(ch, k0 + TK)
        st = [update(ch, s, tb[ci], k0 + HK) for ci, (ch, s) in enumerate(zip(chains, st))]
        return tuple(st)

    init = tuple((jnp.full((1, TQ), NEG_INF, F32), jnp.zeros((ch["rows"], TQ), F32))
                 for ch in chains)
    carry = lax.fori_loop(0, qi, body, init)
    k0 = qi * TK
    hq = TQ - HK
    late = [dict(ch, qbias=None if ch["qbias"] is None else ch["qbias"][:, hq:]) for ch in chains]
    tb = [_scores(ch, k0 + HK, ch["qm"][hq:]) for ch in chains]
    nxt = [_scores(ch, 0, qm) for ch, qm in zip(chains, next_qm)]
    st = [update(ch, s, jnp.where(diag_mask[0:HK], sa_ref[ci], NEG_INF), k0)
          for ci, (ch, s) in enumerate(zip(chains, carry))]
    out = []
    for ci, (ch, (m, acc)) in enumerate(zip(late, st)):
        m_l, acc_l = update(ch, (m[:, hq:], acc[:, hq:]),
                            jnp.where(diag_mask[HK:TK, hq:], tb[ci], NEG_INF), k0 + HK)
        out.append((jnp.concatenate([m[:, :hq], m_l], axis=1),
                    jnp.concatenate([acc[:, :hq], acc_l], axis=1)))
    for ci in range(len(chains)):
        sa_ref[ci] = nxt[ci]
    return tuple(out)


def _next_tile_start(q0, seq_len):
    return pl.multiple_of(jnp.minimum(q0 + TQ, seq_len - TQ), TQ)


def _for_each_query_tile(n_tiles, tile_fn):
    def body(qi, carry):
        tile_fn(qi, pl.multiple_of(qi * TQ, TQ))
        return carry

    lax.fori_loop(0, n_tiles, body, 0)


def _lane_group(ref, rows, g):
    return ref[rows, g * LANES:(g + 1) * LANES]


def _masked_queries(q_ref, q0):
    lane = lax.broadcasted_iota(jnp.int32, (TQ, LANES), 1)
    out = []
    for g in range(ATT_GROUPS):
        q2 = _lane_group(q_ref, pl.ds(q0, TQ), g)
        out += [jnp.where((lane >= j * HEAD_DIM) & (lane < (j + 1) * HEAD_DIM), q2,
                          jnp.zeros_like(q2)) for j in range(2)]
    return out


def _fox_kernel(q_ref, k_ref, vt_ref, frow_ref, o_ref, sa_ref, fcol_ref):
    for g in range(ATT_GROUPS):
        for j in range(frow_ref.shape[2] // LANES):
            fcol_ref[g, j * LANES:(j + 1) * LANES, :] = _rows_to_columns(
                frow_ref[g, :, j * LANES:(j + 1) * LANES])

    vrow = lax.broadcasted_iota(jnp.int32, (LANES, HK), 0)
    r = lax.broadcasted_iota(jnp.int32, (TK, TQ), 0)
    c = lax.broadcasted_iota(jnp.int32, (TK, TQ), 1)

    def make_chains(q0):
        chains = []
        for ci, qm in enumerate(_masked_queries(q_ref, q0)):
            g, j = divmod(ci, 2)
            v_rows = (vrow >= j * HEAD_DIM) & (vrow < (j + 1) * HEAD_DIM)

            def vt(k0, g=g, v_rows=v_rows):
                v = vt_ref[g * LANES:(g + 1) * LANES, pl.ds(k0, HK)]
                return jnp.where(v_rows, v, jnp.ones_like(v))

            chains.append(dict(
                qm=qm, keys=lambda r0, g=g: _lane_group(k_ref, pl.ds(r0, HK), g),
                kbias=lambda r0, g=g, j=j: fcol_ref[g, pl.ds(r0, HK), j:j + 1],
                qbias=frow_ref[g, j:j + 1, pl.ds(q0, TQ)], vt=vt, rows=LANES))
        return chains

    def tile(qi, q0):
        res = _flash_t(make_chains(q0), sa_ref, qi, r <= c,
                       _masked_queries(q_ref, _next_tile_start(q0, q_ref.shape[0])))
        h = HEAD_DIM
        for g in range(ATT_GROUPS):
            a0, a1 = res[2 * g][1], res[2 * g + 1][1]
            o_t = jnp.concatenate([a0[0:h] / a0[h:2 * h], a1[h:2 * h] / a1[0:h]], axis=0)
            o_ref[pl.ds(q0, TQ), g * LANES:(g + 1) * LANES] = o_t.T.astype(o_ref.dtype)

    for ci, ch in enumerate(make_chains(0)):
        sa_ref[ci] = _scores(ch, 0)
    _for_each_query_tile(q_ref.shape[0] // TQ, tile)


def _fox_attention(fq, fk, fvt, frow):
    B, S, _ = fq.shape
    gw = ATT_GROUPS * LANES
    seq_spec = pl.BlockSpec((None, S, gw), lambda b, p: (b, 0, p))
    return pl.pallas_call(
        _fox_kernel,
        grid=(B, FOX_HEADS // 2 // ATT_GROUPS),
        in_specs=[seq_spec, seq_spec,
                  pl.BlockSpec((None, gw, S), lambda b, p: (b, p, 0)),
                  pl.BlockSpec((None, ATT_GROUPS, 2, S), lambda b, p: (b, p, 0, 0))],
        out_specs=seq_spec,
        out_shape=jax.ShapeDtypeStruct((B, S, SEC), BF16),
        scratch_shapes=[pltpu.VMEM((2 * ATT_GROUPS, HK, TQ), F32),
                        pltpu.VMEM((ATT_GROUPS, S, LANES), F32)],
        compiler_params=_cparams(("arbitrary", "arbitrary")),
        name="fox_attention",
    )(fq, fk, fvt, frow)


def _diff_kernel(q_ref, k_ref, vt_ref, lam_ref, g_ref, o_ref, sa_ref):
    r = lax.broadcasted_iota(jnp.int32, (TK, TQ), 0)
    c = lax.broadcasted_iota(jnp.int32, (TK, TQ), 1)
    lp = lam_ref[...]
    lam = (jnp.exp(jnp.sum(lp[0:1] * lp[1:2], axis=1, keepdims=True))
           - jnp.exp(jnp.sum(lp[2:3] * lp[3:4], axis=1, keepdims=True)) + LAMBDA_INIT)

    def make_chains(q0):
        return [dict(qm=qm, keys=lambda r0, g=ci // 2: _lane_group(k_ref, pl.ds(r0, HK), g),
                     kbias=None, qbias=None,
                     vt=lambda k0, g=ci // 2: vt_ref[g, :, pl.ds(k0, HK)], rows=DV_ROWS)
                for ci, qm in enumerate(_masked_queries(q_ref, q0))]

    def tile(qi, q0):
        res = _flash_t(make_chains(q0), sa_ref, qi, (r // CHUNK) <= (c // CHUNK),
                       _masked_queries(q_ref, _next_tile_start(q0, q_ref.shape[0])))
        v = LANES
        for g in range(ATT_GROUPS):
            a0, a1 = res[2 * g][1], res[2 * g + 1][1]
            o = (a0[0:v] / a0[v:v + 1] - lam * (a1[0:v] / a1[v:v + 1])).T
            o_ref[pl.ds(q0, TQ), g * LANES:(g + 1) * LANES] = (
                _rms(o, g_ref[...]) * (1.0 - LAMBDA_INIT)).astype(o_ref.dtype)

    for ci, ch in enumerate(make_chains(0)):
        sa_ref[ci] = _scores(ch, 0)
    _for_each_query_tile(q_ref.shape[0] // TQ, tile)


def _diff_attention(dq, dk, dvt, lam_params, g_subln):
    B, S, _ = dq.shape
    seq_spec = pl.BlockSpec((None, S, ATT_GROUPS * LANES), lambda b, h: (b, 0, h))
    return pl.pallas_call(
        _diff_kernel,
        grid=(B, DIFF_HEADS // ATT_GROUPS),
        in_specs=[seq_spec, seq_spec,
                  pl.BlockSpec((None, ATT_GROUPS, DV_ROWS, S), lambda b, h: (b, h, 0, 0)),
                  pl.BlockSpec((4, HEAD_DIM), lambda b, h: (0, 0)),
                  pl.BlockSpec((1, LANES), lambda b, h: (0, 0))],
        out_specs=seq_spec,
        out_shape=jax.ShapeDtypeStruct((B, S, SEC), BF16),
        scratch_shapes=[pltpu.VMEM((2 * ATT_GROUPS, HK, TQ), F32)],
        compiler_params=_cparams(("arbitrary", "arbitrary")),
        name="diff_attention",
    )(dq, dk, dvt, lam_params, g_subln)


def _route_kernel(d_ref, f_ref, wd_ref, wf_ref, x_ref, mod_ref, gpa_ref, gpf_ref, wr_ref, br_ref,
                  x1_ref, h2_ref, ri_ref, rw_ref, cnt_ref):
    first = (pl.program_id(0) == 0) & (pl.program_id(1) == 0)

    @pl.when(first)
    def _():
        cnt_ref[...] = jnp.zeros_like(cnt_ref)

    tm = x_ref.shape[0]
    rows = tm // ROUTE_ROW_GROUPS
    mixed = [jnp.dot(d_ref[r0:r0 + rows, :], wd_ref[...], preferred_element_type=F32)
             + jnp.dot(f_ref[r0:r0 + rows, :], wf_ref[...], preferred_element_type=F32)
             for r0 in range(0, tm, rows)]
    wr = wr_ref[...].astype(BF16)
    gate_gain = mod_ref[2:3, :] * gpa_ref[...]
    ffn_gain = gpf_ref[...] * (1.0 + mod_ref[4:5, :])
    lts = []
    for gi, r0 in enumerate(range(0, tm, rows)):
        x1 = x_ref[r0:r0 + rows, :] + _rms(mixed[gi], gate_gain)
        x1_ref[r0:r0 + rows, :] = x1
        h2 = _rms(x1, ffn_gain) + mod_ref[3:4, :]
        h2_ref[r0:r0 + rows, :] = _pack_halves(h2)
        lts.append(br_ref[:, 0:1] + lax.dot_general(wr, h2.astype(BF16), _NT,
                                                    preferred_element_type=F32))
    lt = jnp.concatenate(lts, axis=1)

    g = lt[0:N_GROUPS]
    row4 = lax.broadcasted_iota(jnp.int32, g.shape, 0)
    gmax = jnp.max(g, axis=0, keepdims=True)
    gidx = jnp.min(jnp.where(g == gmax, row4, N_GROUPS), axis=0, keepdims=True)
    gw = 1.0 / jnp.sum(jnp.exp(g - gmax), axis=0, keepdims=True)
    esel = jnp.zeros((EXPERTS_PER_GROUP, tm), F32)
    for gg in range(N_GROUPS):
        e0 = SUBLANES + EXPERTS_PER_GROUP * gg
        esel = jnp.where(gidx == gg, lt[e0:e0 + EXPERTS_PER_GROUP], esel)
    row8 = lax.broadcasted_iota(jnp.int32, esel.shape, 0)
    v1 = jnp.max(esel, axis=0, keepdims=True)
    i1 = jnp.min(jnp.where(esel == v1, row8, EXPERTS_PER_GROUP), axis=0, keepdims=True)
    esel2 = jnp.where(row8 == i1, -jnp.inf, esel)
    v2 = jnp.max(esel2, axis=0, keepdims=True)
    i2 = jnp.min(jnp.where(esel2 == v2, row8, EXPERTS_PER_GROUP), axis=0, keepdims=True)
    e = jnp.exp(v2 - v1)
    w1 = gw / (1.0 + e)
    w2 = gw * e / (1.0 + e)
    e1 = gidx * EXPERTS_PER_GROUP + i1
    e2 = gidx * EXPERTS_PER_GROUP + i2

    row32 = lax.broadcasted_iota(jnp.int32, (N_EXPERTS, tm), 0)
    oh1 = row32 == e1
    oh2 = row32 == e2
    oh = jnp.where(oh1 | oh2, 1.0, 0.0)
    t_r = lax.broadcasted_iota(jnp.int32, (tm, tm), 0)
    t_c = lax.broadcasted_iota(jnp.int32, (tm, tm), 1)
    tri = jnp.where(t_r < t_c, 1.0, 0.0).astype(BF16)
    tot = jnp.dot(oh.astype(BF16), tri, preferred_element_type=F32) + cnt_ref[:, 0:1]
    r1 = jnp.sum(jnp.where(oh1, tot, 0.0), axis=0, keepdims=True)
    r2 = jnp.sum(jnp.where(oh2, tot, 0.0), axis=0, keepdims=True)
    cnt_ref[...] = cnt_ref[...] + jnp.sum(oh, axis=1, keepdims=True)

    zi = jnp.zeros((SUBLANES - 4, tm), jnp.int32)
    ri_ref[...] = jnp.concatenate([e1, e2, r1.astype(jnp.int32), r2.astype(jnp.int32), zi], axis=0)
    rw_ref[...] = jnp.concatenate([w1, w2, jnp.zeros((SUBLANES - 2, tm), F32)], axis=0)


def _route(d_out, f_out, w_out_d, w_out_f, x, mod, g_post_attn, g_pre_ffn, w_rt, b_rt):
    B, S, D = x.shape
    tm = TM_ROUTE
    nrt = w_rt.shape[0]
    n_s = S // tm
    T = B * S
    return pl.pallas_call(
        _route_kernel,
        grid=(B, n_s),
        in_specs=[pl.BlockSpec((None, tm, SEC), lambda b, s: (b, s, 0)),
                  pl.BlockSpec((None, tm, SEC), lambda b, s: (b, s, 0)),
                  pl.BlockSpec((SEC, D), lambda b, s: (0, 0)),
                  pl.BlockSpec((SEC, D), lambda b, s: (0, 0)),
                  pl.BlockSpec((None, tm, D), lambda b, s: (b, s, 0)),
                  pl.BlockSpec((None, 6, D), lambda b, s: (b, 0, 0)),
                  pl.BlockSpec((1, D), lambda b, s: (0, 0)),
                  pl.BlockSpec((1, D), lambda b, s: (0, 0)),
                  pl.BlockSpec((nrt, D), lambda b, s: (0, 0)),
                  pl.BlockSpec((nrt, LANES), lambda b, s: (0, 0))],
        out_specs=[pl.BlockSpec((None, tm, D), lambda b, s: (b, s, 0)),
                   pl.BlockSpec((None, tm, D // 2), lambda b, s: (b, s, 0)),
                   pl.BlockSpec((SUBLANES, tm), lambda b, s: (0, b * n_s + s)),
                   pl.BlockSpec((SUBLANES, tm), lambda b, s: (0, b * n_s + s)),
                   pl.BlockSpec((N_EXPERTS, LANES), lambda b, s: (0, 0))],
        out_shape=[jax.ShapeDtypeStruct((B, S, D), F32),
                   jax.ShapeDtypeStruct((B, S, D // 2), jnp.uint32),
                   jax.ShapeDtypeStruct((SUBLANES, T), jnp.int32),
                   jax.ShapeDtypeStruct((SUBLANES, T), F32),
                   jax.ShapeDtypeStruct((N_EXPERTS, LANES), F32)],
        compiler_params=_cparams(("arbitrary", "arbitrary")),
        name="out_proj_route",
    )(d_out, f_out, w_out_d, w_out_f, x, mod, g_post_attn, g_pre_ffn, w_rt, b_rt)


def _sc_mesh():
    return plsc.VectorSubcoreMesh(core_axis_name="c", subcore_axis_name="s",
                                  num_cores=SC_CORES, num_subcores=SC_SUBCORES)


def _sc_worker_base(per_worker):
    return (lax.axis_index("s") * SC_CORES + lax.axis_index("c")) * per_worker


def _dispatch(dest, h2, n_rows):
    T, D = h2.shape
    per_worker = T // SC_WORKERS

    @functools.partial(
        pl.kernel, mesh=_sc_mesh(),
        out_type=jax.ShapeDtypeStruct((n_rows, D), h2.dtype),
        scratch_types=[pltpu.VMEM((SC_WIN,), jnp.int32), pltpu.VMEM((SC_WIN, D), h2.dtype)],
        name="moe_dispatch")
    def run(h_hbm, dest_hbm, xs_hbm, idx_v, rows_v):
        base = _sc_worker_base(per_worker)

        @pl.loop(0, per_worker // SC_WIN)
        def _(ci):
            off = pl.multiple_of(base + ci * SC_WIN, SC_WIN)
            pltpu.sync_copy(h_hbm.at[pl.ds(off, SC_WIN)], rows_v)
            for k in range(2):
                pltpu.sync_copy(dest_hbm.at[pl.ds(k * T + off, SC_WIN)], idx_v)
                pltpu.sync_copy(rows_v, xs_hbm.at[idx_v])

    return run(h2, dest.reshape(2 * T))


def _gather_rows(ys, dest):
    _, D = ys.shape
    K, T = dest.shape
    n = K * T
    per_worker = n // SC_WORKERS

    @functools.partial(
        pl.kernel, mesh=_sc_mesh(),
        out_type=jax.ShapeDtypeStruct((n, D), ys.dtype),
        scratch_types=[pltpu.VMEM((SC_WIN,), jnp.int32), pltpu.VMEM((SC_WIN, D), ys.dtype)],
        name="moe_gather")
    def run(ys_hbm, dest_hbm, g_hbm, idx_v, rows_v):
        base = _sc_worker_base(per_worker)

        @pl.loop(0, per_worker // SC_WIN)
        def _(ci):
            off = pl.multiple_of(base + ci * SC_WIN, SC_WIN)
            pltpu.sync_copy(dest_hbm.at[pl.ds(off, SC_WIN)], idx_v)
            pltpu.sync_copy(ys_hbm.at[idx_v], rows_v)
            pltpu.sync_copy(rows_v, g_hbm.at[pl.ds(off, SC_WIN)])

    return run(ys, dest.reshape(n)).reshape(K, T, D)


def _expert_kernel(be_ref, nu_ref, nv_ref, xs_ref, wg_ref, wu_ref, wd_ref, ys_ref,
                   wgb, wub, wdb):
    i = pl.program_id(0)

    @pl.when(i < nu_ref[0])
    def _():
        prev = be_ref[jnp.maximum(i - 1, 0)]

        @pl.when((i == 0) | (be_ref[i] != prev))
        def _():
            wgb[...] = wg_ref[...].astype(BF16)
            wub[...] = wu_ref[...].astype(BF16)
            wdb[...] = wd_ref[...].astype(BF16)

        half = xs_ref.shape[1]
        rows = xs_ref.shape[0] // EXPERT_ROW_GROUPS
        row = lax.broadcasted_iota(jnp.int32, (rows, half), 0)
        au = []
        for r0 in range(0, xs_ref.shape[0], rows):
            words = jnp.where(row + r0 < nv_ref[i], xs_ref[r0:r0 + rows, :], jnp.uint32(0))
            lo, hi = (v.astype(BF16) for v in _unpack_halves(words))

            def proj(w, lo=lo, hi=hi):
                return (jnp.dot(lo, w[0:half, :], preferred_element_type=F32)
                        + jnp.dot(hi, w[half:2 * half, :], preferred_element_type=F32))

            au.append((proj(wgb), proj(wub)))
        for gi, (a, u) in enumerate(au):
            hmid = (a * jax.nn.sigmoid(a) * u).astype(BF16)
            y = jnp.dot(hmid, wdb[...], preferred_element_type=F32)
            ys_ref[gi * rows:(gi + 1) * rows, :] = _pack_halves(y)

    @pl.when(i >= nu_ref[0])
    def _():
        ys_ref[...] = jnp.zeros_like(ys_ref)


def _experts(block_e, n_used, n_valid, xs, w_gate, w_up, w_down):
    P, Dh = xs.shape
    nb = P // BM
    _, D, Fd = w_gate.shape

    def row_map(i, be, nu, nv):
        return (jnp.minimum(i, nu[0] - 1), 0)

    def w_map(i, be, nu, nv):
        return (be[i], 0, 0)

    return pl.pallas_call(
        _expert_kernel,
        grid_spec=pltpu.PrefetchScalarGridSpec(
            num_scalar_prefetch=3,
            grid=(nb,),
            in_specs=[pl.BlockSpec((BM, Dh), row_map),
                      pl.BlockSpec((None, D, Fd), w_map),
                      pl.BlockSpec((None, D, Fd), w_map),
                      pl.BlockSpec((None, Fd, D), w_map)],
            out_specs=pl.BlockSpec((BM, Dh), lambda i, be, nu, nv: (i, 0)),
            scratch_shapes=[pltpu.VMEM((D, Fd), BF16), pltpu.VMEM((D, Fd), BF16),
                            pltpu.VMEM((Fd, D), BF16)]),
        out_shape=jax.ShapeDtypeStruct((P, Dh), jnp.uint32),
        compiler_params=_cparams(("arbitrary",)),
        name="moe_experts",
    )(block_e, n_used, n_valid, xs, w_gate, w_up, w_down)


def _combine_kernel(g_ref, w_ref, x1_ref, mod_ref, g_post_ref, *rest):
    o_ref = rest[-1]
    lo0, hi0 = _unpack_halves(g_ref[0])
    lo1, hi1 = _unpack_halves(g_ref[1])
    tm = lo0.shape[0]
    wt = jnp.concatenate([_rows_to_columns(w_ref[:, j * LANES:(j + 1) * LANES])
                          for j in range(tm // LANES)], axis=0)
    w0, w1 = wt[:, 0:1], wt[:, 1:2]
    y = jnp.concatenate([w0 * lo0 + w1 * lo1, w0 * hi0 + w1 * hi1], axis=1)
    o_ref[...] = x1_ref[...] + _rms(y, mod_ref[5:6, :] * g_post_ref[...])


def _combine(g, wts, x1, mod, g_post_ffn, S, chunk, prev_out):
    T, D = x1.shape
    tm = TM_ROWS
    per_b = S // tm
    steps = g.shape[1] // tm
    off = chunk * steps
    in_specs = [pl.BlockSpec((2, tm, D // 2), lambda i: (0, i, 0)),
                pl.BlockSpec((SUBLANES, tm), lambda i: (0, i + off)),
                pl.BlockSpec((tm, D), lambda i: (i + off, 0)),
                pl.BlockSpec((None, 6, D), lambda i: ((i + off) // per_b, 0, 0)),
                pl.BlockSpec((1, D), lambda i: (0, 0))]
    args = [g, wts, x1, mod, g_post_ffn]
    aliases = {}
    if prev_out is not None:
        in_specs.append(pl.BlockSpec(memory_space=pl.ANY))
        args.append(prev_out)
        aliases = {len(args) - 1: 0}
    return pl.pallas_call(
        _combine_kernel,
        grid=(steps,),
        in_specs=in_specs,
        out_specs=pl.BlockSpec((tm, D), lambda i: (i + off, 0)),
        out_shape=jax.ShapeDtypeStruct((T, D), F32),
        input_output_aliases=aliases,
        compiler_params=_cparams(("arbitrary",)),
        name="moe_combine",
    )(*args)


def _rope_tables(S):
    inv = 1.0 / (10000.0 ** (jnp.arange(0, HEAD_DIM, 2, dtype=F32) / HEAD_DIM))
    ang = jnp.arange(S, dtype=F32)[:, None] * inv[None, :]
    c, s = jnp.cos(ang), jnp.sin(ang)
    return jnp.tile(c, (1, 4)), jnp.tile(jnp.concatenate([-s, s], axis=1), (1, 2))


def kernel(x, c, w_ada, b_ada, g_pre_attn, g_post_attn, w_in, b_forget, lambda_q1, lambda_k1,
           lambda_q2, lambda_k2, g_diff_subln, w_out, g_pre_ffn, g_post_ffn, w_group, b_group,
           w_router, b_router, w_gate, w_up, w_down):
    B, S, D = x.shape
    T = B * S
    l = 0
    mod = _ada(c, w_ada[l], b_ada[l]).reshape(B, 6, D)

    wi = w_in[l]
    w_main = jnp.concatenate([wi[:, 0:2 * SEC], wi[:, 3 * SEC:5 * SEC]], axis=1).astype(BF16)
    w_vt = jnp.concatenate([wi[:, 2 * SEC:3 * SEC], wi[:, 5 * SEC:6 * SEC], wi[:, 6 * SEC:],
                            jnp.zeros((D, BF16_SUBLANES - FOX_HEADS), F32)], axis=1).T.astype(BF16)
    bf = jnp.broadcast_to(b_forget[l][:, None], (FOX_HEADS, LANES))
    cos, sin = _rope_tables(S)
    dq, dk, fq, fk, dvt, fvt, fcum = _inproj(x, mod, g_pre_attn[l][None, :], w_main, w_vt,
                                             bf, cos, sin)

    lam_params = jnp.stack([lambda_q1[l], lambda_k1[l], lambda_q2[l], lambda_k2[l]])
    d_out = _diff_attention(dq, dk, dvt, lam_params, g_diff_subln[l][None, :])
    frow = fcum.reshape(B, FOX_HEADS // 2, 2, S)
    f_out = _fox_attention(fq, fk, fvt, frow)

    w_o = w_out[l].astype(BF16)
    zpad = jnp.zeros((SUBLANES - N_GROUPS, D), F32)
    w_rt = jnp.concatenate([w_group[l].T, zpad, w_router[l].T], axis=0)
    b_rt = jnp.concatenate([b_group[l], jnp.zeros((SUBLANES - N_GROUPS,), F32), b_router[l]])
    b_rt = jnp.broadcast_to(b_rt[:, None], (SUBLANES + N_EXPERTS, LANES))
    x1, h2, ri, rw, cnt = _route(d_out, f_out, w_o[:SEC], w_o[SEC:], x, mod,
                                 g_post_attn[l][None, :], g_pre_ffn[l][None, :], w_rt, b_rt)

    counts = cnt[:, 0].astype(jnp.int32)
    padded = ((counts + BM - 1) // BM) * BM
    pend = jnp.cumsum(padded)
    pstart = pend - padded
    eids = jnp.arange(N_EXPERTS, dtype=jnp.int32)
    dest = ri[2:4] + jnp.sum(jnp.where(ri[0:2, :, None] == eids, pstart, 0), axis=-1)
    nb = (2 * T) // BM + N_EXPERTS
    n_used = (pend[-1] // BM).astype(jnp.int32).reshape(1)
    blk0 = jnp.arange(nb, dtype=jnp.int32) * BM
    block_e = jnp.minimum(jnp.sum((blk0[:, None] >= pend[None, :]).astype(jnp.int32), axis=1),
                          N_EXPERTS - 1)
    cend = pstart + counts
    blk_end = jnp.sum(jnp.where(block_e[:, None] == eids, cend, 0), axis=-1)
    n_valid = jnp.clip(blk_end - blk0, 0, BM).astype(jnp.int32)

    xs = _dispatch(dest, h2.reshape(T, D // 2), nb * BM)
    ys = _experts(block_e, n_used, n_valid, xs, w_gate[l], w_up[l], w_down[l])
    out = None
    tc = T // COMBINE_CHUNKS
    for ck in range(COMBINE_CHUNKS):
        g = _gather_rows(ys, dest[:, ck * tc:(ck + 1) * tc])
        out = _combine(g, rw, x1.reshape(T, D), mod, g_post_ffn[l][None, :], S, ck, out)
    return out.reshape(B, S, D)
```

```python
import functools
import math

import jax
import jax.numpy as jnp
from jax import lax
from jax.experimental import pallas as pl
from jax.experimental.pallas import tpu as pltpu
from jax.experimental.pallas import tpu_sc as plsc

F32 = jnp.float32
BF16 = jnp.bfloat16

D_MODEL = 1024
HEAD_DIM = 64
CHUNK = 64
DIFF_HEADS = 4
FOX_HEADS = 8
SEC = 512
N_GROUPS = 4
EXPERTS_PER_GROUP = 8
N_EXPERTS = 32
D_EXPERT = 512
EPS = 1e-6
NEG_INF = -1e30
LOG2E = 1.4426950408889634
LAMBDA_INIT = 0.8 - 0.6 * math.exp(-0.3 * 0)

LANES = 128
SUBLANES = 8
BF16_SUBLANES = 16
TN_ADA = 1024
TM_PROJ = 1024
TQ = 512
TK = 512
HK = TK // 2
ATT_GROUPS = 1
DV_ROWS = 2 * HEAD_DIM + BF16_SUBLANES
TM_ROUTE = 512
ROUTE_ROW_GROUPS = 2
BM = 512
EXPERT_ROW_GROUPS = 2
TM_ROWS = 512
COMBINE_CHUNKS = 4
SC_CORES = 2
SC_SUBCORES = 16
SC_WORKERS = SC_CORES * SC_SUBCORES
SC_WIN = 128
VMEM_LIMIT = 48 * 1024 * 1024
_NT = (((1,), (1,)), ((), ()))


def _cparams(sem, flags=None):
    return pltpu.CompilerParams(dimension_semantics=sem, vmem_limit_bytes=VMEM_LIMIT, flags=flags)


def _rms(x, g):
    return x * lax.rsqrt(jnp.mean(x * x, axis=-1, keepdims=True) + EPS) * g


def _pack_halves(x):
    m = x.shape[1] // 2
    bits = lax.bitcast_convert_type(x.astype(BF16).astype(F32), jnp.uint32)
    return (bits[:, :m] >> 16) | (bits[:, m:] & jnp.uint32(0xFFFF0000))


def _unpack_halves(w):
    lo = lax.bitcast_convert_type(w << 16, F32)
    hi = lax.bitcast_convert_type(w & jnp.uint32(0xFFFF0000), F32)
    return lo, hi


def _rows_to_columns(x):
    pad = jnp.zeros((LANES - x.shape[0], LANES), x.dtype)
    return jnp.concatenate([x, pad], axis=0).T


def _ada_kernel(c_ref, w_ref, b_ref, o_ref):
    c = c_ref[...]
    cs = c * jax.nn.sigmoid(c)
    o_ref[...] = jnp.dot(cs, w_ref[...], preferred_element_type=F32,
                         precision=lax.Precision.HIGHEST) + b_ref[...]


def _ada(c, w_ada, b_ada):
    B, D = c.shape
    N = w_ada.shape[1]
    tn = TN_ADA
    return pl.pallas_call(
        _ada_kernel,
        grid=(N // tn,),
        in_specs=[pl.BlockSpec((B, D), lambda j: (0, 0)),
                  pl.BlockSpec((D, tn), lambda j: (0, j)),
                  pl.BlockSpec((1, tn), lambda j: (0, j))],
        out_specs=pl.BlockSpec((B, tn), lambda j: (0, j)),
        out_shape=jax.ShapeDtypeStruct((B, N), F32),
        compiler_params=_cparams(("arbitrary",)),
        name="ada_mod",
    )(c, w_ada, b_ada.reshape(1, N))


def _split3(x):
    hi = x.astype(BF16)
    r1 = x - hi.astype(F32)
    mid = r1.astype(BF16)
    lo = (r1 - mid.astype(F32)).astype(BF16)
    return hi, mid, lo


def _inproj_kernel(x_ref, mod_ref, g_ref, w_ref, wvt_ref, bf_ref, cos_ref, sin_ref,
                   dq_ref, dk_ref, fq_ref, fk_ref, dvt_ref, fvt_ref, fcum_ref, carry_ref):
    si = pl.program_id(1)
    x = x_ref[...]
    h = _rms(x, g_ref[...] * (1.0 + mod_ref[1:2, :])) + mod_ref[0:1, :]
    hb = h.astype(BF16)
    tm = x.shape[0]

    cos = cos_ref[...]
    sin = sin_ref[...]
    lane = lax.broadcasted_iota(jnp.int32, (tm, LANES), 1)
    first_half = (lane % HEAD_DIM) < (HEAD_DIM // 2)
    qscale = HEAD_DIM ** -0.5 * LOG2E

    def proj(sec):
        return jnp.dot(hb, w_ref[:, sec * SEC:(sec + 1) * SEC], preferred_element_type=F32)

    def rope_store(p, out_ref, scale):
        for j in range(SEC // LANES):
            ch = p[:, j * LANES:(j + 1) * LANES]
            partner = jnp.where(first_half,
                                pltpu.roll(ch, LANES - HEAD_DIM // 2, 1),
                                pltpu.roll(ch, HEAD_DIM // 2, 1))
            r = ch * cos + partner * sin
            if scale != 1.0:
                r = r * scale
            out_ref[:, j * LANES:(j + 1) * LANES] = r.astype(out_ref.dtype)

    rope_store(proj(0), dq_ref, qscale)
    rope_store(proj(1), dk_ref, 1.0)
    fq_ref[...] = (proj(2) * qscale).astype(fq_ref.dtype)
    fk_ref[...] = proj(3).astype(fk_ref.dtype)
    vt = lax.dot_general(wvt_ref[...], hb, _NT, preferred_element_type=F32)
    for hd in range(DIFF_HEADS):
        dvt_ref[hd, 0:LANES, :] = vt[hd * LANES:(hd + 1) * LANES].astype(dvt_ref.dtype)
        dvt_ref[hd, LANES:DV_ROWS, :] = jnp.ones((DV_ROWS - LANES, tm), dvt_ref.dtype)
    fvt_ref[...] = vt[SEC:2 * SEC].astype(fvt_ref.dtype)

    z = vt[2 * SEC:2 * SEC + FOX_HEADS] + bf_ref[:, 0:1]
    lf = (jnp.minimum(z, 0.0) - jnp.log1p(jnp.exp(-jnp.abs(z)))) * LOG2E

    @pl.when(si == 0)
    def _():
        carry_ref[...] = jnp.zeros_like(carry_ref)

    r_i = lax.broadcasted_iota(jnp.int32, (LANES, LANES), 0)
    c_i = lax.broadcasted_iota(jnp.int32, (LANES, LANES), 1)
    tri = jnp.where(r_i <= c_i, 1.0, 0.0).astype(BF16)
    run = carry_ref[...]
    for j in range(tm // LANES):
        hi, mid, lo = _split3(lf[:, j * LANES:(j + 1) * LANES])
        cs = (jnp.dot(hi, tri, preferred_element_type=F32)
              + jnp.dot(mid, tri, preferred_element_type=F32)
              + jnp.dot(lo, tri, preferred_element_type=F32)) + run
        fcum_ref[:, j * LANES:(j + 1) * LANES] = cs
        run = jnp.broadcast_to(cs[:, LANES - 1:LANES], run.shape)
    carry_ref[...] = run


def _inproj(x, mod, g_pre, w_main, w_vt, bf, cos, sin):
    B, S, D = x.shape
    tm = TM_PROJ
    sec_out = jax.ShapeDtypeStruct((B, S, SEC), BF16)
    sec_spec = pl.BlockSpec((None, tm, SEC), lambda b, s: (b, s, 0))
    t_out = jax.ShapeDtypeStruct((B, SEC, S), BF16)
    t_spec = pl.BlockSpec((None, SEC, tm), lambda b, s: (b, 0, s))
    return pl.pallas_call(
        _inproj_kernel,
        grid=(B, S // tm),
        in_specs=[pl.BlockSpec((None, tm, D), lambda b, s: (b, s, 0)),
                  pl.BlockSpec((None, 6, D), lambda b, s: (b, 0, 0)),
                  pl.BlockSpec((1, D), lambda b, s: (0, 0)),
                  pl.BlockSpec((D, 4 * SEC), lambda b, s: (0, 0)),
                  pl.BlockSpec(w_vt.shape, lambda b, s: (0, 0)),
                  pl.BlockSpec((FOX_HEADS, LANES), lambda b, s: (0, 0)),
                  pl.BlockSpec((tm, LANES), lambda b, s: (s, 0)),
                  pl.BlockSpec((tm, LANES), lambda b, s: (s, 0))],
        out_specs=[sec_spec] * 4
        + [pl.BlockSpec((None, DIFF_HEADS, DV_ROWS, tm), lambda b, s: (b, 0, 0, s)), t_spec,
           pl.BlockSpec((None, FOX_HEADS, tm), lambda b, s: (b, 0, s))],
        out_shape=[sec_out] * 4
        + [jax.ShapeDtypeStruct((B, DIFF_HEADS, DV_ROWS, S), BF16), t_out,
           jax.ShapeDtypeStruct((B, FOX_HEADS, S), F32)],
        scratch_shapes=[pltpu.VMEM((FOX_HEADS, LANES), F32)],
        compiler_params=_cparams(("arbitrary", "arbitrary")),
        name="in_proj",
    )(x, mod, g_pre, w_main, w_vt, bf, cos, sin)


def _scores(ch, r0, qm=None):
    r0 = pl.multiple_of(r0, HK)
    t = lax.dot_general(ch["keys"](r0), ch["qm"] if qm is None else qm, _NT,
                        preferred_element_type=F32)
    if ch["kbias"] is not None:
        t = t - ch["kbias"](r0)
    return t


def _flash_t(chains, sa_ref, m_ref, acc_ref, qi, diag_mask, next_qm):
    def update(ci, ch, t, r0, cols=slice(None)):
        m = m_ref[ci, :, cols]
        mt = jnp.max(t, axis=0, keepdims=True)
        qb = ch["qbias"]
        qb = None if qb is None else qb[:, cols]
        m_new = jnp.maximum(m, mt if qb is None else mt + qb)
        alpha = jnp.exp2(m - m_new)
        p = jnp.exp2((t - (m_new if qb is None else m_new - qb)).astype(BF16))
        vt = ch["vt"](pl.multiple_of(r0, HK))
        m_ref[ci, :, cols] = m_new
        acc_ref[ci, :, cols] = (alpha * acc_ref[ci, :, cols]
                                + jnp.dot(vt, p, preferred_element_type=F32))

    def body(ki, carry):
        k0 = ki * TK
        tb = [_scores(ch, k0 + HK) for ch in chains]
        for ci, ch in enumerate(chains):
            update(ci, ch, sa_ref[ci], k0)
        for ci, ch in enumerate(chains):
            sa_ref[ci] = _scores(ch, k0 + TK)
        for ci, ch in enumerate(chains):
            update(ci, ch, tb[ci], k0 + HK)
        return carry

    m_ref[...] = jnp.full(m_ref.shape, NEG_INF, F32)
    acc_ref[...] = jnp.zeros(acc_ref.shape, F32)
    lax.fori_loop(0, qi, body, 0)
    k0 = qi * TK
    hq = TQ - HK
    tb = [_scores(ch, k0 + HK, ch["qm"][hq:]) for ch in chains]
    nxt = [_scores(ch, 0, qm) for ch, qm in zip(chains, next_qm)]
    for ci, ch in enumerate(chains):
        update(ci, ch, jnp.where(diag_mask[0:HK], sa_ref[ci], NEG_INF), k0)
    for ci, ch in enumerate(chains):
        update(ci, ch, jnp.where(diag_mask[HK:TK, hq:], tb[ci], NEG_INF), k0 + HK,
               slice(hq, TQ))
    for ci in range(len(chains)):
        sa_ref[ci] = nxt[ci]
    return tuple(acc_ref[ci] for ci in range(len(chains)))


def _next_tile_start(q0, seq_len):
    return pl.multiple_of(jnp.minimum(q0 + TQ, seq_len - TQ), TQ)


def _for_each_query_tile(n_tiles, tile_fn):
    def body(qi, carry):
        tile_fn(qi, pl.multiple_of(qi * TQ, TQ))
        return carry

    lax.fori_loop(0, n_tiles, body, 0)


def _lane_group(ref, rows, g):
    return ref[rows, g * LANES:(g + 1) * LANES]


def _masked_queries(q_ref, q0):
    lane = lax.broadcasted_iota(jnp.int32, (TQ, LANES), 1)
    out = []
    for g in range(ATT_GROUPS):
        q2 = _lane_group(q_ref, pl.ds(q0, TQ), g)
        out += [jnp.where((lane >= j * HEAD_DIM) & (lane < (j + 1) * HEAD_DIM), q2,
                          jnp.zeros_like(q2)) for j in range(2)]
    return out


def _fox_kernel(q_ref, k_ref, vt_ref, frow_ref, o_ref, sa_ref, m_ref, acc_ref, fcol_ref):
    for g in range(ATT_GROUPS):
        for j in range(frow_ref.shape[2] // LANES):
            fcol_ref[g, j * LANES:(j + 1) * LANES, :] = _rows_to_columns(
                frow_ref[g, :, j * LANES:(j + 1) * LANES])

    vrow = lax.broadcasted_iota(jnp.int32, (LANES, HK), 0)
    r = lax.broadcasted_iota(jnp.int32, (TK, TQ), 0)
    c = lax.broadcasted_iota(jnp.int32, (TK, TQ), 1)

    def make_chains(q0):
        chains = []
        for ci, qm in enumerate(_masked_queries(q_ref, q0)):
            g, j = divmod(ci, 2)
            v_rows = (vrow >= j * HEAD_DIM) & (vrow < (j + 1) * HEAD_DIM)

            def vt(k0, g=g, v_rows=v_rows):
                v = vt_ref[g * LANES:(g + 1) * LANES, pl.ds(k0, HK)]
                return jnp.where(v_rows, v, jnp.ones_like(v))

            chains.append(dict(
                qm=qm, keys=lambda r0, g=g: _lane_group(k_ref, pl.ds(r0, HK), g),
                kbias=lambda r0, g=g, j=j: fcol_ref[g, pl.ds(r0, HK), j:j + 1],
                qbias=frow_ref[g, j:j + 1, pl.ds(q0, TQ)], vt=vt, rows=LANES))
        return chains

    def tile(qi, q0):
        res = _flash_t(make_chains(q0), sa_ref, m_ref, acc_ref, qi, r <= c,
                       _masked_queries(q_ref, _next_tile_start(q0, q_ref.shape[0])))
        h = HEAD_DIM
        for g in range(ATT_GROUPS):
            a0, a1 = res[2 * g], res[2 * g + 1]
            o_t = jnp.concatenate([a0[0:h] / a0[h:2 * h], a1[h:2 * h] / a1[0:h]], axis=0)
            o_ref[pl.ds(q0, TQ), g * LANES:(g + 1) * LANES] = o_t.T.astype(o_ref.dtype)

    for ci, ch in enumerate(make_chains(0)):
        sa_ref[ci] = _scores(ch, 0)
    _for_each_query_tile(q_ref.shape[0] // TQ, tile)


def _fox_attention(fq, fk, fvt, frow):
    B, S, _ = fq.shape
    gw = ATT_GROUPS * LANES
    seq_spec = pl.BlockSpec((None, S, gw), lambda b, p: (b, 0, p))
    return pl.pallas_call(
        _fox_kernel,
        grid=(B, FOX_HEADS // 2 // ATT_GROUPS),
        in_specs=[seq_spec, seq_spec,
                  pl.BlockSpec((None, gw, S), lambda b, p: (b, p, 0)),
                  pl.BlockSpec((None, ATT_GROUPS, 2, S), lambda b, p: (b, p, 0, 0))],
        out_specs=seq_spec,
        out_shape=jax.ShapeDtypeStruct((B, S, SEC), BF16),
        scratch_shapes=[pltpu.VMEM((2 * ATT_GROUPS, HK, TQ), F32),
                        pltpu.VMEM((2 * ATT_GROUPS, 1, TQ), F32),
                        pltpu.VMEM((2 * ATT_GROUPS, LANES, TQ), F32),
                        pltpu.VMEM((ATT_GROUPS, S, LANES), F32)],
        compiler_params=_cparams(("arbitrary", "arbitrary")),
        name="fox_attention",
    )(fq, fk, fvt, frow)


def _diff_kernel(q_ref, k_ref, vt_ref, lam_ref, g_ref, o_ref, sa_ref, m_ref, acc_ref):
    r = lax.broadcasted_iota(jnp.int32, (TK, TQ), 0)
    c = lax.broadcasted_iota(jnp.int32, (TK, TQ), 1)
    lp = lam_ref[...]
    lam = (jnp.exp(jnp.sum(lp[0:1] * lp[1:2], axis=1, keepdims=True))
           - jnp.exp(jnp.sum(lp[2:3] * lp[3:4], axis=1, keepdims=True)) + LAMBDA_INIT)

    def make_chains(q0):
        return [dict(qm=qm, keys=lambda r0, g=ci // 2: _lane_group(k_ref, pl.ds(r0, HK), g),
                     kbias=None, qbias=None,
                     vt=lambda k0, g=ci // 2: vt_ref[g, :, pl.ds(k0, HK)], rows=DV_ROWS)
                for ci, qm in enumerate(_masked_queries(q_ref, q0))]

    def tile(qi, q0):
        res = _flash_t(make_chains(q0), sa_ref, m_ref, acc_ref, qi,
                       (r // CHUNK) <= (c // CHUNK),
                       _masked_queries(q_ref, _next_tile_start(q0, q_ref.shape[0])))
        v = LANES
        for g in range(ATT_GROUPS):
            a0, a1 = res[2 * g], res[2 * g + 1]
            o = (a0[0:v] / a0[v:v + 1] - lam * (a1[0:v] / a1[v:v + 1])).T
            o_ref[pl.ds(q0, TQ), g * LANES:(g + 1) * LANES] = (
                _rms(o, g_ref[...]) * (1.0 - LAMBDA_INIT)).astype(o_ref.dtype)

    for ci, ch in enumerate(make_chains(0)):
        sa_ref[ci] = _scores(ch, 0)
    _for_each_query_tile(q_ref.shape[0] // TQ, tile)


def _diff_attention(dq, dk, dvt, lam_params, g_subln):
    B, S, _ = dq.shape
    seq_spec = pl.BlockSpec((None, S, ATT_GROUPS * LANES), lambda b, h: (b, 0, h))
    return pl.pallas_call(
        _diff_kernel,
        grid=(B, DIFF_HEADS // ATT_GROUPS),
        in_specs=[seq_spec, seq_spec,
                  pl.BlockSpec((None, ATT_GROUPS, DV_ROWS, S), lambda b, h: (b, h, 0, 0)),
                  pl.BlockSpec((4, HEAD_DIM), lambda b, h: (0, 0)),
                  pl.BlockSpec((1, LANES), lambda b, h: (0, 0))],
        out_specs=seq_spec,
        out_shape=jax.ShapeDtypeStruct((B, S, SEC), BF16),
        scratch_shapes=[pltpu.VMEM((2 * ATT_GROUPS, HK, TQ), F32),
                        pltpu.VMEM((2 * ATT_GROUPS, 1, TQ), F32),
                        pltpu.VMEM((2 * ATT_GROUPS, DV_ROWS, TQ), F32)],
        compiler_params=_cparams(("arbitrary", "arbitrary")),
        name="diff_attention",
    )(dq, dk, dvt, lam_params, g_subln)


def _route_kernel(d_ref, f_ref, wd_ref, wf_ref, x_ref, mod_ref, gpa_ref, gpf_ref, wr_ref, br_ref,
                  x1_ref, h2_ref, ri_ref, rw_ref, cnt_ref):
    first = (pl.program_id(0) == 0) & (pl.program_id(1) == 0)

    @pl.when(first)
    def _():
        cnt_ref[...] = jnp.zeros_like(cnt_ref)

    tm = x_ref.shape[0]
    rows = tm // ROUTE_ROW_GROUPS
    mixed = [jnp.dot(d_ref[r0:r0 + rows, :], wd_ref[...], preferred_element_type=F32)
             + jnp.dot(f_ref[r0:r0 + rows, :], wf_ref[...], preferred_element_type=F32)
             for r0 in range(0, tm, rows)]
    wr = wr_ref[...].astype(BF16)
    gate_gain = mod_ref[2:3, :] * gpa_ref[...]
    ffn_gain = gpf_ref[...] * (1.0 + mod_ref[4:5, :])
    lts = []
    for gi, r0 in enumerate(range(0, tm, rows)):
        x1 = x_ref[r0:r0 + rows, :] + _rms(mixed[gi], gate_gain)
        x1_ref[r0:r0 + rows, :] = x1
        h2 = _rms(x1, ffn_gain) + mod_ref[3:4, :]
        h2_ref[r0:r0 + rows, :] = _pack_halves(h2)
        lts.append(br_ref[:, 0:1] + lax.dot_general(wr, h2.astype(BF16), _NT,
                                                    preferred_element_type=F32))
    lt = jnp.concatenate(lts, axis=1)

    g = lt[0:N_GROUPS]
    row4 = lax.broadcasted_iota(jnp.int32, g.shape, 0)
    gmax = jnp.max(g, axis=0, keepdims=True)
    gidx = jnp.min(jnp.where(g == gmax, row4, N_GROUPS), axis=0, keepdims=True)
    gw = 1.0 / jnp.sum(jnp.exp(g - gmax), axis=0, keepdims=True)
    esel = jnp.zeros((EXPERTS_PER_GROUP, tm), F32)
    for gg in range(N_GROUPS):
        e0 = SUBLANES + EXPERTS_PER_GROUP * gg
        esel = jnp.where(gidx == gg, lt[e0:e0 + EXPERTS_PER_GROUP], esel)
    row8 = lax.broadcasted_iota(jnp.int32, esel.shape, 0)
    v1 = jnp.max(esel, axis=0, keepdims=True)
    i1 = jnp.min(jnp.where(esel == v1, row8, EXPERTS_PER_GROUP), axis=0, keepdims=True)
    esel2 = jnp.where(row8 == i1, -jnp.inf, esel)
    v2 = jnp.max(esel2, axis=0, keepdims=True)
    i2 = jnp.min(jnp.where(esel2 == v2, row8, EXPERTS_PER_GROUP), axis=0, keepdims=True)
    e = jnp.exp(v2 - v1)
    w1 = gw / (1.0 + e)
    w2 = gw * e / (1.0 + e)
    e1 = gidx * EXPERTS_PER_GROUP + i1
    e2 = gidx * EXPERTS_PER_GROUP + i2

    row32 = lax.broadcasted_iota(jnp.int32, (N_EXPERTS, tm), 0)
    oh1 = row32 == e1
    oh2 = row32 == e2
    oh = jnp.where(oh1 | oh2, 1.0, 0.0)
    t_r = lax.broadcasted_iota(jnp.int32, (tm, tm), 0)
    t_c = lax.broadcasted_iota(jnp.int32, (tm, tm), 1)
    tri = jnp.where(t_r < t_c, 1.0, 0.0).astype(BF16)
    tot = jnp.dot(oh.astype(BF16), tri, preferred_element_type=F32) + cnt_ref[:, 0:1]
    r1 = jnp.sum(jnp.where(oh1, tot, 0.0), axis=0, keepdims=True)
    r2 = jnp.sum(jnp.where(oh2, tot, 0.0), axis=0, keepdims=True)
    cnt_ref[...] = cnt_ref[...] + jnp.sum(oh, axis=1, keepdims=True)

    zi = jnp.zeros((SUBLANES - 4, tm), jnp.int32)
    ri_ref[...] = jnp.concatenate([e1, e2, r1.astype(jnp.int32), r2.astype(jnp.int32), zi], axis=0)
    rw_ref[...] = jnp.concatenate([w1, w2, jnp.zeros((SUBLANES - 2, tm), F32)], axis=0)


def _route(d_out, f_out, w_out_d, w_out_f, x, mod, g_post_attn, g_pre_ffn, w_rt, b_rt):
    B, S, D = x.shape
    tm = TM_ROUTE
    nrt = w_rt.shape[0]
    n_s = S // tm
    T = B * S
    return pl.pallas_call(
        _route_kernel,
        grid=(B, n_s),
        in_specs=[pl.BlockSpec((None, tm, SEC), lambda b, s: (b, s, 0)),
                  pl.BlockSpec((None, tm, SEC), lambda b, s: (b, s, 0)),
                  pl.BlockSpec((SEC, D), lambda b, s: (0, 0)),
                  pl.BlockSpec((SEC, D), lambda b, s: (0, 0)),
                  pl.BlockSpec((None, tm, D), lambda b, s: (b, s, 0)),
                  pl.BlockSpec((None, 6, D), lambda b, s: (b, 0, 0)),
                  pl.BlockSpec((1, D), lambda b, s: (0, 0)),
                  pl.BlockSpec((1, D), lambda b, s: (0, 0)),
                  pl.BlockSpec((nrt, D), lambda b, s: (0, 0)),
                  pl.BlockSpec((nrt, LANES), lambda b, s: (0, 0))],
        out_specs=[pl.BlockSpec((None, tm, D), lambda b, s: (b, s, 0)),
                   pl.BlockSpec((None, tm, D // 2), lambda b, s: (b, s, 0)),
                   pl.BlockSpec((SUBLANES, tm), lambda b, s: (0, b * n_s + s)),
                   pl.BlockSpec((SUBLANES, tm), lambda b, s: (0, b * n_s + s)),
                   pl.BlockSpec((N_EXPERTS, LANES), lambda b, s: (0, 0))],
        out_shape=[jax.ShapeDtypeStruct((B, S, D), F32),
                   jax.ShapeDtypeStruct((B, S, D // 2), jnp.uint32),
                   jax.ShapeDtypeStruct((SUBLANES, T), jnp.int32),
                   jax.ShapeDtypeStruct((SUBLANES, T), F32),
                   jax.ShapeDtypeStruct((N_EXPERTS, LANES), F32)],
        compiler_params=_cparams(("arbitrary", "arbitrary")),
        name="out_proj_route",
    )(d_out, f_out, w_out_d, w_out_f, x, mod, g_post_attn, g_pre_ffn, w_rt, b_rt)


def _sc_mesh():
    return plsc.VectorSubcoreMesh(core_axis_name="c", subcore_axis_name="s",
                                  num_cores=SC_CORES, num_subcores=SC_SUBCORES)


def _sc_worker_base(per_worker):
    return (lax.axis_index("s") * SC_CORES + lax.axis_index("c")) * per_worker


def _dispatch(dest, h2, n_rows):
    T, D = h2.shape
    per_worker = T // SC_WORKERS

    @functools.partial(
        pl.kernel, mesh=_sc_mesh(),
        out_type=jax.ShapeDtypeStruct((n_rows, D), h2.dtype),
        scratch_types=[pltpu.VMEM((SC_WIN,), jnp.int32), pltpu.VMEM((SC_WIN, D), h2.dtype)],
        name="moe_dispatch")
    def run(h_hbm, dest_hbm, xs_hbm, idx_v, rows_v):
        base = _sc_worker_base(per_worker)

        @pl.loop(0, per_worker // SC_WIN)
        def _(ci):
            off = pl.multiple_of(base + ci * SC_WIN, SC_WIN)
            pltpu.sync_copy(h_hbm.at[pl.ds(off, SC_WIN)], rows_v)
            for k in range(2):
                pltpu.sync_copy(dest_hbm.at[pl.ds(k * T + off, SC_WIN)], idx_v)
                pltpu.sync_copy(rows_v, xs_hbm.at[idx_v])

    return run(h2, dest.reshape(2 * T))


def _gather_rows(ys, dest):
    _, D = ys.shape
    K, T = dest.shape
    n = K * T
    per_worker = n // SC_WORKERS

    @functools.partial(
        pl.kernel, mesh=_sc_mesh(),
        out_type=jax.ShapeDtypeStruct((n, D), ys.dtype),
        scratch_types=[pltpu.VMEM((SC_WIN,), jnp.int32), pltpu.VMEM((SC_WIN, D), ys.dtype)],
        name="moe_gather")
    def run(ys_hbm, dest_hbm, g_hbm, idx_v, rows_v):
        base = _sc_worker_base(per_worker)

        @pl.loop(0, per_worker // SC_WIN)
        def _(ci):
            off = pl.multiple_of(base + ci * SC_WIN, SC_WIN)
            pltpu.sync_copy(dest_hbm.at[pl.ds(off, SC_WIN)], idx_v)
            pltpu.sync_copy(ys_hbm.at[idx_v], rows_v)
            pltpu.sync_copy(rows_v, g_hbm.at[pl.ds(off, SC_WIN)])

    return run(ys, dest.reshape(n)).reshape(K, T, D)


def _expert_kernel(be_ref, nu_ref, nv_ref, xs_ref, wg_ref, wu_ref, wd_ref, ys_ref,
                   wgb, wub, wdb):
    i = pl.program_id(0)

    @pl.when(i < nu_ref[0])
    def _():
        prev = be_ref[jnp.maximum(i - 1, 0)]

        @pl.when((i == 0) | (be_ref[i] != prev))
        def _():
            wgb[...] = wg_ref[...].astype(BF16)
            wub[...] = wu_ref[...].astype(BF16)
            wdb[...] = wd_ref[...].astype(BF16)

        half = xs_ref.shape[1]
        rows = xs_ref.shape[0] // EXPERT_ROW_GROUPS
        row = lax.broadcasted_iota(jnp.int32, (rows, half), 0)
        au = []
        for r0 in range(0, xs_ref.shape[0], rows):
            words = jnp.where(row + r0 < nv_ref[i], xs_ref[r0:r0 + rows, :], jnp.uint32(0))
            lo, hi = (v.astype(BF16) for v in _unpack_halves(words))

            def proj(w, lo=lo, hi=hi):
                return (jnp.dot(lo, w[0:half, :], preferred_element_type=F32)
                        + jnp.dot(hi, w[half:2 * half, :], preferred_element_type=F32))

            au.append((proj(wgb), proj(wub)))
        for gi, (a, u) in enumerate(au):
            hmid = (a * jax.nn.sigmoid(a) * u).astype(BF16)
            y = jnp.dot(hmid, wdb[...], preferred_element_type=F32)
            ys_ref[gi * rows:(gi + 1) * rows, :] = _pack_halves(y)

    @pl.when(i >= nu_ref[0])
    def _():
        ys_ref[...] = jnp.zeros_like(ys_ref)


def _experts(block_e, n_used, n_valid, xs, w_gate, w_up, w_down):
    P, Dh = xs.shape
    nb = P // BM
    _, D, Fd = w_gate.shape

    def row_map(i, be, nu, nv):
        return (jnp.minimum(i, nu[0] - 1), 0)

    def w_map(i, be, nu, nv):
        return (be[i], 0, 0)

    return pl.pallas_call(
        _expert_kernel,
        grid_spec=pltpu.PrefetchScalarGridSpec(
            num_scalar_prefetch=3,
            grid=(nb,),
            in_specs=[pl.BlockSpec((BM, Dh), row_map),
                      pl.BlockSpec((None, D, Fd), w_map),
                      pl.BlockSpec((None, D, Fd), w_map),
                      pl.BlockSpec((None, Fd, D), w_map)],
            out_specs=pl.BlockSpec((BM, Dh), lambda i, be, nu, nv: (i, 0)),
            scratch_shapes=[pltpu.VMEM((D, Fd), BF16), pltpu.VMEM((D, Fd), BF16),
                            pltpu.VMEM((Fd, D), BF16)]),
        out_shape=jax.ShapeDtypeStruct((P, Dh), jnp.uint32),
        compiler_params=_cparams(("arbitrary",)),
        name="moe_experts",
    )(block_e, n_used, n_valid, xs, w_gate, w_up, w_down)


def _combine_kernel(g_ref, w_ref, x1_ref, mod_ref, g_post_ref, *rest):
    o_ref = rest[-1]
    lo0, hi0 = _unpack_halves(g_ref[0])
    lo1, hi1 = _unpack_halves(g_ref[1])
    tm = lo0.shape[0]
    wt = jnp.concatenate([_rows_to_columns(w_ref[:, j * LANES:(j + 1) * LANES])
                          for j in range(tm // LANES)], axis=0)
    w0, w1 = wt[:, 0:1], wt[:, 1:2]
    y = jnp.concatenate([w0 * lo0 + w1 * lo1, w0 * hi0 + w1 * hi1], axis=1)
    o_ref[...] = x1_ref[...] + _rms(y, mod_ref[5:6, :] * g_post_ref[...])


def _combine(g, wts, x1, mod, g_post_ffn, S, chunk, prev_out):
    T, D = x1.shape
    tm = TM_ROWS
    per_b = S // tm
    steps = g.shape[1] // tm
    off = chunk * steps
    in_specs = [pl.BlockSpec((2, tm, D // 2), lambda i: (0, i, 0)),
                pl.BlockSpec((SUBLANES, tm), lambda i: (0, i + off)),
                pl.BlockSpec((tm, D), lambda i: (i + off, 0)),
                pl.BlockSpec((None, 6, D), lambda i: ((i + off) // per_b, 0, 0)),
                pl.BlockSpec((1, D), lambda i: (0, 0))]
    args = [g, wts, x1, mod, g_post_ffn]
    aliases = {}
    if prev_out is not None:
        in_specs.append(pl.BlockSpec(memory_space=pl.ANY))
        args.append(prev_out)
        aliases = {len(args) - 1: 0}
    return pl.pallas_call(
        _combine_kernel,
        grid=(steps,),
        in_specs=in_specs,
        out_specs=pl.BlockSpec((tm, D), lambda i: (i + off, 0)),
        out_shape=jax.ShapeDtypeStruct((T, D), F32),
        input_output_aliases=aliases,
        compiler_params=_cparams(("arbitrary",)),
        name="moe_combine",
    )(*args)


def _rope_tables(S):
    inv = 1.0 / (10000.0 ** (jnp.arange(0, HEAD_DIM, 2, dtype=F32) / HEAD_DIM))
    ang = jnp.arange(S, dtype=F32)[:, None] * inv[None, :]
    c, s = jnp.cos(ang), jnp.sin(ang)
    return jnp.tile(c, (1, 4)), jnp.tile(jnp.concatenate([-s, s], axis=1), (1, 2))


def kernel(x, c, w_ada, b_ada, g_pre_attn, g_post_attn, w_in, b_forget, lambda_q1, lambda_k1,
           lambda_q2, lambda_k2, g_diff_subln, w_out, g_pre_ffn, g_post_ffn, w_group, b_group,
           w_router, b_router, w_gate, w_up, w_down):
    B, S, D = x.shape
    T = B * S
    l = 0
    mod = _ada(c, w_ada[l], b_ada[l]).reshape(B, 6, D)

    wi = w_in[l]
    w_main = jnp.concatenate([wi[:, 0:2 * SEC], wi[:, 3 * SEC:5 * SEC]], axis=1).astype(BF16)
    w_vt = jnp.concatenate([wi[:, 2 * SEC:3 * SEC], wi[:, 5 * SEC:6 * SEC], wi[:, 6 * SEC:],
                            jnp.zeros((D, BF16_SUBLANES - FOX_HEADS), F32)], axis=1).T.astype(BF16)
    bf = jnp.broadcast_to(b_forget[l][:, None], (FOX_HEADS, LANES))
    cos, sin = _rope_tables(S)
    dq, dk, fq, fk, dvt, fvt, fcum = _inproj(x, mod, g_pre_attn[l][None, :], w_main, w_vt,
                                             bf, cos, sin)

    lam_params = jnp.stack([lambda_q1[l], lambda_k1[l], lambda_q2[l], lambda_k2[l]])
    d_out = _diff_attention(dq, dk, dvt, lam_params, g_diff_subln[l][None, :])
    frow = fcum.reshape(B, FOX_HEADS // 2, 2, S)
    f_out = _fox_attention(fq, fk, fvt, frow)

    w_o = w_out[l].astype(BF16)
    zpad = jnp.zeros((SUBLANES - N_GROUPS, D), F32)
    w_rt = jnp.concatenate([w_group[l].T, zpad, w_router[l].T], axis=0)
    b_rt = jnp.concatenate([b_group[l], jnp.zeros((SUBLANES - N_GROUPS,), F32), b_router[l]])
    b_rt = jnp.broadcast_to(b_rt[:, None], (SUBLANES + N_EXPERTS, LANES))
    x1, h2, ri, rw, cnt = _route(d_out, f_out, w_o[:SEC], w_o[SEC:], x, mod,
                                 g_post_attn[l][None, :], g_pre_ffn[l][None, :], w_rt, b_rt)

    counts = cnt[:, 0].astype(jnp.int32)
    padded = ((counts + BM - 1) // BM) * BM
    pend = jnp.cumsum(padded)
    pstart = pend - padded
    eids = jnp.arange(N_EXPERTS, dtype=jnp.int32)
    dest = ri[2:4] + jnp.sum(jnp.where(ri[0:2, :, None] == eids, pstart, 0), axis=-1)
    nb = (2 * T) // BM + N_EXPERTS
    n_used = (pend[-1] // BM).astype(jnp.int32).reshape(1)
    blk0 = jnp.arange(nb, dtype=jnp.int32) * BM
    block_e = jnp.minimum(jnp.sum((blk0[:, None] >= pend[None, :]).astype(jnp.int32), axis=1),
                          N_EXPERTS - 1)
    cend = pstart + counts
    blk_end = jnp.sum(jnp.where(block_e[:, None] == eids, cend, 0), axis=-1)
    n_valid = jnp.clip(blk_end - blk0, 0, BM).astype(jnp.int32)

    xs = _dispatch(dest, h2.reshape(T, D // 2), nb * BM)
    ys = _experts(block_e, n_used, n_valid, xs, w_gate[l], w_up[l], w_down[l])
    out = None
    tc = T // COMBINE_CHUNKS
    for ck in range(COMBINE_CHUNKS):
        g = _gather_rows(ys, dest[:, ck * tc:(ck + 1) * tc])
        out = _combine(g, rw, x1.reshape(T, D), mod, g_post_ffn[l][None, :], S, ck, out)
    return out.reshape(B, S, D)
```

```python
import functools
import math

import jax
import jax.numpy as jnp
from jax import lax
from jax.experimental import pallas as pl
from jax.experimental.pallas import tpu as pltpu
from jax.experimental.pallas import tpu_sc as plsc

F32 = jnp.float32
BF16 = jnp.bfloat16

D_MODEL = 1024
HEAD_DIM = 64
CHUNK = 64
DIFF_HEADS = 4
FOX_HEADS = 8
SEC = 512
N_GROUPS = 4
EXPERTS_PER_GROUP = 8
N_EXPERTS = 32
D_EXPERT = 512
EPS = 1e-6
NEG_INF = -1e30
LOG2E = 1.4426950408889634
LAMBDA_INIT = 0.8 - 0.6 * math.exp(-0.3 * 0)

LANES = 128
SUBLANES = 8
BF16_SUBLANES = 16
TN_ADA = 1024
TM_PROJ = 1024
TQ = 512
TK = 512
HK = TK // 2
ATT_GROUPS = 1
DV_ROWS = 2 * HEAD_DIM + BF16_SUBLANES
TM_ROUTE = 512
ROUTE_ROW_GROUPS = 2
BM = 512
EXPERT_ROW_GROUPS = 2
TM_ROWS = 512
COMBINE_CHUNKS = 4
SC_CORES = 2
SC_SUBCORES = 16
SC_WORKERS = SC_CORES * SC_SUBCORES
SC_WIN = 128
VMEM_LIMIT = 48 * 1024 * 1024
_NT = (((1,), (1,)), ((), ()))


def _cparams(sem, flags=None):
    return pltpu.CompilerParams(dimension_semantics=sem, vmem_limit_bytes=VMEM_LIMIT, flags=flags)


def _rms(x, g):
    return x * lax.rsqrt(jnp.mean(x * x, axis=-1, keepdims=True) + EPS) * g


def _pack_halves(x):
    m = x.shape[1] // 2
    bits = lax.bitcast_convert_type(x.astype(BF16).astype(F32), jnp.uint32)
    return (bits[:, :m] >> 16) | (bits[:, m:] & jnp.uint32(0xFFFF0000))


def _unpack_halves(w):
    lo = lax.bitcast_convert_type(w << 16, F32)
    hi = lax.bitcast_convert_type(w & jnp.uint32(0xFFFF0000), F32)
    return lo, hi


def _rows_to_columns(x):
    pad = jnp.zeros((LANES - x.shape[0], LANES), x.dtype)
    return jnp.concatenate([x, pad], axis=0).T


def _ada_kernel(c_ref, w_ref, b_ref, o_ref):
    c = c_ref[...]
    cs = c * jax.nn.sigmoid(c)
    o_ref[...] = jnp.dot(cs, w_ref[...], preferred_element_type=F32,
                         precision=lax.Precision.HIGHEST) + b_ref[...]


def _ada(c, w_ada, b_ada):
    B, D = c.shape
    N = w_ada.shape[1]
    tn = TN_ADA
    return pl.pallas_call(
        _ada_kernel,
        grid=(N // tn,),
        in_specs=[pl.BlockSpec((B, D), lambda j: (0, 0)),
                  pl.BlockSpec((D, tn), lambda j: (0, j)),
                  pl.BlockSpec((1, tn), lambda j: (0, j))],
        out_specs=pl.BlockSpec((B, tn), lambda j: (0, j)),
        out_shape=jax.ShapeDtypeStruct((B, N), F32),
        compiler_params=_cparams(("arbitrary",)),
        name="ada_mod",
    )(c, w_ada, b_ada.reshape(1, N))


def _split3(x):
    hi = x.astype(BF16)
    r1 = x - hi.astype(F32)
    mid = r1.astype(BF16)
    lo = (r1 - mid.astype(F32)).astype(BF16)
    return hi, mid, lo


def _inproj_kernel(x_ref, mod_ref, g_ref, w_ref, wvt_ref, bf_ref, cos_ref, sin_ref,
                   dq_ref, dk_ref, fq_ref, fk_ref, dvt_ref, fvt_ref, fcum_ref, carry_ref):
    si = pl.program_id(1)
    x = x_ref[...]
    h = _rms(x, g_ref[...] * (1.0 + mod_ref[1:2, :])) + mod_ref[0:1, :]
    hb = h.astype(BF16)
    tm = x.shape[0]

    cos = cos_ref[...]
    sin = sin_ref[...]
    lane = lax.broadcasted_iota(jnp.int32, (tm, LANES), 1)
    first_half = (lane % HEAD_DIM) < (HEAD_DIM // 2)
    qscale = HEAD_DIM ** -0.5 * LOG2E

    def proj(sec):
        return jnp.dot(hb, w_ref[:, sec * SEC:(sec + 1) * SEC], preferred_element_type=F32)

    def rope_store(p, out_ref, scale):
        for j in range(SEC // LANES):
            ch = p[:, j * LANES:(j + 1) * LANES]
            partner = jnp.where(first_half,
                                pltpu.roll(ch, LANES - HEAD_DIM // 2, 1),
                                pltpu.roll(ch, HEAD_DIM // 2, 1))
            r = ch * cos + partner * sin
            if scale != 1.0:
                r = r * scale
            out_ref[:, j * LANES:(j + 1) * LANES] = r.astype(out_ref.dtype)

    rope_store(proj(0), dq_ref, qscale)
    rope_store(proj(1), dk_ref, 1.0)
    fq_ref[...] = (proj(2) * qscale).astype(fq_ref.dtype)
    fk_ref[...] = proj(3).astype(fk_ref.dtype)
    vt = lax.dot_general(wvt_ref[...], hb, _NT, preferred_element_type=F32)
    for hd in range(DIFF_HEADS):
        dvt_ref[hd, 0:LANES, :] = vt[hd * LANES:(hd + 1) * LANES].astype(dvt_ref.dtype)
        dvt_ref[hd, LANES:DV_ROWS, :] = jnp.ones((DV_ROWS - LANES, tm), dvt_ref.dtype)
    fvt_ref[...] = vt[SEC:2 * SEC].astype(fvt_ref.dtype)

    z = vt[2 * SEC:2 * SEC + FOX_HEADS] + bf_ref[:, 0:1]
    lf = (jnp.minimum(z, 0.0) - jnp.log1p(jnp.exp(-jnp.abs(z)))) * LOG2E

    @pl.when(si == 0)
    def _():
        carry_ref[...] = jnp.zeros_like(carry_ref)

    r_i = lax.broadcasted_iota(jnp.int32, (LANES, LANES), 0)
    c_i = lax.broadcasted_iota(jnp.int32, (LANES, LANES), 1)
    tri = jnp.where(r_i <= c_i, 1.0, 0.0).astype(BF16)
    run = carry_ref[...]
    for j in range(tm // LANES):
        hi, mid, lo = _split3(lf[:, j * LANES:(j + 1) * LANES])
        cs = (jnp.dot(hi, tri, preferred_element_type=F32)
              + jnp.dot(mid, tri, preferred_element_type=F32)
              + jnp.dot(lo, tri, preferred_element_type=F32)) + run
        fcum_ref[:, j * LANES:(j + 1) * LANES] = cs
        run = jnp.broadcast_to(cs[:, LANES - 1:LANES], run.shape)
    carry_ref[...] = run


def _inproj(x, mod, g_pre, w_main, w_vt, bf, cos, sin):
    B, S, D = x.shape
    tm = TM_PROJ
    sec_out = jax.ShapeDtypeStruct((B, S, SEC), BF16)
    sec_spec = pl.BlockSpec((None, tm, SEC), lambda b, s: (b, s, 0))
    t_out = jax.ShapeDtypeStruct((B, SEC, S), BF16)
    t_spec = pl.BlockSpec((None, SEC, tm), lambda b, s: (b, 0, s))
    return pl.pallas_call(
        _inproj_kernel,
        grid=(B, S // tm),
        in_specs=[pl.BlockSpec((None, tm, D), lambda b, s: (b, s, 0)),
                  pl.BlockSpec((None, 6, D), lambda b, s: (b, 0, 0)),
                  pl.BlockSpec((1, D), lambda b, s: (0, 0)),
                  pl.BlockSpec((D, 4 * SEC), lambda b, s: (0, 0)),
                  pl.BlockSpec(w_vt.shape, lambda b, s: (0, 0)),
                  pl.BlockSpec((FOX_HEADS, LANES), lambda b, s: (0, 0)),
                  pl.BlockSpec((tm, LANES), lambda b, s: (s, 0)),
                  pl.BlockSpec((tm, LANES), lambda b, s: (s, 0))],
        out_specs=[sec_spec] * 4
        + [pl.BlockSpec((None, DIFF_HEADS, DV_ROWS, tm), lambda b, s: (b, 0, 0, s)), t_spec,
           pl.BlockSpec((None, FOX_HEADS, tm), lambda b, s: (b, 0, s))],
        out_shape=[sec_out] * 4
        + [jax.ShapeDtypeStruct((B, DIFF_HEADS, DV_ROWS, S), BF16), t_out,
           jax.ShapeDtypeStruct((B, FOX_HEADS, S), F32)],
        scratch_shapes=[pltpu.VMEM((FOX_HEADS, LANES), F32)],
        compiler_params=_cparams(("arbitrary", "arbitrary")),
        name="in_proj",
    )(x, mod, g_pre, w_main, w_vt, bf, cos, sin)


def _scores(ch, r0, qm=None):
    r0 = pl.multiple_of(r0, HK)
    t = lax.dot_general(ch["keys"](r0), ch["qm"] if qm is None else qm, _NT,
                        preferred_element_type=F32)
    if ch["kbias"] is not None:
        t = t - ch["kbias"](r0)
    return t


def _flash_t(chains, sa_ref, m_ref, acc_ref, qi, diag_mask, next_qm):
    def update(ci, ch, t, r0, cols=slice(None)):
        m = m_ref[ci, :, cols]
        mt = jnp.max(t, axis=0, keepdims=True)
        qb = ch["qbias"]
        qb = None if qb is None else qb[:, cols]
        m_new = jnp.maximum(m, mt if qb is None else mt + qb)
        alpha = jnp.exp2(m - m_new)
        p = jnp.exp2((t - (m_new if qb is None else m_new - qb)).astype(BF16))
        vt = ch["vt"](pl.multiple_of(r0, HK))
        m_ref[ci, :, cols] = m_new
        acc_ref[ci, :, cols] = (alpha * acc_ref[ci, :, cols]
                                + jnp.dot(vt, p, preferred_element_type=F32))

    def full_tiles(k0, n):
        tb = [_scores(ch, k0 + HK) for ch in chains]
        for t in range(n):
            kt = k0 + t * TK
            for ci, ch in enumerate(chains):
                update(ci, ch, sa_ref[ci], kt)
            for ci, ch in enumerate(chains):
                sa_ref[ci] = _scores(ch, kt + TK)
            tb_next = [_scores(ch, kt + TK + HK) for ch in chains] if t + 1 < n else None
            for ci, ch in enumerate(chains):
                update(ci, ch, tb[ci], kt + HK)
            tb = tb_next

    def tile_pair(kj, carry):
        full_tiles(kj * (2 * TK), 2)
        return carry

    m_ref[...] = jnp.full(m_ref.shape, NEG_INF, F32)
    acc_ref[...] = jnp.zeros(acc_ref.shape, F32)
    lax.fori_loop(0, qi // 2, tile_pair, 0)

    @pl.when(qi % 2 == 1)
    def _():
        full_tiles((qi - 1) * TK, 1)
    k0 = qi * TK
    hq = TQ - HK
    tb = [_scores(ch, k0 + HK, ch["qm"][hq:]) for ch in chains]
    nxt = [_scores(ch, 0, qm) for ch, qm in zip(chains, next_qm)]
    for ci, ch in enumerate(chains):
        update(ci, ch, jnp.where(diag_mask[0:HK], sa_ref[ci], NEG_INF), k0)
    for ci, ch in enumerate(chains):
        update(ci, ch, jnp.where(diag_mask[HK:TK, hq:], tb[ci], NEG_INF), k0 + HK,
               slice(hq, TQ))
    for ci in range(len(chains)):
        sa_ref[ci] = nxt[ci]
    return tuple(acc_ref[ci] for ci in range(len(chains)))


def _next_tile_start(q0, seq_len):
    return pl.multiple_of(jnp.minimum(q0 + TQ, seq_len - TQ), TQ)


def _for_each_query_tile(n_tiles, tile_fn):
    def body(qi, carry):
        tile_fn(qi, pl.multiple_of(qi * TQ, TQ))
        return carry

    lax.fori_loop(0, n_tiles, body, 0)


def _lane_group(ref, rows, g):
    return ref[rows, g * LANES:(g + 1) * LANES]


def _masked_queries(q_ref, q0):
    lane = lax.broadcasted_iota(jnp.int32, (TQ, LANES), 1)
    out = []
    for g in range(ATT_GROUPS):
        q2 = _lane_group(q_ref, pl.ds(q0, TQ), g)
        out += [jnp.where((lane >= j * HEAD_DIM) & (lane < (j + 1) * HEAD_DIM), q2,
                          jnp.zeros_like(q2)) for j in range(2)]
    return out


def _fox_kernel(q_ref, k_ref, vt_ref, frow_ref, o_ref, sa_ref, m_ref, acc_ref, fcol_ref):
    for g in range(ATT_GROUPS):
        for j in range(frow_ref.shape[2] // LANES):
            fcol_ref[g, j * LANES:(j + 1) * LANES, :] = _rows_to_columns(
                frow_ref[g, :, j * LANES:(j + 1) * LANES])

    vrow = lax.broadcasted_iota(jnp.int32, (LANES, HK), 0)
    r = lax.broadcasted_iota(jnp.int32, (TK, TQ), 0)
    c = lax.broadcasted_iota(jnp.int32, (TK, TQ), 1)

    def make_chains(q0):
        chains = []
        for ci, qm in enumerate(_masked_queries(q_ref, q0)):
            g, j = divmod(ci, 2)
            v_rows = (vrow >= j * HEAD_DIM) & (vrow < (j + 1) * HEAD_DIM)

            def vt(k0, g=g, v_rows=v_rows):
                v = vt_ref[g * LANES:(g + 1) * LANES, pl.ds(k0, HK)]
                return jnp.where(v_rows, v, jnp.ones_like(v))

            chains.append(dict(
                qm=qm, keys=lambda r0, g=g: _lane_group(k_ref, pl.ds(r0, HK), g),
                kbias=lambda r0, g=g, j=j: fcol_ref[g, pl.ds(r0, HK), j:j + 1],
                qbias=frow_ref[g, j:j + 1, pl.ds(q0, TQ)], vt=vt, rows=LANES))
        return chains

    def tile(qi, q0):
        res = _flash_t(make_chains(q0), sa_ref, m_ref, acc_ref, qi, r <= c,
                       _masked_queries(q_ref, _next_tile_start(q0, q_ref.shape[0])))
        h = HEAD_DIM
        for g in range(ATT_GROUPS):
            a0, a1 = res[2 * g], res[2 * g + 1]
            o_t = jnp.concatenate([a0[0:h] / a0[h:2 * h], a1[h:2 * h] / a1[0:h]], axis=0)
            o_ref[pl.ds(q0, TQ), g * LANES:(g + 1) * LANES] = o_t.T.astype(o_ref.dtype)

    for ci, ch in enumerate(make_chains(0)):
        sa_ref[ci] = _scores(ch, 0)
    _for_each_query_tile(q_ref.shape[0] // TQ, tile)


def _fox_attention(fq, fk, fvt, frow):
    B, S, _ = fq.shape
    gw = ATT_GROUPS * LANES
    seq_spec = pl.BlockSpec((None, S, gw), lambda b, p: (b, 0, p))
    return pl.pallas_call(
        _fox_kernel,
        grid=(B, FOX_HEADS // 2 // ATT_GROUPS),
        in_specs=[seq_spec, seq_spec,
                  pl.BlockSpec((None, gw, S), lambda b, p: (b, p, 0)),
                  pl.BlockSpec((None, ATT_GROUPS, 2, S), lambda b, p: (b, p, 0, 0))],
        out_specs=seq_spec,
        out_shape=jax.ShapeDtypeStruct((B, S, SEC), BF16),
        scratch_shapes=[pltpu.VMEM((2 * ATT_GROUPS, HK, TQ), F32),
                        pltpu.VMEM((2 * ATT_GROUPS, 1, TQ), F32),
                        pltpu.VMEM((2 * ATT_GROUPS, LANES, TQ), F32),
                        pltpu.VMEM((ATT_GROUPS, S, LANES), F32)],
        compiler_params=_cparams(("arbitrary", "arbitrary")),
        name="fox_attention",
    )(fq, fk, fvt, frow)


def _diff_kernel(q_ref, k_ref, vt_ref, lam_ref, g_ref, o_ref, sa_ref, m_ref, acc_ref):
    r = lax.broadcasted_iota(jnp.int32, (TK, TQ), 0)
    c = lax.broadcasted_iota(jnp.int32, (TK, TQ), 1)
    lp = lam_ref[...]
    lam = (jnp.exp(jnp.sum(lp[0:1] * lp[1:2], axis=1, keepdims=True))
           - jnp.exp(jnp.sum(lp[2:3] * lp[3:4], axis=1, keepdims=True)) + LAMBDA_INIT)

    def make_chains(q0):
        return [dict(qm=qm, keys=lambda r0, g=ci // 2: _lane_group(k_ref, pl.ds(r0, HK), g),
                     kbias=None, qbias=None,
                     vt=lambda k0, g=ci // 2: vt_ref[g, :, pl.ds(k0, HK)], rows=DV_ROWS)
                for ci, qm in enumerate(_masked_queries(q_ref, q0))]

    def tile(qi, q0):
        res = _flash_t(make_chains(q0), sa_ref, m_ref, acc_ref, qi,
                       (r // CHUNK) <= (c // CHUNK),
                       _masked_queries(q_ref, _next_tile_start(q0, q_ref.shape[0])))
        v = LANES
        for g in range(ATT_GROUPS):
            a0, a1 = res[2 * g], res[2 * g + 1]
            o = (a0[0:v] / a0[v:v + 1] - lam * (a1[0:v] / a1[v:v + 1])).T
            o_ref[pl.ds(q0, TQ), g * LANES:(g + 1) * LANES] = (
                _rms(o, g_ref[...]) * (1.0 - LAMBDA_INIT)).astype(o_ref.dtype)

    for ci, ch in enumerate(make_chains(0)):
        sa_ref[ci] = _scores(ch, 0)
    _for_each_query_tile(q_ref.shape[0] // TQ, tile)


def _diff_attention(dq, dk, dvt, lam_params, g_subln):
    B, S, _ = dq.shape
    seq_spec = pl.BlockSpec((None, S, ATT_GROUPS * LANES), lambda b, h: (b, 0, h))
    return pl.pallas_call(
        _diff_kernel,
        grid=(B, DIFF_HEADS // ATT_GROUPS),
        in_specs=[seq_spec, seq_spec,
                  pl.BlockSpec((None, ATT_GROUPS, DV_ROWS, S), lambda b, h: (b, h, 0, 0)),
                  pl.BlockSpec((4, HEAD_DIM), lambda b, h: (0, 0)),
                  pl.BlockSpec((1, LANES), lambda b, h: (0, 0))],
        out_specs=seq_spec,
        out_shape=jax.ShapeDtypeStruct((B, S, SEC), BF16),
        scratch_shapes=[pltpu.VMEM((2 * ATT_GROUPS, HK, TQ), F32),
                        pltpu.VMEM((2 * ATT_GROUPS, 1, TQ), F32),
                        pltpu.VMEM((2 * ATT_GROUPS, DV_ROWS, TQ), F32)],
        compiler_params=_cparams(("arbitrary", "arbitrary")),
        name="diff_attention",
    )(dq, dk, dvt, lam_params, g_subln)


def _route_kernel(d_ref, f_ref, wd_ref, wf_ref, x_ref, mod_ref, gpa_ref, gpf_ref, wr_ref, br_ref,
                  x1_ref, h2_ref, ri_ref, rw_ref, cnt_ref):
    first = (pl.program_id(0) == 0) & (pl.program_id(1) == 0)

    @pl.when(first)
    def _():
        cnt_ref[...] = jnp.zeros_like(cnt_ref)

    tm = x_ref.shape[0]
    rows = tm // ROUTE_ROW_GROUPS
    mixed = [jnp.dot(d_ref[r0:r0 + rows, :], wd_ref[...], preferred_element_type=F32)
             + jnp.dot(f_ref[r0:r0 + rows, :], wf_ref[...], preferred_element_type=F32)
             for r0 in range(0, tm, rows)]
    wr = wr_ref[...].astype(BF16)
    gate_gain = mod_ref[2:3, :] * gpa_ref[...]
    ffn_gain = gpf_ref[...] * (1.0 + mod_ref[4:5, :])
    lts = []
    for gi, r0 in enumerate(range(0, tm, rows)):
        x1 = x_ref[r0:r0 + rows, :] + _rms(mixed[gi], gate_gain)
        x1_ref[r0:r0 + rows, :] = x1
        h2 = _rms(x1, ffn_gain) + mod_ref[3:4, :]
        h2_ref[r0:r0 + rows, :] = _pack_halves(h2)
        lts.append(br_ref[:, 0:1] + lax.dot_general(wr, h2.astype(BF16), _NT,
                                                    preferred_element_type=F32))
    lt = jnp.concatenate(lts, axis=1)

    g = lt[0:N_GROUPS]
    row4 = lax.broadcasted_iota(jnp.int32, g.shape, 0)
    gmax = jnp.max(g, axis=0, keepdims=True)
    gidx = jnp.min(jnp.where(g == gmax, row4, N_GROUPS), axis=0, keepdims=True)
    gw = 1.0 / jnp.sum(jnp.exp(g - gmax), axis=0, keepdims=True)
    esel = jnp.zeros((EXPERTS_PER_GROUP, tm), F32)
    for gg in range(N_GROUPS):
        e0 = SUBLANES + EXPERTS_PER_GROUP * gg
        esel = jnp.where(gidx == gg, lt[e0:e0 + EXPERTS_PER_GROUP], esel)
    row8 = lax.broadcasted_iota(jnp.int32, esel.shape, 0)
    v1 = jnp.max(esel, axis=0, keepdims=True)
    i1 = jnp.min(jnp.where(esel == v1, row8, EXPERTS_PER_GROUP), axis=0, keepdims=True)
    esel2 = jnp.where(row8 == i1, -jnp.inf, esel)
    v2 = jnp.max(esel2, axis=0, keepdims=True)
    i2 = jnp.min(jnp.where(esel2 == v2, row8, EXPERTS_PER_GROUP), axis=0, keepdims=True)
    e = jnp.exp(v2 - v1)
    w1 = gw / (1.0 + e)
    w2 = gw * e / (1.0 + e)
    e1 = gidx * EXPERTS_PER_GROUP + i1
    e2 = gidx * EXPERTS_PER_GROUP + i2

    row32 = lax.broadcasted_iota(jnp.int32, (N_EXPERTS, tm), 0)
    oh1 = row32 == e1
    oh2 = row32 == e2
    oh = jnp.where(oh1 | oh2, 1.0, 0.0)
    t_r = lax.broadcasted_iota(jnp.int32, (tm, tm), 0)
    t_c = lax.broadcasted_iota(jnp.int32, (tm, tm), 1)
    tri = jnp.where(t_r < t_c, 1.0, 0.0).astype(BF16)
    tot = jnp.dot(oh.astype(BF16), tri, preferred_element_type=F32) + cnt_ref[:, 0:1]
    r1 = jnp.sum(jnp.where(oh1, tot, 0.0), axis=0, keepdims=True)
    r2 = jnp.sum(jnp.where(oh2, tot, 0.0), axis=0, keepdims=True)
    cnt_ref[...] = cnt_ref[...] + jnp.sum(oh, axis=1, keepdims=True)

    zi = jnp.zeros((SUBLANES - 4, tm), jnp.int32)
    ri_ref[...] = jnp.concatenate([e1, e2, r1.astype(jnp.int32), r2.astype(jnp.int32), zi], axis=0)
    rw_ref[...] = jnp.concatenate([w1, w2, jnp.zeros((SUBLANES - 2, tm), F32)], axis=0)


def _route(d_out, f_out, w_out_d, w_out_f, x, mod, g_post_attn, g_pre_ffn, w_rt, b_rt):
    B, S, D = x.shape
    tm = TM_ROUTE
    nrt = w_rt.shape[0]
    n_s = S // tm
    T = B * S
    return pl.pallas_call(
        _route_kernel,
        grid=(B, n_s),
        in_specs=[pl.BlockSpec((None, tm, SEC), lambda b, s: (b, s, 0)),
                  pl.BlockSpec((None, tm, SEC), lambda b, s: (b, s, 0)),
                  pl.BlockSpec((SEC, D), lambda b, s: (0, 0)),
                  pl.BlockSpec((SEC, D), lambda b, s: (0, 0)),
                  pl.BlockSpec((None, tm, D), lambda b, s: (b, s, 0)),
                  pl.BlockSpec((None, 6, D), lambda b, s: (b, 0, 0)),
                  pl.BlockSpec((1, D), lambda b, s: (0, 0)),
                  pl.BlockSpec((1, D), lambda b, s: (0, 0)),
                  pl.BlockSpec((nrt, D), lambda b, s: (0, 0)),
                  pl.BlockSpec((nrt, LANES), lambda b, s: (0, 0))],
        out_specs=[pl.BlockSpec((None, tm, D), lambda b, s: (b, s, 0)),
                   pl.BlockSpec((None, tm, D // 2), lambda b, s: (b, s, 0)),
                   pl.BlockSpec((SUBLANES, tm), lambda b, s: (0, b * n_s + s)),
                   pl.BlockSpec((SUBLANES, tm), lambda b, s: (0, b * n_s + s)),
                   pl.BlockSpec((N_EXPERTS, LANES), lambda b, s: (0, 0))],
        out_shape=[jax.ShapeDtypeStruct((B, S, D), F32),
                   jax.ShapeDtypeStruct((B, S, D // 2), jnp.uint32),
                   jax.ShapeDtypeStruct((SUBLANES, T), jnp.int32),
                   jax.ShapeDtypeStruct((SUBLANES, T), F32),
                   jax.ShapeDtypeStruct((N_EXPERTS, LANES), F32)],
        compiler_params=_cparams(("arbitrary", "arbitrary")),
        name="out_proj_route",
    )(d_out, f_out, w_out_d, w_out_f, x, mod, g_post_attn, g_pre_ffn, w_rt, b_rt)


def _sc_mesh():
    return plsc.VectorSubcoreMesh(core_axis_name="c", subcore_axis_name="s",
                                  num_cores=SC_CORES, num_subcores=SC_SUBCORES)


def _sc_worker_base(per_worker):
    return (lax.axis_index("s") * SC_CORES + lax.axis_index("c")) * per_worker


def _dispatch(dest, h2, n_rows):
    T, D = h2.shape
    per_worker = T // SC_WORKERS

    @functools.partial(
        pl.kernel, mesh=_sc_mesh(),
        out_type=jax.ShapeDtypeStruct((n_rows, D), h2.dtype),
        scratch_types=[pltpu.VMEM((SC_WIN,), jnp.int32), pltpu.VMEM((SC_WIN, D), h2.dtype)],
        name="moe_dispatch")
    def run(h_hbm, dest_hbm, xs_hbm, idx_v, rows_v):
        base = _sc_worker_base(per_worker)

        @pl.loop(0, per_worker // SC_WIN)
        def _(ci):
            off = pl.multiple_of(base + ci * SC_WIN, SC_WIN)
            pltpu.sync_copy(h_hbm.at[pl.ds(off, SC_WIN)], rows_v)
            for k in range(2):
                pltpu.sync_copy(dest_hbm.at[pl.ds(k * T + off, SC_WIN)], idx_v)
                pltpu.sync_copy(rows_v, xs_hbm.at[idx_v])

    return run(h2, dest.reshape(2 * T))


def _gather_rows(ys, dest):
    _, D = ys.shape
    K, T = dest.shape
    n = K * T
    per_worker = n // SC_WORKERS

    @functools.partial(
        pl.kernel, mesh=_sc_mesh(),
        out_type=jax.ShapeDtypeStruct((n, D), ys.dtype),
        scratch_types=[pltpu.VMEM((SC_WIN,), jnp.int32), pltpu.VMEM((SC_WIN, D), ys.dtype)],
        name="moe_gather")
    def run(ys_hbm, dest_hbm, g_hbm, idx_v, rows_v):
        base = _sc_worker_base(per_worker)

        @pl.loop(0, per_worker // SC_WIN)
        def _(ci):
            off = pl.multiple_of(base + ci * SC_WIN, SC_WIN)
            pltpu.sync_copy(dest_hbm.at[pl.ds(off, SC_WIN)], idx_v)
            pltpu.sync_copy(ys_hbm.at[idx_v], rows_v)
            pltpu.sync_copy(rows_v, g_hbm.at[pl.ds(off, SC_WIN)])

    return run(ys, dest.reshape(n)).reshape(K, T, D)


def _expert_kernel(be_ref, nu_ref, nv_ref, xs_ref, wg_ref, wu_ref, wd_ref, ys_ref,
                   wgb, wub, wdb):
    i = pl.program_id(0)

    @pl.when(i < nu_ref[0])
    def _():
        prev = be_ref[jnp.maximum(i - 1, 0)]

        @pl.when((i == 0) | (be_ref[i] != prev))
        def _():
            wgb[...] = wg_ref[...].astype(BF16)
            wub[...] = wu_ref[...].astype(BF16)
            wdb[...] = wd_ref[...].astype(BF16)

        half = xs_ref.shape[1]
        rows = xs_ref.shape[0] // EXPERT_ROW_GROUPS
        row = lax.broadcasted_iota(jnp.int32, (rows, half), 0)
        au = []
        for r0 in range(0, xs_ref.shape[0], rows):
            words = jnp.where(row + r0 < nv_ref[i], xs_ref[r0:r0 + rows, :], jnp.uint32(0))
            lo, hi = (v.astype(BF16) for v in _unpack_halves(words))

            def proj(w, lo=lo, hi=hi):
                return (jnp.dot(lo, w[0:half, :], preferred_element_type=F32)
                        + jnp.dot(hi, w[half:2 * half, :], preferred_element_type=F32))

            au.append((proj(wgb), proj(wub)))
        for gi, (a, u) in enumerate(au):
            hmid = (a * jax.nn.sigmoid(a) * u).astype(BF16)
            y = jnp.dot(hmid, wdb[...], preferred_element_type=F32)
            ys_ref[gi * rows:(gi + 1) * rows, :] = _pack_halves(y)

    @pl.when(i >= nu_ref[0])
    def _():
        ys_ref[...] = jnp.zeros_like(ys_ref)


def _experts(block_e, n_used, n_valid, xs, w_gate, w_up, w_down):
    P, Dh = xs.shape
    nb = P // BM
    _, D, Fd = w_gate.shape

    def row_map(i, be, nu, nv):
        return (jnp.minimum(i, nu[0] - 1), 0)

    def w_map(i, be, nu, nv):
        return (be[i], 0, 0)

    return pl.pallas_call(
        _expert_kernel,
        grid_spec=pltpu.PrefetchScalarGridSpec(
            num_scalar_prefetch=3,
            grid=(nb,),
            in_specs=[pl.BlockSpec((BM, Dh), row_map),
                      pl.BlockSpec((None, D, Fd), w_map),
                      pl.BlockSpec((None, D, Fd), w_map),
                      pl.BlockSpec((None, Fd, D), w_map)],
            out_specs=pl.BlockSpec((BM, Dh), lambda i, be, nu, nv: (i, 0)),
            scratch_shapes=[pltpu.VMEM((D, Fd), BF16), pltpu.VMEM((D, Fd), BF16),
                            pltpu.VMEM((Fd, D), BF16)]),
        out_shape=jax.ShapeDtypeStruct((P, Dh), jnp.uint32),
        compiler_params=_cparams(("arbitrary",)),
        name="moe_experts",
    )(block_e, n_used, n_valid, xs, w_gate, w_up, w_down)


def _combine_kernel(g_ref, w_ref, x1_ref, mod_ref, g_post_ref, *rest):
    o_ref = rest[-1]
    lo0, hi0 = _unpack_halves(g_ref[0])
    lo1, hi1 = _unpack_halves(g_ref[1])
    tm = lo0.shape[0]
    wt = jnp.concatenate([_rows_to_columns(w_ref[:, j * LANES:(j + 1) * LANES])
                          for j in range(tm // LANES)], axis=0)
    w0, w1 = wt[:, 0:1], wt[:, 1:2]
    y = jnp.concatenate([w0 * lo0 + w1 * lo1, w0 * hi0 + w1 * hi1], axis=1)
    o_ref[...] = x1_ref[...] + _rms(y, mod_ref[5:6, :] * g_post_ref[...])


def _combine(g, wts, x1, mod, g_post_ffn, S, chunk, prev_out):
    T, D = x1.shape
    tm = TM_ROWS
    per_b = S // tm
    steps = g.shape[1] // tm
    off = chunk * steps
    in_specs = [pl.BlockSpec((2, tm, D // 2), lambda i: (0, i, 0)),
                pl.BlockSpec((SUBLANES, tm), lambda i: (0, i + off)),
                pl.BlockSpec((tm, D), lambda i: (i + off, 0)),
                pl.BlockSpec((None, 6, D), lambda i: ((i + off) // per_b, 0, 0)),
                pl.BlockSpec((1, D), lambda i: (0, 0))]
    args = [g, wts, x1, mod, g_post_ffn]
    aliases = {}
    if prev_out is not None:
        in_specs.append(pl.BlockSpec(memory_space=pl.ANY))
        args.append(prev_out)
        aliases = {len(args) - 1: 0}
    return pl.pallas_call(
        _combine_kernel,
        grid=(steps,),
        in_specs=in_specs,
        out_specs=pl.BlockSpec((tm, D), lambda i: (i + off, 0)),
        out_shape=jax.ShapeDtypeStruct((T, D), F32),
        input_output_aliases=aliases,
        compiler_params=_cparams(("arbitrary",)),
        name="moe_combine",
    )(*args)


def _rope_tables(S):
    inv = 1.0 / (10000.0 ** (jnp.arange(0, HEAD_DIM, 2, dtype=F32) / HEAD_DIM))
    ang = jnp.arange(S, dtype=F32)[:, None] * inv[None, :]
    c, s = jnp.cos(ang), jnp.sin(ang)
    return jnp.tile(c, (1, 4)), jnp.tile(jnp.concatenate([-s, s], axis=1), (1, 2))


def kernel(x, c, w_ada, b_ada, g_pre_attn, g_post_attn, w_in, b_forget, lambda_q1, lambda_k1,
           lambda_q2, lambda_k2, g_diff_subln, w_out, g_pre_ffn, g_post_ffn, w_group, b_group,
           w_router, b_router, w_gate, w_up, w_down):
    B, S, D = x.shape
    T = B * S
    l = 0
    mod = _ada(c, w_ada[l], b_ada[l]).reshape(B, 6, D)

    wi = w_in[l]
    w_main = jnp.concatenate([wi[:, 0:2 * SEC], wi[:, 3 * SEC:5 * SEC]], axis=1).astype(BF16)
    w_vt = jnp.concatenate([wi[:, 2 * SEC:3 * SEC], wi[:, 5 * SEC:6 * SEC], wi[:, 6 * SEC:],
                            jnp.zeros((D, BF16_SUBLANES - FOX_HEADS), F32)], axis=1).T.astype(BF16)
    bf = jnp.broadcast_to(b_forget[l][:, None], (FOX_HEADS, LANES))
    cos, sin = _rope_tables(S)
    dq, dk, fq, fk, dvt, fvt, fcum = _inproj(x, mod, g_pre_attn[l][None, :], w_main, w_vt,
                                             bf, cos, sin)

    lam_params = jnp.stack([lambda_q1[l], lambda_k1[l], lambda_q2[l], lambda_k2[l]])
    d_out = _diff_attention(dq, dk, dvt, lam_params, g_diff_subln[l][None, :])
    frow = fcum.reshape(B, FOX_HEADS // 2, 2, S)
    f_out = _fox_attention(fq, fk, fvt, frow)

    w_o = w_out[l].astype(BF16)
    zpad = jnp.zeros((SUBLANES - N_GROUPS, D), F32)
    w_rt = jnp.concatenate([w_group[l].T, zpad, w_router[l].T], axis=0)
    b_rt = jnp.concatenate([b_group[l], jnp.zeros((SUBLANES - N_GROUPS,), F32), b_router[l]])
    b_rt = jnp.broadcast_to(b_rt[:, None], (SUBLANES + N_EXPERTS, LANES))
    x1, h2, ri, rw, cnt = _route(d_out, f_out, w_o[:SEC], w_o[SEC:], x, mod,
                                 g_post_attn[l][None, :], g_pre_ffn[l][None, :], w_rt, b_rt)

    counts = cnt[:, 0].astype(jnp.int32)
    padded = ((counts + BM - 1) // BM) * BM
    pend = jnp.cumsum(padded)
    pstart = pend - padded
    eids = jnp.arange(N_EXPERTS, dtype=jnp.int32)
    dest = ri[2:4] + jnp.sum(jnp.where(ri[0:2, :, None] == eids, pstart, 0), axis=-1)
    nb = (2 * T) // BM + N_EXPERTS
    n_used = (pend[-1] // BM).astype(jnp.int32).reshape(1)
    blk0 = jnp.arange(nb, dtype=jnp.int32) * BM
    block_e = jnp.minimum(jnp.sum((blk0[:, None] >= pend[None, :]).astype(jnp.int32), axis=1),
                          N_EXPERTS - 1)
    cend = pstart + counts
    blk_end = jnp.sum(jnp.where(block_e[:, None] == eids, cend, 0), axis=-1)
    n_valid = jnp.clip(blk_end - blk0, 0, BM).astype(jnp.int32)

    xs = _dispatch(dest, h2.reshape(T, D // 2), nb * BM)
    ys = _experts(block_e, n_used, n_valid, xs, w_gate[l], w_up[l], w_down[l])
    out = None
    tc = T // COMBINE_CHUNKS
    for ck in range(COMBINE_CHUNKS):
        g = _gather_rows(ys, dest[:, ck * tc:(ck + 1) * tc])
        out = _combine(g, rw, x1.reshape(T, D), mod, g_post_ffn[l][None, :], S, ck, out)
    return out.reshape(B, S, D)
```

```python
import functools
import math

import jax
import jax.numpy as jnp
from jax import lax
from jax.experimental import pallas as pl
from jax.experimental.pallas import tpu as pltpu
from jax.experimental.pallas import tpu_sc as plsc

F32 = jnp.float32
BF16 = jnp.bfloat16

D_MODEL = 1024
HEAD_DIM = 64
CHUNK = 64
DIFF_HEADS = 4
FOX_HEADS = 8
SEC = 512
N_GROUPS = 4
EXPERTS_PER_GROUP = 8
N_EXPERTS = 32
D_EXPERT = 512
EPS = 1e-6
NEG_INF = -1e30
LOG2E = 1.4426950408889634
LAMBDA_INIT = 0.8 - 0.6 * math.exp(-0.3 * 0)

LANES = 128
SUBLANES = 8
BF16_SUBLANES = 16
TN_ADA = 1024
TM_PROJ = 1024
TQ = 512
TK = 512
HK = TK // 2
ATT_GROUPS = 1
DV_ROWS = 2 * HEAD_DIM + BF16_SUBLANES
TM_ROUTE = 512
ROUTE_ROW_GROUPS = 4
BM = 512
EXPERT_ROW_GROUPS = 2
TM_ROWS = 512
COMBINE_CHUNKS = 4
SC_CORES = 2
SC_SUBCORES = 16
SC_WORKERS = SC_CORES * SC_SUBCORES
SC_WIN = 128
VMEM_LIMIT = 48 * 1024 * 1024
_NT = (((1,), (1,)), ((), ()))


def _cparams(sem, flags=None):
    return pltpu.CompilerParams(dimension_semantics=sem, vmem_limit_bytes=VMEM_LIMIT, flags=flags)


def _rms(x, g):
    return x * lax.rsqrt(jnp.mean(x * x, axis=-1, keepdims=True) + EPS) * g


def _pack_halves(x):
    m = x.shape[1] // 2
    bits = lax.bitcast_convert_type(x.astype(BF16).astype(F32), jnp.uint32)
    return (bits[:, :m] >> 16) | (bits[:, m:] & jnp.uint32(0xFFFF0000))


def _unpack_halves(w):
    lo = lax.bitcast_convert_type(w << 16, F32)
    hi = lax.bitcast_convert_type(w & jnp.uint32(0xFFFF0000), F32)
    return lo, hi


def _rows_to_columns(x):
    pad = jnp.zeros((LANES - x.shape[0], LANES), x.dtype)
    return jnp.concatenate([x, pad], axis=0).T


def _ada_kernel(c_ref, w_ref, b_ref, o_ref):
    c = c_ref[...]
    cs = c * jax.nn.sigmoid(c)
    o_ref[...] = jnp.dot(cs, w_ref[...], preferred_element_type=F32,
                         precision=lax.Precision.HIGHEST) + b_ref[...]


def _ada(c, w_ada, b_ada):
    B, D = c.shape
    N = w_ada.shape[1]
    tn = TN_ADA
    return pl.pallas_call(
        _ada_kernel,
        grid=(N // tn,),
        in_specs=[pl.BlockSpec((B, D), lambda j: (0, 0)),
                  pl.BlockSpec((D, tn), lambda j: (0, j)),
                  pl.BlockSpec((1, tn), lambda j: (0, j))],
        out_specs=pl.BlockSpec((B, tn), lambda j: (0, j)),
        out_shape=jax.ShapeDtypeStruct((B, N), F32),
        compiler_params=_cparams(("arbitrary",)),
        name="ada_mod",
    )(c, w_ada, b_ada.reshape(1, N))


def _split3(x):
    hi = x.astype(BF16)
    r1 = x - hi.astype(F32)
    mid = r1.astype(BF16)
    lo = (r1 - mid.astype(F32)).astype(BF16)
    return hi, mid, lo


def _inproj_kernel(x_ref, mod_ref, g_ref, w_ref, wvt_ref, bf_ref, cos_ref, sin_ref,
                   dq_ref, dk_ref, fq_ref, fk_ref, dvt_ref, fvt_ref, fcum_ref, carry_ref):
    si = pl.program_id(1)
    x = x_ref[...]
    h = _rms(x, g_ref[...] * (1.0 + mod_ref[1:2, :])) + mod_ref[0:1, :]
    hb = h.astype(BF16)
    tm = x.shape[0]

    cos = cos_ref[...]
    sin = sin_ref[...]
    lane = lax.broadcasted_iota(jnp.int32, (tm, LANES), 1)
    first_half = (lane % HEAD_DIM) < (HEAD_DIM // 2)
    qscale = HEAD_DIM ** -0.5 * LOG2E

    def proj(sec):
        return jnp.dot(hb, w_ref[:, sec * SEC:(sec + 1) * SEC], preferred_element_type=F32)

    def rope_store(p, out_ref, scale):
        for j in range(SEC // LANES):
            ch = p[:, j * LANES:(j + 1) * LANES]
            partner = jnp.where(first_half,
                                pltpu.roll(ch, LANES - HEAD_DIM // 2, 1),
                                pltpu.roll(ch, HEAD_DIM // 2, 1))
            r = ch * cos + partner * sin
            if scale != 1.0:
                r = r * scale
            out_ref[:, j * LANES:(j + 1) * LANES] = r.astype(out_ref.dtype)

    rope_store(proj(0), dq_ref, qscale)
    rope_store(proj(1), dk_ref, 1.0)
    fq_ref[...] = (proj(2) * qscale).astype(fq_ref.dtype)
    fk_ref[...] = proj(3).astype(fk_ref.dtype)
    vt = lax.dot_general(wvt_ref[...], hb, _NT, preferred_element_type=F32)
    for hd in range(DIFF_HEADS):
        dvt_ref[hd, 0:LANES, :] = vt[hd * LANES:(hd + 1) * LANES].astype(dvt_ref.dtype)
        dvt_ref[hd, LANES:DV_ROWS, :] = jnp.ones((DV_ROWS - LANES, tm), dvt_ref.dtype)
    fvt_ref[...] = vt[SEC:2 * SEC].astype(fvt_ref.dtype)

    z = vt[2 * SEC:2 * SEC + FOX_HEADS] + bf_ref[:, 0:1]
    lf = (jnp.minimum(z, 0.0) - jnp.log1p(jnp.exp(-jnp.abs(z)))) * LOG2E

    @pl.when(si == 0)
    def _():
        carry_ref[...] = jnp.zeros_like(carry_ref)

    r_i = lax.broadcasted_iota(jnp.int32, (LANES, LANES), 0)
    c_i = lax.broadcasted_iota(jnp.int32, (LANES, LANES), 1)
    tri = jnp.where(r_i <= c_i, 1.0, 0.0).astype(BF16)
    run = carry_ref[...]
    for j in range(tm // LANES):
        hi, mid, lo = _split3(lf[:, j * LANES:(j + 1) * LANES])
        cs = (jnp.dot(hi, tri, preferred_element_type=F32)
              + jnp.dot(mid, tri, preferred_element_type=F32)
              + jnp.dot(lo, tri, preferred_element_type=F32)) + run
        fcum_ref[:, j * LANES:(j + 1) * LANES] = cs
        run = jnp.broadcast_to(cs[:, LANES - 1:LANES], run.shape)
    carry_ref[...] = run


def _inproj(x, mod, g_pre, w_main, w_vt, bf, cos, sin):
    B, S, D = x.shape
    tm = TM_PROJ
    sec_out = jax.ShapeDtypeStruct((B, S, SEC), BF16)
    sec_spec = pl.BlockSpec((None, tm, SEC), lambda b, s: (b, s, 0))
    t_out = jax.ShapeDtypeStruct((B, SEC, S), BF16)
    t_spec = pl.BlockSpec((None, SEC, tm), lambda b, s: (b, 0, s))
    return pl.pallas_call(
        _inproj_kernel,
        grid=(B, S // tm),
        in_specs=[pl.BlockSpec((None, tm, D), lambda b, s: (b, s, 0)),
                  pl.BlockSpec((None, 6, D), lambda b, s: (b, 0, 0)),
                  pl.BlockSpec((1, D), lambda b, s: (0, 0)),
                  pl.BlockSpec((D, 4 * SEC), lambda b, s: (0, 0)),
                  pl.BlockSpec(w_vt.shape, lambda b, s: (0, 0)),
                  pl.BlockSpec((FOX_HEADS, LANES), lambda b, s: (0, 0)),
                  pl.BlockSpec((tm, LANES), lambda b, s: (s, 0)),
                  pl.BlockSpec((tm, LANES), lambda b, s: (s, 0))],
        out_specs=[sec_spec] * 4
        + [pl.BlockSpec((None, DIFF_HEADS, DV_ROWS, tm), lambda b, s: (b, 0, 0, s)), t_spec,
           pl.BlockSpec((None, FOX_HEADS, tm), lambda b, s: (b, 0, s))],
        out_shape=[sec_out] * 4
        + [jax.ShapeDtypeStruct((B, DIFF_HEADS, DV_ROWS, S), BF16), t_out,
           jax.ShapeDtypeStruct((B, FOX_HEADS, S), F32)],
        scratch_shapes=[pltpu.VMEM((FOX_HEADS, LANES), F32)],
        compiler_params=_cparams(("arbitrary", "arbitrary")),
        name="in_proj",
    )(x, mod, g_pre, w_main, w_vt, bf, cos, sin)


def _scores(ch, r0, qm=None):
    r0 = pl.multiple_of(r0, HK)
    t = lax.dot_general(ch["keys"](r0), ch["qm"] if qm is None else qm, _NT,
                        preferred_element_type=F32)
    if ch["kbias"] is not None:
        t = t - ch["kbias"](r0)
    return t


def _flash_t(chains, sa_ref, m_ref, acc_ref, qi, diag_mask, next_qm):
    def update(ci, ch, t, r0, cols=slice(None)):
        m = m_ref[ci, :, cols]
        mt = jnp.max(t, axis=0, keepdims=True)
        qb = ch["qbias"]
        qb = None if qb is None else qb[:, cols]
        m_new = jnp.maximum(m, mt if qb is None else mt + qb)
        alpha = jnp.exp2(m - m_new)
        p = jnp.exp2((t - (m_new if qb is None else m_new - qb)).astype(BF16))
        vt = ch["vt"](pl.multiple_of(r0, HK))
        m_ref[ci, :, cols] = m_new
        acc_ref[ci, :, cols] = (alpha * acc_ref[ci, :, cols]
                                + jnp.dot(vt, p, preferred_element_type=F32))

    def full_tiles(k0, n):
        tb = [_scores(ch, k0 + HK) for ch in chains]
        for t in range(n):
            kt = k0 + t * TK
            for ci, ch in enumerate(chains):
                update(ci, ch, sa_ref[ci], kt)
            for ci, ch in enumerate(chains):
                sa_ref[ci] = _scores(ch, kt + TK)
            tb_next = [_scores(ch, kt + TK + HK) for ch in chains] if t + 1 < n else None
            for ci, ch in enumerate(chains):
                update(ci, ch, tb[ci], kt + HK)
            tb = tb_next

    def tile_pair(kj, carry):
        full_tiles(kj * (2 * TK), 2)
        return carry

    m_ref[...] = jnp.full(m_ref.shape, NEG_INF, F32)
    acc_ref[...] = jnp.zeros(acc_ref.shape, F32)
    lax.fori_loop(0, qi // 2, tile_pair, 0)

    @pl.when(qi % 2 == 1)
    def _():
        full_tiles((qi - 1) * TK, 1)
    k0 = qi * TK
    hq = TQ - HK
    tb = [_scores(ch, k0 + HK, ch["qm"][hq:]) for ch in chains]
    nxt = [_scores(ch, 0, qm) for ch, qm in zip(chains, next_qm)]
    for ci, ch in enumerate(chains):
        update(ci, ch, jnp.where(diag_mask[0:HK], sa_ref[ci], NEG_INF), k0)
    for ci, ch in enumerate(chains):
        update(ci, ch, jnp.where(diag_mask[HK:TK, hq:], tb[ci], NEG_INF), k0 + HK,
               slice(hq, TQ))
    for ci in range(len(chains)):
        sa_ref[ci] = nxt[ci]
    return tuple(acc_ref[ci] for ci in range(len(chains)))


def _next_tile_start(q0, seq_len):
    return pl.multiple_of(jnp.minimum(q0 + TQ, seq_len - TQ), TQ)


def _for_each_query_tile(n_tiles, tile_fn):
    def body(qi, carry):
        tile_fn(qi, pl.multiple_of(qi * TQ, TQ))
        return carry

    lax.fori_loop(0, n_tiles, body, 0)


def _lane_group(ref, rows, g):
    return ref[rows, g * LANES:(g + 1) * LANES]


def _masked_queries(q_ref, q0):
    lane = lax.broadcasted_iota(jnp.int32, (TQ, LANES), 1)
    out = []
    for g in range(ATT_GROUPS):
        q2 = _lane_group(q_ref, pl.ds(q0, TQ), g)
        out += [jnp.where((lane >= j * HEAD_DIM) & (lane < (j + 1) * HEAD_DIM), q2,
                          jnp.zeros_like(q2)) for j in range(2)]
    return out


def _fox_kernel(q_ref, k_ref, vt_ref, frow_ref, o_ref, sa_ref, m_ref, acc_ref, fcol_ref):
    for g in range(ATT_GROUPS):
        for j in range(frow_ref.shape[2] // LANES):
            fcol_ref[g, j * LANES:(j + 1) * LANES, :] = _rows_to_columns(
                frow_ref[g, :, j * LANES:(j + 1) * LANES])

    vrow = lax.broadcasted_iota(jnp.int32, (LANES, HK), 0)
    r = lax.broadcasted_iota(jnp.int32, (TK, TQ), 0)
    c = lax.broadcasted_iota(jnp.int32, (TK, TQ), 1)

    def make_chains(q0):
        chains = []
        for ci, qm in enumerate(_masked_queries(q_ref, q0)):
            g, j = divmod(ci, 2)
            v_rows = (vrow >= j * HEAD_DIM) & (vrow < (j + 1) * HEAD_DIM)

            def vt(k0, g=g, v_rows=v_rows):
                v = vt_ref[g * LANES:(g + 1) * LANES, pl.ds(k0, HK)]
                return jnp.where(v_rows, v, jnp.ones_like(v))

            chains.append(dict(
                qm=qm, keys=lambda r0, g=g: _lane_group(k_ref, pl.ds(r0, HK), g),
                kbias=lambda r0, g=g, j=j: fcol_ref[g, pl.ds(r0, HK), j:j + 1],
                qbias=frow_ref[g, j:j + 1, pl.ds(q0, TQ)], vt=vt, rows=LANES))
        return chains

    def tile(qi, q0):
        res = _flash_t(make_chains(q0), sa_ref, m_ref, acc_ref, qi, r <= c,
                       _masked_queries(q_ref, _next_tile_start(q0, q_ref.shape[0])))
        h = HEAD_DIM
        for g in range(ATT_GROUPS):
            a0, a1 = res[2 * g], res[2 * g + 1]
            o_t = jnp.concatenate([a0[0:h] / a0[h:2 * h], a1[h:2 * h] / a1[0:h]], axis=0)
            o_ref[pl.ds(q0, TQ), g * LANES:(g + 1) * LANES] = o_t.T.astype(o_ref.dtype)

    for ci, ch in enumerate(make_chains(0)):
        sa_ref[ci] = _scores(ch, 0)
    _for_each_query_tile(q_ref.shape[0] // TQ, tile)


def _fox_attention(fq, fk, fvt, frow):
    B, S, _ = fq.shape
    gw = ATT_GROUPS * LANES
    seq_spec = pl.BlockSpec((None, S, gw), lambda b, p: (b, 0, p))
    return pl.pallas_call(
        _fox_kernel,
        grid=(B, FOX_HEADS // 2 // ATT_GROUPS),
        in_specs=[seq_spec, seq_spec,
                  pl.BlockSpec((None, gw, S), lambda b, p: (b, p, 0)),
                  pl.BlockSpec((None, ATT_GROUPS, 2, S), lambda b, p: (b, p, 0, 0))],
        out_specs=seq_spec,
        out_shape=jax.ShapeDtypeStruct((B, S, SEC), BF16),
        scratch_shapes=[pltpu.VMEM((2 * ATT_GROUPS, HK, TQ), F32),
                        pltpu.VMEM((2 * ATT_GROUPS, 1, TQ), F32),
                        pltpu.VMEM((2 * ATT_GROUPS, LANES, TQ), F32),
                        pltpu.VMEM((ATT_GROUPS, S, LANES), F32)],
        compiler_params=_cparams(("arbitrary", "arbitrary")),
        name="fox_attention",
    )(fq, fk, fvt, frow)


def _diff_kernel(q_ref, k_ref, vt_ref, lam_ref, g_ref, o_ref, sa_ref, m_ref, acc_ref):
    r = lax.broadcasted_iota(jnp.int32, (TK, TQ), 0)
    c = lax.broadcasted_iota(jnp.int32, (TK, TQ), 1)
    lp = lam_ref[...]
    lam = (jnp.exp(jnp.sum(lp[0:1] * lp[1:2], axis=1, keepdims=True))
           - jnp.exp(jnp.sum(lp[2:3] * lp[3:4], axis=1, keepdims=True)) + LAMBDA_INIT)
    gain_col = _rows_to_columns(g_ref[...] * (1.0 - LAMBDA_INIT))[:, 0:1]

    def make_chains(q0):
        return [dict(qm=qm, keys=lambda r0, g=ci // 2: _lane_group(k_ref, pl.ds(r0, HK), g),
                     kbias=None, qbias=None,
                     vt=lambda k0, g=ci // 2: vt_ref[g, :, pl.ds(k0, HK)], rows=DV_ROWS)
                for ci, qm in enumerate(_masked_queries(q_ref, q0))]

    def tile(qi, q0):
        res = _flash_t(make_chains(q0), sa_ref, m_ref, acc_ref, qi,
                       (r // CHUNK) <= (c // CHUNK),
                       _masked_queries(q_ref, _next_tile_start(q0, q_ref.shape[0])))
        v = LANES
        for g in range(ATT_GROUPS):
            a0, a1 = res[2 * g], res[2 * g + 1]
            o_t = a0[0:v] / a0[v:v + 1] - lam * (a1[0:v] / a1[v:v + 1])
            inv = lax.rsqrt(jnp.mean(o_t * o_t, axis=0, keepdims=True) + EPS)
            o_t = o_t * inv * gain_col
            o_ref[pl.ds(q0, TQ), g * LANES:(g + 1) * LANES] = o_t.T.astype(o_ref.dtype)

    for ci, ch in enumerate(make_chains(0)):
        sa_ref[ci] = _scores(ch, 0)
    _for_each_query_tile(q_ref.shape[0] // TQ, tile)


def _diff_attention(dq, dk, dvt, lam_params, g_subln):
    B, S, _ = dq.shape
    seq_spec = pl.BlockSpec((None, S, ATT_GROUPS * LANES), lambda b, h: (b, 0, h))
    return pl.pallas_call(
        _diff_kernel,
        grid=(B, DIFF_HEADS // ATT_GROUPS),
        in_specs=[seq_spec, seq_spec,
                  pl.BlockSpec((None, ATT_GROUPS, DV_ROWS, S), lambda b, h: (b, h, 0, 0)),
                  pl.BlockSpec((4, HEAD_DIM), lambda b, h: (0, 0)),
                  pl.BlockSpec((1, LANES), lambda b, h: (0, 0))],
        out_specs=seq_spec,
        out_shape=jax.ShapeDtypeStruct((B, S, SEC), BF16),
        scratch_shapes=[pltpu.VMEM((2 * ATT_GROUPS, HK, TQ), F32),
                        pltpu.VMEM((2 * ATT_GROUPS, 1, TQ), F32),
                        pltpu.VMEM((2 * ATT_GROUPS, DV_ROWS, TQ), F32)],
        compiler_params=_cparams(("arbitrary", "arbitrary")),
        name="diff_attention",
    )(dq, dk, dvt, lam_params, g_subln)


def _route_kernel(d_ref, f_ref, wd_ref, wf_ref, x_ref, mod_ref, gpa_ref, gpf_ref, wr_ref, br_ref,
                  x1_ref, h2_ref, ri_ref, rw_ref, cnt_ref):
    first = (pl.program_id(0) == 0) & (pl.program_id(1) == 0)

    @pl.when(first)
    def _():
        cnt_ref[...] = jnp.zeros_like(cnt_ref)

    tm = x_ref.shape[0]
    rows = tm // ROUTE_ROW_GROUPS
    mixed = [jnp.dot(d_ref[r0:r0 + rows, :], wd_ref[...], preferred_element_type=F32)
             + jnp.dot(f_ref[r0:r0 + rows, :], wf_ref[...], preferred_element_type=F32)
             for r0 in range(0, tm, rows)]
    wr = wr_ref[...].astype(BF16)
    gate_gain = mod_ref[2:3, :] * gpa_ref[...]
    ffn_gain = gpf_ref[...] * (1.0 + mod_ref[4:5, :])
    lts = []
    for gi, r0 in enumerate(range(0, tm, rows)):
        x1 = x_ref[r0:r0 + rows, :] + _rms(mixed[gi], gate_gain)
        x1_ref[r0:r0 + rows, :] = x1
        h2 = _rms(x1, ffn_gain) + mod_ref[3:4, :]
        h2_ref[r0:r0 + rows, :] = _pack_halves(h2)
        lts.append(br_ref[:, 0:1] + lax.dot_general(wr, h2.astype(BF16), _NT,
                                                    preferred_element_type=F32))
    lt = jnp.concatenate(lts, axis=1)

    g = lt[0:N_GROUPS]
    row4 = lax.broadcasted_iota(jnp.int32, g.shape, 0)
    gmax = jnp.max(g, axis=0, keepdims=True)
    gidx = jnp.min(jnp.where(g == gmax, row4, N_GROUPS), axis=0, keepdims=True)
    gw = 1.0 / jnp.sum(jnp.exp(g - gmax), axis=0, keepdims=True)
    esel = jnp.zeros((EXPERTS_PER_GROUP, tm), F32)
    for gg in range(N_GROUPS):
        e0 = SUBLANES + EXPERTS_PER_GROUP * gg
        esel = jnp.where(gidx == gg, lt[e0:e0 + EXPERTS_PER_GROUP], esel)
    row8 = lax.broadcasted_iota(jnp.int32, esel.shape, 0)
    v1 = jnp.max(esel, axis=0, keepdims=True)
    i1 = jnp.min(jnp.where(esel == v1, row8, EXPERTS_PER_GROUP), axis=0, keepdims=True)
    esel2 = jnp.where(row8 == i1, -jnp.inf, esel)
    v2 = jnp.max(esel2, axis=0, keepdims=True)
    i2 = jnp.min(jnp.where(esel2 == v2, row8, EXPERTS_PER_GROUP), axis=0, keepdims=True)
    e = jnp.exp(v2 - v1)
    w1 = gw / (1.0 + e)
    w2 = gw * e / (1.0 + e)
    e1 = gidx * EXPERTS_PER_GROUP + i1
    e2 = gidx * EXPERTS_PER_GROUP + i2

    row32 = lax.broadcasted_iota(jnp.int32, (N_EXPERTS, tm), 0)
    oh1 = row32 == e1
    oh2 = row32 == e2
    oh = jnp.where(oh1 | oh2, 1.0, 0.0)
    t_r = lax.broadcasted_iota(jnp.int32, (tm, tm), 0)
    t_c = lax.broadcasted_iota(jnp.int32, (tm, tm), 1)
    tri = jnp.where(t_r < t_c, 1.0, 0.0).astype(BF16)
    tot = jnp.dot(oh.astype(BF16), tri, preferred_element_type=F32) + cnt_ref[:, 0:1]
    r1 = jnp.sum(jnp.where(oh1, tot, 0.0), axis=0, keepdims=True)
    r2 = jnp.sum(jnp.where(oh2, tot, 0.0), axis=0, keepdims=True)
    cnt_ref[...] = cnt_ref[...] + jnp.sum(oh, axis=1, keepdims=True)

    zi = jnp.zeros((SUBLANES - 4, tm), jnp.int32)
    ri_ref[...] = jnp.concatenate([e1, e2, r1.astype(jnp.int32), r2.astype(jnp.int32), zi], axis=0)
    rw_ref[...] = jnp.concatenate([w1, w2, jnp.zeros((SUBLANES - 2, tm), F32)], axis=0)


def _route(d_out, f_out, w_out_d, w_out_f, x, mod, g_post_attn, g_pre_ffn, w_rt, b_rt):
    B, S, D = x.shape
    tm = TM_ROUTE
    nrt = w_rt.shape[0]
    n_s = S // tm
    T = B * S
    return pl.pallas_call(
        _route_kernel,
        grid=(B, n_s),
        in_specs=[pl.BlockSpec((None, tm, SEC), lambda b, s: (b, s, 0)),
                  pl.BlockSpec((None, tm, SEC), lambda b, s: (b, s, 0)),
                  pl.BlockSpec((SEC, D), lambda b, s: (0, 0)),
                  pl.BlockSpec((SEC, D), lambda b, s: (0, 0)),
                  pl.BlockSpec((None, tm, D), lambda b, s: (b, s, 0)),
                  pl.BlockSpec((None, 6, D), lambda b, s: (b, 0, 0)),
                  pl.BlockSpec((1, D), lambda b, s: (0, 0)),
                  pl.BlockSpec((1, D), lambda b, s: (0, 0)),
                  pl.BlockSpec((nrt, D), lambda b, s: (0, 0)),
                  pl.BlockSpec((nrt, LANES), lambda b, s: (0, 0))],
        out_specs=[pl.BlockSpec((None, tm, D), lambda b, s: (b, s, 0)),
                   pl.BlockSpec((None, tm, D // 2), lambda b, s: (b, s, 0)),
                   pl.BlockSpec((SUBLANES, tm), lambda b, s: (0, b * n_s + s)),
                   pl.BlockSpec((SUBLANES, tm), lambda b, s: (0, b * n_s + s)),
                   pl.BlockSpec((N_EXPERTS, LANES), lambda b, s: (0, 0))],
        out_shape=[jax.ShapeDtypeStruct((B, S, D), F32),
                   jax.ShapeDtypeStruct((B, S, D // 2), jnp.uint32),
                   jax.ShapeDtypeStruct((SUBLANES, T), jnp.int32),
                   jax.ShapeDtypeStruct((SUBLANES, T), F32),
                   jax.ShapeDtypeStruct((N_EXPERTS, LANES), F32)],
        compiler_params=_cparams(("arbitrary", "arbitrary")),
        name="out_proj_route",
    )(d_out, f_out, w_out_d, w_out_f, x, mod, g_post_attn, g_pre_ffn, w_rt, b_rt)


def _sc_mesh():
    return plsc.VectorSubcoreMesh(core_axis_name="c", subcore_axis_name="s",
                                  num_cores=SC_CORES, num_subcores=SC_SUBCORES)


def _sc_worker_base(per_worker):
    return (lax.axis_index("s") * SC_CORES + lax.axis_index("c")) * per_worker


def _dispatch(dest, h2, n_rows):
    T, D = h2.shape
    per_worker = T // SC_WORKERS

    @functools.partial(
        pl.kernel, mesh=_sc_mesh(),
        out_type=jax.ShapeDtypeStruct((n_rows, D), h2.dtype),
        scratch_types=[pltpu.VMEM((SC_WIN,), jnp.int32), pltpu.VMEM((SC_WIN, D), h2.dtype)],
        name="moe_dispatch")
    def run(h_hbm, dest_hbm, xs_hbm, idx_v, rows_v):
        base = _sc_worker_base(per_worker)

        @pl.loop(0, per_worker // SC_WIN)
        def _(ci):
            off = pl.multiple_of(base + ci * SC_WIN, SC_WIN)
            pltpu.sync_copy(h_hbm.at[pl.ds(off, SC_WIN)], rows_v)
            for k in range(2):
                pltpu.sync_copy(dest_hbm.at[pl.ds(k * T + off, SC_WIN)], idx_v)
                pltpu.sync_copy(rows_v, xs_hbm.at[idx_v])

    return run(h2, dest.reshape(2 * T))


def _gather_rows(ys, dest):
    _, D = ys.shape
    K, T = dest.shape
    n = K * T
    per_worker = n // SC_WORKERS

    @functools.partial(
        pl.kernel, mesh=_sc_mesh(),
        out_type=jax.ShapeDtypeStruct((n, D), ys.dtype),
        scratch_types=[pltpu.VMEM((SC_WIN,), jnp.int32), pltpu.VMEM((SC_WIN, D), ys.dtype)],
        name="moe_gather")
    def run(ys_hbm, dest_hbm, g_hbm, idx_v, rows_v):
        base = _sc_worker_base(per_worker)

        @pl.loop(0, per_worker // SC_WIN)
        def _(ci):
            off = pl.multiple_of(base + ci * SC_WIN, SC_WIN)
            pltpu.sync_copy(dest_hbm.at[pl.ds(off, SC_WIN)], idx_v)
            pltpu.sync_copy(ys_hbm.at[idx_v], rows_v)
            pltpu.sync_copy(rows_v, g_hbm.at[pl.ds(off, SC_WIN)])

    return run(ys, dest.reshape(n)).reshape(K, T, D)


def _expert_kernel(be_ref, nu_ref, nv_ref, xs_ref, wg_ref, wu_ref, wd_ref, ys_ref,
                   wgb, wub, wdb):
    i = pl.program_id(0)

    @pl.when(i < nu_ref[0])
    def _():
        prev = be_ref[jnp.maximum(i - 1, 0)]

        @pl.when((i == 0) | (be_ref[i] != prev))
        def _():
            wgb[...] = wg_ref[...].astype(BF16)
            wub[...] = wu_ref[...].astype(BF16)
            wdb[...] = wd_ref[...].astype(BF16)

        half = xs_ref.shape[1]
        rows = xs_ref.shape[0] // EXPERT_ROW_GROUPS
        row = lax.broadcasted_iota(jnp.int32, (rows, half), 0)
        au = []
        for r0 in range(0, xs_ref.shape[0], rows):
            words = jnp.where(row + r0 < nv_ref[i], xs_ref[r0:r0 + rows, :], jnp.uint32(0))
            lo, hi = (v.astype(BF16) for v in _unpack_halves(words))

            def proj(w, lo=lo, hi=hi):
                return (jnp.dot(lo, w[0:half, :], preferred_element_type=F32)
                        + jnp.dot(hi, w[half:2 * half, :], preferred_element_type=F32))

            au.append((proj(wgb), proj(wub)))
        for gi, (a, u) in enumerate(au):
            hmid = (a * jax.nn.sigmoid(a) * u).astype(BF16)
            y = jnp.dot(hmid, wdb[...], preferred_element_type=F32)
            ys_ref[gi * rows:(gi + 1) * rows, :] = _pack_halves(y)

    @pl.when(i >= nu_ref[0])
    def _():
        ys_ref[...] = jnp.zeros_like(ys_ref)


def _experts(block_e, n_used, n_valid, xs, w_gate, w_up, w_down):
    P, Dh = xs.shape
    nb = P // BM
    _, D, Fd = w_gate.shape

    def row_map(i, be, nu, nv):
        return (jnp.minimum(i, nu[0] - 1), 0)

    def w_map(i, be, nu, nv):
        return (be[i], 0, 0)

    return pl.pallas_call(
        _expert_kernel,
        grid_spec=pltpu.PrefetchScalarGridSpec(
            num_scalar_prefetch=3,
            grid=(nb,),
            in_specs=[pl.BlockSpec((BM, Dh), row_map),
                      pl.BlockSpec((None, D, Fd), w_map),
                      pl.BlockSpec((None, D, Fd), w_map),
                      pl.BlockSpec((None, Fd, D), w_map)],
            out_specs=pl.BlockSpec((BM, Dh), lambda i, be, nu, nv: (i, 0)),
            scratch_shapes=[pltpu.VMEM((D, Fd), BF16), pltpu.VMEM((D, Fd), BF16),
                            pltpu.VMEM((Fd, D), BF16)]),
        out_shape=jax.ShapeDtypeStruct((P, Dh), jnp.uint32),
        compiler_params=_cparams(("arbitrary",)),
        name="moe_experts",
    )(block_e, n_used, n_valid, xs, w_gate, w_up, w_down)


def _combine_kernel(g_ref, w_ref, x1_ref, mod_ref, g_post_ref, *rest):
    o_ref = rest[-1]
    lo0, hi0 = _unpack_halves(g_ref[0])
    lo1, hi1 = _unpack_halves(g_ref[1])
    tm = lo0.shape[0]
    wt = jnp.concatenate([_rows_to_columns(w_ref[:, j * LANES:(j + 1) * LANES])
                          for j in range(tm // LANES)], axis=0)
    w0, w1 = wt[:, 0:1], wt[:, 1:2]
    y = jnp.concatenate([w0 * lo0 + w1 * lo1, w0 * hi0 + w1 * hi1], axis=1)
    o_ref[...] = x1_ref[...] + _rms(y, mod_ref[5:6, :] * g_post_ref[...])


def _combine(g, wts, x1, mod, g_post_ffn, S, chunk, prev_out):
    T, D = x1.shape
    tm = TM_ROWS
    per_b = S // tm
    steps = g.shape[1] // tm
    off = chunk * steps
    in_specs = [pl.BlockSpec((2, tm, D // 2), lambda i: (0, i, 0)),
                pl.BlockSpec((SUBLANES, tm), lambda i: (0, i + off)),
                pl.BlockSpec((tm, D), lambda i: (i + off, 0)),
                pl.BlockSpec((None, 6, D), lambda i: ((i + off) // per_b, 0, 0)),
                pl.BlockSpec((1, D), lambda i: (0, 0))]
    args = [g, wts, x1, mod, g_post_ffn]
    aliases = {}
    if prev_out is not None:
        in_specs.append(pl.BlockSpec(memory_space=pl.ANY))
        args.append(prev_out)
        aliases = {len(args) - 1: 0}
    return pl.pallas_call(
        _combine_kernel,
        grid=(steps,),
        in_specs=in_specs,
        out_specs=pl.BlockSpec((tm, D), lambda i: (i + off, 0)),
        out_shape=jax.ShapeDtypeStruct((T, D), F32),
        input_output_aliases=aliases,
        compiler_params=_cparams(("arbitrary",)),
        name="moe_combine",
    )(*args)


def _rope_tables(S):
    inv = 1.0 / (10000.0 ** (jnp.arange(0, HEAD_DIM, 2, dtype=F32) / HEAD_DIM))
    ang = jnp.arange(S, dtype=F32)[:, None] * inv[None, :]
    c, s = jnp.cos(ang), jnp.sin(ang)
    return jnp.tile(c, (1, 4)), jnp.tile(jnp.concatenate([-s, s], axis=1), (1, 2))


def kernel(x, c, w_ada, b_ada, g_pre_attn, g_post_attn, w_in, b_forget, lambda_q1, lambda_k1,
           lambda_q2, lambda_k2, g_diff_subln, w_out, g_pre_ffn, g_post_ffn, w_group, b_group,
           w_router, b_router, w_gate, w_up, w_down):
    B, S, D = x.shape
    T = B * S
    l = 0
    mod = _ada(c, w_ada[l], b_ada[l]).reshape(B, 6, D)

    wi = w_in[l]
    w_main = jnp.concatenate([wi[:, 0:2 * SEC], wi[:, 3 * SEC:5 * SEC]], axis=1).astype(BF16)
    w_vt = jnp.concatenate([wi[:, 2 * SEC:3 * SEC], wi[:, 5 * SEC:6 * SEC], wi[:, 6 * SEC:],
                            jnp.zeros((D, BF16_SUBLANES - FOX_HEADS), F32)], axis=1).T.astype(BF16)
    bf = jnp.broadcast_to(b_forget[l][:, None], (FOX_HEADS, LANES))
    cos, sin = _rope_tables(S)
    dq, dk, fq, fk, dvt, fvt, fcum = _inproj(x, mod, g_pre_attn[l][None, :], w_main, w_vt,
                                             bf, cos, sin)

    lam_params = jnp.stack([lambda_q1[l], lambda_k1[l], lambda_q2[l], lambda_k2[l]])
    d_out = _diff_attention(dq, dk, dvt, lam_params, g_diff_subln[l][None, :])
    frow = fcum.reshape(B, FOX_HEADS // 2, 2, S)
    f_out = _fox_attention(fq, fk, fvt, frow)

    w_o = w_out[l].astype(BF16)
    zpad = jnp.zeros((SUBLANES - N_GROUPS, D), F32)
    w_rt = jnp.concatenate([w_group[l].T, zpad, w_router[l].T], axis=0)
    b_rt = jnp.concatenate([b_group[l], jnp.zeros((SUBLANES - N_GROUPS,), F32), b_router[l]])
    b_rt = jnp.broadcast_to(b_rt[:, None], (SUBLANES + N_EXPERTS, LANES))
    x1, h2, ri, rw, cnt = _route(d_out, f_out, w_o[:SEC], w_o[SEC:], x, mod,
                                 g_post_attn[l][None, :], g_pre_ffn[l][None, :], w_rt, b_rt)

    counts = cnt[:, 0].astype(jnp.int32)
    padded = ((counts + BM - 1) // BM) * BM
    pend = jnp.cumsum(padded)
    pstart = pend - padded
    eids = jnp.arange(N_EXPERTS, dtype=jnp.int32)
    dest = ri[2:4] + jnp.sum(jnp.where(ri[0:2, :, None] == eids, pstart, 0), axis=-1)
    nb = (2 * T) // BM + N_EXPERTS
    n_used = (pend[-1] // BM).astype(jnp.int32).reshape(1)
    blk0 = jnp.arange(nb, dtype=jnp.int32) * BM
    block_e = jnp.minimum(jnp.sum((blk0[:, None] >= pend[None, :]).astype(jnp.int32), axis=1),
                          N_EXPERTS - 1)
    cend = pstart + counts
    blk_end = jnp.sum(jnp.where(block_e[:, None] == eids, cend, 0), axis=-1)
    n_valid = jnp.clip(blk_end - blk0, 0, BM).astype(jnp.int32)

    xs = _dispatch(dest, h2.reshape(T, D // 2), nb * BM)
    ys = _experts(block_e, n_used, n_valid, xs, w_gate[l], w_up[l], w_down[l])
    out = None
    tc = T // COMBINE_CHUNKS
    for ck in range(COMBINE_CHUNKS):
        g = _gather_rows(ys, dest[:, ck * tc:(ck + 1) * tc])
        out = _combine(g, rw, x1.reshape(T, D), mod, g_post_ffn[l][None, :], S, ck, out)
    return out.reshape(B, S, D)
```

```python
import functools
import math

import jax
import jax.numpy as jnp
from jax import lax
from jax.experimental import pallas as pl
from jax.experimental.pallas import tpu as pltpu
from jax.experimental.pallas import tpu_sc as plsc

F32 = jnp.float32
BF16 = jnp.bfloat16

D_MODEL = 1024
HEAD_DIM = 64
CHUNK = 64
DIFF_HEADS = 4
FOX_HEADS = 8
SEC = 512
N_GROUPS = 4
EXPERTS_PER_GROUP = 8
N_EXPERTS = 32
D_EXPERT = 512
EPS = 1e-6
NEG_INF = -1e30
LOG2E = 1.4426950408889634
LAMBDA_INIT = 0.8 - 0.6 * math.exp(-0.3 * 0)

LANES = 128
SUBLANES = 8
BF16_SUBLANES = 16
TN_ADA = 1024
TM_PROJ = 1024
TQ = 512
TK = 512
HK = TK // 2
ATT_GROUPS = 1
DV_ROWS = 2 * HEAD_DIM + BF16_SUBLANES
TM_ROUTE = 512
ROUTE_ROW_GROUPS = 2
BM = 512
EXPERT_ROW_GROUPS = 2
TM_ROWS = 512
COMBINE_CHUNKS = 4
SC_CORES = 2
SC_SUBCORES = 16
SC_WORKERS = SC_CORES * SC_SUBCORES
SC_WIN = 128
VMEM_LIMIT = 48 * 1024 * 1024
_NT = (((1,), (1,)), ((), ()))


def _cparams(sem, flags=None):
    return pltpu.CompilerParams(dimension_semantics=sem, vmem_limit_bytes=VMEM_LIMIT, flags=flags)


def _rms(x, g):
    return x * lax.rsqrt(jnp.mean(x * x, axis=-1, keepdims=True) + EPS) * g


def _pack_halves(x):
    m = x.shape[1] // 2
    bits = lax.bitcast_convert_type(x.astype(BF16).astype(F32), jnp.uint32)
    return (bits[:, :m] >> 16) | (bits[:, m:] & jnp.uint32(0xFFFF0000))


def _unpack_halves(w):
    lo = lax.bitcast_convert_type(w << 16, F32)
    hi = lax.bitcast_convert_type(w & jnp.uint32(0xFFFF0000), F32)
    return lo, hi


def _rows_to_columns(x):
    pad = jnp.zeros((LANES - x.shape[0], LANES), x.dtype)
    return jnp.concatenate([x, pad], axis=0).T


def _ada_kernel(c_ref, w_ref, b_ref, o_ref):
    c = c_ref[...]
    cs = c * jax.nn.sigmoid(c)
    o_ref[...] = jnp.dot(cs, w_ref[...], preferred_element_type=F32,
                         precision=lax.Precision.HIGHEST) + b_ref[...]


def _ada(c, w_ada, b_ada):
    B, D = c.shape
    N = w_ada.shape[1]
    tn = TN_ADA
    return pl.pallas_call(
        _ada_kernel,
        grid=(N // tn,),
        in_specs=[pl.BlockSpec((B, D), lambda j: (0, 0)),
                  pl.BlockSpec((D, tn), lambda j: (0, j)),
                  pl.BlockSpec((1, tn), lambda j: (0, j))],
        out_specs=pl.BlockSpec((B, tn), lambda j: (0, j)),
        out_shape=jax.ShapeDtypeStruct((B, N), F32),
        compiler_params=_cparams(("arbitrary",)),
        name="ada_mod",
    )(c, w_ada, b_ada.reshape(1, N))


def _split3(x):
    hi = x.astype(BF16)
    r1 = x - hi.astype(F32)
    mid = r1.astype(BF16)
    lo = (r1 - mid.astype(F32)).astype(BF16)
    return hi, mid, lo


def _inproj_kernel(x_ref, mod_ref, g_ref, w_ref, wvt_ref, bf_ref, cos_ref, sin_ref,
                   dq_ref, dk_ref, fq_ref, fk_ref, dvt_ref, fvt_ref, fcum_ref, carry_ref):
    si = pl.program_id(1)
    x = x_ref[...]
    h = _rms(x, g_ref[...] * (1.0 + mod_ref[1:2, :])) + mod_ref[0:1, :]
    hb = h.astype(BF16)
    tm = x.shape[0]

    cos = cos_ref[...]
    sin = sin_ref[...]
    lane = lax.broadcasted_iota(jnp.int32, (tm, LANES), 1)
    first_half = (lane % HEAD_DIM) < (HEAD_DIM // 2)
    qscale = HEAD_DIM ** -0.5 * LOG2E

    def proj(sec):
        return jnp.dot(hb, w_ref[:, sec * SEC:(sec + 1) * SEC], preferred_element_type=F32)

    def rope_store(p, out_ref, scale):
        for j in range(SEC // LANES):
            ch = p[:, j * LANES:(j + 1) * LANES]
            partner = jnp.where(first_half,
                                pltpu.roll(ch, LANES - HEAD_DIM // 2, 1),
                                pltpu.roll(ch, HEAD_DIM // 2, 1))
            r = ch * cos + partner * sin
            if scale != 1.0:
                r = r * scale
            out_ref[:, j * LANES:(j + 1) * LANES] = r.astype(out_ref.dtype)

    rope_store(proj(0), dq_ref, qscale)
    rope_store(proj(1), dk_ref, 1.0)
    fq_ref[...] = (proj(2) * qscale).astype(fq_ref.dtype)
    fk_ref[...] = proj(3).astype(fk_ref.dtype)
    vt = lax.dot_general(wvt_ref[...], hb, _NT, preferred_element_type=F32)
    for hd in range(DIFF_HEADS):
        dvt_ref[hd, 0:LANES, :] = vt[hd * LANES:(hd + 1) * LANES].astype(dvt_ref.dtype)
        dvt_ref[hd, LANES:DV_ROWS, :] = jnp.ones((DV_ROWS - LANES, tm), dvt_ref.dtype)
    fvt_ref[...] = vt[SEC:2 * SEC].astype(fvt_ref.dtype)

    z = vt[2 * SEC:2 * SEC + FOX_HEADS] + bf_ref[:, 0:1]
    lf = (jnp.minimum(z, 0.0) - jnp.log1p(jnp.exp(-jnp.abs(z)))) * LOG2E

    @pl.when(si == 0)
    def _():
        carry_ref[...] = jnp.zeros_like(carry_ref)

    r_i = lax.broadcasted_iota(jnp.int32, (LANES, LANES), 0)
    c_i = lax.broadcasted_iota(jnp.int32, (LANES, LANES), 1)
    tri = jnp.where(r_i <= c_i, 1.0, 0.0).astype(BF16)
    run = carry_ref[...]
    for j in range(tm // LANES):
        hi, mid, lo = _split3(lf[:, j * LANES:(j + 1) * LANES])
        cs = (jnp.dot(hi, tri, preferred_element_type=F32)
              + jnp.dot(mid, tri, preferred_element_type=F32)
              + jnp.dot(lo, tri, preferred_element_type=F32)) + run
        fcum_ref[:, j * LANES:(j + 1) * LANES] = cs
        run = jnp.broadcast_to(cs[:, LANES - 1:LANES], run.shape)
    carry_ref[...] = run


def _inproj(x, mod, g_pre, w_main, w_vt, bf, cos, sin):
    B, S, D = x.shape
    tm = TM_PROJ
    sec_out = jax.ShapeDtypeStruct((B, S, SEC), BF16)
    sec_spec = pl.BlockSpec((None, tm, SEC), lambda b, s: (b, s, 0))
    t_out = jax.ShapeDtypeStruct((B, SEC, S), BF16)
    t_spec = pl.BlockSpec((None, SEC, tm), lambda b, s: (b, 0, s))
    return pl.pallas_call(
        _inproj_kernel,
        grid=(B, S // tm),
        in_specs=[pl.BlockSpec((None, tm, D), lambda b, s: (b, s, 0)),
                  pl.BlockSpec((None, 6, D), lambda b, s: (b, 0, 0)),
                  pl.BlockSpec((1, D), lambda b, s: (0, 0)),
                  pl.BlockSpec((D, 4 * SEC), lambda b, s: (0, 0)),
                  pl.BlockSpec(w_vt.shape, lambda b, s: (0, 0)),
                  pl.BlockSpec((FOX_HEADS, LANES), lambda b, s: (0, 0)),
                  pl.BlockSpec((tm, LANES), lambda b, s: (s, 0)),
                  pl.BlockSpec((tm, LANES), lambda b, s: (s, 0))],
        out_specs=[sec_spec] * 4
        + [pl.BlockSpec((None, DIFF_HEADS, DV_ROWS, tm), lambda b, s: (b, 0, 0, s)), t_spec,
           pl.BlockSpec((None, FOX_HEADS, tm), lambda b, s: (b, 0, s))],
        out_shape=[sec_out] * 4
        + [jax.ShapeDtypeStruct((B, DIFF_HEADS, DV_ROWS, S), BF16), t_out,
           jax.ShapeDtypeStruct((B, FOX_HEADS, S), F32)],
        scratch_shapes=[pltpu.VMEM((FOX_HEADS, LANES), F32)],
        compiler_params=_cparams(("arbitrary", "arbitrary")),
        name="in_proj",
    )(x, mod, g_pre, w_main, w_vt, bf, cos, sin)


def _scores(ch, r0, qm=None):
    r0 = pl.multiple_of(r0, HK)
    t = lax.dot_general(ch["keys"](r0), ch["qm"] if qm is None else qm, _NT,
                        preferred_element_type=F32)
    if ch["kbias"] is not None:
        t = t - ch["kbias"](r0)
    return t


def _flash_t(chains, sa_ref, m_ref, acc_ref, qi, diag_mask, next_qm):
    def update(ci, ch, t, r0, cols=slice(None)):
        m = m_ref[ci, :, cols]
        mt = jnp.max(t, axis=0, keepdims=True)
        qb = ch["qbias"]
        qb = None if qb is None else qb[:, cols]
        m_new = jnp.maximum(m, mt if qb is None else mt + qb)
        alpha = jnp.exp2(m - m_new)
        p = jnp.exp2((t - (m_new if qb is None else m_new - qb)).astype(BF16))
        vt = ch["vt"](pl.multiple_of(r0, HK))
        m_ref[ci, :, cols] = m_new
        acc_ref[ci, :, cols] = (alpha * acc_ref[ci, :, cols]
                                + jnp.dot(vt, p, preferred_element_type=F32))

    def full_tiles(k0, n):
        tb = [_scores(ch, k0 + HK) for ch in chains]
        for t in range(n):
            kt = k0 + t * TK
            for ci, ch in enumerate(chains):
                update(ci, ch, sa_ref[ci], kt)
            for ci, ch in enumerate(chains):
                sa_ref[ci] = _scores(ch, kt + TK)
            tb_next = [_scores(ch, kt + TK + HK) for ch in chains] if t + 1 < n else None
            for ci, ch in enumerate(chains):
                update(ci, ch, tb[ci], kt + HK)
            tb = tb_next

    def tile_pair(kj, carry):
        full_tiles(kj * (2 * TK), 2)
        return carry

    m_ref[...] = jnp.full(m_ref.shape, NEG_INF, F32)
    acc_ref[...] = jnp.zeros(acc_ref.shape, F32)
    lax.fori_loop(0, qi // 2, tile_pair, 0)

    @pl.when(qi % 2 == 1)
    def _():
        full_tiles((qi - 1) * TK, 1)
    k0 = qi * TK
    hq = TQ - HK
    tb = [_scores(ch, k0 + HK, ch["qm"][hq:]) for ch in chains]
    nxt = [_scores(ch, 0, qm) for ch, qm in zip(chains, next_qm)]
    for ci, ch in enumerate(chains):
        update(ci, ch, jnp.where(diag_mask[0:HK], sa_ref[ci], NEG_INF), k0)
    for ci, ch in enumerate(chains):
        update(ci, ch, jnp.where(diag_mask[HK:TK, hq:], tb[ci], NEG_INF), k0 + HK,
               slice(hq, TQ))
    for ci in range(len(chains)):
        sa_ref[ci] = nxt[ci]
    return tuple(acc_ref[ci] for ci in range(len(chains)))


def _next_tile_start(q0, seq_len):
    return pl.multiple_of(jnp.minimum(q0 + TQ, seq_len - TQ), TQ)


def _for_each_query_tile(n_tiles, tile_fn):
    def body(qi, carry):
        tile_fn(qi, pl.multiple_of(qi * TQ, TQ))
        return carry

    lax.fori_loop(0, n_tiles, body, 0)


def _lane_group(ref, rows, g):
    return ref[rows, g * LANES:(g + 1) * LANES]


def _masked_queries(q_ref, q0):
    lane = lax.broadcasted_iota(jnp.int32, (TQ, LANES), 1)
    out = []
    for g in range(ATT_GROUPS):
        q2 = _lane_group(q_ref, pl.ds(q0, TQ), g)
        out += [jnp.where((lane >= j * HEAD_DIM) & (lane < (j + 1) * HEAD_DIM), q2,
                          jnp.zeros_like(q2)) for j in range(2)]
    return out


def _fox_kernel(q_ref, k_ref, vt_ref, frow_ref, o_ref, sa_ref, m_ref, acc_ref, fcol_ref):
    for g in range(ATT_GROUPS):
        for j in range(frow_ref.shape[2] // LANES):
            fcol_ref[g, j * LANES:(j + 1) * LANES, :] = _rows_to_columns(
                frow_ref[g, :, j * LANES:(j + 1) * LANES])

    vrow = lax.broadcasted_iota(jnp.int32, (LANES, HK), 0)
    r = lax.broadcasted_iota(jnp.int32, (TK, TQ), 0)
    c = lax.broadcasted_iota(jnp.int32, (TK, TQ), 1)

    def make_chains(q0):
        chains = []
        for ci, qm in enumerate(_masked_queries(q_ref, q0)):
            g, j = divmod(ci, 2)
            v_rows = (vrow >= j * HEAD_DIM) & (vrow < (j + 1) * HEAD_DIM)

            def vt(k0, g=g, v_rows=v_rows):
                v = vt_ref[g * LANES:(g + 1) * LANES, pl.ds(k0, HK)]
                return jnp.where(v_rows, v, jnp.ones_like(v))

            chains.append(dict(
                qm=qm, keys=lambda r0, g=g: _lane_group(k_ref, pl.ds(r0, HK), g),
                kbias=lambda r0, g=g, j=j: fcol_ref[g, pl.ds(r0, HK), j:j + 1],
                qbias=frow_ref[g, j:j + 1, pl.ds(q0, TQ)], vt=vt, rows=LANES))
        return chains

    def tile(qi, q0):
        res = _flash_t(make_chains(q0), sa_ref, m_ref, acc_ref, qi, r <= c,
                       _masked_queries(q_ref, _next_tile_start(q0, q_ref.shape[0])))
        h = HEAD_DIM
        for g in range(ATT_GROUPS):
            a0, a1 = res[2 * g], res[2 * g + 1]
            o_t = jnp.concatenate([a0[0:h] * (1.0 / a0[h:h + 1]),
                                   a1[h:2 * h] * (1.0 / a1[0:1])], axis=0)
            o_ref[pl.ds(q0, TQ), g * LANES:(g + 1) * LANES] = o_t.T.astype(o_ref.dtype)

    for ci, ch in enumerate(make_chains(0)):
        sa_ref[ci] = _scores(ch, 0)
    _for_each_query_tile(q_ref.shape[0] // TQ, tile)


def _fox_attention(fq, fk, fvt, frow):
    B, S, _ = fq.shape
    gw = ATT_GROUPS * LANES
    seq_spec = pl.BlockSpec((None, S, gw), lambda b, p: (b, 0, p))
    return pl.pallas_call(
        _fox_kernel,
        grid=(B, FOX_HEADS // 2 // ATT_GROUPS),
        in_specs=[seq_spec, seq_spec,
                  pl.BlockSpec((None, gw, S), lambda b, p: (b, p, 0)),
                  pl.BlockSpec((None, ATT_GROUPS, 2, S), lambda b, p: (b, p, 0, 0))],
        out_specs=seq_spec,
        out_shape=jax.ShapeDtypeStruct((B, S, SEC), BF16),
        scratch_shapes=[pltpu.VMEM((2 * ATT_GROUPS, HK, TQ), F32),
                        pltpu.VMEM((2 * ATT_GROUPS, 1, TQ), F32),
                        pltpu.VMEM((2 * ATT_GROUPS, LANES, TQ), F32),
                        pltpu.VMEM((ATT_GROUPS, S, LANES), F32)],
        compiler_params=_cparams(("arbitrary", "arbitrary")),
        name="fox_attention",
    )(fq, fk, fvt, frow)


def _diff_kernel(q_ref, k_ref, vt_ref, lam_ref, g_ref, o_ref, sa_ref, m_ref, acc_ref):
    r = lax.broadcasted_iota(jnp.int32, (TK, TQ), 0)
    c = lax.broadcasted_iota(jnp.int32, (TK, TQ), 1)
    lp = lam_ref[...]
    lam = (jnp.exp(jnp.sum(lp[0:1] * lp[1:2], axis=1, keepdims=True))
           - jnp.exp(jnp.sum(lp[2:3] * lp[3:4], axis=1, keepdims=True)) + LAMBDA_INIT)
    gain_col = _rows_to_columns(g_ref[...] * (1.0 - LAMBDA_INIT))[:, 0:1]

    def make_chains(q0):
        return [dict(qm=qm, keys=lambda r0, g=ci // 2: _lane_group(k_ref, pl.ds(r0, HK), g),
                     kbias=None, qbias=None,
                     vt=lambda k0, g=ci // 2: vt_ref[g, :, pl.ds(k0, HK)], rows=DV_ROWS)
                for ci, qm in enumerate(_masked_queries(q_ref, q0))]

    def tile(qi, q0):
        res = _flash_t(make_chains(q0), sa_ref, m_ref, acc_ref, qi,
                       (r // CHUNK) <= (c // CHUNK),
                       _masked_queries(q_ref, _next_tile_start(q0, q_ref.shape[0])))
        v = LANES
        for g in range(ATT_GROUPS):
            a0, a1 = res[2 * g], res[2 * g + 1]
            o_t = (a0[0:v] * (1.0 / a0[v:v + 1])
                   - a1[0:v] * (lam / a1[v:v + 1]))
            inv = lax.rsqrt(jnp.mean(o_t * o_t, axis=0, keepdims=True) + EPS)
            o_t = o_t * inv * gain_col
            o_ref[pl.ds(q0, TQ), g * LANES:(g + 1) * LANES] = o_t.T.astype(o_ref.dtype)

    for ci, ch in enumerate(make_chains(0)):
        sa_ref[ci] = _scores(ch, 0)
    _for_each_query_tile(q_ref.shape[0] // TQ, tile)


def _diff_attention(dq, dk, dvt, lam_params, g_subln):
    B, S, _ = dq.shape
    seq_spec = pl.BlockSpec((None, S, ATT_GROUPS * LANES), lambda b, h: (b, 0, h))
    return pl.pallas_call(
        _diff_kernel,
        grid=(B, DIFF_HEADS // ATT_GROUPS),
        in_specs=[seq_spec, seq_spec,
                  pl.BlockSpec((None, ATT_GROUPS, DV_ROWS, S), lambda b, h: (b, h, 0, 0)),
                  pl.BlockSpec((4, HEAD_DIM), lambda b, h: (0, 0)),
                  pl.BlockSpec((1, LANES), lambda b, h: (0, 0))],
        out_specs=seq_spec,
        out_shape=jax.ShapeDtypeStruct((B, S, SEC), BF16),
        scratch_shapes=[pltpu.VMEM((2 * ATT_GROUPS, HK, TQ), F32),
                        pltpu.VMEM((2 * ATT_GROUPS, 1, TQ), F32),
                        pltpu.VMEM((2 * ATT_GROUPS, DV_ROWS, TQ), F32)],
        compiler_params=_cparams(("arbitrary", "arbitrary")),
        name="diff_attention",
    )(dq, dk, dvt, lam_params, g_subln)


def _route_kernel(d_ref, f_ref, wd_ref, wf_ref, x_ref, mod_ref, gpa_ref, gpf_ref, wr_ref, br_ref,
                  x1_ref, h2_ref, ri_ref, rw_ref, cnt_ref):
    first = (pl.program_id(0) == 0) & (pl.program_id(1) == 0)

    @pl.when(first)
    def _():
        cnt_ref[...] = jnp.zeros_like(cnt_ref)

    tm = x_ref.shape[0]
    rows = tm // ROUTE_ROW_GROUPS
    mixed = [jnp.dot(d_ref[r0:r0 + rows, :], wd_ref[...], preferred_element_type=F32)
             + jnp.dot(f_ref[r0:r0 + rows, :], wf_ref[...], preferred_element_type=F32)
             for r0 in range(0, tm, rows)]
    wr = wr_ref[...].astype(BF16)
    gate_gain = mod_ref[2:3, :] * gpa_ref[...]
    ffn_gain = gpf_ref[...] * (1.0 + mod_ref[4:5, :])
    lts = []
    for gi, r0 in enumerate(range(0, tm, rows)):
        x1 = x_ref[r0:r0 + rows, :] + _rms(mixed[gi], gate_gain)
        x1_ref[r0:r0 + rows, :] = x1
        h2 = _rms(x1, ffn_gain) + mod_ref[3:4, :]
        h2_ref[r0:r0 + rows, :] = _pack_halves(h2)
        lts.append(br_ref[:, 0:1] + lax.dot_general(wr, h2.astype(BF16), _NT,
                                                    preferred_element_type=F32))
    lt = jnp.concatenate(lts, axis=1)

    g = lt[0:N_GROUPS]
    row4 = lax.broadcasted_iota(jnp.int32, g.shape, 0)
    gmax = jnp.max(g, axis=0, keepdims=True)
    gidx = jnp.min(jnp.where(g == gmax, row4, N_GROUPS), axis=0, keepdims=True)
    gw = 1.0 / jnp.sum(jnp.exp(g - gmax), axis=0, keepdims=True)
    esel = jnp.zeros((EXPERTS_PER_GROUP, tm), F32)
    for gg in range(N_GROUPS):
        e0 = SUBLANES + EXPERTS_PER_GROUP * gg
        esel = jnp.where(gidx == gg, lt[e0:e0 + EXPERTS_PER_GROUP], esel)
    row8 = lax.broadcasted_iota(jnp.int32, esel.shape, 0)
    v1 = jnp.max(esel, axis=0, keepdims=True)
    i1 = jnp.min(jnp.where(esel == v1, row8, EXPERTS_PER_GROUP), axis=0, keepdims=True)
    esel2 = jnp.where(row8 == i1, -jnp.inf, esel)
    v2 = jnp.max(esel2, axis=0, keepdims=True)
    i2 = jnp.min(jnp.where(esel2 == v2, row8, EXPERTS_PER_GROUP), axis=0, keepdims=True)
    e = jnp.exp(v2 - v1)
    w1 = gw / (1.0 + e)
    w2 = gw * e / (1.0 + e)
    e1 = gidx * EXPERTS_PER_GROUP + i1
    e2 = gidx * EXPERTS_PER_GROUP + i2

    row32 = lax.broadcasted_iota(jnp.int32, (N_EXPERTS, tm), 0)
    oh1 = row32 == e1
    oh2 = row32 == e2
    oh = jnp.where(oh1 | oh2, 1.0, 0.0)
    t_r = lax.broadcasted_iota(jnp.int32, (tm, tm), 0)
    t_c = lax.broadcasted_iota(jnp.int32, (tm, tm), 1)
    tri = jnp.where(t_r < t_c, 1.0, 0.0).astype(BF16)
    tot = jnp.dot(oh.astype(BF16), tri, preferred_element_type=F32) + cnt_ref[:, 0:1]
    r1 = jnp.sum(jnp.where(oh1, tot, 0.0), axis=0, keepdims=True)
    r2 = jnp.sum(jnp.where(oh2, tot, 0.0), axis=0, keepdims=True)
    cnt_ref[...] = cnt_ref[...] + jnp.sum(oh, axis=1, keepdims=True)

    zi = jnp.zeros((SUBLANES - 4, tm), jnp.int32)
    ri_ref[...] = jnp.concatenate([e1, e2, r1.astype(jnp.int32), r2.astype(jnp.int32), zi], axis=0)
    rw_ref[...] = jnp.concatenate([w1, w2, jnp.zeros((SUBLANES - 2, tm), F32)], axis=0)


def _route(d_out, f_out, w_out_d, w_out_f, x, mod, g_post_attn, g_pre_ffn, w_rt, b_rt):
    B, S, D = x.shape
    tm = TM_ROUTE
    nrt = w_rt.shape[0]
    n_s = S // tm
    T = B * S
    return pl.pallas_call(
        _route_kernel,
        grid=(B, n_s),
        in_specs=[pl.BlockSpec((None, tm, SEC), lambda b, s: (b, s, 0)),
                  pl.BlockSpec((None, tm, SEC), lambda b, s: (b, s, 0)),
                  pl.BlockSpec((SEC, D), lambda b, s: (0, 0)),
                  pl.BlockSpec((SEC, D), lambda b, s: (0, 0)),
                  pl.BlockSpec((None, tm, D), lambda b, s: (b, s, 0)),
                  pl.BlockSpec((None, 6, D), lambda b, s: (b, 0, 0)),
                  pl.BlockSpec((1, D), lambda b, s: (0, 0)),
                  pl.BlockSpec((1, D), lambda b, s: (0, 0)),
                  pl.BlockSpec((nrt, D), lambda b, s: (0, 0)),
                  pl.BlockSpec((nrt, LANES), lambda b, s: (0, 0))],
        out_specs=[pl.BlockSpec((None, tm, D), lambda b, s: (b, s, 0)),
                   pl.BlockSpec((None, tm, D // 2), lambda b, s: (b, s, 0)),
                   pl.BlockSpec((SUBLANES, tm), lambda b, s: (0, b * n_s + s)),
                   pl.BlockSpec((SUBLANES, tm), lambda b, s: (0, b * n_s + s)),
                   pl.BlockSpec((N_EXPERTS, LANES), lambda b, s: (0, 0))],
        out_shape=[jax.ShapeDtypeStruct((B, S, D), F32),
                   jax.ShapeDtypeStruct((B, S, D // 2), jnp.uint32),
                   jax.ShapeDtypeStruct((SUBLANES, T), jnp.int32),
                   jax.ShapeDtypeStruct((SUBLANES, T), F32),
                   jax.ShapeDtypeStruct((N_EXPERTS, LANES), F32)],
        compiler_params=_cparams(("arbitrary", "arbitrary")),
        name="out_proj_route",
    )(d_out, f_out, w_out_d, w_out_f, x, mod, g_post_attn, g_pre_ffn, w_rt, b_rt)


def _sc_mesh():
    return plsc.VectorSubcoreMesh(core_axis_name="c", subcore_axis_name="s",
                                  num_cores=SC_CORES, num_subcores=SC_SUBCORES)


def _sc_worker_base(per_worker):
    return (lax.axis_index("s") * SC_CORES + lax.axis_index("c")) * per_worker


def _dispatch(dest, h2, n_rows):
    T, D = h2.shape
    per_worker = T // SC_WORKERS

    @functools.partial(
        pl.kernel, mesh=_sc_mesh(),
        out_type=jax.ShapeDtypeStruct((n_rows, D), h2.dtype),
        scratch_types=[pltpu.VMEM((SC_WIN,), jnp.int32), pltpu.VMEM((SC_WIN, D), h2.dtype)],
        name="moe_dispatch")
    def run(h_hbm, dest_hbm, xs_hbm, idx_v, rows_v):
        base = _sc_worker_base(per_worker)

        @pl.loop(0, per_worker // SC_WIN)
        def _(ci):
            off = pl.multiple_of(base + ci * SC_WIN, SC_WIN)
            pltpu.sync_copy(h_hbm.at[pl.ds(off, SC_WIN)], rows_v)
            for k in range(2):
                pltpu.sync_copy(dest_hbm.at[pl.ds(k * T + off, SC_WIN)], idx_v)
                pltpu.sync_copy(rows_v, xs_hbm.at[idx_v])

    return run(h2, dest.reshape(2 * T))


def _gather_rows(ys, dest):
    _, D = ys.shape
    K, T = dest.shape
    n = K * T
    per_worker = n // SC_WORKERS

    @functools.partial(
        pl.kernel, mesh=_sc_mesh(),
        out_type=jax.ShapeDtypeStruct((n, D), ys.dtype),
        scratch_types=[pltpu.VMEM((SC_WIN,), jnp.int32), pltpu.VMEM((SC_WIN, D), ys.dtype)],
        name="moe_gather")
    def run(ys_hbm, dest_hbm, g_hbm, idx_v, rows_v):
        base = _sc_worker_base(per_worker)

        @pl.loop(0, per_worker // SC_WIN)
        def _(ci):
            off = pl.multiple_of(base + ci * SC_WIN, SC_WIN)
            pltpu.sync_copy(dest_hbm.at[pl.ds(off, SC_WIN)], idx_v)
            pltpu.sync_copy(ys_hbm.at[idx_v], rows_v)
            pltpu.sync_copy(rows_v, g_hbm.at[pl.ds(off, SC_WIN)])

    return run(ys, dest.reshape(n)).reshape(K, T, D)


def _expert_kernel(be_ref, nu_ref, nv_ref, xs_ref, wg_ref, wu_ref, wd_ref, ys_ref,
                   wgb, wub, wdb):
    i = pl.program_id(0)

    @pl.when(i < nu_ref[0])
    def _():
        prev = be_ref[jnp.maximum(i - 1, 0)]

        @pl.when((i == 0) | (be_ref[i] != prev))
        def _():
            wgb[...] = wg_ref[...].astype(BF16)
            wub[...] = wu_ref[...].astype(BF16)
            wdb[...] = wd_ref[...].astype(BF16)

        half = xs_ref.shape[1]
        rows = xs_ref.shape[0] // EXPERT_ROW_GROUPS
        row = lax.broadcasted_iota(jnp.int32, (rows, half), 0)
        au = []
        for r0 in range(0, xs_ref.shape[0], rows):
            words = jnp.where(row + r0 < nv_ref[i], xs_ref[r0:r0 + rows, :], jnp.uint32(0))
            lo, hi = (v.astype(BF16) for v in _unpack_halves(words))

            def proj(w, lo=lo, hi=hi):
                return (jnp.dot(lo, w[0:half, :], preferred_element_type=F32)
                        + jnp.dot(hi, w[half:2 * half, :], preferred_element_type=F32))

            au.append((proj(wgb), proj(wub)))
        for gi, (a, u) in enumerate(au):
            hmid = (a * jax.nn.sigmoid(a) * u).astype(BF16)
            y = jnp.dot(hmid, wdb[...], preferred_element_type=F32)
            ys_ref[gi * rows:(gi + 1) * rows, :] = _pack_halves(y)

    @pl.when(i >= nu_ref[0])
    def _():
        ys_ref[...] = jnp.zeros_like(ys_ref)


def _experts(block_e, n_used, n_valid, xs, w_gate, w_up, w_down):
    P, Dh = xs.shape
    nb = P // BM
    _, D, Fd = w_gate.shape

    def row_map(i, be, nu, nv):
        return (jnp.minimum(i, nu[0] - 1), 0)

    def w_map(i, be, nu, nv):
        return (be[i], 0, 0)

    return pl.pallas_call(
        _expert_kernel,
        grid_spec=pltpu.PrefetchScalarGridSpec(
            num_scalar_prefetch=3,
            grid=(nb,),
            in_specs=[pl.BlockSpec((BM, Dh), row_map),
                      pl.BlockSpec((None, D, Fd), w_map),
                      pl.BlockSpec((None, D, Fd), w_map),
                      pl.BlockSpec((None, Fd, D), w_map)],
            out_specs=pl.BlockSpec((BM, Dh), lambda i, be, nu, nv: (i, 0)),
            scratch_shapes=[pltpu.VMEM((D, Fd), BF16), pltpu.VMEM((D, Fd), BF16),
                            pltpu.VMEM((Fd, D), BF16)]),
        out_shape=jax.ShapeDtypeStruct((P, Dh), jnp.uint32),
        compiler_params=_cparams(("arbitrary",)),
        name="moe_experts",
    )(block_e, n_used, n_valid, xs, w_gate, w_up, w_down)


def _combine_kernel(g_ref, w_ref, x1_ref, mod_ref, g_post_ref, *rest):
    o_ref = rest[-1]
    lo0, hi0 = _unpack_halves(g_ref[0])
    lo1, hi1 = _unpack_halves(g_ref[1])
    tm = lo0.shape[0]
    wt = jnp.concatenate([_rows_to_columns(w_ref[:, j * LANES:(j + 1) * LANES])
                          for j in range(tm // LANES)], axis=0)
    w0, w1 = wt[:, 0:1], wt[:, 1:2]
    y = jnp.concatenate([w0 * lo0 + w1 * lo1, w0 * hi0 + w1 * hi1], axis=1)
    o_ref[...] = x1_ref[...] + _rms(y, mod_ref[5:6, :] * g_post_ref[...])


def _combine(g, wts, x1, mod, g_post_ffn, S, chunk, prev_out):
    T, D = x1.shape
    tm = TM_ROWS
    per_b = S // tm
    steps = g.shape[1] // tm
    off = chunk * steps
    in_specs = [pl.BlockSpec((2, tm, D // 2), lambda i: (0, i, 0)),
                pl.BlockSpec((SUBLANES, tm), lambda i: (0, i + off)),
                pl.BlockSpec((tm, D), lambda i: (i + off, 0)),
                pl.BlockSpec((None, 6, D), lambda i: ((i + off) // per_b, 0, 0)),
                pl.BlockSpec((1, D), lambda i: (0, 0))]
    args = [g, wts, x1, mod, g_post_ffn]
    aliases = {}
    if prev_out is not None:
        in_specs.append(pl.BlockSpec(memory_space=pl.ANY))
        args.append(prev_out)
        aliases = {len(args) - 1: 0}
    return pl.pallas_call(
        _combine_kernel,
        grid=(steps,),
        in_specs=in_specs,
        out_specs=pl.BlockSpec((tm, D), lambda i: (i + off, 0)),
        out_shape=jax.ShapeDtypeStruct((T, D), F32),
        input_output_aliases=aliases,
        compiler_params=_cparams(("arbitrary",)),
        name="moe_combine",
    )(*args)


def _rope_tables(S):
    inv = 1.0 / (10000.0 ** (jnp.arange(0, HEAD_DIM, 2, dtype=F32) / HEAD_DIM))
    ang = jnp.arange(S, dtype=F32)[:, None] * inv[None, :]
    c, s = jnp.cos(ang), jnp.sin(ang)
    return jnp.tile(c, (1, 4)), jnp.tile(jnp.concatenate([-s, s], axis=1), (1, 2))


def kernel(x, c, w_ada, b_ada, g_pre_attn, g_post_attn, w_in, b_forget, lambda_q1, lambda_k1,
           lambda_q2, lambda_k2, g_diff_subln, w_out, g_pre_ffn, g_post_ffn, w_group, b_group,
           w_router, b_router, w_gate, w_up, w_down):
    B, S, D = x.shape
    T = B * S
    l = 0
    mod = _ada(c, w_ada[l], b_ada[l]).reshape(B, 6, D)

    wi = w_in[l]
    w_main = jnp.concatenate([wi[:, 0:2 * SEC], wi[:, 3 * SEC:5 * SEC]], axis=1).astype(BF16)
    w_vt = jnp.concatenate([wi[:, 2 * SEC:3 * SEC], wi[:, 5 * SEC:6 * SEC], wi[:, 6 * SEC:],
                            jnp.zeros((D, BF16_SUBLANES - FOX_HEADS), F32)], axis=1).T.astype(BF16)
    bf = jnp.broadcast_to(b_forget[l][:, None], (FOX_HEADS, LANES))
    cos, sin = _rope_tables(S)
    dq, dk, fq, fk, dvt, fvt, fcum = _inproj(x, mod, g_pre_attn[l][None, :], w_main, w_vt,
                                             bf, cos, sin)

    lam_params = jnp.stack([lambda_q1[l], lambda_k1[l], lambda_q2[l], lambda_k2[l]])
    d_out = _diff_attention(dq, dk, dvt, lam_params, g_diff_subln[l][None, :])
    frow = fcum.reshape(B, FOX_HEADS // 2, 2, S)
    f_out = _fox_attention(fq, fk, fvt, frow)

    w_o = w_out[l].astype(BF16)
    zpad = jnp.zeros((SUBLANES - N_GROUPS, D), F32)
    w_rt = jnp.concatenate([w_group[l].T, zpad, w_router[l].T], axis=0)
    b_rt = jnp.concatenate([b_group[l], jnp.zeros((SUBLANES - N_GROUPS,), F32), b_router[l]])
    b_rt = jnp.broadcast_to(b_rt[:, None], (SUBLANES + N_EXPERTS, LANES))
    x1, h2, ri, rw, cnt = _route(d_out, f_out, w_o[:SEC], w_o[SEC:], x, mod,
                                 g_post_attn[l][None, :], g_pre_ffn[l][None, :], w_rt, b_rt)

    counts = cnt[:, 0].astype(jnp.int32)
    padded = ((counts + BM - 1) // BM) * BM
    pend = jnp.cumsum(padded)
    pstart = pend - padded
    eids = jnp.arange(N_EXPERTS, dtype=jnp.int32)
    dest = ri[2:4] + jnp.sum(jnp.where(ri[0:2, :, None] == eids, pstart, 0), axis=-1)
    nb = (2 * T) // BM + N_EXPERTS
    n_used = (pend[-1] // BM).astype(jnp.int32).reshape(1)
    blk0 = jnp.arange(nb, dtype=jnp.int32) * BM
    block_e = jnp.minimum(jnp.sum((blk0[:, None] >= pend[None, :]).astype(jnp.int32), axis=1),
                          N_EXPERTS - 1)
    cend = pstart + counts
    blk_end = jnp.sum(jnp.where(block_e[:, None] == eids, cend, 0), axis=-1)
    n_valid = jnp.clip(blk_end - blk0, 0, BM).astype(jnp.int32)

    xs = _dispatch(dest, h2.reshape(T, D // 2), nb * BM)
    ys = _experts(block_e, n_used, n_valid, xs, w_gate[l], w_up[l], w_down[l])
    out = None
    tc = T // COMBINE_CHUNKS
    for ck in range(COMBINE_CHUNKS):
        g = _gather_rows(ys, dest[:, ck * tc:(ck + 1) * tc])
        out = _combine(g, rw, x1.reshape(T, D), mod, g_post_ffn[l][None, :], S, ck, out)
    return out.reshape(B, S, D)
```

```python
import functools
import math

import jax
import jax.numpy as jnp
from jax import lax
from jax.experimental import pallas as pl
from jax.experimental.pallas import tpu as pltpu
from jax.experimental.pallas import tpu_sc as plsc

F32 = jnp.float32
BF16 = jnp.bfloat16

D_MODEL = 1024
HEAD_DIM = 64
CHUNK = 64
DIFF_HEADS = 4
FOX_HEADS = 8
SEC = 512
N_GROUPS = 4
EXPERTS_PER_GROUP = 8
N_EXPERTS = 32
D_EXPERT = 512
EPS = 1e-6
NEG_INF = -1e30
LOG2E = 1.4426950408889634
LAMBDA_INIT = 0.8 - 0.6 * math.exp(-0.3 * 0)

LANES = 128
SUBLANES = 8
BF16_SUBLANES = 16
TN_ADA = 1024
TM_PROJ = 1024
TQ = 512
TK = 512
HK = TK // 2
ATT_GROUPS = 1
DV_ROWS = 2 * HEAD_DIM + BF16_SUBLANES
TM_ROUTE = 512
ROUTE_ROW_GROUPS = 2
BM = 512
EXPERT_ROW_GROUPS = 2
TM_ROWS = 512
COMBINE_CHUNKS = 8
SC_CORES = 2
SC_SUBCORES = 16
SC_WORKERS = SC_CORES * SC_SUBCORES
SC_WIN = 128
VMEM_LIMIT = 48 * 1024 * 1024
_NT = (((1,), (1,)), ((), ()))


def _cparams(sem, flags=None):
    return pltpu.CompilerParams(dimension_semantics=sem, vmem_limit_bytes=VMEM_LIMIT, flags=flags)


def _rms(x, g):
    return x * lax.rsqrt(jnp.mean(x * x, axis=-1, keepdims=True) + EPS) * g


def _pack_halves(x):
    m = x.shape[1] // 2
    bits = lax.bitcast_convert_type(x.astype(BF16).astype(F32), jnp.uint32)
    return (bits[:, :m] >> 16) | (bits[:, m:] & jnp.uint32(0xFFFF0000))


def _unpack_halves(w):
    lo = lax.bitcast_convert_type(w << 16, F32)
    hi = lax.bitcast_convert_type(w & jnp.uint32(0xFFFF0000), F32)
    return lo, hi


def _rows_to_columns(x):
    pad = jnp.zeros((LANES - x.shape[0], LANES), x.dtype)
    return jnp.concatenate([x, pad], axis=0).T


def _ada_kernel(c_ref, w_ref, b_ref, o_ref):
    c = c_ref[...]
    cs = c * jax.nn.sigmoid(c)
    o_ref[...] = jnp.dot(cs, w_ref[...], preferred_element_type=F32,
                         precision=lax.Precision.HIGHEST) + b_ref[...]


def _ada(c, w_ada, b_ada):
    B, D = c.shape
    N = w_ada.shape[1]
    tn = TN_ADA
    return pl.pallas_call(
        _ada_kernel,
        grid=(N // tn,),
        in_specs=[pl.BlockSpec((B, D), lambda j: (0, 0)),
                  pl.BlockSpec((D, tn), lambda j: (0, j)),
                  pl.BlockSpec((1, tn), lambda j: (0, j))],
        out_specs=pl.BlockSpec((B, tn), lambda j: (0, j)),
        out_shape=jax.ShapeDtypeStruct((B, N), F32),
        compiler_params=_cparams(("arbitrary",)),
        name="ada_mod",
    )(c, w_ada, b_ada.reshape(1, N))


def _split3(x):
    hi = x.astype(BF16)
    r1 = x - hi.astype(F32)
    mid = r1.astype(BF16)
    lo = (r1 - mid.astype(F32)).astype(BF16)
    return hi, mid, lo


def _inproj_kernel(x_ref, mod_ref, g_ref, w_ref, wvt_ref, bf_ref, cos_ref, sin_ref,
                   dq_ref, dk_ref, fq_ref, fk_ref, dvt_ref, fvt_ref, fcum_ref, carry_ref):
    si = pl.program_id(1)
    x = x_ref[...]
    h = _rms(x, g_ref[...] * (1.0 + mod_ref[1:2, :])) + mod_ref[0:1, :]
    hb = h.astype(BF16)
    tm = x.shape[0]

    cos = cos_ref[...]
    sin = sin_ref[...]
    lane = lax.broadcasted_iota(jnp.int32, (tm, LANES), 1)
    first_half = (lane % HEAD_DIM) < (HEAD_DIM // 2)
    qscale = HEAD_DIM ** -0.5 * LOG2E

    def proj(sec):
        return jnp.dot(hb, w_ref[:, sec * SEC:(sec + 1) * SEC], preferred_element_type=F32)

    def rope_store(p, out_ref, scale):
        for j in range(SEC // LANES):
            ch = p[:, j * LANES:(j + 1) * LANES]
            partner = jnp.where(first_half,
                                pltpu.roll(ch, LANES - HEAD_DIM // 2, 1),
                                pltpu.roll(ch, HEAD_DIM // 2, 1))
            r = ch * cos + partner * sin
            if scale != 1.0:
                r = r * scale
            out_ref[:, j * LANES:(j + 1) * LANES] = r.astype(out_ref.dtype)

    rope_store(proj(0), dq_ref, qscale)
    rope_store(proj(1), dk_ref, 1.0)
    fq_ref[...] = (proj(2) * qscale).astype(fq_ref.dtype)
    fk_ref[...] = proj(3).astype(fk_ref.dtype)
    vt = lax.dot_general(wvt_ref[...], hb, _NT, preferred_element_type=F32)
    for hd in range(DIFF_HEADS):
        dvt_ref[hd, 0:LANES, :] = vt[hd * LANES:(hd + 1) * LANES].astype(dvt_ref.dtype)
        dvt_ref[hd, LANES:DV_ROWS, :] = jnp.ones((DV_ROWS - LANES, tm), dvt_ref.dtype)
    fvt_ref[...] = vt[SEC:2 * SEC].astype(fvt_ref.dtype)

    z = vt[2 * SEC:2 * SEC + FOX_HEADS] + bf_ref[:, 0:1]
    lf = (jnp.minimum(z, 0.0) - jnp.log1p(jnp.exp(-jnp.abs(z)))) * LOG2E

    @pl.when(si == 0)
    def _():
        carry_ref[...] = jnp.zeros_like(carry_ref)

    r_i = lax.broadcasted_iota(jnp.int32, (LANES, LANES), 0)
    c_i = lax.broadcasted_iota(jnp.int32, (LANES, LANES), 1)
    tri = jnp.where(r_i <= c_i, 1.0, 0.0).astype(BF16)
    run = carry_ref[...]
    for j in range(tm // LANES):
        hi, mid, lo = _split3(lf[:, j * LANES:(j + 1) * LANES])
        cs = (jnp.dot(hi, tri, preferred_element_type=F32)
              + jnp.dot(mid, tri, preferred_element_type=F32)
              + jnp.dot(lo, tri, preferred_element_type=F32)) + run
        fcum_ref[:, j * LANES:(j + 1) * LANES] = cs
        run = jnp.broadcast_to(cs[:, LANES - 1:LANES], run.shape)
    carry_ref[...] = run


def _inproj(x, mod, g_pre, w_main, w_vt, bf, cos, sin):
    B, S, D = x.shape
    tm = TM_PROJ
    sec_out = jax.ShapeDtypeStruct((B, S, SEC), BF16)
    sec_spec = pl.BlockSpec((None, tm, SEC), lambda b, s: (b, s, 0))
    t_out = jax.ShapeDtypeStruct((B, SEC, S), BF16)
    t_spec = pl.BlockSpec((None, SEC, tm), lambda b, s: (b, 0, s))
    return pl.pallas_call(
        _inproj_kernel,
        grid=(B, S // tm),
        in_specs=[pl.BlockSpec((None, tm, D), lambda b, s: (b, s, 0)),
                  pl.BlockSpec((None, 6, D), lambda b, s: (b, 0, 0)),
                  pl.BlockSpec((1, D), lambda b, s: (0, 0)),
                  pl.BlockSpec((D, 4 * SEC), lambda b, s: (0, 0)),
                  pl.BlockSpec(w_vt.shape, lambda b, s: (0, 0)),
                  pl.BlockSpec((FOX_HEADS, LANES), lambda b, s: (0, 0)),
                  pl.BlockSpec((tm, LANES), lambda b, s: (s, 0)),
                  pl.BlockSpec((tm, LANES), lambda b, s: (s, 0))],
        out_specs=[sec_spec] * 4
        + [pl.BlockSpec((None, DIFF_HEADS, DV_ROWS, tm), lambda b, s: (b, 0, 0, s)), t_spec,
           pl.BlockSpec((None, FOX_HEADS, tm), lambda b, s: (b, 0, s))],
        out_shape=[sec_out] * 4
        + [jax.ShapeDtypeStruct((B, DIFF_HEADS, DV_ROWS, S), BF16), t_out,
           jax.ShapeDtypeStruct((B, FOX_HEADS, S), F32)],
        scratch_shapes=[pltpu.VMEM((FOX_HEADS, LANES), F32)],
        compiler_params=_cparams(("arbitrary", "arbitrary")),
        name="in_proj",
    )(x, mod, g_pre, w_main, w_vt, bf, cos, sin)


def _scores(ch, r0, qm=None):
    r0 = pl.multiple_of(r0, HK)
    t = lax.dot_general(ch["keys"](r0), ch["qm"] if qm is None else qm, _NT,
                        preferred_element_type=F32)
    if ch["kbias"] is not None:
        t = t - ch["kbias"](r0)
    return t


def _flash_t(chains, sa_ref, m_ref, acc_ref, qi, diag_mask, next_qm):
    def update(ci, ch, t, r0, cols=slice(None)):
        m = m_ref[ci, :, cols]
        mt = jnp.max(t, axis=0, keepdims=True)
        qb = ch["qbias"]
        qb = None if qb is None else qb[:, cols]
        m_new = jnp.maximum(m, mt if qb is None else mt + qb)
        alpha = jnp.exp2(m - m_new)
        p = jnp.exp2((t - (m_new if qb is None else m_new - qb)).astype(BF16))
        vt = ch["vt"](pl.multiple_of(r0, HK))
        m_ref[ci, :, cols] = m_new
        acc_ref[ci, :, cols] = (alpha * acc_ref[ci, :, cols]
                                + jnp.dot(vt, p, preferred_element_type=F32))

    def full_tiles(k0, n):
        tb = [_scores(ch, k0 + HK) for ch in chains]
        for t in range(n):
            kt = k0 + t * TK
            for ci, ch in enumerate(chains):
                update(ci, ch, sa_ref[ci], kt)
            for ci, ch in enumerate(chains):
                sa_ref[ci] = _scores(ch, kt + TK)
            tb_next = [_scores(ch, kt + TK + HK) for ch in chains] if t + 1 < n else None
            for ci, ch in enumerate(chains):
                update(ci, ch, tb[ci], kt + HK)
            tb = tb_next

    def tile_pair(kj, carry):
        full_tiles(kj * (2 * TK), 2)
        return carry

    m_ref[...] = jnp.full(m_ref.shape, NEG_INF, F32)
    acc_ref[...] = jnp.zeros(acc_ref.shape, F32)
    lax.fori_loop(0, qi // 2, tile_pair, 0)

    @pl.when(qi % 2 == 1)
    def _():
        full_tiles((qi - 1) * TK, 1)
    k0 = qi * TK
    hq = TQ - HK
    tb = [_scores(ch, k0 + HK, ch["qm"][hq:]) for ch in chains]
    nxt = [_scores(ch, 0, qm) for ch, qm in zip(chains, next_qm)]
    for ci, ch in enumerate(chains):
        update(ci, ch, jnp.where(diag_mask[0:HK], sa_ref[ci], NEG_INF), k0)
    for ci, ch in enumerate(chains):
        update(ci, ch, jnp.where(diag_mask[HK:TK, hq:], tb[ci], NEG_INF), k0 + HK,
               slice(hq, TQ))
    for ci in range(len(chains)):
        sa_ref[ci] = nxt[ci]
    return tuple(acc_ref[ci] for ci in range(len(chains)))


def _next_tile_start(q0, seq_len):
    return pl.multiple_of(jnp.minimum(q0 + TQ, seq_len - TQ), TQ)


def _for_each_query_tile(n_tiles, tile_fn):
    def body(qi, carry):
        tile_fn(qi, pl.multiple_of(qi * TQ, TQ))
        return carry

    lax.fori_loop(0, n_tiles, body, 0)


def _lane_group(ref, rows, g):
    return ref[rows, g * LANES:(g + 1) * LANES]


def _masked_queries(q_ref, q0):
    lane = lax.broadcasted_iota(jnp.int32, (TQ, LANES), 1)
    out = []
    for g in range(ATT_GROUPS):
        q2 = _lane_group(q_ref, pl.ds(q0, TQ), g)
        out += [jnp.where((lane >= j * HEAD_DIM) & (lane < (j + 1) * HEAD_DIM), q2,
                          jnp.zeros_like(q2)) for j in range(2)]
    return out


def _fox_kernel(q_ref, k_ref, vt_ref, frow_ref, o_ref, sa_ref, m_ref, acc_ref, fcol_ref):
    for g in range(ATT_GROUPS):
        for j in range(frow_ref.shape[2] // LANES):
            fcol_ref[g, j * LANES:(j + 1) * LANES, :] = _rows_to_columns(
                frow_ref[g, :, j * LANES:(j + 1) * LANES])

    vrow = lax.broadcasted_iota(jnp.int32, (LANES, HK), 0)
    r = lax.broadcasted_iota(jnp.int32, (TK, TQ), 0)
    c = lax.broadcasted_iota(jnp.int32, (TK, TQ), 1)

    def make_chains(q0):
        chains = []
        for ci, qm in enumerate(_masked_queries(q_ref, q0)):
            g, j = divmod(ci, 2)
            v_rows = (vrow >= j * HEAD_DIM) & (vrow < (j + 1) * HEAD_DIM)

            def vt(k0, g=g, v_rows=v_rows):
                v = vt_ref[g * LANES:(g + 1) * LANES, pl.ds(k0, HK)]
                return jnp.where(v_rows, v, jnp.ones_like(v))

            chains.append(dict(
                qm=qm, keys=lambda r0, g=g: _lane_group(k_ref, pl.ds(r0, HK), g),
                kbias=lambda r0, g=g, j=j: fcol_ref[g, pl.ds(r0, HK), j:j + 1],
                qbias=frow_ref[g, j:j + 1, pl.ds(q0, TQ)], vt=vt, rows=LANES))
        return chains

    def tile(qi, q0):
        res = _flash_t(make_chains(q0), sa_ref, m_ref, acc_ref, qi, r <= c,
                       _masked_queries(q_ref, _next_tile_start(q0, q_ref.shape[0])))
        h = HEAD_DIM
        for g in range(ATT_GROUPS):
            a0, a1 = res[2 * g], res[2 * g + 1]
            o_t = jnp.concatenate([a0[0:h] / a0[h:2 * h], a1[h:2 * h] / a1[0:h]], axis=0)
            o_ref[pl.ds(q0, TQ), g * LANES:(g + 1) * LANES] = o_t.T.astype(o_ref.dtype)

    for ci, ch in enumerate(make_chains(0)):
        sa_ref[ci] = _scores(ch, 0)
    _for_each_query_tile(q_ref.shape[0] // TQ, tile)


def _fox_attention(fq, fk, fvt, frow):
    B, S, _ = fq.shape
    gw = ATT_GROUPS * LANES
    seq_spec = pl.BlockSpec((None, S, gw), lambda b, p: (b, 0, p))
    return pl.pallas_call(
        _fox_kernel,
        grid=(B, FOX_HEADS // 2 // ATT_GROUPS),
        in_specs=[seq_spec, seq_spec,
                  pl.BlockSpec((None, gw, S), lambda b, p: (b, p, 0)),
                  pl.BlockSpec((None, ATT_GROUPS, 2, S), lambda b, p: (b, p, 0, 0))],
        out_specs=seq_spec,
        out_shape=jax.ShapeDtypeStruct((B, S, SEC), BF16),
        scratch_shapes=[pltpu.VMEM((2 * ATT_GROUPS, HK, TQ), F32),
                        pltpu.VMEM((2 * ATT_GROUPS, 1, TQ), F32),
                        pltpu.VMEM((2 * ATT_GROUPS, LANES, TQ), F32),
                        pltpu.VMEM((ATT_GROUPS, S, LANES), F32)],
        compiler_params=_cparams(("arbitrary", "arbitrary")),
        name="fox_attention",
    )(fq, fk, fvt, frow)


def _diff_kernel(q_ref, k_ref, vt_ref, lam_ref, g_ref, o_ref, sa_ref, m_ref, acc_ref):
    r = lax.broadcasted_iota(jnp.int32, (TK, TQ), 0)
    c = lax.broadcasted_iota(jnp.int32, (TK, TQ), 1)
    lp = lam_ref[...]
    lam = (jnp.exp(jnp.sum(lp[0:1] * lp[1:2], axis=1, keepdims=True))
           - jnp.exp(jnp.sum(lp[2:3] * lp[3:4], axis=1, keepdims=True)) + LAMBDA_INIT)
    gain_col = _rows_to_columns(g_ref[...] * (1.0 - LAMBDA_INIT))[:, 0:1]

    def make_chains(q0):
        return [dict(qm=qm, keys=lambda r0, g=ci // 2: _lane_group(k_ref, pl.ds(r0, HK), g),
                     kbias=None, qbias=None,
                     vt=lambda k0, g=ci // 2: vt_ref[g, :, pl.ds(k0, HK)], rows=DV_ROWS)
                for ci, qm in enumerate(_masked_queries(q_ref, q0))]

    def tile(qi, q0):
        res = _flash_t(make_chains(q0), sa_ref, m_ref, acc_ref, qi,
                       (r // CHUNK) <= (c // CHUNK),
                       _masked_queries(q_ref, _next_tile_start(q0, q_ref.shape[0])))
        v = LANES
        for g in range(ATT_GROUPS):
            a0, a1 = res[2 * g], res[2 * g + 1]
            o_t = a0[0:v] / a0[v:v + 1] - lam * (a1[0:v] / a1[v:v + 1])
            inv = lax.rsqrt(jnp.mean(o_t * o_t, axis=0, keepdims=True) + EPS)
            o_t = o_t * inv * gain_col
            o_ref[pl.ds(q0, TQ), g * LANES:(g + 1) * LANES] = o_t.T.astype(o_ref.dtype)

    for ci, ch in enumerate(make_chains(0)):
        sa_ref[ci] = _scores(ch, 0)
    _for_each_query_tile(q_ref.shape[0] // TQ, tile)


def _diff_attention(dq, dk, dvt, lam_params, g_subln):
    B, S, _ = dq.shape
    seq_spec = pl.BlockSpec((None, S, ATT_GROUPS * LANES), lambda b, h: (b, 0, h))
    return pl.pallas_call(
        _diff_kernel,
        grid=(B, DIFF_HEADS // ATT_GROUPS),
        in_specs=[seq_spec, seq_spec,
                  pl.BlockSpec((None, ATT_GROUPS, DV_ROWS, S), lambda b, h: (b, h, 0, 0)),
                  pl.BlockSpec((4, HEAD_DIM), lambda b, h: (0, 0)),
                  pl.BlockSpec((1, LANES), lambda b, h: (0, 0))],
        out_specs=seq_spec,
        out_shape=jax.ShapeDtypeStruct((B, S, SEC), BF16),
        scratch_shapes=[pltpu.VMEM((2 * ATT_GROUPS, HK, TQ), F32),
                        pltpu.VMEM((2 * ATT_GROUPS, 1, TQ), F32),
                        pltpu.VMEM((2 * ATT_GROUPS, DV_ROWS, TQ), F32)],
        compiler_params=_cparams(("arbitrary", "arbitrary")),
        name="diff_attention",
    )(dq, dk, dvt, lam_params, g_subln)


def _route_kernel(d_ref, f_ref, wd_ref, wf_ref, x_ref, mod_ref, gpa_ref, gpf_ref, wr_ref, br_ref,
                  x1_ref, h2_ref, ri_ref, rw_ref, cnt_ref):
    first = (pl.program_id(0) == 0) & (pl.program_id(1) == 0)

    @pl.when(first)
    def _():
        cnt_ref[...] = jnp.zeros_like(cnt_ref)

    tm = x_ref.shape[0]
    rows = tm // ROUTE_ROW_GROUPS
    mixed = [jnp.dot(d_ref[r0:r0 + rows, :], wd_ref[...], preferred_element_type=F32)
             + jnp.dot(f_ref[r0:r0 + rows, :], wf_ref[...], preferred_element_type=F32)
             for r0 in range(0, tm, rows)]
    wr = wr_ref[...].astype(BF16)
    gate_gain = mod_ref[2:3, :] * gpa_ref[...]
    ffn_gain = gpf_ref[...] * (1.0 + mod_ref[4:5, :])
    lts = []
    for gi, r0 in enumerate(range(0, tm, rows)):
        x1 = x_ref[r0:r0 + rows, :] + _rms(mixed[gi], gate_gain)
        x1_ref[r0:r0 + rows, :] = x1
        h2 = _rms(x1, ffn_gain) + mod_ref[3:4, :]
        h2_ref[r0:r0 + rows, :] = _pack_halves(h2)
        lts.append(br_ref[:, 0:1] + lax.dot_general(wr, h2.astype(BF16), _NT,
                                                    preferred_element_type=F32))
    lt = jnp.concatenate(lts, axis=1)

    g = lt[0:N_GROUPS]
    row4 = lax.broadcasted_iota(jnp.int32, g.shape, 0)
    gmax = jnp.max(g, axis=0, keepdims=True)
    gidx = jnp.min(jnp.where(g == gmax, row4, N_GROUPS), axis=0, keepdims=True)
    gw = 1.0 / jnp.sum(jnp.exp(g - gmax), axis=0, keepdims=True)
    esel = jnp.zeros((EXPERTS_PER_GROUP, tm), F32)
    for gg in range(N_GROUPS):
        e0 = SUBLANES + EXPERTS_PER_GROUP * gg
        esel = jnp.where(gidx == gg, lt[e0:e0 + EXPERTS_PER_GROUP], esel)
    row8 = lax.broadcasted_iota(jnp.int32, esel.shape, 0)
    v1 = jnp.max(esel, axis=0, keepdims=True)
    i1 = jnp.min(jnp.where(esel == v1, row8, EXPERTS_PER_GROUP), axis=0, keepdims=True)
    esel2 = jnp.where(row8 == i1, -jnp.inf, esel)
    v2 = jnp.max(esel2, axis=0, keepdims=True)
    i2 = jnp.min(jnp.where(esel2 == v2, row8, EXPERTS_PER_GROUP), axis=0, keepdims=True)
    e = jnp.exp(v2 - v1)
    w1 = gw / (1.0 + e)
    w2 = gw * e / (1.0 + e)
    e1 = gidx * EXPERTS_PER_GROUP + i1
    e2 = gidx * EXPERTS_PER_GROUP + i2

    row32 = lax.broadcasted_iota(jnp.int32, (N_EXPERTS, tm), 0)
    oh1 = row32 == e1
    oh2 = row32 == e2
    oh = jnp.where(oh1 | oh2, 1.0, 0.0)
    t_r = lax.broadcasted_iota(jnp.int32, (tm, tm), 0)
    t_c = lax.broadcasted_iota(jnp.int32, (tm, tm), 1)
    tri = jnp.where(t_r < t_c, 1.0, 0.0).astype(BF16)
    tot = jnp.dot(oh.astype(BF16), tri, preferred_element_type=F32) + cnt_ref[:, 0:1]
    r1 = jnp.sum(jnp.where(oh1, tot, 0.0), axis=0, keepdims=True)
    r2 = jnp.sum(jnp.where(oh2, tot, 0.0), axis=0, keepdims=True)
    cnt_ref[...] = cnt_ref[...] + jnp.sum(oh, axis=1, keepdims=True)

    zi = jnp.zeros((SUBLANES - 4, tm), jnp.int32)
    ri_ref[...] = jnp.concatenate([e1, e2, r1.astype(jnp.int32), r2.astype(jnp.int32), zi], axis=0)
    rw_ref[...] = jnp.concatenate([w1, w2, jnp.zeros((SUBLANES - 2, tm), F32)], axis=0)


def _route(d_out, f_out, w_out_d, w_out_f, x, mod, g_post_attn, g_pre_ffn, w_rt, b_rt):
    B, S, D = x.shape
    tm = TM_ROUTE
    nrt = w_rt.shape[0]
    n_s = S // tm
    T = B * S
    return pl.pallas_call(
        _route_kernel,
        grid=(B, n_s),
        in_specs=[pl.BlockSpec((None, tm, SEC), lambda b, s: (b, s, 0)),
                  pl.BlockSpec((None, tm, SEC), lambda b, s: (b, s, 0)),
                  pl.BlockSpec((SEC, D), lambda b, s: (0, 0)),
                  pl.BlockSpec((SEC, D), lambda b, s: (0, 0)),
                  pl.BlockSpec((None, tm, D), lambda b, s: (b, s, 0)),
                  pl.BlockSpec((None, 6, D), lambda b, s: (b, 0, 0)),
                  pl.BlockSpec((1, D), lambda b, s: (0, 0)),
                  pl.BlockSpec((1, D), lambda b, s: (0, 0)),
                  pl.BlockSpec((nrt, D), lambda b, s: (0, 0)),
                  pl.BlockSpec((nrt, LANES), lambda b, s: (0, 0))],
        out_specs=[pl.BlockSpec((None, tm, D), lambda b, s: (b, s, 0)),
                   pl.BlockSpec((None, tm, D // 2), lambda b, s: (b, s, 0)),
                   pl.BlockSpec((SUBLANES, tm), lambda b, s: (0, b * n_s + s)),
                   pl.BlockSpec((SUBLANES, tm), lambda b, s: (0, b * n_s + s)),
                   pl.BlockSpec((N_EXPERTS, LANES), lambda b, s: (0, 0))],
        out_shape=[jax.ShapeDtypeStruct((B, S, D), F32),
                   jax.ShapeDtypeStruct((B, S, D // 2), jnp.uint32),
                   jax.ShapeDtypeStruct((SUBLANES, T), jnp.int32),
                   jax.ShapeDtypeStruct((SUBLANES, T), F32),
                   jax.ShapeDtypeStruct((N_EXPERTS, LANES), F32)],
        compiler_params=_cparams(("arbitrary", "arbitrary")),
        name="out_proj_route",
    )(d_out, f_out, w_out_d, w_out_f, x, mod, g_post_attn, g_pre_ffn, w_rt, b_rt)


def _sc_mesh():
    return plsc.VectorSubcoreMesh(core_axis_name="c", subcore_axis_name="s",
                                  num_cores=SC_CORES, num_subcores=SC_SUBCORES)


def _sc_worker_base(per_worker):
    return (lax.axis_index("s") * SC_CORES + lax.axis_index("c")) * per_worker


def _dispatch(dest, h2, n_rows):
    T, D = h2.shape
    per_worker = T // SC_WORKERS

    @functools.partial(
        pl.kernel, mesh=_sc_mesh(),
        out_type=jax.ShapeDtypeStruct((n_rows, D), h2.dtype),
        scratch_types=[pltpu.VMEM((SC_WIN,), jnp.int32), pltpu.VMEM((SC_WIN, D), h2.dtype)],
        name="moe_dispatch")
    def run(h_hbm, dest_hbm, xs_hbm, idx_v, rows_v):
        base = _sc_worker_base(per_worker)

        @pl.loop(0, per_worker // SC_WIN)
        def _(ci):
            off = pl.multiple_of(base + ci * SC_WIN, SC_WIN)
            pltpu.sync_copy(h_hbm.at[pl.ds(off, SC_WIN)], rows_v)
            for k in range(2):
                pltpu.sync_copy(dest_hbm.at[pl.ds(k * T + off, SC_WIN)], idx_v)
                pltpu.sync_copy(rows_v, xs_hbm.at[idx_v])

    return run(h2, dest.reshape(2 * T))


def _gather_rows(ys, dest):
    _, D = ys.shape
    K, T = dest.shape
    n = K * T
    per_worker = n // SC_WORKERS

    @functools.partial(
        pl.kernel, mesh=_sc_mesh(),
        out_type=jax.ShapeDtypeStruct((n, D), ys.dtype),
        scratch_types=[pltpu.VMEM((SC_WIN,), jnp.int32), pltpu.VMEM((SC_WIN, D), ys.dtype)],
        name="moe_gather")
    def run(ys_hbm, dest_hbm, g_hbm, idx_v, rows_v):
        base = _sc_worker_base(per_worker)

        @pl.loop(0, per_worker // SC_WIN)
        def _(ci):
            off = pl.multiple_of(base + ci * SC_WIN, SC_WIN)
            pltpu.sync_copy(dest_hbm.at[pl.ds(off, SC_WIN)], idx_v)
            pltpu.sync_copy(ys_hbm.at[idx_v], rows_v)
            pltpu.sync_copy(rows_v, g_hbm.at[pl.ds(off, SC_WIN)])

    return run(ys, dest.reshape(n)).reshape(K, T, D)


def _expert_kernel(be_ref, nu_ref, nv_ref, xs_ref, wg_ref, wu_ref, wd_ref, ys_ref,
                   wgb, wub, wdb):
    i = pl.program_id(0)

    @pl.when(i < nu_ref[0])
    def _():
        prev = be_ref[jnp.maximum(i - 1, 0)]

        @pl.when((i == 0) | (be_ref[i] != prev))
        def _():
            wgb[...] = wg_ref[...].astype(BF16)
            wub[...] = wu_ref[...].astype(BF16)
            wdb[...] = wd_ref[...].astype(BF16)

        half = xs_ref.shape[1]
        rows = xs_ref.shape[0] // EXPERT_ROW_GROUPS
        row = lax.broadcasted_iota(jnp.int32, (rows, half), 0)
        au = []
        for r0 in range(0, xs_ref.shape[0], rows):
            words = jnp.where(row + r0 < nv_ref[i], xs_ref[r0:r0 + rows, :], jnp.uint32(0))
            lo, hi = (v.astype(BF16) for v in _unpack_halves(words))

            def proj(w, lo=lo, hi=hi):
                return (jnp.dot(lo, w[0:half, :], preferred_element_type=F32)
                        + jnp.dot(hi, w[half:2 * half, :], preferred_element_type=F32))

            au.append((proj(wgb), proj(wub)))
        for gi, (a, u) in enumerate(au):
            hmid = (a * jax.nn.sigmoid(a) * u).astype(BF16)
            y = jnp.dot(hmid, wdb[...], preferred_element_type=F32)
            ys_ref[gi * rows:(gi + 1) * rows, :] = _pack_halves(y)

    @pl.when(i >= nu_ref[0])
    def _():
        ys_ref[...] = jnp.zeros_like(ys_ref)


def _experts(block_e, n_used, n_valid, xs, w_gate, w_up, w_down):
    P, Dh = xs.shape
    nb = P // BM
    _, D, Fd = w_gate.shape

    def row_map(i, be, nu, nv):
        return (jnp.minimum(i, nu[0] - 1), 0)

    def w_map(i, be, nu, nv):
        return (be[i], 0, 0)

    return pl.pallas_call(
        _expert_kernel,
        grid_spec=pltpu.PrefetchScalarGridSpec(
            num_scalar_prefetch=3,
            grid=(nb,),
            in_specs=[pl.BlockSpec((BM, Dh), row_map),
                      pl.BlockSpec((None, D, Fd), w_map),
                      pl.BlockSpec((None, D, Fd), w_map),
                      pl.BlockSpec((None, Fd, D), w_map)],
            out_specs=pl.BlockSpec((BM, Dh), lambda i, be, nu, nv: (i, 0)),
            scratch_shapes=[pltpu.VMEM((D, Fd), BF16), pltpu.VMEM((D, Fd), BF16),
                            pltpu.VMEM((Fd, D), BF16)]),
        out_shape=jax.ShapeDtypeStruct((P, Dh), jnp.uint32),
        compiler_params=_cparams(("arbitrary",)),
        name="moe_experts",
    )(block_e, n_used, n_valid, xs, w_gate, w_up, w_down)


def _combine_kernel(g_ref, w_ref, x1_ref, mod_ref, g_post_ref, *rest):
    o_ref = rest[-1]
    lo0, hi0 = _unpack_halves(g_ref[0])
    lo1, hi1 = _unpack_halves(g_ref[1])
    tm = lo0.shape[0]
    wt = jnp.concatenate([_rows_to_columns(w_ref[:, j * LANES:(j + 1) * LANES])
                          for j in range(tm // LANES)], axis=0)
    w0, w1 = wt[:, 0:1], wt[:, 1:2]
    y = jnp.concatenate([w0 * lo0 + w1 * lo1, w0 * hi0 + w1 * hi1], axis=1)
    o_ref[...] = x1_ref[...] + _rms(y, mod_ref[5:6, :] * g_post_ref[...])


def _combine(g, wts, x1, mod, g_post_ffn, S, chunk, prev_out):
    T, D = x1.shape
    tm = TM_ROWS
    per_b = S // tm
    steps = g.shape[1] // tm
    off = chunk * steps
    in_specs = [pl.BlockSpec((2, tm, D // 2), lambda i: (0, i, 0)),
                pl.BlockSpec((SUBLANES, tm), lambda i: (0, i + off)),
                pl.BlockSpec((tm, D), lambda i: (i + off, 0)),
                pl.BlockSpec((None, 6, D), lambda i: ((i + off) // per_b, 0, 0)),
                pl.BlockSpec((1, D), lambda i: (0, 0))]
    args = [g, wts, x1, mod, g_post_ffn]
    aliases = {}
    if prev_out is not None:
        in_specs.append(pl.BlockSpec(memory_space=pl.ANY))
        args.append(prev_out)
        aliases = {len(args) - 1: 0}
    return pl.pallas_call(
        _combine_kernel,
        grid=(steps,),
        in_specs=in_specs,
        out_specs=pl.BlockSpec((tm, D), lambda i: (i + off, 0)),
        out_shape=jax.ShapeDtypeStruct((T, D), F32),
        input_output_aliases=aliases,
        compiler_params=_cparams(("arbitrary",)),
        name="moe_combine",
    )(*args)


def _rope_tables(S):
    inv = 1.0 / (10000.0 ** (jnp.arange(0, HEAD_DIM, 2, dtype=F32) / HEAD_DIM))
    ang = jnp.arange(S, dtype=F32)[:, None] * inv[None, :]
    c, s = jnp.cos(ang), jnp.sin(ang)
    return jnp.tile(c, (1, 4)), jnp.tile(jnp.concatenate([-s, s], axis=1), (1, 2))


def kernel(x, c, w_ada, b_ada, g_pre_attn, g_post_attn, w_in, b_forget, lambda_q1, lambda_k1,
           lambda_q2, lambda_k2, g_diff_subln, w_out, g_pre_ffn, g_post_ffn, w_group, b_group,
           w_router, b_router, w_gate, w_up, w_down):
    B, S, D = x.shape
    T = B * S
    l = 0
    mod = _ada(c, w_ada[l], b_ada[l]).reshape(B, 6, D)

    wi = w_in[l]
    w_main = jnp.concatenate([wi[:, 0:2 * SEC], wi[:, 3 * SEC:5 * SEC]], axis=1).astype(BF16)
    w_vt = jnp.concatenate([wi[:, 2 * SEC:3 * SEC], wi[:, 5 * SEC:6 * SEC], wi[:, 6 * SEC:],
                            jnp.zeros((D, BF16_SUBLANES - FOX_HEADS), F32)], axis=1).T.astype(BF16)
    bf = jnp.broadcast_to(b_forget[l][:, None], (FOX_HEADS, LANES))
    cos, sin = _rope_tables(S)
    dq, dk, fq, fk, dvt, fvt, fcum = _inproj(x, mod, g_pre_attn[l][None, :], w_main, w_vt,
                                             bf, cos, sin)

    lam_params = jnp.stack([lambda_q1[l], lambda_k1[l], lambda_q2[l], lambda_k2[l]])
    d_out = _diff_attention(dq, dk, dvt, lam_params, g_diff_subln[l][None, :])
    frow = fcum.reshape(B, FOX_HEADS // 2, 2, S)
    f_out = _fox_attention(fq, fk, fvt, frow)

    w_o = w_out[l].astype(BF16)
    zpad = jnp.zeros((SUBLANES - N_GROUPS, D), F32)
    w_rt = jnp.concatenate([w_group[l].T, zpad, w_router[l].T], axis=0)
    b_rt = jnp.concatenate([b_group[l], jnp.zeros((SUBLANES - N_GROUPS,), F32), b_router[l]])
    b_rt = jnp.broadcast_to(b_rt[:, None], (SUBLANES + N_EXPERTS, LANES))
    x1, h2, ri, rw, cnt = _route(d_out, f_out, w_o[:SEC], w_o[SEC:], x, mod,
                                 g_post_attn[l][None, :], g_pre_ffn[l][None, :], w_rt, b_rt)

    counts = cnt[:, 0].astype(jnp.int32)
    padded = ((counts + BM - 1) // BM) * BM
    pend = jnp.cumsum(padded)
    pstart = pend - padded
    eids = jnp.arange(N_EXPERTS, dtype=jnp.int32)
    dest = ri[2:4] + jnp.sum(jnp.where(ri[0:2, :, None] == eids, pstart, 0), axis=-1)
    nb = (2 * T) // BM + N_EXPERTS
    n_used = (pend[-1] // BM).astype(jnp.int32).reshape(1)
    blk0 = jnp.arange(nb, dtype=jnp.int32) * BM
    block_e = jnp.minimum(jnp.sum((blk0[:, None] >= pend[None, :]).astype(jnp.int32), axis=1),
                          N_EXPERTS - 1)
    cend = pstart + counts
    blk_end = jnp.sum(jnp.where(block_e[:, None] == eids, cend, 0), axis=-1)
    n_valid = jnp.clip(blk_end - blk0, 0, BM).astype(jnp.int32)

    xs = _dispatch(dest, h2.reshape(T, D // 2), nb * BM)
    ys = _experts(block_e, n_used, n_valid, xs, w_gate[l], w_up[l], w_down[l])
    out = None
    tc = T // COMBINE_CHUNKS
    for ck in range(COMBINE_CHUNKS):
        g = _gather_rows(ys, dest[:, ck * tc:(ck + 1) * tc])
        out = _combine(g, rw, x1.reshape(T, D), mod, g_post_ffn[l][None, :], S, ck, out)
    return out.reshape(B, S, D)
```

```python
import functools
import math

import jax
import jax.numpy as jnp
from jax import lax
from jax.experimental import pallas as pl
from jax.experimental.pallas import tpu as pltpu
from jax.experimental.pallas import tpu_sc as plsc

F32 = jnp.float32
BF16 = jnp.bfloat16

D_MODEL = 1024
HEAD_DIM = 64
CHUNK = 64
DIFF_HEADS = 4
FOX_HEADS = 8
SEC = 512
N_GROUPS = 4
EXPERTS_PER_GROUP = 8
N_EXPERTS = 32
D_EXPERT = 512
EPS = 1e-6
NEG_INF = -1e30
LOG2E = 1.4426950408889634
LAMBDA_INIT = 0.8 - 0.6 * math.exp(-0.3 * 0)

LANES = 128
SUBLANES = 8
BF16_SUBLANES = 16
TN_ADA = 1024
TM_PROJ = 1024
TQ = 512
TK = 512
HK = TK // 2
ATT_GROUPS = 1
DV_ROWS = 2 * HEAD_DIM + BF16_SUBLANES
TM_ROUTE = 512
ROUTE_ROW_GROUPS = 4
BM = 512
EXPERT_ROW_GROUPS = 2
TM_ROWS = 512
COMBINE_CHUNKS = 4
SC_CORES = 2
SC_SUBCORES = 16
SC_WORKERS = SC_CORES * SC_SUBCORES
SC_WIN = 128
VMEM_LIMIT = 48 * 1024 * 1024
_NT = (((1,), (1,)), ((), ()))


def _cparams(sem, flags=None):
    return pltpu.CompilerParams(dimension_semantics=sem, vmem_limit_bytes=VMEM_LIMIT, flags=flags)


def _rms(x, g):
    return x * lax.rsqrt(jnp.mean(x * x, axis=-1, keepdims=True) + EPS) * g


def _pack_halves(x):
    m = x.shape[1] // 2
    bits = lax.bitcast_convert_type(x.astype(BF16).astype(F32), jnp.uint32)
    return (bits[:, :m] >> 16) | (bits[:, m:] & jnp.uint32(0xFFFF0000))


def _unpack_halves(w):
    lo = lax.bitcast_convert_type(w << 16, F32)
    hi = lax.bitcast_convert_type(w & jnp.uint32(0xFFFF0000), F32)
    return lo, hi


def _rows_to_columns(x):
    pad = jnp.zeros((LANES - x.shape[0], LANES), x.dtype)
    return jnp.concatenate([x, pad], axis=0).T


def _ada_kernel(c_ref, w_ref, b_ref, o_ref):
    c = c_ref[...]
    cs = c * jax.nn.sigmoid(c)
    o_ref[...] = jnp.dot(cs, w_ref[...], preferred_element_type=F32,
                         precision=lax.Precision.HIGHEST) + b_ref[...]


def _ada(c, w_ada, b_ada):
    B, D = c.shape
    N = w_ada.shape[1]
    tn = TN_ADA
    return pl.pallas_call(
        _ada_kernel,
        grid=(N // tn,),
        in_specs=[pl.BlockSpec((B, D), lambda j: (0, 0)),
                  pl.BlockSpec((D, tn), lambda j: (0, j)),
                  pl.BlockSpec((1, tn), lambda j: (0, j))],
        out_specs=pl.BlockSpec((B, tn), lambda j: (0, j)),
        out_shape=jax.ShapeDtypeStruct((B, N), F32),
        compiler_params=_cparams(("arbitrary",)),
        name="ada_mod",
    )(c, w_ada, b_ada.reshape(1, N))


def _split3(x):
    hi = x.astype(BF16)
    r1 = x - hi.astype(F32)
    mid = r1.astype(BF16)
    lo = (r1 - mid.astype(F32)).astype(BF16)
    return hi, mid, lo


def _inproj_kernel(x_ref, mod_ref, g_ref, w_ref, wvt_ref, bf_ref, cos_ref, sin_ref,
                   dq_ref, dk_ref, fq_ref, fk_ref, dvt_ref, fvt_ref, fcum_ref, carry_ref):
    si = pl.program_id(1)
    x = x_ref[...]
    h = _rms(x, g_ref[...] * (1.0 + mod_ref[1:2, :])) + mod_ref[0:1, :]
    hb = h.astype(BF16)
    tm = x.shape[0]

    cos = cos_ref[...]
    sin = sin_ref[...]
    lane = lax.broadcasted_iota(jnp.int32, (tm, LANES), 1)
    first_half = (lane % HEAD_DIM) < (HEAD_DIM // 2)
    qscale = HEAD_DIM ** -0.5 * LOG2E

    def proj(sec):
        return jnp.dot(hb, w_ref[:, sec * SEC:(sec + 1) * SEC], preferred_element_type=F32)

    def rope_store(p, out_ref, scale):
        for j in range(SEC // LANES):
            ch = p[:, j * LANES:(j + 1) * LANES]
            partner = jnp.where(first_half,
                                pltpu.roll(ch, LANES - HEAD_DIM // 2, 1),
                                pltpu.roll(ch, HEAD_DIM // 2, 1))
            r = ch * cos + partner * sin
            if scale != 1.0:
                r = r * scale
            out_ref[:, j * LANES:(j + 1) * LANES] = r.astype(out_ref.dtype)

    rope_store(proj(0), dq_ref, qscale)
    rope_store(proj(1), dk_ref, 1.0)
    fq_ref[...] = (proj(2) * qscale).astype(fq_ref.dtype)
    fk_ref[...] = proj(3).astype(fk_ref.dtype)
    vt = lax.dot_general(wvt_ref[...], hb, _NT, preferred_element_type=F32)
    for hd in range(DIFF_HEADS):
        dvt_ref[hd, 0:LANES, :] = vt[hd * LANES:(hd + 1) * LANES].astype(dvt_ref.dtype)
        dvt_ref[hd, LANES:DV_ROWS, :] = jnp.ones((DV_ROWS - LANES, tm), dvt_ref.dtype)
    fvt_ref[...] = vt[SEC:2 * SEC].astype(fvt_ref.dtype)

    z = vt[2 * SEC:2 * SEC + FOX_HEADS] + bf_ref[:, 0:1]
    lf = (jnp.minimum(z, 0.0) - jnp.log1p(jnp.exp(-jnp.abs(z)))) * LOG2E

    @pl.when(si == 0)
    def _():
        carry_ref[...] = jnp.zeros_like(carry_ref)

    r_i = lax.broadcasted_iota(jnp.int32, (LANES, LANES), 0)
    c_i = lax.broadcasted_iota(jnp.int32, (LANES, LANES), 1)
    tri = jnp.where(r_i <= c_i, 1.0, 0.0).astype(BF16)
    run = carry_ref[...]
    for j in range(tm // LANES):
        hi, mid, lo = _split3(lf[:, j * LANES:(j + 1) * LANES])
        cs = (jnp.dot(hi, tri, preferred_element_type=F32)
              + jnp.dot(mid, tri, preferred_element_type=F32)
              + jnp.dot(lo, tri, preferred_element_type=F32)) + run
        fcum_ref[:, j * LANES:(j + 1) * LANES] = cs
        run = jnp.broadcast_to(cs[:, LANES - 1:LANES], run.shape)
    carry_ref[...] = run


def _inproj(x, mod, g_pre, w_main, w_vt, bf, cos, sin):
    B, S, D = x.shape
    tm = TM_PROJ
    sec_out = jax.ShapeDtypeStruct((B, S, SEC), BF16)
    sec_spec = pl.BlockSpec((None, tm, SEC), lambda b, s: (b, s, 0))
    t_out = jax.ShapeDtypeStruct((B, SEC, S), BF16)
    t_spec = pl.BlockSpec((None, SEC, tm), lambda b, s: (b, 0, s))
    return pl.pallas_call(
        _inproj_kernel,
        grid=(B, S // tm),
        in_specs=[pl.BlockSpec((None, tm, D), lambda b, s: (b, s, 0)),
                  pl.BlockSpec((None, 6, D), lambda b, s: (b, 0, 0)),
                  pl.BlockSpec((1, D), lambda b, s: (0, 0)),
                  pl.BlockSpec((D, 4 * SEC), lambda b, s: (0, 0)),
                  pl.BlockSpec(w_vt.shape, lambda b, s: (0, 0)),
                  pl.BlockSpec((FOX_HEADS, LANES), lambda b, s: (0, 0)),
                  pl.BlockSpec((tm, LANES), lambda b, s: (s, 0)),
                  pl.BlockSpec((tm, LANES), lambda b, s: (s, 0))],
        out_specs=[sec_spec] * 4
        + [pl.BlockSpec((None, DIFF_HEADS, DV_ROWS, tm), lambda b, s: (b, 0, 0, s)), t_spec,
           pl.BlockSpec((None, FOX_HEADS, tm), lambda b, s: (b, 0, s))],
        out_shape=[sec_out] * 4
        + [jax.ShapeDtypeStruct((B, DIFF_HEADS, DV_ROWS, S), BF16), t_out,
           jax.ShapeDtypeStruct((B, FOX_HEADS, S), F32)],
        scratch_shapes=[pltpu.VMEM((FOX_HEADS, LANES), F32)],
        compiler_params=_cparams(("arbitrary", "arbitrary")),
        name="in_proj",
    )(x, mod, g_pre, w_main, w_vt, bf, cos, sin)


def _scores(ch, r0, qm=None):
    r0 = pl.multiple_of(r0, HK)
    t = lax.dot_general(ch["keys"](r0), ch["qm"] if qm is None else qm, _NT,
                        preferred_element_type=F32)
    if ch["kbias"] is not None:
        t = t - ch["kbias"](r0)
    return t


def _flash_t(chains, sa_ref, m_ref, acc_ref, qi, diag_mask, next_qm):
    def update(ci, ch, t, r0, cols=slice(None)):
        m = m_ref[ci, :, cols]
        mt = jnp.max(t, axis=0, keepdims=True)
        qb = ch["qbias"]
        qb = None if qb is None else qb[:, cols]
        m_new = jnp.maximum(m, mt if qb is None else mt + qb)
        alpha = jnp.exp2(m - m_new)
        p = jnp.exp2((t - (m_new if qb is None else m_new - qb)).astype(BF16))
        vt = ch["vt"](pl.multiple_of(r0, HK))
        m_ref[ci, :, cols] = m_new
        acc_ref[ci, :, cols] = (alpha * acc_ref[ci, :, cols]
                                + jnp.dot(vt, p, preferred_element_type=F32))

    def full_tiles(k0, n):
        tb = [_scores(ch, k0 + HK) for ch in chains]
        for t in range(n):
            kt = k0 + t * TK
            for ci, ch in enumerate(chains):
                update(ci, ch, sa_ref[ci], kt)
            for ci, ch in enumerate(chains):
                sa_ref[ci] = _scores(ch, kt + TK)
            tb_next = [_scores(ch, kt + TK + HK) for ch in chains] if t + 1 < n else None
            for ci, ch in enumerate(chains):
                update(ci, ch, tb[ci], kt + HK)
            tb = tb_next

    def tile_pair(kj, carry):
        full_tiles(kj * (2 * TK), 2)
        return carry

    m_ref[...] = jnp.full(m_ref.shape, NEG_INF, F32)
    acc_ref[...] = jnp.zeros(acc_ref.shape, F32)
    lax.fori_loop(0, qi // 2, tile_pair, 0)

    @pl.when(qi % 2 == 1)
    def _():
        full_tiles((qi - 1) * TK, 1)
    k0 = qi * TK
    hq = TQ - HK
    tb = [_scores(ch, k0 + HK, ch["qm"][hq:]) for ch in chains]
    nxt = [_scores(ch, 0, qm) for ch, qm in zip(chains, next_qm)]
    for ci, ch in enumerate(chains):
        update(ci, ch, jnp.where(diag_mask[0:HK], sa_ref[ci], NEG_INF), k0)
    for ci, ch in enumerate(chains):
        update(ci, ch, jnp.where(diag_mask[HK:TK, hq:], tb[ci], NEG_INF), k0 + HK,
               slice(hq, TQ))
    for ci in range(len(chains)):
        sa_ref[ci] = nxt[ci]
    return tuple(acc_ref[ci] for ci in range(len(chains)))


def _next_tile_start(q0, seq_len):
    return pl.multiple_of(jnp.minimum(q0 + TQ, seq_len - TQ), TQ)


def _for_each_query_tile(n_tiles, tile_fn):
    def body(qi, carry):
        tile_fn(qi, pl.multiple_of(qi * TQ, TQ))
        return carry

    lax.fori_loop(0, n_tiles, body, 0)


def _lane_group(ref, rows, g):
    return ref[rows, g * LANES:(g + 1) * LANES]


def _masked_queries(q_ref, q0):
    lane = lax.broadcasted_iota(jnp.int32, (TQ, LANES), 1)
    out = []
    for g in range(ATT_GROUPS):
        q2 = _lane_group(q_ref, pl.ds(q0, TQ), g)
        out += [jnp.where((lane >= j * HEAD_DIM) & (lane < (j + 1) * HEAD_DIM), q2,
                          jnp.zeros_like(q2)) for j in range(2)]
    return out


def _fox_kernel(q_ref, k_ref, vt_ref, frow_ref, o_ref, sa_ref, m_ref, acc_ref, fcol_ref):
    for g in range(ATT_GROUPS):
        for j in range(frow_ref.shape[2] // LANES):
            fcol_ref[g, j * LANES:(j + 1) * LANES, :] = _rows_to_columns(
                frow_ref[g, :, j * LANES:(j + 1) * LANES])

    vrow = lax.broadcasted_iota(jnp.int32, (LANES, HK), 0)
    r = lax.broadcasted_iota(jnp.int32, (TK, TQ), 0)
    c = lax.broadcasted_iota(jnp.int32, (TK, TQ), 1)

    def make_chains(q0):
        chains = []
        for ci, qm in enumerate(_masked_queries(q_ref, q0)):
            g, j = divmod(ci, 2)
            v_rows = (vrow >= j * HEAD_DIM) & (vrow < (j + 1) * HEAD_DIM)

            def vt(k0, g=g, v_rows=v_rows):
                v = vt_ref[g * LANES:(g + 1) * LANES, pl.ds(k0, HK)]
                return jnp.where(v_rows, v, jnp.ones_like(v))

            chains.append(dict(
                qm=qm, keys=lambda r0, g=g: _lane_group(k_ref, pl.ds(r0, HK), g),
                kbias=lambda r0, g=g, j=j: fcol_ref[g, pl.ds(r0, HK), j:j + 1],
                qbias=frow_ref[g, j:j + 1, pl.ds(q0, TQ)], vt=vt, rows=LANES))
        return chains

    def tile(qi, q0):
        res = _flash_t(make_chains(q0), sa_ref, m_ref, acc_ref, qi, r <= c,
                       _masked_queries(q_ref, _next_tile_start(q0, q_ref.shape[0])))
        h = HEAD_DIM
        for g in range(ATT_GROUPS):
            a0, a1 = res[2 * g], res[2 * g + 1]
            o_t = jnp.concatenate([a0[0:h] / a0[h:2 * h], a1[h:2 * h] / a1[0:h]], axis=0)
            o_ref[pl.ds(q0, TQ), g * LANES:(g + 1) * LANES] = o_t.T.astype(o_ref.dtype)

    for ci, ch in enumerate(make_chains(0)):
        sa_ref[ci] = _scores(ch, 0)
    _for_each_query_tile(q_ref.shape[0] // TQ, tile)


def _fox_attention(fq, fk, fvt, frow):
    B, S, _ = fq.shape
    gw = ATT_GROUPS * LANES
    seq_spec = pl.BlockSpec((None, S, gw), lambda b, p: (b, 0, p))
    return pl.pallas_call(
        _fox_kernel,
        grid=(B, FOX_HEADS // 2 // ATT_GROUPS),
        in_specs=[seq_spec, seq_spec,
                  pl.BlockSpec((None, gw, S), lambda b, p: (b, p, 0)),
                  pl.BlockSpec((None, ATT_GROUPS, 2, S), lambda b, p: (b, p, 0, 0))],
        out_specs=seq_spec,
        out_shape=jax.ShapeDtypeStruct((B, S, SEC), BF16),
        scratch_shapes=[pltpu.VMEM((2 * ATT_GROUPS, HK, TQ), F32),
                        pltpu.VMEM((2 * ATT_GROUPS, 1, TQ), F32),
                        pltpu.VMEM((2 * ATT_GROUPS, LANES, TQ), F32),
                        pltpu.VMEM((ATT_GROUPS, S, LANES), F32)],
        compiler_params=_cparams(("arbitrary", "arbitrary")),
        name="fox_attention",
    )(fq, fk, fvt, frow)


def _diff_kernel(q_ref, k_ref, vt_ref, lam_ref, g_ref, o_ref, sa_ref, m_ref, acc_ref):
    r = lax.broadcasted_iota(jnp.int32, (TK, TQ), 0)
    c = lax.broadcasted_iota(jnp.int32, (TK, TQ), 1)
    lp = lam_ref[...]
    lam = (jnp.exp(jnp.sum(lp[0:1] * lp[1:2], axis=1, keepdims=True))
           - jnp.exp(jnp.sum(lp[2:3] * lp[3:4], axis=1, keepdims=True)) + LAMBDA_INIT)
    gain_col = _rows_to_columns(g_ref[...] * (1.0 - LAMBDA_INIT))[:, 0:1]

    def make_chains(q0):
        return [dict(qm=qm, keys=lambda r0, g=ci // 2: _lane_group(k_ref, pl.ds(r0, HK), g),
                     kbias=None, qbias=None,
                     vt=lambda k0, g=ci // 2: vt_ref[g, :, pl.ds(k0, HK)], rows=DV_ROWS)
                for ci, qm in enumerate(_masked_queries(q_ref, q0))]

    def tile(qi, q0):
        res = _flash_t(make_chains(q0), sa_ref, m_ref, acc_ref, qi,
                       (r // CHUNK) <= (c // CHUNK),
                       _masked_queries(q_ref, _next_tile_start(q0, q_ref.shape[0])))
        v = LANES
        for g in range(ATT_GROUPS):
            a0, a1 = res[2 * g], res[2 * g + 1]
            o_t = a0[0:v] / a0[v:v + 1] - lam * (a1[0:v] / a1[v:v + 1])
            inv = lax.rsqrt(jnp.mean(o_t * o_t, axis=0, keepdims=True) + EPS)
            o_t = o_t * inv * gain_col
            o_ref[pl.ds(q0, TQ), g * LANES:(g + 1) * LANES] = o_t.T.astype(o_ref.dtype)

    for ci, ch in enumerate(make_chains(0)):
        sa_ref[ci] = _scores(ch, 0)
    _for_each_query_tile(q_ref.shape[0] // TQ, tile)


def _diff_attention(dq, dk, dvt, lam_params, g_subln):
    B, S, _ = dq.shape
    seq_spec = pl.BlockSpec((None, S, ATT_GROUPS * LANES), lambda b, h: (b, 0, h))
    return pl.pallas_call(
        _diff_kernel,
        grid=(B, DIFF_HEADS // ATT_GROUPS),
        in_specs=[seq_spec, seq_spec,
                  pl.BlockSpec((None, ATT_GROUPS, DV_ROWS, S), lambda b, h: (b, h, 0, 0)),
                  pl.BlockSpec((4, HEAD_DIM), lambda b, h: (0, 0)),
                  pl.BlockSpec((1, LANES), lambda b, h: (0, 0))],
        out_specs=seq_spec,
        out_shape=jax.ShapeDtypeStruct((B, S, SEC), BF16),
        scratch_shapes=[pltpu.VMEM((2 * ATT_GROUPS, HK, TQ), F32),
                        pltpu.VMEM((2 * ATT_GROUPS, 1, TQ), F32),
                        pltpu.VMEM((2 * ATT_GROUPS, DV_ROWS, TQ), F32)],
        compiler_params=_cparams(("arbitrary", "arbitrary")),
        name="diff_attention",
    )(dq, dk, dvt, lam_params, g_subln)


def _route_kernel(d_ref, f_ref, wd_ref, wf_ref, x_ref, mod_ref, gpa_ref, gpf_ref, wr_ref, br_ref,
                  x1_ref, h2_ref, ri_ref, rw_ref, cnt_ref):
    first = (pl.program_id(0) == 0) & (pl.program_id(1) == 0)

    @pl.when(first)
    def _():
        cnt_ref[...] = jnp.zeros_like(cnt_ref)

    tm = x_ref.shape[0]
    rows = tm // ROUTE_ROW_GROUPS
    mixed = [jnp.dot(d_ref[r0:r0 + rows, :], wd_ref[...], preferred_element_type=F32)
             + jnp.dot(f_ref[r0:r0 + rows, :], wf_ref[...], preferred_element_type=F32)
             for r0 in range(0, tm, rows)]
    wr = wr_ref[...].astype(BF16)
    gate_gain = mod_ref[2:3, :] * gpa_ref[...]
    ffn_gain = gpf_ref[...] * (1.0 + mod_ref[4:5, :])
    lts = []
    for gi, r0 in enumerate(range(0, tm, rows)):
        x1 = x_ref[r0:r0 + rows, :] + _rms(mixed[gi], gate_gain)
        x1_ref[r0:r0 + rows, :] = x1
        h2 = _rms(x1, ffn_gain) + mod_ref[3:4, :]
        h2_ref[r0:r0 + rows, :] = _pack_halves(h2)
        lts.append(br_ref[:, 0:1] + lax.dot_general(wr, h2.astype(BF16), _NT,
                                                    preferred_element_type=F32))
    lt = jnp.concatenate(lts, axis=1)

    g = lt[0:N_GROUPS]
    row4 = lax.broadcasted_iota(jnp.int32, g.shape, 0)
    gmax = jnp.max(g, axis=0, keepdims=True)
    gidx = jnp.min(jnp.where(g == gmax, row4, N_GROUPS), axis=0, keepdims=True)
    gw = 1.0 / jnp.sum(jnp.exp(g - gmax), axis=0, keepdims=True)
    esel = jnp.zeros((EXPERTS_PER_GROUP, tm), F32)
    for gg in range(N_GROUPS):
        e0 = SUBLANES + EXPERTS_PER_GROUP * gg
        esel = jnp.where(gidx == gg, lt[e0:e0 + EXPERTS_PER_GROUP], esel)
    row8 = lax.broadcasted_iota(jnp.int32, esel.shape, 0)
    v1 = jnp.max(esel, axis=0, keepdims=True)
    i1 = jnp.min(jnp.where(esel == v1, row8, EXPERTS_PER_GROUP), axis=0, keepdims=True)
    esel2 = jnp.where(row8 == i1, -jnp.inf, esel)
    v2 = jnp.max(esel2, axis=0, keepdims=True)
    i2 = jnp.min(jnp.where(esel2 == v2, row8, EXPERTS_PER_GROUP), axis=0, keepdims=True)
    e = jnp.exp(v2 - v1)
    w1 = gw / (1.0 + e)
    w2 = gw * e / (1.0 + e)
    e1 = gidx * EXPERTS_PER_GROUP + i1
    e2 = gidx * EXPERTS_PER_GROUP + i2

    row32 = lax.broadcasted_iota(jnp.int32, (N_EXPERTS, tm), 0)
    oh1 = row32 == e1
    oh2 = row32 == e2
    oh = jnp.where(oh1 | oh2, 1.0, 0.0)
    t_r = lax.broadcasted_iota(jnp.int32, (tm, tm), 0)
    t_c = lax.broadcasted_iota(jnp.int32, (tm, tm), 1)
    tri = jnp.where(t_r < t_c, 1.0, 0.0).astype(BF16)
    tot = jnp.dot(oh.astype(BF16), tri, preferred_element_type=F32) + cnt_ref[:, 0:1]
    r1 = jnp.sum(jnp.where(oh1, tot, 0.0), axis=0, keepdims=True)
    r2 = jnp.sum(jnp.where(oh2, tot, 0.0), axis=0, keepdims=True)
    cnt_ref[...] = cnt_ref[...] + jnp.sum(oh, axis=1, keepdims=True)

    zi = jnp.zeros((SUBLANES - 4, tm), jnp.int32)
    ri_ref[...] = jnp.concatenate([e1, e2, r1.astype(jnp.int32), r2.astype(jnp.int32), zi], axis=0)
    rw_ref[...] = jnp.concatenate([w1, w2, jnp.zeros((SUBLANES - 2, tm), F32)], axis=0)


def _route(d_out, f_out, w_out_d, w_out_f, x, mod, g_post_attn, g_pre_ffn, w_rt, b_rt):
    B, S, D = x.shape
    tm = TM_ROUTE
    nrt = w_rt.shape[0]
    n_s = S // tm
    T = B * S
    return pl.pallas_call(
        _route_kernel,
        grid=(B, n_s),
        in_specs=[pl.BlockSpec((None, tm, SEC), lambda b, s: (b, s, 0)),
                  pl.BlockSpec((None, tm, SEC), lambda b, s: (b, s, 0)),
                  pl.BlockSpec((SEC, D), lambda b, s: (0, 0)),
                  pl.BlockSpec((SEC, D), lambda b, s: (0, 0)),
                  pl.BlockSpec((None, tm, D), lambda b, s: (b, s, 0)),
                  pl.BlockSpec((None, 6, D), lambda b, s: (b, 0, 0)),
                  pl.BlockSpec((1, D), lambda b, s: (0, 0)),
                  pl.BlockSpec((1, D), lambda b, s: (0, 0)),
                  pl.BlockSpec((nrt, D), lambda b, s: (0, 0)),
                  pl.BlockSpec((nrt, LANES), lambda b, s: (0, 0))],
        out_specs=[pl.BlockSpec((None, tm, D), lambda b, s: (b, s, 0)),
                   pl.BlockSpec((None, tm, D // 2), lambda b, s: (b, s, 0)),
                   pl.BlockSpec((SUBLANES, tm), lambda b, s: (0, b * n_s + s)),
                   pl.BlockSpec((SUBLANES, tm), lambda b, s: (0, b * n_s + s)),
                   pl.BlockSpec((N_EXPERTS, LANES), lambda b, s: (0, 0))],
        out_shape=[jax.ShapeDtypeStruct((B, S, D), F32),
                   jax.ShapeDtypeStruct((B, S, D // 2), jnp.uint32),
                   jax.ShapeDtypeStruct((SUBLANES, T), jnp.int32),
                   jax.ShapeDtypeStruct((SUBLANES, T), F32),
                   jax.ShapeDtypeStruct((N_EXPERTS, LANES), F32)],
        compiler_params=_cparams(("arbitrary", "arbitrary")),
        name="out_proj_route",
    )(d_out, f_out, w_out_d, w_out_f, x, mod, g_post_attn, g_pre_ffn, w_rt, b_rt)


def _sc_mesh():
    return plsc.VectorSubcoreMesh(core_axis_name="c", subcore_axis_name="s",
                                  num_cores=SC_CORES, num_subcores=SC_SUBCORES)


def _sc_worker_base(per_worker):
    return (lax.axis_index("s") * SC_CORES + lax.axis_index("c")) * per_worker


def _dispatch(dest, h2, n_rows):
    T, D = h2.shape
    per_worker = T // SC_WORKERS

    @functools.partial(
        pl.kernel, mesh=_sc_mesh(),
        out_type=jax.ShapeDtypeStruct((n_rows, D), h2.dtype),
        scratch_types=[pltpu.VMEM((SC_WIN,), jnp.int32), pltpu.VMEM((SC_WIN, D), h2.dtype)],
        name="moe_dispatch")
    def run(h_hbm, dest_hbm, xs_hbm, idx_v, rows_v):
        base = _sc_worker_base(per_worker)

        @pl.loop(0, per_worker // SC_WIN)
        def _(ci):
            off = pl.multiple_of(base + ci * SC_WIN, SC_WIN)
            pltpu.sync_copy(h_hbm.at[pl.ds(off, SC_WIN)], rows_v)
            for k in range(2):
                pltpu.sync_copy(dest_hbm.at[pl.ds(k * T + off, SC_WIN)], idx_v)
                pltpu.sync_copy(rows_v, xs_hbm.at[idx_v])

    return run(h2, dest.reshape(2 * T))


def _gather_rows(ys, dest):
    _, D = ys.shape
    K, T = dest.shape
    n = K * T
    per_worker = n // SC_WORKERS

    @functools.partial(
        pl.kernel, mesh=_sc_mesh(),
        out_type=jax.ShapeDtypeStruct((n, D), ys.dtype),
        scratch_types=[pltpu.VMEM((SC_WIN,), jnp.int32), pltpu.VMEM((SC_WIN, D), ys.dtype)],
        name="moe_gather")
    def run(ys_hbm, dest_hbm, g_hbm, idx_v, rows_v):
        base = _sc_worker_base(per_worker)

        @pl.loop(0, per_worker // SC_WIN)
        def _(ci):
            off = pl.multiple_of(base + ci * SC_WIN, SC_WIN)
            pltpu.sync_copy(dest_hbm.at[pl.ds(off, SC_WIN)], idx_v)
            pltpu.sync_copy(ys_hbm.at[idx_v], rows_v)
            pltpu.sync_copy(rows_v, g_hbm.at[pl.ds(off, SC_WIN)])

    return run(ys, dest.reshape(n)).reshape(K, T, D)


def _expert_kernel(be_ref, nu_ref, nv_ref, ia_ref, ib_ref, par_ref, xs_ref,
                   wga_ref, wua_ref, wda_ref, wgb_ref, wub_ref, wdb_ref, ys_ref,
                   wgb, wub, wdb):
    i = pl.program_id(0)

    @pl.when(i < nu_ref[0])
    def _():
        prev = be_ref[jnp.maximum(i - 1, 0)]
        changed = (i == 0) | (be_ref[i] != prev)

        @pl.when(changed & (par_ref[i] == 0))
        def _():
            wgb[...] = wga_ref[...].astype(BF16)
            wub[...] = wua_ref[...].astype(BF16)
            wdb[...] = wda_ref[...].astype(BF16)

        @pl.when(changed & (par_ref[i] == 1))
        def _():
            wgb[...] = wgb_ref[...].astype(BF16)
            wub[...] = wub_ref[...].astype(BF16)
            wdb[...] = wdb_ref[...].astype(BF16)

        half = xs_ref.shape[1]
        rows = xs_ref.shape[0] // EXPERT_ROW_GROUPS
        row = lax.broadcasted_iota(jnp.int32, (rows, half), 0)
        au = []
        for r0 in range(0, xs_ref.shape[0], rows):
            words = jnp.where(row + r0 < nv_ref[i], xs_ref[r0:r0 + rows, :], jnp.uint32(0))
            lo, hi = (v.astype(BF16) for v in _unpack_halves(words))

            def proj(w, lo=lo, hi=hi):
                return (jnp.dot(lo, w[0:half, :], preferred_element_type=F32)
                        + jnp.dot(hi, w[half:2 * half, :], preferred_element_type=F32))

            au.append((proj(wgb), proj(wub)))
        for gi, (a, u) in enumerate(au):
            hmid = (a * jax.nn.sigmoid(a) * u).astype(BF16)
            y = jnp.dot(hmid, wdb[...], preferred_element_type=F32)
            ys_ref[gi * rows:(gi + 1) * rows, :] = _pack_halves(y)

    @pl.when(i >= nu_ref[0])
    def _():
        ys_ref[...] = jnp.zeros_like(ys_ref)


def _experts(block_e, n_used, n_valid, idx_a, idx_b, parity, xs, w_gate, w_up, w_down):
    P, Dh = xs.shape
    nb = P // BM
    _, D, Fd = w_gate.shape

    def row_map(i, be, nu, nv, ia, ib, par):
        return (jnp.minimum(i, nu[0] - 1), 0)

    def a_map(i, be, nu, nv, ia, ib, par):
        return (ia[i], 0, 0)

    def b_map(i, be, nu, nv, ia, ib, par):
        return (ib[i], 0, 0)

    return pl.pallas_call(
        _expert_kernel,
        grid_spec=pltpu.PrefetchScalarGridSpec(
            num_scalar_prefetch=6,
            grid=(nb,),
            in_specs=[pl.BlockSpec((BM, Dh), row_map),
                      pl.BlockSpec((None, D, Fd), a_map),
                      pl.BlockSpec((None, D, Fd), a_map),
                      pl.BlockSpec((None, Fd, D), a_map),
                      pl.BlockSpec((None, D, Fd), b_map),
                      pl.BlockSpec((None, D, Fd), b_map),
                      pl.BlockSpec((None, Fd, D), b_map)],
            out_specs=pl.BlockSpec((BM, Dh), lambda i, be, nu, nv, ia, ib, par: (i, 0)),
            scratch_shapes=[pltpu.VMEM((D, Fd), BF16), pltpu.VMEM((D, Fd), BF16),
                            pltpu.VMEM((Fd, D), BF16)]),
        out_shape=jax.ShapeDtypeStruct((P, Dh), jnp.uint32),
        compiler_params=_cparams(("arbitrary",)),
        name="moe_experts",
    )(block_e, n_used, n_valid, idx_a, idx_b, parity, xs, w_gate, w_up, w_down, w_gate, w_up,
      w_down)


def _combine_kernel(g_ref, w_ref, x1_ref, mod_ref, g_post_ref, *rest):
    o_ref = rest[-1]
    lo0, hi0 = _unpack_halves(g_ref[0])
    lo1, hi1 = _unpack_halves(g_ref[1])
    tm = lo0.shape[0]
    wt = jnp.concatenate([_rows_to_columns(w_ref[:, j * LANES:(j + 1) * LANES])
                          for j in range(tm // LANES)], axis=0)
    w0, w1 = wt[:, 0:1], wt[:, 1:2]
    y = jnp.concatenate([w0 * lo0 + w1 * lo1, w0 * hi0 + w1 * hi1], axis=1)
    o_ref[...] = x1_ref[...] + _rms(y, mod_ref[5:6, :] * g_post_ref[...])


def _combine(g, wts, x1, mod, g_post_ffn, S, chunk, prev_out):
    T, D = x1.shape
    tm = TM_ROWS
    per_b = S // tm
    steps = g.shape[1] // tm
    off = chunk * steps
    in_specs = [pl.BlockSpec((2, tm, D // 2), lambda i: (0, i, 0)),
                pl.BlockSpec((SUBLANES, tm), lambda i: (0, i + off)),
                pl.BlockSpec((tm, D), lambda i: (i + off, 0)),
                pl.BlockSpec((None, 6, D), lambda i: ((i + off) // per_b, 0, 0)),
                pl.BlockSpec((1, D), lambda i: (0, 0))]
    args = [g, wts, x1, mod, g_post_ffn]
    aliases = {}
    if prev_out is not None:
        in_specs.append(pl.BlockSpec(memory_space=pl.ANY))
        args.append(prev_out)
        aliases = {len(args) - 1: 0}
    return pl.pallas_call(
        _combine_kernel,
        grid=(steps,),
        in_specs=in_specs,
        out_specs=pl.BlockSpec((tm, D), lambda i: (i + off, 0)),
        out_shape=jax.ShapeDtypeStruct((T, D), F32),
        input_output_aliases=aliases,
        compiler_params=_cparams(("arbitrary",)),
        name="moe_combine",
    )(*args)


def _rope_tables(S):
    inv = 1.0 / (10000.0 ** (jnp.arange(0, HEAD_DIM, 2, dtype=F32) / HEAD_DIM))
    ang = jnp.arange(S, dtype=F32)[:, None] * inv[None, :]
    c, s = jnp.cos(ang), jnp.sin(ang)
    return jnp.tile(c, (1, 4)), jnp.tile(jnp.concatenate([-s, s], axis=1), (1, 2))


def kernel(x, c, w_ada, b_ada, g_pre_attn, g_post_attn, w_in, b_forget, lambda_q1, lambda_k1,
           lambda_q2, lambda_k2, g_diff_subln, w_out, g_pre_ffn, g_post_ffn, w_group, b_group,
           w_router, b_router, w_gate, w_up, w_down):
    B, S, D = x.shape
    T = B * S
    l = 0
    mod = _ada(c, w_ada[l], b_ada[l]).reshape(B, 6, D)

    wi = w_in[l]
    w_main = jnp.concatenate([wi[:, 0:2 * SEC], wi[:, 3 * SEC:5 * SEC]], axis=1).astype(BF16)
    w_vt = jnp.concatenate([wi[:, 2 * SEC:3 * SEC], wi[:, 5 * SEC:6 * SEC], wi[:, 6 * SEC:],
                            jnp.zeros((D, BF16_SUBLANES - FOX_HEADS), F32)], axis=1).T.astype(BF16)
    bf = jnp.broadcast_to(b_forget[l][:, None], (FOX_HEADS, LANES))
    cos, sin = _rope_tables(S)
    dq, dk, fq, fk, dvt, fvt, fcum = _inproj(x, mod, g_pre_attn[l][None, :], w_main, w_vt,
                                             bf, cos, sin)

    lam_params = jnp.stack([lambda_q1[l], lambda_k1[l], lambda_q2[l], lambda_k2[l]])
    d_out = _diff_attention(dq, dk, dvt, lam_params, g_diff_subln[l][None, :])
    frow = fcum.reshape(B, FOX_HEADS // 2, 2, S)
    f_out = _fox_attention(fq, fk, fvt, frow)

    w_o = w_out[l].astype(BF16)
    zpad = jnp.zeros((SUBLANES - N_GROUPS, D), F32)
    w_rt = jnp.concatenate([w_group[l].T, zpad, w_router[l].T], axis=0)
    b_rt = jnp.concatenate([b_group[l], jnp.zeros((SUBLANES - N_GROUPS,), F32), b_router[l]])
    b_rt = jnp.broadcast_to(b_rt[:, None], (SUBLANES + N_EXPERTS, LANES))
    x1, h2, ri, rw, cnt = _route(d_out, f_out, w_o[:SEC], w_o[SEC:], x, mod,
                                 g_post_attn[l][None, :], g_pre_ffn[l][None, :], w_rt, b_rt)

    counts = cnt[:, 0].astype(jnp.int32)
    padded = ((counts + BM - 1) // BM) * BM
    pend = jnp.cumsum(padded)
    pstart = pend - padded
    eids = jnp.arange(N_EXPERTS, dtype=jnp.int32)
    dest = ri[2:4] + jnp.sum(jnp.where(ri[0:2, :, None] == eids, pstart, 0), axis=-1)
    nb = (2 * T) // BM + N_EXPERTS
    n_used = (pend[-1] // BM).astype(jnp.int32).reshape(1)
    blk0 = jnp.arange(nb, dtype=jnp.int32) * BM
    block_e = jnp.minimum(jnp.sum((blk0[:, None] >= pend[None, :]).astype(jnp.int32), axis=1),
                          N_EXPERTS - 1)
    cend = pstart + counts
    blk_end = jnp.sum(jnp.where(block_e[:, None] == eids, cend, 0), axis=-1)
    n_valid = jnp.clip(blk_end - blk0, 0, BM).astype(jnp.int32)

    xs = _dispatch(dest, h2.reshape(T, D // 2), nb * BM)
    present = padded > 0
    run_rank = jnp.cumsum(present.astype(jnp.int32)) - present.astype(jnp.int32)
    later = jnp.where(present[None, :] & (eids[None, :] > eids[:, None]), eids[None, :], N_EXPERTS)
    nxt = jnp.min(later, axis=1)
    nxt = jnp.where(nxt >= N_EXPERTS, eids, nxt)
    onehot = block_e[:, None] == eids
    parity = jnp.sum(jnp.where(onehot, run_rank % 2, 0), axis=-1).astype(jnp.int32)
    nxt_b = jnp.sum(jnp.where(onehot, nxt, 0), axis=-1).astype(jnp.int32)
    idx_a = jnp.where(parity == 0, block_e, nxt_b).astype(jnp.int32)
    idx_b = jnp.where(parity == 1, block_e, nxt_b).astype(jnp.int32)
    ys = _experts(block_e, n_used, n_valid, idx_a, idx_b, parity, xs, w_gate[l], w_up[l],
                  w_down[l])
    out = None
    tc = T // COMBINE_CHUNKS
    for ck in range(COMBINE_CHUNKS):
        g = _gather_rows(ys, dest[:, ck * tc:(ck + 1) * tc])
        out = _combine(g, rw, x1.reshape(T, D), mod, g_post_ffn[l][None, :], S, ck, out)
    return out.reshape(B, S, D)
```

```python
import functools
import math

import jax
import jax.numpy as jnp
from jax import lax
from jax.experimental import pallas as pl
from jax.experimental.pallas import tpu as pltpu
from jax.experimental.pallas import tpu_sc as plsc

F32 = jnp.float32
BF16 = jnp.bfloat16

D_MODEL = 1024
HEAD_DIM = 64
CHUNK = 64
DIFF_HEADS = 4
FOX_HEADS = 8
SEC = 512
N_GROUPS = 4
EXPERTS_PER_GROUP = 8
N_EXPERTS = 32
D_EXPERT = 512
EPS = 1e-6
NEG_INF = -1e30
LOG2E = 1.4426950408889634
LAMBDA_INIT = 0.8 - 0.6 * math.exp(-0.3 * 0)

LANES = 128
SUBLANES = 8
BF16_SUBLANES = 16
TN_ADA = 1024
TM_PROJ = 1024
TQ = 512
TK = 512
HK = TK // 2
ATT_GROUPS = 1
DV_ROWS = 2 * HEAD_DIM + BF16_SUBLANES
TM_ROUTE = 512
ROUTE_ROW_GROUPS = 4
BM = 512
EXPERT_ROW_GROUPS = 2
TM_ROWS = 512
COMBINE_CHUNKS = 4
SC_CORES = 2
SC_SUBCORES = 16
SC_WORKERS = SC_CORES * SC_SUBCORES
SC_WIN = 128
VMEM_LIMIT = 48 * 1024 * 1024
_NT = (((1,), (1,)), ((), ()))


def _cparams(sem, flags=None):
    return pltpu.CompilerParams(dimension_semantics=sem, vmem_limit_bytes=VMEM_LIMIT, flags=flags)


def _rms(x, g):
    return x * lax.rsqrt(jnp.mean(x * x, axis=-1, keepdims=True) + EPS) * g


def _pack_halves(x):
    m = x.shape[1] // 2
    bits = lax.bitcast_convert_type(x.astype(BF16).astype(F32), jnp.uint32)
    return (bits[:, :m] >> 16) | (bits[:, m:] & jnp.uint32(0xFFFF0000))


def _unpack_halves(w):
    lo = lax.bitcast_convert_type(w << 16, F32)
    hi = lax.bitcast_convert_type(w & jnp.uint32(0xFFFF0000), F32)
    return lo, hi


def _rows_to_columns(x):
    pad = jnp.zeros((LANES - x.shape[0], LANES), x.dtype)
    return jnp.concatenate([x, pad], axis=0).T


def _ada_kernel(c_ref, w_ref, b_ref, o_ref):
    c = c_ref[...]
    cs = c * jax.nn.sigmoid(c)
    o_ref[...] = jnp.dot(cs, w_ref[...], preferred_element_type=F32,
                         precision=lax.Precision.HIGHEST) + b_ref[...]


def _ada(c, w_ada, b_ada):
    B, D = c.shape
    N = w_ada.shape[1]
    tn = TN_ADA
    return pl.pallas_call(
        _ada_kernel,
        grid=(N // tn,),
        in_specs=[pl.BlockSpec((B, D), lambda j: (0, 0)),
                  pl.BlockSpec((D, tn), lambda j: (0, j)),
                  pl.BlockSpec((1, tn), lambda j: (0, j))],
        out_specs=pl.BlockSpec((B, tn), lambda j: (0, j)),
        out_shape=jax.ShapeDtypeStruct((B, N), F32),
        compiler_params=_cparams(("arbitrary",)),
        name="ada_mod",
    )(c, w_ada, b_ada.reshape(1, N))


def _split3(x):
    hi = x.astype(BF16)
    r1 = x - hi.astype(F32)
    mid = r1.astype(BF16)
    lo = (r1 - mid.astype(F32)).astype(BF16)
    return hi, mid, lo


def _inproj_kernel(x_ref, mod_ref, g_ref, w_ref, wvt_ref, bf_ref, cos_ref, sin_ref,
                   dq_ref, dk_ref, fq_ref, fk_ref, dvt_ref, fvt_ref, fcum_ref, carry_ref):
    si = pl.program_id(1)
    x = x_ref[...]
    h = _rms(x, g_ref[...] * (1.0 + mod_ref[1:2, :])) + mod_ref[0:1, :]
    hb = h.astype(BF16)
    tm = x.shape[0]

    cos = cos_ref[...]
    sin = sin_ref[...]
    lane = lax.broadcasted_iota(jnp.int32, (tm, LANES), 1)
    first_half = (lane % HEAD_DIM) < (HEAD_DIM // 2)
    qscale = HEAD_DIM ** -0.5 * LOG2E

    def proj(sec):
        return jnp.dot(hb, w_ref[:, sec * SEC:(sec + 1) * SEC], preferred_element_type=F32)

    def rope_store(p, out_ref, scale):
        for j in range(SEC // LANES):
            ch = p[:, j * LANES:(j + 1) * LANES]
            partner = jnp.where(first_half,
                                pltpu.roll(ch, LANES - HEAD_DIM // 2, 1),
                                pltpu.roll(ch, HEAD_DIM // 2, 1))
            r = ch * cos + partner * sin
            if scale != 1.0:
                r = r * scale
            out_ref[:, j * LANES:(j + 1) * LANES] = r.astype(out_ref.dtype)

    rope_store(proj(0), dq_ref, qscale)
    rope_store(proj(1), dk_ref, 1.0)
    fq_ref[...] = (proj(2) * qscale).astype(fq_ref.dtype)
    fk_ref[...] = proj(3).astype(fk_ref.dtype)
    vt = lax.dot_general(wvt_ref[...], hb, _NT, preferred_element_type=F32)
    for hd in range(DIFF_HEADS):
        dvt_ref[hd, 0:LANES, :] = vt[hd * LANES:(hd + 1) * LANES].astype(dvt_ref.dtype)
        dvt_ref[hd, LANES:DV_ROWS, :] = jnp.ones((DV_ROWS - LANES, tm), dvt_ref.dtype)
    fvt_ref[...] = vt[SEC:2 * SEC].astype(fvt_ref.dtype)

    z = vt[2 * SEC:2 * SEC + FOX_HEADS] + bf_ref[:, 0:1]
    lf = (jnp.minimum(z, 0.0) - jnp.log1p(jnp.exp(-jnp.abs(z)))) * LOG2E

    @pl.when(si == 0)
    def _():
        carry_ref[...] = jnp.zeros_like(carry_ref)

    r_i = lax.broadcasted_iota(jnp.int32, (LANES, LANES), 0)
    c_i = lax.broadcasted_iota(jnp.int32, (LANES, LANES), 1)
    tri = jnp.where(r_i <= c_i, 1.0, 0.0).astype(BF16)
    run = carry_ref[...]
    for j in range(tm // LANES):
        hi, mid, lo = _split3(lf[:, j * LANES:(j + 1) * LANES])
        cs = (jnp.dot(hi, tri, preferred_element_type=F32)
              + jnp.dot(mid, tri, preferred_element_type=F32)
              + jnp.dot(lo, tri, preferred_element_type=F32)) + run
        fcum_ref[:, j * LANES:(j + 1) * LANES] = cs
        run = jnp.broadcast_to(cs[:, LANES - 1:LANES], run.shape)
    carry_ref[...] = run


def _inproj(x, mod, g_pre, w_main, w_vt, bf, cos, sin):
    B, S, D = x.shape
    tm = TM_PROJ
    sec_out = jax.ShapeDtypeStruct((B, S, SEC), BF16)
    sec_spec = pl.BlockSpec((None, tm, SEC), lambda b, s: (b, s, 0))
    t_out = jax.ShapeDtypeStruct((B, SEC, S), BF16)
    t_spec = pl.BlockSpec((None, SEC, tm), lambda b, s: (b, 0, s))
    return pl.pallas_call(
        _inproj_kernel,
        grid=(B, S // tm),
        in_specs=[pl.BlockSpec((None, tm, D), lambda b, s: (b, s, 0)),
                  pl.BlockSpec((None, 6, D), lambda b, s: (b, 0, 0)),
                  pl.BlockSpec((1, D), lambda b, s: (0, 0)),
                  pl.BlockSpec((D, 4 * SEC), lambda b, s: (0, 0)),
                  pl.BlockSpec(w_vt.shape, lambda b, s: (0, 0)),
                  pl.BlockSpec((FOX_HEADS, LANES), lambda b, s: (0, 0)),
                  pl.BlockSpec((tm, LANES), lambda b, s: (s, 0)),
                  pl.BlockSpec((tm, LANES), lambda b, s: (s, 0))],
        out_specs=[sec_spec] * 4
        + [pl.BlockSpec((None, DIFF_HEADS, DV_ROWS, tm), lambda b, s: (b, 0, 0, s)), t_spec,
           pl.BlockSpec((None, FOX_HEADS, tm), lambda b, s: (b, 0, s))],
        out_shape=[sec_out] * 4
        + [jax.ShapeDtypeStruct((B, DIFF_HEADS, DV_ROWS, S), BF16), t_out,
           jax.ShapeDtypeStruct((B, FOX_HEADS, S), F32)],
        scratch_shapes=[pltpu.VMEM((FOX_HEADS, LANES), F32)],
        compiler_params=_cparams(("arbitrary", "arbitrary")),
        name="in_proj",
    )(x, mod, g_pre, w_main, w_vt, bf, cos, sin)


def _scores(ch, r0, qm=None):
    r0 = pl.multiple_of(r0, HK)
    t = lax.dot_general(ch["keys"](r0), ch["qm"] if qm is None else qm, _NT,
                        preferred_element_type=F32)
    if ch["kbias"] is not None:
        t = t - ch["kbias"](r0)
    return t


def _flash_t(chains, sa_ref, m_ref, acc_ref, qi, diag_mask, next_qm):
    def update(ci, ch, t, r0, cols=slice(None)):
        m = m_ref[ci, :, cols]
        mt = jnp.max(t, axis=0, keepdims=True)
        qb = ch["qbias"]
        qb = None if qb is None else qb[:, cols]
        m_new = jnp.maximum(m, mt if qb is None else mt + qb)
        alpha = jnp.exp2(m - m_new)
        p = jnp.exp2((t - (m_new if qb is None else m_new - qb)).astype(BF16))
        vt = ch["vt"](pl.multiple_of(r0, HK))
        m_ref[ci, :, cols] = m_new
        acc_ref[ci, :, cols] = (alpha * acc_ref[ci, :, cols]
                                + jnp.dot(vt, p, preferred_element_type=F32))

    def full_tiles(k0, n):
        tb = [_scores(ch, k0 + HK) for ch in chains]
        for t in range(n):
            kt = k0 + t * TK
            for ci, ch in enumerate(chains):
                update(ci, ch, sa_ref[ci], kt)
            for ci, ch in enumerate(chains):
                sa_ref[ci] = _scores(ch, kt + TK)
            tb_next = [_scores(ch, kt + TK + HK) for ch in chains] if t + 1 < n else None
            for ci, ch in enumerate(chains):
                update(ci, ch, tb[ci], kt + HK)
            tb = tb_next

    def tile_pair(kj, carry):
        full_tiles(kj * (2 * TK), 2)
        return carry

    m_ref[...] = jnp.full(m_ref.shape, NEG_INF, F32)
    acc_ref[...] = jnp.zeros(acc_ref.shape, F32)
    lax.fori_loop(0, qi // 2, tile_pair, 0)

    def odd_tile(_, carry):
        full_tiles((qi - 1) * TK, 1)
        return carry

    lax.fori_loop(0, qi % 2, odd_tile, 0)
    k0 = qi * TK
    hq = TQ - HK
    tb = [_scores(ch, k0 + HK, ch["qm"][hq:]) for ch in chains]
    nxt = [_scores(ch, 0, qm) for ch, qm in zip(chains, next_qm)]
    for ci, ch in enumerate(chains):
        update(ci, ch, jnp.where(diag_mask[0:HK], sa_ref[ci], NEG_INF), k0)
    for ci, ch in enumerate(chains):
        update(ci, ch, jnp.where(diag_mask[HK:TK, hq:], tb[ci], NEG_INF), k0 + HK,
               slice(hq, TQ))
    for ci in range(len(chains)):
        sa_ref[ci] = nxt[ci]
    return tuple(acc_ref[ci] for ci in range(len(chains)))


def _next_tile_start(q0, seq_len):
    return pl.multiple_of(jnp.minimum(q0 + TQ, seq_len - TQ), TQ)


def _for_each_query_tile(n_tiles, tile_fn):
    def body(qi, carry):
        tile_fn(qi, pl.multiple_of(qi * TQ, TQ))
        return carry

    lax.fori_loop(0, n_tiles, body, 0)


def _lane_group(ref, rows, g):
    return ref[rows, g * LANES:(g + 1) * LANES]


def _masked_queries(q_ref, q0):
    lane = lax.broadcasted_iota(jnp.int32, (TQ, LANES), 1)
    out = []
    for g in range(ATT_GROUPS):
        q2 = _lane_group(q_ref, pl.ds(q0, TQ), g)
        out += [jnp.where((lane >= j * HEAD_DIM) & (lane < (j + 1) * HEAD_DIM), q2,
                          jnp.zeros_like(q2)) for j in range(2)]
    return out


def _fox_kernel(q_ref, k_ref, vt_ref, frow_ref, o_ref, sa_ref, m_ref, acc_ref, fcol_ref):
    for g in range(ATT_GROUPS):
        for j in range(frow_ref.shape[2] // LANES):
            fcol_ref[g, j * LANES:(j + 1) * LANES, :] = _rows_to_columns(
                frow_ref[g, :, j * LANES:(j + 1) * LANES])

    vrow = lax.broadcasted_iota(jnp.int32, (LANES, HK), 0)
    r = lax.broadcasted_iota(jnp.int32, (TK, TQ), 0)
    c = lax.broadcasted_iota(jnp.int32, (TK, TQ), 1)

    def make_chains(q0):
        chains = []
        for ci, qm in enumerate(_masked_queries(q_ref, q0)):
            g, j = divmod(ci, 2)
            v_rows = (vrow >= j * HEAD_DIM) & (vrow < (j + 1) * HEAD_DIM)

            def vt(k0, g=g, v_rows=v_rows):
                v = vt_ref[g * LANES:(g + 1) * LANES, pl.ds(k0, HK)]
                return jnp.where(v_rows, v, jnp.ones_like(v))

            chains.append(dict(
                qm=qm, keys=lambda r0, g=g: _lane_group(k_ref, pl.ds(r0, HK), g),
                kbias=lambda r0, g=g, j=j: fcol_ref[g, pl.ds(r0, HK), j:j + 1],
                qbias=frow_ref[g, j:j + 1, pl.ds(q0, TQ)], vt=vt, rows=LANES))
        return chains

    def tile(qi, q0):
        res = _flash_t(make_chains(q0), sa_ref, m_ref, acc_ref, qi, r <= c,
                       _masked_queries(q_ref, _next_tile_start(q0, q_ref.shape[0])))
        h = HEAD_DIM
        for g in range(ATT_GROUPS):
            a0, a1 = res[2 * g], res[2 * g + 1]
            o_t = jnp.concatenate([a0[0:h] / a0[h:2 * h], a1[h:2 * h] / a1[0:h]], axis=0)
            o_ref[pl.ds(q0, TQ), g * LANES:(g + 1) * LANES] = o_t.T.astype(o_ref.dtype)

    for ci, ch in enumerate(make_chains(0)):
        sa_ref[ci] = _scores(ch, 0)
    _for_each_query_tile(q_ref.shape[0] // TQ, tile)


def _fox_attention(fq, fk, fvt, frow):
    B, S, _ = fq.shape
    gw = ATT_GROUPS * LANES
    seq_spec = pl.BlockSpec((None, S, gw), lambda b, p: (b, 0, p))
    return pl.pallas_call(
        _fox_kernel,
        grid=(B, FOX_HEADS // 2 // ATT_GROUPS),
        in_specs=[seq_spec, seq_spec,
                  pl.BlockSpec((None, gw, S), lambda b, p: (b, p, 0)),
                  pl.BlockSpec((None, ATT_GROUPS, 2, S), lambda b, p: (b, p, 0, 0))],
        out_specs=seq_spec,
        out_shape=jax.ShapeDtypeStruct((B, S, SEC), BF16),
        scratch_shapes=[pltpu.VMEM((2 * ATT_GROUPS, HK, TQ), F32),
                        pltpu.VMEM((2 * ATT_GROUPS, 1, TQ), F32),
                        pltpu.VMEM((2 * ATT_GROUPS, LANES, TQ), F32),
                        pltpu.VMEM((ATT_GROUPS, S, LANES), F32)],
        compiler_params=_cparams(("arbitrary", "arbitrary")),
        name="fox_attention",
    )(fq, fk, fvt, frow)


def _diff_kernel(q_ref, k_ref, vt_ref, lam_ref, g_ref, o_ref, sa_ref, m_ref, acc_ref):
    r = lax.broadcasted_iota(jnp.int32, (TK, TQ), 0)
    c = lax.broadcasted_iota(jnp.int32, (TK, TQ), 1)
    lp = lam_ref[...]
    lam = (jnp.exp(jnp.sum(lp[0:1] * lp[1:2], axis=1, keepdims=True))
           - jnp.exp(jnp.sum(lp[2:3] * lp[3:4], axis=1, keepdims=True)) + LAMBDA_INIT)
    gain_col = _rows_to_columns(g_ref[...] * (1.0 - LAMBDA_INIT))[:, 0:1]

    def make_chains(q0):
        return [dict(qm=qm, keys=lambda r0, g=ci // 2: _lane_group(k_ref, pl.ds(r0, HK), g),
                     kbias=None, qbias=None,
                     vt=lambda k0, g=ci // 2: vt_ref[g, :, pl.ds(k0, HK)], rows=DV_ROWS)
                for ci, qm in enumerate(_masked_queries(q_ref, q0))]

    def tile(qi, q0):
        res = _flash_t(make_chains(q0), sa_ref, m_ref, acc_ref, qi,
                       (r // CHUNK) <= (c // CHUNK),
                       _masked_queries(q_ref, _next_tile_start(q0, q_ref.shape[0])))
        v = LANES
        for g in range(ATT_GROUPS):
            a0, a1 = res[2 * g], res[2 * g + 1]
            o_t = a0[0:v] / a0[v:v + 1] - lam * (a1[0:v] / a1[v:v + 1])
            inv = lax.rsqrt(jnp.mean(o_t * o_t, axis=0, keepdims=True) + EPS)
            o_t = o_t * inv * gain_col
            o_ref[pl.ds(q0, TQ), g * LANES:(g + 1) * LANES] = o_t.T.astype(o_ref.dtype)

    for ci, ch in enumerate(make_chains(0)):
        sa_ref[ci] = _scores(ch, 0)
    _for_each_query_tile(q_ref.shape[0] // TQ, tile)


def _diff_attention(dq, dk, dvt, lam_params, g_subln):
    B, S, _ = dq.shape
    seq_spec = pl.BlockSpec((None, S, ATT_GROUPS * LANES), lambda b, h: (b, 0, h))
    return pl.pallas_call(
        _diff_kernel,
        grid=(B, DIFF_HEADS // ATT_GROUPS),
        in_specs=[seq_spec, seq_spec,
                  pl.BlockSpec((None, ATT_GROUPS, DV_ROWS, S), lambda b, h: (b, h, 0, 0)),
                  pl.BlockSpec((4, HEAD_DIM), lambda b, h: (0, 0)),
                  pl.BlockSpec((1, LANES), lambda b, h: (0, 0))],
        out_specs=seq_spec,
        out_shape=jax.ShapeDtypeStruct((B, S, SEC), BF16),
        scratch_shapes=[pltpu.VMEM((2 * ATT_GROUPS, HK, TQ), F32),
                        pltpu.VMEM((2 * ATT_GROUPS, 1, TQ), F32),
                        pltpu.VMEM((2 * ATT_GROUPS, DV_ROWS, TQ), F32)],
        compiler_params=_cparams(("arbitrary", "arbitrary")),
        name="diff_attention",
    )(dq, dk, dvt, lam_params, g_subln)


def _route_kernel(d_ref, f_ref, wd_ref, wf_ref, x_ref, mod_ref, gpa_ref, gpf_ref, wr_ref, br_ref,
                  x1_ref, h2_ref, ri_ref, rw_ref, cnt_ref):
    first = (pl.program_id(0) == 0) & (pl.program_id(1) == 0)

    @pl.when(first)
    def _():
        cnt_ref[...] = jnp.zeros_like(cnt_ref)

    tm = x_ref.shape[0]
    rows = tm // ROUTE_ROW_GROUPS
    mixed = [jnp.dot(d_ref[r0:r0 + rows, :], wd_ref[...], preferred_element_type=F32)
             + jnp.dot(f_ref[r0:r0 + rows, :], wf_ref[...], preferred_element_type=F32)
             for r0 in range(0, tm, rows)]
    wr = wr_ref[...].astype(BF16)
    gate_gain = mod_ref[2:3, :] * gpa_ref[...]
    ffn_gain = gpf_ref[...] * (1.0 + mod_ref[4:5, :])
    lts = []
    for gi, r0 in enumerate(range(0, tm, rows)):
        x1 = x_ref[r0:r0 + rows, :] + _rms(mixed[gi], gate_gain)
        x1_ref[r0:r0 + rows, :] = x1
        h2 = _rms(x1, ffn_gain) + mod_ref[3:4, :]
        h2_ref[r0:r0 + rows, :] = _pack_halves(h2)
        lts.append(br_ref[:, 0:1] + lax.dot_general(wr, h2.astype(BF16), _NT,
                                                    preferred_element_type=F32))
    lt = jnp.concatenate(lts, axis=1)

    g = lt[0:N_GROUPS]
    row4 = lax.broadcasted_iota(jnp.int32, g.shape, 0)
    gmax = jnp.max(g, axis=0, keepdims=True)
    gidx = jnp.min(jnp.where(g == gmax, row4, N_GROUPS), axis=0, keepdims=True)
    gw = 1.0 / jnp.sum(jnp.exp(g - gmax), axis=0, keepdims=True)
    esel = jnp.zeros((EXPERTS_PER_GROUP, tm), F32)
    for gg in range(N_GROUPS):
        e0 = SUBLANES + EXPERTS_PER_GROUP * gg
        esel = jnp.where(gidx == gg, lt[e0:e0 + EXPERTS_PER_GROUP], esel)
    row8 = lax.broadcasted_iota(jnp.int32, esel.shape, 0)
    v1 = jnp.max(esel, axis=0, keepdims=True)
    i1 = jnp.min(jnp.where(esel == v1, row8, EXPERTS_PER_GROUP), axis=0, keepdims=True)
    esel2 = jnp.where(row8 == i1, -jnp.inf, esel)
    v2 = jnp.max(esel2, axis=0, keepdims=True)
    i2 = jnp.min(jnp.where(esel2 == v2, row8, EXPERTS_PER_GROUP), axis=0, keepdims=True)
    e = jnp.exp(v2 - v1)
    w1 = gw / (1.0 + e)
    w2 = gw * e / (1.0 + e)
    e1 = gidx * EXPERTS_PER_GROUP + i1
    e2 = gidx * EXPERTS_PER_GROUP + i2

    row32 = lax.broadcasted_iota(jnp.int32, (N_EXPERTS, tm), 0)
    oh1 = row32 == e1
    oh2 = row32 == e2
    oh = jnp.where(oh1 | oh2, 1.0, 0.0)
    t_r = lax.broadcasted_iota(jnp.int32, (tm, tm), 0)
    t_c = lax.broadcasted_iota(jnp.int32, (tm, tm), 1)
    tri = jnp.where(t_r < t_c, 1.0, 0.0).astype(BF16)
    tot = jnp.dot(oh.astype(BF16), tri, preferred_element_type=F32) + cnt_ref[:, 0:1]
    r1 = jnp.sum(jnp.where(oh1, tot, 0.0), axis=0, keepdims=True)
    r2 = jnp.sum(jnp.where(oh2, tot, 0.0), axis=0, keepdims=True)
    cnt_ref[...] = cnt_ref[...] + jnp.sum(oh, axis=1, keepdims=True)

    zi = jnp.zeros((SUBLANES - 4, tm), jnp.int32)
    ri_ref[...] = jnp.concatenate([e1, e2, r1.astype(jnp.int32), r2.astype(jnp.int32), zi], axis=0)
    rw_ref[...] = jnp.concatenate([w1, w2, jnp.zeros((SUBLANES - 2, tm), F32)], axis=0)


def _route(d_out, f_out, w_out_d, w_out_f, x, mod, g_post_attn, g_pre_ffn, w_rt, b_rt):
    B, S, D = x.shape
    tm = TM_ROUTE
    nrt = w_rt.shape[0]
    n_s = S // tm
    T = B * S
    return pl.pallas_call(
        _route_kernel,
        grid=(B, n_s),
        in_specs=[pl.BlockSpec((None, tm, SEC), lambda b, s: (b, s, 0)),
                  pl.BlockSpec((None, tm, SEC), lambda b, s: (b, s, 0)),
                  pl.BlockSpec((SEC, D), lambda b, s: (0, 0)),
                  pl.BlockSpec((SEC, D), lambda b, s: (0, 0)),
                  pl.BlockSpec((None, tm, D), lambda b, s: (b, s, 0)),
                  pl.BlockSpec((None, 6, D), lambda b, s: (b, 0, 0)),
                  pl.BlockSpec((1, D), lambda b, s: (0, 0)),
                  pl.BlockSpec((1, D), lambda b, s: (0, 0)),
                  pl.BlockSpec((nrt, D), lambda b, s: (0, 0)),
                  pl.BlockSpec((nrt, LANES), lambda b, s: (0, 0))],
        out_specs=[pl.BlockSpec((None, tm, D), lambda b, s: (b, s, 0)),
                   pl.BlockSpec((None, tm, D // 2), lambda b, s: (b, s, 0)),
                   pl.BlockSpec((SUBLANES, tm), lambda b, s: (0, b * n_s + s)),
                   pl.BlockSpec((SUBLANES, tm), lambda b, s: (0, b * n_s + s)),
                   pl.BlockSpec((N_EXPERTS, LANES), lambda b, s: (0, 0))],
        out_shape=[jax.ShapeDtypeStruct((B, S, D), F32),
                   jax.ShapeDtypeStruct((B, S, D // 2), jnp.uint32),
                   jax.ShapeDtypeStruct((SUBLANES, T), jnp.int32),
                   jax.ShapeDtypeStruct((SUBLANES, T), F32),
                   jax.ShapeDtypeStruct((N_EXPERTS, LANES), F32)],
        compiler_params=_cparams(("arbitrary", "arbitrary")),
        name="out_proj_route",
    )(d_out, f_out, w_out_d, w_out_f, x, mod, g_post_attn, g_pre_ffn, w_rt, b_rt)


def _sc_mesh():
    return plsc.VectorSubcoreMesh(core_axis_name="c", subcore_axis_name="s",
                                  num_cores=SC_CORES, num_subcores=SC_SUBCORES)


def _sc_worker_base(per_worker):
    return (lax.axis_index("s") * SC_CORES + lax.axis_index("c")) * per_worker


def _dispatch(dest, h2, n_rows):
    T, D = h2.shape
    per_worker = T // SC_WORKERS

    @functools.partial(
        pl.kernel, mesh=_sc_mesh(),
        out_type=jax.ShapeDtypeStruct((n_rows, D), h2.dtype),
        scratch_types=[pltpu.VMEM((SC_WIN,), jnp.int32), pltpu.VMEM((SC_WIN, D), h2.dtype)],
        name="moe_dispatch")
    def run(h_hbm, dest_hbm, xs_hbm, idx_v, rows_v):
        base = _sc_worker_base(per_worker)

        @pl.loop(0, per_worker // SC_WIN)
        def _(ci):
            off = pl.multiple_of(base + ci * SC_WIN, SC_WIN)
            pltpu.sync_copy(h_hbm.at[pl.ds(off, SC_WIN)], rows_v)
            for k in range(2):
                pltpu.sync_copy(dest_hbm.at[pl.ds(k * T + off, SC_WIN)], idx_v)
                pltpu.sync_copy(rows_v, xs_hbm.at[idx_v])

    return run(h2, dest.reshape(2 * T))


def _gather_rows(ys, dest):
    _, D = ys.shape
    K, T = dest.shape
    n = K * T
    per_worker = n // SC_WORKERS

    @functools.partial(
        pl.kernel, mesh=_sc_mesh(),
        out_type=jax.ShapeDtypeStruct((n, D), ys.dtype),
        scratch_types=[pltpu.VMEM((SC_WIN,), jnp.int32), pltpu.VMEM((SC_WIN, D), ys.dtype)],
        name="moe_gather")
    def run(ys_hbm, dest_hbm, g_hbm, idx_v, rows_v):
        base = _sc_worker_base(per_worker)

        @pl.loop(0, per_worker // SC_WIN)
        def _(ci):
            off = pl.multiple_of(base + ci * SC_WIN, SC_WIN)
            pltpu.sync_copy(dest_hbm.at[pl.ds(off, SC_WIN)], idx_v)
            pltpu.sync_copy(ys_hbm.at[idx_v], rows_v)
            pltpu.sync_copy(rows_v, g_hbm.at[pl.ds(off, SC_WIN)])

    return run(ys, dest.reshape(n)).reshape(K, T, D)


def _expert_kernel(be_ref, nu_ref, nv_ref, xs_ref, wg_ref, wu_ref, wd_ref, ys_ref,
                   wgb, wub, wdb):
    i = pl.program_id(0)

    @pl.when(i < nu_ref[0])
    def _():
        prev = be_ref[jnp.maximum(i - 1, 0)]

        @pl.when((i == 0) | (be_ref[i] != prev))
        def _():
            wgb[...] = wg_ref[...].astype(BF16)
            wub[...] = wu_ref[...].astype(BF16)
            wdb[...] = wd_ref[...].astype(BF16)

        half = xs_ref.shape[1]
        rows = xs_ref.shape[0] // EXPERT_ROW_GROUPS
        row = lax.broadcasted_iota(jnp.int32, (rows, half), 0)
        au = []
        for r0 in range(0, xs_ref.shape[0], rows):
            words = jnp.where(row + r0 < nv_ref[i], xs_ref[r0:r0 + rows, :], jnp.uint32(0))
            lo, hi = (v.astype(BF16) for v in _unpack_halves(words))

            def proj(w, lo=lo, hi=hi):
                return (jnp.dot(lo, w[0:half, :], preferred_element_type=F32)
                        + jnp.dot(hi, w[half:2 * half, :], preferred_element_type=F32))

            au.append((proj(wgb), proj(wub)))
        for gi, (a, u) in enumerate(au):
            hmid = (a * jax.nn.sigmoid(a) * u).astype(BF16)
            y = jnp.dot(hmid, wdb[...], preferred_element_type=F32)
            ys_ref[gi * rows:(gi + 1) * rows, :] = _pack_halves(y)

    @pl.when(i >= nu_ref[0])
    def _():
        ys_ref[...] = jnp.zeros_like(ys_ref)


def _experts(block_e, n_used, n_valid, xs, w_gate, w_up, w_down):
    P, Dh = xs.shape
    nb = P // BM
    _, D, Fd = w_gate.shape

    def row_map(i, be, nu, nv):
        return (jnp.minimum(i, nu[0] - 1), 0)

    def w_map(i, be, nu, nv):
        return (be[i], 0, 0)

    return pl.pallas_call(
        _expert_kernel,
        grid_spec=pltpu.PrefetchScalarGridSpec(
            num_scalar_prefetch=3,
            grid=(nb,),
            in_specs=[pl.BlockSpec((BM, Dh), row_map),
                      pl.BlockSpec((None, D, Fd), w_map),
                      pl.BlockSpec((None, D, Fd), w_map),
                      pl.BlockSpec((None, Fd, D), w_map)],
            out_specs=pl.BlockSpec((BM, Dh), lambda i, be, nu, nv: (i, 0)),
            scratch_shapes=[pltpu.VMEM((D, Fd), BF16), pltpu.VMEM((D, Fd), BF16),
                            pltpu.VMEM((Fd, D), BF16)]),
        out_shape=jax.ShapeDtypeStruct((P, Dh), jnp.uint32),
        compiler_params=_cparams(("arbitrary",)),
        name="moe_experts",
    )(block_e, n_used, n_valid, xs, w_gate, w_up, w_down)


def _combine_kernel(g_ref, w_ref, x1_ref, mod_ref, g_post_ref, *rest):
    o_ref = rest[-1]
    lo0, hi0 = _unpack_halves(g_ref[0])
    lo1, hi1 = _unpack_halves(g_ref[1])
    tm = lo0.shape[0]
    wt = jnp.concatenate([_rows_to_columns(w_ref[:, j * LANES:(j + 1) * LANES])
                          for j in range(tm // LANES)], axis=0)
    w0, w1 = wt[:, 0:1], wt[:, 1:2]
    y = jnp.concatenate([w0 * lo0 + w1 * lo1, w0 * hi0 + w1 * hi1], axis=1)
    o_ref[...] = x1_ref[...] + _rms(y, mod_ref[5:6, :] * g_post_ref[...])


def _combine(g, wts, x1, mod, g_post_ffn, S, chunk, prev_out):
    T, D = x1.shape
    tm = TM_ROWS
    per_b = S // tm
    steps = g.shape[1] // tm
    off = chunk * steps
    in_specs = [pl.BlockSpec((2, tm, D // 2), lambda i: (0, i, 0)),
                pl.BlockSpec((SUBLANES, tm), lambda i: (0, i + off)),
                pl.BlockSpec((tm, D), lambda i: (i + off, 0)),
                pl.BlockSpec((None, 6, D), lambda i: ((i + off) // per_b, 0, 0)),
                pl.BlockSpec((1, D), lambda i: (0, 0))]
    args = [g, wts, x1, mod, g_post_ffn]
    aliases = {}
    if prev_out is not None:
        in_specs.append(pl.BlockSpec(memory_space=pl.ANY))
        args.append(prev_out)
        aliases = {len(args) - 1: 0}
    return pl.pallas_call(
        _combine_kernel,
        grid=(steps,),
        in_specs=in_specs,
        out_specs=pl.BlockSpec((tm, D), lambda i: (i + off, 0)),
        out_shape=jax.ShapeDtypeStruct((T, D), F32),
        input_output_aliases=aliases,
        compiler_params=_cparams(("arbitrary",)),
        name="moe_combine",
    )(*args)


def _rope_tables(S):
    inv = 1.0 / (10000.0 ** (jnp.arange(0, HEAD_DIM, 2, dtype=F32) / HEAD_DIM))
    ang = jnp.arange(S, dtype=F32)[:, None] * inv[None, :]
    c, s = jnp.cos(ang), jnp.sin(ang)
    return jnp.tile(c, (1, 4)), jnp.tile(jnp.concatenate([-s, s], axis=1), (1, 2))


def kernel(x, c, w_ada, b_ada, g_pre_attn, g_post_attn, w_in, b_forget, lambda_q1, lambda_k1,
           lambda_q2, lambda_k2, g_diff_subln, w_out, g_pre_ffn, g_post_ffn, w_group, b_group,
           w_router, b_router, w_gate, w_up, w_down):
    B, S, D = x.shape
    T = B * S
    l = 0
    mod = _ada(c, w_ada[l], b_ada[l]).reshape(B, 6, D)

    wi = w_in[l]
    w_main = jnp.concatenate([wi[:, 0:2 * SEC], wi[:, 3 * SEC:5 * SEC]], axis=1).astype(BF16)
    w_vt = jnp.concatenate([wi[:, 2 * SEC:3 * SEC], wi[:, 5 * SEC:6 * SEC], wi[:, 6 * SEC:],
                            jnp.zeros((D, BF16_SUBLANES - FOX_HEADS), F32)], axis=1).T.astype(BF16)
    bf = jnp.broadcast_to(b_forget[l][:, None], (FOX_HEADS, LANES))
    cos, sin = _rope_tables(S)
    dq, dk, fq, fk, dvt, fvt, fcum = _inproj(x, mod, g_pre_attn[l][None, :], w_main, w_vt,
                                             bf, cos, sin)

    lam_params = jnp.stack([lambda_q1[l], lambda_k1[l], lambda_q2[l], lambda_k2[l]])
    d_out = _diff_attention(dq, dk, dvt, lam_params, g_diff_subln[l][None, :])
    frow = fcum.reshape(B, FOX_HEADS // 2, 2, S)
    f_out = _fox_attention(fq, fk, fvt, frow)

    w_o = w_out[l].astype(BF16)
    zpad = jnp.zeros((SUBLANES - N_GROUPS, D), F32)
    w_rt = jnp.concatenate([w_group[l].T, zpad, w_router[l].T], axis=0)
    b_rt = jnp.concatenate([b_group[l], jnp.zeros((SUBLANES - N_GROUPS,), F32), b_router[l]])
    b_rt = jnp.broadcast_to(b_rt[:, None], (SUBLANES + N_EXPERTS, LANES))
    x1, h2, ri, rw, cnt = _route(d_out, f_out, w_o[:SEC], w_o[SEC:], x, mod,
                                 g_post_attn[l][None, :], g_pre_ffn[l][None, :], w_rt, b_rt)

    counts = cnt[:, 0].astype(jnp.int32)
    padded = ((counts + BM - 1) // BM) * BM
    pend = jnp.cumsum(padded)
    pstart = pend - padded
    eids = jnp.arange(N_EXPERTS, dtype=jnp.int32)
    dest = ri[2:4] + jnp.sum(jnp.where(ri[0:2, :, None] == eids, pstart, 0), axis=-1)
    nb = (2 * T) // BM + N_EXPERTS
    n_used = (pend[-1] // BM).astype(jnp.int32).reshape(1)
    blk0 = jnp.arange(nb, dtype=jnp.int32) * BM
    block_e = jnp.minimum(jnp.sum((blk0[:, None] >= pend[None, :]).astype(jnp.int32), axis=1),
                          N_EXPERTS - 1)
    cend = pstart + counts
    blk_end = jnp.sum(jnp.where(block_e[:, None] == eids, cend, 0), axis=-1)
    n_valid = jnp.clip(blk_end - blk0, 0, BM).astype(jnp.int32)

    xs = _dispatch(dest, h2.reshape(T, D // 2), nb * BM)
    ys = _experts(block_e, n_used, n_valid, xs, w_gate[l], w_up[l], w_down[l])
    out = None
    tc = T // COMBINE_CHUNKS
    for ck in range(COMBINE_CHUNKS):
        g = _gather_rows(ys, dest[:, ck * tc:(ck + 1) * tc])
        out = _combine(g, rw, x1.reshape(T, D), mod, g_post_ffn[l][None, :], S, ck, out)
    return out.reshape(B, S, D)
```
